```python
import math
import jax, jax.numpy as jnp
from jax import lax
import numpy as np

D_MODEL = 1024
BATCH = 16
SEQ = 256
DEPTH = 2
DEC_BATCH = 2
DEC_SEQ = 2048
PAST_LEN = 512

GRID_W = 64
HEAD_DIM = 64
NA_HEADS = 8
NA_WIDTH = NA_HEADS * HEAD_DIM
WIN_R = 8
WIN_C = 16
DIFF_HEADS = 4
DIFF_QK = 32
DIFF_V = 2 * DIFF_QK
DIFF_WIDTH = DIFF_HEADS * DIFF_V
SSM_GROUPS = 16
SSM_GROUP_CH = 16
SSM_WIDTH = SSM_GROUPS * SSM_GROUP_CH
SSM_STATE = 64
MIX_WIDTH = NA_WIDTH + DIFF_WIDTH + SSM_WIDTH
IN_WIDTH = 3 * NA_WIDTH + 3 * DIFF_WIDTH + SSM_WIDTH
N_EXPERTS = 16
EXPERT_FF = 2048
EC_CAPACITY_FACTOR = 2
ROPE_BASE = 10000.0
EPS = 1e-6
Q_BLOCK = 128
NEG_INF = -1e30

kernel_name = "hybrid_diffusion_prefix_trunk_step"

F32 = jnp.float32


def rmsnorm(x, g):
    xf = x.astype(F32)
    y = xf * lax.rsqrt(jnp.mean(xf * xf, axis=-1, keepdims=True) + EPS)
    return (y * g.astype(F32)).astype(x.dtype)


def adaln(cond, w, b, dtype):
    m = jax.nn.silu(cond.astype(F32)) @ w.astype(F32) + b.astype(F32)
    m = m.astype(dtype).reshape(cond.shape[0], 6, 1, D_MODEL)
    return [m[:, i] for i in range(6)]


def modulate(x, g, shift, scale):
    return rmsnorm(x, g) * (1.0 + scale) + shift


def split_projection(z):
    b, n, _ = z.shape
    i1, i2, i3 = NA_WIDTH, 2 * NA_WIDTH, 3 * NA_WIDTH
    i4, i5, i6 = i3 + DIFF_WIDTH, i3 + 2 * DIFF_WIDTH, i3 + 3 * DIFF_WIDTH
    qa = z[..., :i1].reshape(b, n, NA_HEADS, HEAD_DIM)
    ka = z[..., i1:i2].reshape(b, n, NA_HEADS, HEAD_DIM)
    va = z[..., i2:i3].reshape(b, n, NA_HEADS, HEAD_DIM)
    qb = z[..., i3:i4].reshape(b, n, DIFF_HEADS, DIFF_V)
    kb = z[..., i4:i5].reshape(b, n, DIFF_HEADS, DIFF_V)
    vb = z[..., i5:i6].reshape(b, n, DIFF_HEADS, DIFF_V)
    u = z[..., i6:]
    return qa, ka, va, qb, kb, vb, u


def axial_rope(x):
    n, d = x.shape[1], x.shape[-1]
    half = d // 2
    t = jnp.arange(n)
    row = (t // GRID_W).astype(F32)
    col = (t % GRID_W).astype(F32)

    def rot(xa, pos):
        nf = xa.shape[-1] // 2
        inv = ROPE_BASE ** (-jnp.arange(nf, dtype=F32) / nf)
        ang = pos[:, None] * inv[None, :]
        cos = jnp.cos(ang)[None, :, None, :]
        sin = jnp.sin(ang)[None, :, None, :]
        x1, x2 = xa[..., :nf], xa[..., nf:]
        return jnp.concatenate([x1 * cos - x2 * sin, x1 * sin + x2 * cos], axis=-1)

    xf = x.astype(F32)
    return jnp.concatenate([rot(xf[..., :half], row), rot(xf[..., half:], col)], axis=-1).astype(x.dtype)


def block_attend(q, k, v, scale):
    b, lq, h, d = q.shape
    nb = lq // Q_BLOCK
    qb = q.reshape(b, nb, Q_BLOCK, h, d).transpose(1, 0, 2, 3, 4)
    kf, vf = k.astype(F32), v.astype(F32)

    def one(qblk):
        s = jnp.einsum('bqhd,bkhd->bhqk', qblk.astype(F32), kf) * scale
        p = jax.nn.softmax(s, axis=-1)
        return jnp.einsum('bhqk,bkhe->bqhe', p, vf)

    o = lax.map(one, qb)
    return o.transpose(1, 0, 2, 3, 4).reshape(b, lq, h, -1).astype(q.dtype)


def neighbourhood_attend(q, k, v, k_ctx, v_ctx, rpb):
    b, n, h, d = q.shape
    rows = n // GRID_W
    kr = min(WIN_R, rows)
    kc = WIN_C
    scale = d ** -0.5
    qg = q.reshape(b, rows, GRID_W, h, d).astype(F32)
    kg = k.reshape(b, rows, GRID_W, h, d).astype(F32)
    vg = v.reshape(b, rows, GRID_W, h, d).astype(F32)
    r = jnp.arange(rows)
    row_idx = jnp.clip(r - kr // 2, 0, rows - kr)[:, None] + jnp.arange(kr)[None, :]
    cc = jnp.arange(GRID_W)
    col_start = jnp.clip(cc - kc // 2, 0, GRID_W - kc)
    col_ok = (cc[None, :] >= col_start[:, None]) & (cc[None, :] < col_start[:, None] + kc)
    k_blk = kg[:, row_idx]
    v_blk = vg[:, row_idx]
    s_win = jnp.einsum('brqhd,brkwhd->brhqkw', qg, k_blk) * scale
    roff = row_idx - r[:, None] + WIN_R - 1
    coff = jnp.clip(cc[None, :] - cc[:, None] + WIN_C - 1, 0, 2 * WIN_C - 2)
    bias = rpb.astype(F32)[:, roff[:, :, None, None], coff[None, None, :, :]]
    bias = bias.transpose(1, 0, 3, 2, 4)
    s_win = jnp.where(col_ok[:, None, :], s_win + bias[None], NEG_INF)
    s_win = s_win.reshape(b, rows, h, GRID_W, kr * GRID_W)
    s_ctx = jnp.einsum('brqhd,bchd->brhqc', qg, k_ctx.astype(F32)) * scale
    p = jax.nn.softmax(jnp.concatenate([s_win, s_ctx], axis=-1), axis=-1)
    p_win = p[..., :kr * GRID_W].reshape(b, rows, h, GRID_W, kr, GRID_W)
    p_ctx = p[..., kr * GRID_W:]
    o = (jnp.einsum('brhqkw,brkwhe->brqhe', p_win, v_blk)
         + jnp.einsum('brhqc,bche->brqhe', p_ctx, v_ctx.astype(F32)))
    return o.reshape(b, n, h, d).astype(q.dtype)


def diff_lambda_value(lam_p, lam_init):
    lp = lam_p.astype(F32)
    return jnp.exp(jnp.sum(lp[0] * lp[1])) - jnp.exp(jnp.sum(lp[2] * lp[3])) + lam_init


def diff_output(o1, o2, lam, lam_init, g):
    o = o1.astype(F32) - lam * o2.astype(F32)
    return rmsnorm(o, g) * (1.0 - lam_init)


def _ssm_combine(e1, e2):
    a1, b1 = e1
    a2, b2 = e2
    return a1 * a2, a2 * b1 + b2


def s5_mixer(u, p, h0):
    b, n, _ = u.shape
    uf = u.astype(F32).reshape(b, n, SSM_GROUPS, SSM_GROUP_CH)
    uc = uf.astype(jnp.complex64)
    y = uf * p['ssm_d'].astype(F32).reshape(SSM_GROUPS, SSM_GROUP_CH)
    finals = []
    for direction in range(2):
        reverse = direction == 1
        lam = lax.complex(p['a_re'][direction].astype(F32), p['a_im'][direction].astype(F32))
        dt = jnp.exp(p['log_dt'][direction].astype(F32))[:, None]
        a_bar = jnp.exp(lam * dt)
        b_bar = ((a_bar - 1.0) / lam)[..., None] * lax.complex(
            p['b_re'][direction].astype(F32), p['b_im'][direction].astype(F32))
        bu = jnp.einsum('blgh,gph->blgp', uc, b_bar)
        first = n - 1 if reverse else 0
        last = 0 if reverse else n - 1
        bu = bu.at[:, first].add(a_bar[None] * h0[:, direction])
        _, states = lax.associative_scan(_ssm_combine, (jnp.broadcast_to(a_bar, bu.shape), bu),
                                         axis=1, reverse=reverse)
        c_mat = lax.complex(p['c_re'][direction].astype(F32), p['c_im'][direction].astype(F32))
        y = y + jnp.einsum('blgp,ghp->blgh', states, c_mat).real
        finals.append(states[:, last])
    g = jax.nn.gelu(y.reshape(b, n, SSM_WIDTH))
    out = g * jax.nn.sigmoid(g @ p['w_glu'].astype(F32) + p['b_glu'].astype(F32))
    return out.astype(u.dtype), jnp.stack(finals, axis=1)


def context_mixer(h, p, lam_init):
    b, n, _ = h.shape
    qa, ka, va, qb, kb, vb, u = split_projection(h @ p['w_in'])
    o_a = block_attend(qa, ka, va, HEAD_DIM ** -0.5)
    lam = diff_lambda_value(p['lam'], lam_init)
    o1 = block_attend(qb[..., :DIFF_QK], kb[..., :DIFF_QK], vb, DIFF_QK ** -0.5)
    o2 = block_attend(qb[..., DIFF_QK:], kb[..., DIFF_QK:], vb, DIFF_QK ** -0.5)
    o_b = diff_output(o1, o2, lam, lam_init, p['subln'])
    zero = jnp.zeros((b, 2, SSM_GROUPS, SSM_STATE), jnp.complex64)
    o_c, finals = s5_mixer(u, p, zero)
    o = jnp.concatenate([o_a.reshape(b, n, NA_WIDTH), o_b.reshape(b, n, DIFF_WIDTH).astype(h.dtype), o_c], axis=-1)
    return o @ p['w_out'], (ka, va, kb, vb, finals)


def latent_mixer(h, p, lam_init, ka_c, va_c, kb_c, vb_c, s_re, s_im):
    b, n, _ = h.shape
    qa, ka, va, qb, kb, vb, u = split_projection(h @ p['w_in'])
    o_a = neighbourhood_attend(qa, ka, va, ka_c, va_c, p['rpb'])
    lam = diff_lambda_value(p['lam'], lam_init)
    q1, q2 = axial_rope(qb[..., :DIFF_QK]), axial_rope(qb[..., DIFF_QK:])
    k1 = jnp.concatenate([axial_rope(kb[..., :DIFF_QK]), kb_c[..., :DIFF_QK].astype(kb.dtype)], axis=1)
    k2 = jnp.concatenate([axial_rope(kb[..., DIFF_QK:]), kb_c[..., DIFF_QK:].astype(kb.dtype)], axis=1)
    v_all = jnp.concatenate([vb, vb_c.astype(vb.dtype)], axis=1)
    o1 = block_attend(q1, k1, v_all, DIFF_QK ** -0.5)
    o2 = block_attend(q2, k2, v_all, DIFF_QK ** -0.5)
    o_b = diff_output(o1, o2, lam, lam_init, p['subln'])
    h0 = lax.complex(s_re.astype(F32), s_im.astype(F32))
    o_c, _ = s5_mixer(u, p, h0)
    o = jnp.concatenate([o_a.reshape(b, n, NA_WIDTH), o_b.reshape(b, n, DIFF_WIDTH).astype(h.dtype), o_c], axis=-1)
    return o @ p['w_out']


def expert_choice_ffn(h, p):
    b, n, d = h.shape
    cap = EC_CAPACITY_FACTOR * n // N_EXPERTS
    aff = jax.nn.softmax(h.astype(F32) @ p['w_router'].astype(F32), axis=-1)
    gate, idx = lax.top_k(jnp.swapaxes(aff, 1, 2), cap)
    xs = jax.vmap(lambda hb, ib: hb[ib])(h, idx)
    hid = (jax.nn.silu(jnp.einsum('becd,edf->becf', xs, p['w_gate']))
           * jnp.einsum('becd,edf->becf', xs, p['w_up']))
    ye = jnp.einsum('becf,efd->becd', hid, p['w_down']) * gate[..., None].astype(h.dtype)
    return jax.vmap(lambda yb, ib: jnp.zeros((n, d), yb.dtype).at[ib.reshape(-1)].add(yb.reshape(-1, d)))(ye, idx)


def setup_inputs(seed: int = 0) -> dict:
    key = jax.random.key(seed)
    ks = iter(jax.random.split(key, 48))

    def nrm(shape, s):
        return jax.random.normal(next(ks), shape, F32) * s

    G, P, Hc = SSM_GROUPS, SSM_STATE, SSM_GROUP_CH
    a_im_init = jnp.broadcast_to(math.pi * jnp.arange(P, dtype=F32), (DEPTH, 2, G, P))
    return {
        'x_prompt': nrm((BATCH, SEQ, D_MODEL), 1.0),
        'x_sample': nrm((DEC_BATCH, DEC_SEQ, D_MODEL), 1.0),
        'cache_na_k': nrm((DEC_BATCH, DEPTH, PAST_LEN, NA_HEADS, HEAD_DIM), 1.0),
        'cache_na_v': nrm((DEC_BATCH, DEPTH, PAST_LEN, NA_HEADS, HEAD_DIM), 1.0),
        'cache_diff_k': nrm((DEC_BATCH, DEPTH, PAST_LEN, DIFF_HEADS, DIFF_V), 1.0),
        'cache_diff_v': nrm((DEC_BATCH, DEPTH, PAST_LEN, DIFF_HEADS, DIFF_V), 1.0),
        'state_ssm_re': nrm((DEC_BATCH, DEPTH, 2, G, P), 0.1),
        'state_ssm_im': nrm((DEC_BATCH, DEPTH, 2, G, P), 0.1),
        'c': nrm((DEC_BATCH, D_MODEL), 1.0),
        'c_ctx': nrm((D_MODEL,), 1.0),
        'w_ada': nrm((DEPTH, D_MODEL, 6 * D_MODEL), 0.5 * D_MODEL ** -0.5),
        'b_ada': nrm((DEPTH, 6 * D_MODEL), 0.02),
        'norm_mix': 1.0 + nrm((DEPTH, D_MODEL), 0.02),
        'norm_ffn': 1.0 + nrm((DEPTH, D_MODEL), 0.02),
        'w_in': nrm((DEPTH, D_MODEL, IN_WIDTH), D_MODEL ** -0.5),
        'w_out': nrm((DEPTH, MIX_WIDTH, D_MODEL), MIX_WIDTH ** -0.5),
        'na_rpb': nrm((DEPTH, NA_HEADS, 2 * WIN_R - 1, 2 * WIN_C - 1), 0.1),
        'diff_lambda': nrm((DEPTH, 4, DIFF_QK), 0.1),
        'diff_subln': 1.0 + nrm((DEPTH, DIFF_V), 0.02),
        'ssm_a_re': -0.5 + nrm((DEPTH, 2, G, P), 0.01),
        'ssm_a_im': a_im_init + nrm((DEPTH, 2, G, P), 0.01),
        'ssm_log_dt': jax.random.uniform(next(ks), (DEPTH, 2, G), F32, math.log(1e-3), math.log(1e-1)),
        'ssm_b_re': nrm((DEPTH, 2, G, P, Hc), (2 * Hc) ** -0.5),
        'ssm_b_im': nrm((DEPTH, 2, G, P, Hc), (2 * Hc) ** -0.5),
        'ssm_c_re': nrm((DEPTH, 2, G, Hc, P), (2 * P) ** -0.5),
        'ssm_c_im': nrm((DEPTH, 2, G, Hc, P), (2 * P) ** -0.5),
        'ssm_d': nrm((DEPTH, SSM_WIDTH), 1.0),
        'ssm_w_glu': nrm((DEPTH, SSM_WIDTH, SSM_WIDTH), SSM_WIDTH ** -0.5),
        'ssm_b_glu': nrm((DEPTH, SSM_WIDTH), 0.02),
        'w_router': nrm((DEPTH, D_MODEL, N_EXPERTS), D_MODEL ** -0.5),
        'w_gate': nrm((DEPTH, N_EXPERTS, D_MODEL, EXPERT_FF), D_MODEL ** -0.5),
        'w_up': nrm((DEPTH, N_EXPERTS, D_MODEL, EXPERT_FF), D_MODEL ** -0.5),
        'w_down': nrm((DEPTH, N_EXPERTS, EXPERT_FF, D_MODEL), EXPERT_FF ** -0.5),
        'final_norm': 1.0 + nrm((D_MODEL,), 0.02),
    }


def reference(x_prompt, x_sample, cache_na_k, cache_na_v, cache_diff_k, cache_diff_v,
              state_ssm_re, state_ssm_im, c, c_ctx, w_ada, b_ada, norm_mix, norm_ffn,
              w_in, w_out, na_rpb, diff_lambda, diff_subln, ssm_a_re, ssm_a_im, ssm_log_dt,
              ssm_b_re, ssm_b_im, ssm_c_re, ssm_c_im, ssm_d, ssm_w_glu, ssm_b_glu,
              w_router, w_gate, w_up, w_down, final_norm):
    xp, xs = x_prompt, x_sample
    cond_ctx = jnp.broadcast_to(c_ctx[None, :], (xp.shape[0], D_MODEL))
    new_ka, new_va, new_kb, new_vb, new_sre, new_sim = [], [], [], [], [], []
    for l in range(DEPTH):
        p = {'w_in': w_in[l], 'w_out': w_out[l], 'rpb': na_rpb[l], 'lam': diff_lambda[l],
             'subln': diff_subln[l], 'a_re': ssm_a_re[l], 'a_im': ssm_a_im[l], 'log_dt': ssm_log_dt[l],
             'b_re': ssm_b_re[l], 'b_im': ssm_b_im[l], 'c_re': ssm_c_re[l], 'c_im': ssm_c_im[l],
             'ssm_d': ssm_d[l], 'w_glu': ssm_w_glu[l], 'b_glu': ssm_b_glu[l], 'w_router': w_router[l],
             'w_gate': w_gate[l], 'w_up': w_up[l], 'w_down': w_down[l]}
        lam_init = 0.8 - 0.6 * math.exp(-0.3 * l)
        mp = adaln(cond_ctx, w_ada[l], b_ada[l], xp.dtype)
        op, (ka, va, kb, vb, finals) = context_mixer(modulate(xp, norm_mix[l], mp[0], mp[1]), p, lam_init)
        xp = xp + mp[2] * op
        xp = xp + mp[5] * expert_choice_ffn(modulate(xp, norm_ffn[l], mp[3], mp[4]), p)
        new_ka.append(ka)
        new_va.append(va)
        new_kb.append(kb)
        new_vb.append(vb)
        new_sre.append(finals.real.astype(xp.dtype))
        new_sim.append(finals.imag.astype(xp.dtype))
        ms = adaln(c, w_ada[l], b_ada[l], xs.dtype)
        os_ = latent_mixer(modulate(xs, norm_mix[l], ms[0], ms[1]), p, lam_init,
                           cache_na_k[:, l], cache_na_v[:, l], cache_diff_k[:, l], cache_diff_v[:, l],
                           state_ssm_re[:, l], state_ssm_im[:, l])
        xs = xs + ms[2] * os_
        xs = xs + ms[5] * expert_choice_ffn(modulate(xs, norm_ffn[l], ms[3], ms[4]), p)
    y_prompt = rmsnorm(xp, final_norm)
    y_sample = rmsnorm(xs, final_norm)
    return (y_prompt, y_sample, jnp.stack(new_ka, axis=1), jnp.stack(new_va, axis=1),
            jnp.stack(new_kb, axis=1), jnp.stack(new_vb, axis=1),
            jnp.stack(new_sre, axis=1), jnp.stack(new_sim, axis=1))
```

```python
import functools
import math

import jax
import jax.numpy as jnp
from jax import lax
from jax.experimental import pallas as pl
from jax.experimental.pallas import tpu as pltpu

F32 = jnp.float32
BF16 = jnp.bfloat16
I32 = jnp.int32

D_MODEL = 1024
GRID_W = 64
HEAD_DIM = 64
NA_HEADS = 8
NA_WIDTH = NA_HEADS * HEAD_DIM
WIN_R = 8
WIN_C = 16
DIFF_HEADS = 4
DIFF_QK = 32
DIFF_V = 64
DIFF_WIDTH = DIFF_HEADS * DIFF_V
SSM_GROUPS = 16
SSM_GROUP_CH = 16
SSM_WIDTH = SSM_GROUPS * SSM_GROUP_CH
SSM_STATE = 64
SSM_N = SSM_GROUPS * SSM_STATE
IN_WIDTH = 3 * NA_WIDTH + 3 * DIFF_WIDTH + SSM_WIDTH
N_EXPERTS = 16
EXPERT_FF = 2048
EC_CAPACITY_FACTOR = 2
ROPE_BASE = 10000.0
EPS = 1e-6
NEG_INF = -1e30

LANES = 128
SUBLANES = 8
MXU_DIM = 256
VMEM_LIMIT_CAP = 60000 * 1024

ROW_TILE = 256
NA_QROWS = 4
NA_KROWS = 12
SCAN_TILE = 256
FF_TILE = 512
COMBINE_TILE = 512


def _params(vmem_bytes, ndims):
    return pltpu.CompilerParams(
        dimension_semantics=("arbitrary",) * ndims,
        vmem_limit_bytes=int(min(VMEM_LIMIT_CAP, vmem_bytes)),
    )


def _nbytes(shape, dtype):
    return math.prod(shape) * jnp.dtype(dtype).itemsize


def _dot(a, b):
    return jnp.dot(a, b, preferred_element_type=F32)


def _dot_nt(a, b):
    return lax.dot_general(a, b, (((1,), (1,)), ((), ())), preferred_element_type=F32)


def _dot_tn(a, b):
    return lax.dot_general(a, b, (((0,), (0,)), ((), ())), preferred_element_type=F32)


def _cast_rows(dst_ref, src_ref, rows):
    n = src_ref.shape[0]
    for r in range(0, n, rows):
        dst_ref[r:r + rows, :] = src_ref[r:r + rows, :].astype(BF16)


def _modulated_norm(x, g, shift, scale):
    ms = jnp.mean(x * x, axis=-1, keepdims=True)
    return (x * lax.rsqrt(ms + EPS) * g) * (1.0 + scale) + shift


def _ada_kernel(c_ref, w_ref, b_ref, o_ref):
    c = c_ref[...]
    s = c * jax.nn.sigmoid(c)
    o_ref[0] = _dot(s.astype(BF16), w_ref[0].astype(BF16)) + b_ref[0]


def _ada(cond, w_ada, b_ada):
    depth = w_ada.shape[0]
    tn = 1536
    nt = 6 * D_MODEL // tn
    vm = 2 * _nbytes((D_MODEL, tn), F32) + _nbytes((D_MODEL, tn), BF16) + (4 << 20)
    return pl.pallas_call(
        _ada_kernel,
        out_shape=jax.ShapeDtypeStruct((depth, SUBLANES, 6 * D_MODEL), F32),
        grid=(depth, nt),
        in_specs=[
            pl.BlockSpec((SUBLANES, D_MODEL), lambda l, j: (0, 0)),
            pl.BlockSpec((1, D_MODEL, tn), lambda l, j: (l, 0, j)),
            pl.BlockSpec((1, 1, tn), lambda l, j: (l, 0, j)),
        ],
        out_specs=pl.BlockSpec((1, SUBLANES, tn), lambda l, j: (l, 0, j)),
        compiler_params=_params(vm, 2),
        name="ada",
    )(cond, w_ada, b_ada.reshape(depth, 1, 6 * D_MODEL))


def _rope_apply(x, cos, sin_signed):
    lane = lax.broadcasted_iota(I32, (1, LANES), 1)
    first = (lane % 16) < 8
    outs = []
    for t in range(x.shape[1] // LANES):
        xt = x[:, t * LANES:(t + 1) * LANES]
        partner = jnp.where(first, pltpu.roll(xt, LANES - 8, axis=1), pltpu.roll(xt, 8, axis=1))
        outs.append(xt * cos[:, t * LANES:(t + 1) * LANES] + partner * sin_signed[:, t * LANES:(t + 1) * LANES])
    return jnp.concatenate(outs, axis=1)


def _proj_in_kernel(*refs, rope):
    if rope:
        x_ref, mod_ref, g_ref, w_ref, cos_ref, sin_ref = refs[:6]
        outs = refs[6:13]
        wbf_ref = refs[13]
    else:
        x_ref, mod_ref, g_ref, w_ref = refs[:4]
        outs = refs[4:11]
        wbf_ref = refs[11]
    qa_ref, ka_ref, va_ref, qb_ref, kb_ref, vb_ref, u_ref = outs

    @pl.when((pl.program_id(0) == 0) & (pl.program_id(1) == 0))
    def _():
        _cast_rows(wbf_ref, w_ref, 128)

    h = _modulated_norm(x_ref[0], g_ref[...], mod_ref[0, 0:1, :], mod_ref[0, 1:2, :])
    z = _dot(h.astype(BF16), wbf_ref[...])
    i1, i2, i3 = NA_WIDTH, 2 * NA_WIDTH, 3 * NA_WIDTH
    i4, i5, i6 = i3 + DIFF_WIDTH, i3 + 2 * DIFF_WIDTH, i3 + 3 * DIFF_WIDTH
    qa_ref[0] = z[:, :i1]
    ka_ref[0] = z[:, i1:i2]
    va_ref[0] = z[:, i2:i3]
    qb = z[:, i3:i4]
    kb = z[:, i4:i5]
    if rope:
        qb = _rope_apply(qb, cos_ref[...], sin_ref[...])
        kb = _rope_apply(kb, cos_ref[...], sin_ref[...])
    qb_ref[0] = qb
    kb_ref[0] = kb
    vb_ref[0] = z[:, i5:i6]
    u_ref[0] = z[:, i6:]


def _proj_in(x, mod, g_norm, w_in, layer, rope_tabs):
    b, n, d = x.shape
    tm = ROW_TILE
    bm = mod.shape[0]
    rope = rope_tabs is not None
    mod_map = (lambda i, j: (i, 0, 0)) if bm > 1 else (lambda i, j: (0, 0, 0))
    in_specs = [
        pl.BlockSpec((1, tm, d), lambda i, j: (i, j, 0)),
        pl.BlockSpec((1, 6, d), mod_map),
        pl.BlockSpec((None, 1, d), lambda i, j: (layer, 0, 0)),
        pl.BlockSpec((None, d, IN_WIDTH), lambda i, j: (layer, 0, 0), pipeline_mode=pl.Buffered(1)),
    ]
    args = [x, mod, g_norm, w_in]
    if rope:
        in_specs += [pl.BlockSpec((tm, DIFF_WIDTH), lambda i, j: (j, 0))] * 2
        args += list(rope_tabs)
    widths = (NA_WIDTH, NA_WIDTH, NA_WIDTH, DIFF_WIDTH, DIFF_WIDTH, DIFF_WIDTH, SSM_WIDTH)
    vm = (_nbytes((d, IN_WIDTH), F32) + _nbytes((d, IN_WIDTH), BF16)
          + 4 * _nbytes((tm, d), F32) + 6 * _nbytes((tm, IN_WIDTH), F32) + (4 << 20))
    return pl.pallas_call(
        functools.partial(_proj_in_kernel, rope=rope),
        out_shape=[jax.ShapeDtypeStruct((b, n, w), F32) for w in widths],
        grid=(b, n // tm),
        in_specs=in_specs,
        out_specs=[pl.BlockSpec((1, tm, w), lambda i, j: (i, j, 0)) for w in widths],
        scratch_shapes=[pltpu.VMEM((d, IN_WIDTH), BF16)],
        compiler_params=_params(vm, 2),
        name="proj_in",
    )(*args)


def _softmax_av(qm, ks, vs, scale, biases):
    ss = []
    for k, bias in zip(ks, biases):
        s = _dot_nt(qm, k) * scale
        if bias is not None:
            s = bias(s)
        ss.append(s)
    m = ss[0].max(axis=-1, keepdims=True)
    for s in ss[1:]:
        m = jnp.maximum(m, s.max(axis=-1, keepdims=True))
    acc = None
    den = None
    for s, v in zip(ss, vs):
        p = jnp.exp(s - m)
        den = p.sum(axis=-1, keepdims=True) if den is None else den + p.sum(axis=-1, keepdims=True)
        pv = _dot(p.astype(BF16), v)
        acc = pv if acc is None else acc + pv
    return acc / den


def _diff_lambda(lam_ref, lam_init):
    lp = lam_ref[...]
    a = jnp.sum(lp[0:1] * lp[1:2], axis=-1, keepdims=True)
    b = jnp.sum(lp[2:3] * lp[3:4], axis=-1, keepdims=True)
    return jnp.exp(a) - jnp.exp(b) + lam_init


def _attn_kernel(*refs, has_ctx, diff, lam_init):
    refs = list(refs)
    q_ref, k_ref, v_ref = refs[:3]
    pos = 3
    if has_ctx:
        kc_ref, vc_ref = refs[pos:pos + 2]
        pos += 2
    if diff:
        lam_ref, subln_ref = refs[pos:pos + 2]
        pos += 2
    o_ref = refs[pos]
    width = q_ref.shape[2]
    lane = lax.broadcasted_iota(I32, (1, LANES), 1)
    if diff:
        lam = _diff_lambda(lam_ref, lam_init)
        scale = DIFF_QK ** -0.5
    else:
        scale = HEAD_DIM ** -0.5
    for p in range(width // LANES):
        sl = slice(p * LANES, (p + 1) * LANES)
        qp = q_ref[0, :, sl]
        ks = [k_ref[0, :, sl].astype(BF16)]
        vs = [v_ref[0, :, sl].astype(BF16)]
        if has_ctx:
            ks.append(kc_ref[:, sl].astype(BF16))
            vs.append(vc_ref[:, sl].astype(BF16))
        nob = [None] * len(ks)
        pair = None
        for sub in range(LANES // HEAD_DIM):
            head_mask = (lane // HEAD_DIM) == sub
            if diff:
                os_ = []
                for half in range(2):
                    qmask = (lane // DIFF_QK) == (2 * sub + half)
                    qm = jnp.where(qmask, qp, 0.0).astype(BF16)
                    os_.append(_softmax_av(qm, ks, vs, scale, nob))
                o = jnp.where(head_mask, os_[0] - lam * os_[1], 0.0)
                ms = jnp.sum(o * o, axis=-1, keepdims=True) * (1.0 / DIFF_V)
                o = (o * lax.rsqrt(ms + EPS) * subln_ref[...]) * (1.0 - lam_init)
            else:
                qm = jnp.where(head_mask, qp, 0.0).astype(BF16)
                o = jnp.where(head_mask, _softmax_av(qm, ks, vs, scale, nob), 0.0)
            pair = o if pair is None else pair + o
        o_ref[0, :, sl] = pair


def _attention(q, k, v, ctx_kv, layer, diff_params, lam_init):
    b, nq, w = q.shape
    nk = k.shape[1]
    tq = ROW_TILE
    has_ctx = ctx_kv is not None
    diff = diff_params is not None
    in_specs = [
        pl.BlockSpec((1, tq, w), lambda i, j: (i, j, 0)),
        pl.BlockSpec((1, nk, w), lambda i, j: (i, 0, 0)),
        pl.BlockSpec((1, nk, w), lambda i, j: (i, 0, 0)),
    ]
    args = [q, k, v]
    nc = 0
    if has_ctx:
        nc = ctx_kv[0].shape[2]
        in_specs += [pl.BlockSpec((None, None, nc, w), lambda i, j: (i, layer, 0, 0))] * 2
        args += list(ctx_kv)
    if diff:
        lam_p, subln = diff_params
        in_specs += [
            pl.BlockSpec((None, 4, DIFF_QK), lambda i, j: (layer, 0, 0)),
            pl.BlockSpec((None, 1, LANES), lambda i, j: (layer, 0, 0)),
        ]
        args += [lam_p, subln]
    vm = (4 * _nbytes((nk + nc, w), F32) + 2 * _nbytes((nk + nc, LANES), BF16)
          + 8 * _nbytes((tq, nk + nc), F32) + 8 * _nbytes((tq, w), F32) + (4 << 20))
    return pl.pallas_call(
        functools.partial(_attn_kernel, has_ctx=has_ctx, diff=diff, lam_init=lam_init),
        out_shape=jax.ShapeDtypeStruct((b, nq, w), F32),
        grid=(b, nq // tq),
        in_specs=in_specs,
        out_specs=pl.BlockSpec((1, tq, w), lambda i, j: (i, j, 0)),
        compiler_params=_params(vm, 2),
        name="attn_diff" if diff else "attn",
    )(*args)


def _na_kernel(q_ref, k_ref, v_ref, kc_ref, vc_ref, bias_ref, o_ref, *, rows):
    i = pl.program_id(1)
    tq = NA_QROWS * GRID_W
    nkw = NA_KROWS * GRID_W
    ustart = jnp.clip(NA_QROWS * i - WIN_R // 2, 0, rows - NA_KROWS)
    koff = pl.multiple_of(ustart * GRID_W, GRID_W)
    delta = ustart - NA_QROWS * i
    qrow = NA_QROWS * i + lax.broadcasted_iota(I32, (tq, nkw), 0) // GRID_W
    krow = ustart + lax.broadcasted_iota(I32, (tq, nkw), 1) // GRID_W
    wstart = jnp.clip(qrow - WIN_R // 2, 0, rows - WIN_R)
    row_ok = (krow >= wstart) & (krow < wstart + WIN_R)
    lane = lax.broadcasted_iota(I32, (1, LANES), 1)
    scale = HEAD_DIM ** -0.5
    for p in range(NA_WIDTH // LANES):
        sl = slice(p * LANES, (p + 1) * LANES)
        qp = q_ref[0, :, sl]
        ks = [k_ref[0, pl.ds(koff, nkw), sl].astype(BF16), kc_ref[:, sl].astype(BF16)]
        vs = [v_ref[0, pl.ds(koff, nkw), sl].astype(BF16), vc_ref[:, sl].astype(BF16)]
        pair = None
        for sub in range(LANES // HEAD_DIM):
            h = p * (LANES // HEAD_DIM) + sub
            head_mask = (lane // HEAD_DIM) == sub

            def win_bias(s, h=h):
                rows_ = []
                for ri in range(NA_QROWS):
                    tiles = []
                    for pj in range(NA_KROWS // 2):
                        dr = delta + 2 * pj - ri
                        idx = jnp.clip(dr, -WIN_R, WIN_R - 1) + WIN_R
                        tiles.append(bias_ref[h, idx])
                    rows_.append(jnp.concatenate(tiles, axis=1))
                bias = jnp.concatenate(rows_, axis=0)
                return jnp.where(row_ok, s + bias, NEG_INF)

            qm = jnp.where(head_mask, qp, 0.0).astype(BF16)
            o = _softmax_av(qm, ks, vs, scale, [win_bias, None])
            o = jnp.where(head_mask, o, 0.0)
            pair = o if pair is None else pair + o
        o_ref[0, :, sl] = pair


def _na_bias_table(rpb):
    cc = jnp.arange(GRID_W)
    col_start = jnp.clip(cc - WIN_C // 2, 0, GRID_W - WIN_C)
    col_ok = (cc[None, :] >= col_start[:, None]) & (cc[None, :] < col_start[:, None] + WIN_C)
    coff = jnp.clip(cc[None, :] - cc[:, None] + WIN_C - 1, 0, 2 * WIN_C - 2)
    t = rpb.astype(F32)[:, :, coff]
    t = jnp.where(col_ok[None, None], t, NEG_INF)
    zero = jnp.zeros_like(t[:, :1])
    t = jnp.concatenate([zero, t, zero], axis=1)
    return jnp.concatenate([t[:, :-1], t[:, 1:]], axis=-1)


def _neighbourhood_attention(q, k, v, kc, vc, bias_tab, layer):
    b, n, w = q.shape
    rows = n // GRID_W
    tq = NA_QROWS * GRID_W
    nc = kc.shape[2]
    nkw = NA_KROWS * GRID_W
    vm = (4 * _nbytes((n, w), F32) + 4 * _nbytes((nc, w), F32) + 2 * _nbytes(bias_tab.shape, F32)
          + 12 * _nbytes((tq, nkw + nc), F32) + (6 << 20))
    return pl.pallas_call(
        functools.partial(_na_kernel, rows=rows),
        out_shape=jax.ShapeDtypeStruct((b, n, w), F32),
        grid=(b, n // tq),
        in_specs=[
            pl.BlockSpec((1, tq, w), lambda i, j: (i, j, 0)),
            pl.BlockSpec((1, n, w), lambda i, j: (i, 0, 0)),
            pl.BlockSpec((1, n, w), lambda i, j: (i, 0, 0)),
            pl.BlockSpec((None, None, nc, w), lambda i, j: (i, layer, 0, 0)),
            pl.BlockSpec((None, None, nc, w), lambda i, j: (i, layer, 0, 0)),
            pl.BlockSpec(bias_tab.shape, lambda i, j: (0, 0, 0, 0)),
        ],
        out_specs=pl.BlockSpec((1, tq, w), lambda i, j: (i, j, 0)),
        compiler_params=_params(vm, 2),
        name="na_attn",
    )(q, k, v, kc, vc, bias_tab)


def _s5_kernel(u_ref, h0re_ref, h0im_ref, bmat_ref, cmat_ref, apow_ref, ptab_ref, d_ref, wglu_ref, bglu_ref,
               o_ref, fre_ref, fim_ref, bu_ref, hs_ref, y_ref):
    seq = u_ref.shape[1]
    tt = SCAN_TILE
    ntile = seq // tt
    ngrp = tt // SUBLANES
    nlt = SSM_N // LANES
    row = lax.broadcasted_iota(I32, (SUBLANES, LANES), 0)

    y_ref[...] = u_ref[0] * d_ref[...]

    for d in range(2):
        reverse = d == 1

        def group(g, carry, d=d, reverse=reverse):
            gg = (ngrp - 1 - g) if reverse else g
            r0 = pl.multiple_of(gg * SUBLANES, SUBLANES)
            new = []
            for lt in range(nlt):
                lre = slice(lt * LANES, (lt + 1) * LANES)
                lim = slice(SSM_N + lt * LANES, SSM_N + (lt + 1) * LANES)
                xr = bu_ref[pl.ds(r0, SUBLANES), lre]
                xi = bu_ref[pl.ds(r0, SUBLANES), lim]
                for k, s in enumerate((1, 2, 4)):
                    ar = apow_ref[d * 6 + 2 * k:d * 6 + 2 * k + 1, lre]
                    ai = apow_ref[d * 6 + 2 * k + 1:d * 6 + 2 * k + 2, lre]
                    if reverse:
                        keep = row < SUBLANES - s
                        sr = jnp.where(keep, pltpu.roll(xr, SUBLANES - s, axis=0), 0.0)
                        si = jnp.where(keep, pltpu.roll(xi, SUBLANES - s, axis=0), 0.0)
                    else:
                        keep = row >= s
                        sr = jnp.where(keep, pltpu.roll(xr, s, axis=0), 0.0)
                        si = jnp.where(keep, pltpu.roll(xi, s, axis=0), 0.0)
                    xr, xi = xr + (ar * sr - ai * si), xi + (ar * si + ai * sr)
                cr, ci = carry[2 * lt], carry[2 * lt + 1]
                pr = ptab_ref[2 * d, :, lre]
                pi = ptab_ref[2 * d + 1, :, lre]
                hr = xr + (pr * cr - pi * ci)
                hi = xi + (pr * ci + pi * cr)
                hs_ref[pl.ds(r0, SUBLANES), lre] = hr
                hs_ref[pl.ds(r0, SUBLANES), lim] = hi
                edge = 0 if reverse else SUBLANES - 1
                new.append(jnp.broadcast_to(hr[edge:edge + 1, :], (SUBLANES, LANES)))
                new.append(jnp.broadcast_to(hi[edge:edge + 1, :], (SUBLANES, LANES)))
            return tuple(new)

        def tile(t, carry, d=d, reverse=reverse):
            tix = (ntile - 1 - t) if reverse else t
            t0 = pl.multiple_of(tix * tt, tt)
            ub = u_ref[0, pl.ds(t0, tt), :].astype(BF16)
            bu_ref[...] = _dot(ub, bmat_ref[:, d * 2 * SSM_N:(d + 1) * 2 * SSM_N])
            carry = lax.fori_loop(0, ngrp, group, carry)
            y_ref[pl.ds(t0, tt), :] += _dot(hs_ref[...].astype(BF16), cmat_ref[d * 2 * SSM_N:(d + 1) * 2 * SSM_N, :])
            return carry

        init = []
        for lt in range(nlt):
            sl = slice(lt * LANES, (lt + 1) * LANES)
            init.append(jnp.broadcast_to(h0re_ref[0, d:d + 1, sl], (SUBLANES, LANES)))
            init.append(jnp.broadcast_to(h0im_ref[0, d:d + 1, sl], (SUBLANES, LANES)))
        fin = lax.fori_loop(0, ntile, tile, tuple(init))
        for lt in range(nlt):
            sl = slice(lt * LANES, (lt + 1) * LANES)
            fre_ref[0, d:d + 1, sl] = fin[2 * lt][0:1, :]
            fim_ref[0, d:d + 1, sl] = fin[2 * lt + 1][0:1, :]

    wglu = wglu_ref[...].astype(BF16)

    def glu(t, _):
        t0 = pl.multiple_of(t * tt, tt)
        g = jax.nn.gelu(y_ref[pl.ds(t0, tt), :])
        z = _dot(g.astype(BF16), wglu) + bglu_ref[...]
        o_ref[0, pl.ds(t0, tt), :] = g * jax.nn.sigmoid(z)
        return 0

    lax.fori_loop(0, ntile, glu, 0)


def _s5_tables(a_re, a_im, log_dt, b_re, b_im, c_re, c_im):
    g, p, hc = SSM_GROUPS, SSM_STATE, SSM_GROUP_CH
    eye = jnp.eye(g, dtype=F32)
    bcols, crows, apow, ptab = [], [], [], []
    for d in range(2):
        lam = lax.complex(a_re[d].astype(F32), a_im[d].astype(F32))
        dt = jnp.exp(log_dt[d].astype(F32))[:, None]
        a_bar = jnp.exp(lam * dt)
        b_bar = ((a_bar - 1.0) / lam)[..., None] * lax.complex(b_re[d].astype(F32), b_im[d].astype(F32))
        c_mat = lax.complex(c_re[d].astype(F32), c_im[d].astype(F32))
        for part in (jnp.real(b_bar), jnp.imag(b_bar)):
            bcols.append(jnp.einsum('gph,gk->ghkp', part, eye).reshape(g * hc, g * p))
        for part in (jnp.real(c_mat), -jnp.imag(c_mat)):
            crows.append(jnp.einsum('ghp,gk->gpkh', part, eye).reshape(g * p, g * hc))
        a1 = a_bar.reshape(g * p)
        pw = [a1]
        for _ in range(SUBLANES - 1):
            pw.append(pw[-1] * a1)
        for a in (pw[0], pw[1], pw[3]):
            apow += [jnp.real(a), jnp.imag(a)]
        order = pw[::-1] if d == 1 else pw
        ptab.append(jnp.stack([jnp.real(a) for a in order]))
        ptab.append(jnp.stack([jnp.imag(a) for a in order]))
    return (jnp.concatenate(bcols, axis=1).astype(BF16), jnp.concatenate(crows, axis=0).astype(BF16),
            jnp.stack(apow), jnp.stack(ptab))


def _s5(u, h0re, h0im, tabs, d_skip, w_glu, b_glu, layer):
    b, seq, w = u.shape
    bmat, cmat, apow, ptab = tabs
    vm = (6 * _nbytes((seq, w), F32) + 2 * (bmat.size + cmat.size) * 2 + 4 * _nbytes((SCAN_TILE, 2 * SSM_N), F32)
          + (8 << 20))
    full = lambda a: pl.BlockSpec(a.shape, lambda i: (0,) * a.ndim)
    return pl.pallas_call(
        _s5_kernel,
        out_shape=[jax.ShapeDtypeStruct((b, seq, w), F32),
                   jax.ShapeDtypeStruct((b, 2, SSM_N), F32),
                   jax.ShapeDtypeStruct((b, 2, SSM_N), F32)],
        grid=(b,),
        in_specs=[
            pl.BlockSpec((1, seq, w), lambda i: (i, 0, 0)),
            pl.BlockSpec((1, 2, SSM_N), lambda i: (i, 0, 0)),
            pl.BlockSpec((1, 2, SSM_N), lambda i: (i, 0, 0)),
            full(bmat), full(cmat), full(apow), full(ptab),
            pl.BlockSpec((None, 1, w), lambda i: (layer, 0, 0)),
            pl.BlockSpec((None, w, w), lambda i: (layer, 0, 0)),
            pl.BlockSpec((None, 1, w), lambda i: (layer, 0, 0)),
        ],
        out_specs=[pl.BlockSpec((1, seq, w), lambda i: (i, 0, 0)),
                   pl.BlockSpec((1, 2, SSM_N), lambda i: (i, 0, 0)),
                   pl.BlockSpec((1, 2, SSM_N), lambda i: (i, 0, 0))],
        scratch_shapes=[pltpu.VMEM((SCAN_TILE, 2 * SSM_N), F32),
                        pltpu.VMEM((SCAN_TILE, 2 * SSM_N), F32),
                        pltpu.VMEM((seq, w), F32)],
        compiler_params=_params(vm, 1),
        name="s5",
    )(u, h0re, h0im, bmat, cmat, apow, ptab, d_skip, w_glu, b_glu)


def _proj_out_kernel(oa_ref, ob_ref, oc_ref, w_ref, x_ref, mod_ref, o_ref, wbf_ref):
    @pl.when((pl.program_id(0) == 0) & (pl.program_id(1) == 0))
    def _():
        _cast_rows(wbf_ref, w_ref, 128)

    i1, i2 = NA_WIDTH, NA_WIDTH + DIFF_WIDTH
    y = _dot(oa_ref[0].astype(BF16), wbf_ref[:i1, :])
    y += _dot(ob_ref[0].astype(BF16), wbf_ref[i1:i2, :])
    y += _dot(oc_ref[0].astype(BF16), wbf_ref[i2:, :])
    o_ref[0] = x_ref[0] + mod_ref[0, 2:3, :] * y


def _proj_out(oa, ob, oc, w_out, x, mod, layer):
    b, n, d = x.shape
    tm = ROW_TILE
    bm = mod.shape[0]
    mod_map = (lambda i, j: (i, 0, 0)) if bm > 1 else (lambda i, j: (0, 0, 0))
    kw = w_out.shape[1]
    vm = _nbytes((kw, d), F32) + _nbytes((kw, d), BF16) + 10 * _nbytes((tm, d), F32) + (4 << 20)
    return pl.pallas_call(
        _proj_out_kernel,
        out_shape=jax.ShapeDtypeStruct((b, n, d), F32),
        grid=(b, n // tm),
        in_specs=[
            pl.BlockSpec((1, tm, NA_WIDTH), lambda i, j: (i, j, 0)),
            pl.BlockSpec((1, tm, DIFF_WIDTH), lambda i, j: (i, j, 0)),
            pl.BlockSpec((1, tm, SSM_WIDTH), lambda i, j: (i, j, 0)),
            pl.BlockSpec((None, kw, d), lambda i, j: (layer, 0, 0), pipeline_mode=pl.Buffered(1)),
            pl.BlockSpec((1, tm, d), lambda i, j: (i, j, 0)),
            pl.BlockSpec((1, 6, d), mod_map),
        ],
        out_specs=pl.BlockSpec((1, tm, d), lambda i, j: (i, j, 0)),
        scratch_shapes=[pltpu.VMEM((kw, d), BF16)],
        compiler_params=_params(vm, 2),
        name="proj_out",
    )(oa, ob, oc, w_out, x, mod)


def _excl_cumsum_lanes(m):
    e, n = m.shape
    blk = MXU_DIM
    nb = n // blk
    r = lax.broadcasted_iota(I32, (blk, blk), 0)
    c = lax.broadcasted_iota(I32, (blk, blk), 1)
    tri = jnp.where(r < c, 1.0, 0.0).astype(BF16)
    stacked = jnp.concatenate([m[:, k * blk:(k + 1) * blk] for k in range(nb)], axis=0).astype(BF16)
    within = _dot(stacked, tri)
    outs = []
    off = jnp.zeros((e, 1), F32)
    for k in range(nb):
        outs.append(within[k * e:(k + 1) * e, :] + off)
        off = off + jnp.sum(m[:, k * blk:(k + 1) * blk], axis=1, keepdims=True)
    return jnp.concatenate(outs, axis=1)


def _route_kernel(x_ref, mod_ref, g_ref, wr_ref, h_ref, pos_ref, gate_ref, *, cap):
    n = x_ref.shape[1]
    parts = []
    for r in range(0, n, ROW_TILE):
        h = _modulated_norm(x_ref[0, r:r + ROW_TILE, :], g_ref[...], mod_ref[0, 3:4, :], mod_ref[0, 4:5, :])
        h_ref[0, r:r + ROW_TILE, :] = h.astype(BF16)
        parts.append(lax.dot_general(wr_ref[...], h, (((1,), (1,)), ((), ())),
                                     precision=lax.Precision.HIGHEST, preferred_element_type=F32))
    logits = jnp.concatenate(parts, axis=1)
    mx = logits.max(axis=0, keepdims=True)
    ex = jnp.exp(logits - mx)
    aff = ex / ex.sum(axis=0, keepdims=True)
    bits = pltpu.bitcast(aff, I32)

    def search(it, cur):
        cand = cur | (jnp.int32(1) << (jnp.int32(30) - it))
        cnt = jnp.sum(jnp.where(bits >= cand, 1.0, 0.0), axis=1, keepdims=True)
        return jnp.where(cnt >= cap, cand, cur)

    thr = lax.fori_loop(0, 31, search, jnp.zeros((N_EXPERTS, 1), I32))
    gt = bits > thr
    eq = bits == thr
    need = cap - jnp.sum(jnp.where(gt, 1.0, 0.0), axis=1, keepdims=True)
    eq_rank = _excl_cumsum_lanes(jnp.where(eq, 1.0, 0.0))
    sel = gt | (eq & (eq_rank < need))
    self_ = jnp.where(sel, 1.0, 0.0)
    slot = _excl_cumsum_lanes(self_)
    pos_ref[:, 0, 0, :] = jnp.where(sel, slot, -1.0).astype(I32)
    gate_ref[:, 0, 0, :] = jnp.where(sel, aff, 0.0)


def _route(x, mod, g_norm, wr_t, layer):
    b, n, d = x.shape
    cap = EC_CAPACITY_FACTOR * n // N_EXPERTS
    bm = mod.shape[0]
    mod_map = (lambda i: (i, 0, 0)) if bm > 1 else (lambda i: (0, 0, 0))
    vm = 10 * _nbytes((n, d), F32) + (8 << 20)
    return pl.pallas_call(
        functools.partial(_route_kernel, cap=cap),
        out_shape=[jax.ShapeDtypeStruct((b, n, d), BF16),
                   jax.ShapeDtypeStruct((N_EXPERTS, b, 1, n), I32),
                   jax.ShapeDtypeStruct((N_EXPERTS, b, 1, n), F32)],
        grid=(b,),
        in_specs=[
            pl.BlockSpec((1, n, d), lambda i: (i, 0, 0)),
            pl.BlockSpec((1, 6, d), mod_map),
            pl.BlockSpec((None, 1, d), lambda i: (layer, 0, 0)),
            pl.BlockSpec((None, N_EXPERTS, d), lambda i: (layer, 0, 0)),
        ],
        out_specs=[pl.BlockSpec((1, n, d), lambda i: (i, 0, 0)),
                   pl.BlockSpec((N_EXPERTS, 1, 1, n), lambda i: (0, i, 0, 0)),
                   pl.BlockSpec((N_EXPERTS, 1, 1, n), lambda i: (0, i, 0, 0))],
        compiler_params=_params(vm, 1),
        name="route",
    )(x, mod, g_norm, wr_t)


def _one_hot_rows(pos_row, cap):
    n = pos_row.shape[1]
    slot = lax.broadcasted_iota(I32, (cap, n), 0)
    return slot == pos_row


def _gather_kernel(h_ref, pos_ref, gate_ref, xs_ref, gs_ref, *, cap):
    h = h_ref[0]
    for e in range(N_EXPERTS):
        oh = _one_hot_rows(pos_ref[e, 0], cap)
        xs_ref[e] = _dot(jnp.where(oh, 1.0, 0.0).astype(BF16), h).astype(BF16)
        g = jnp.sum(jnp.where(oh, gate_ref[e, 0], 0.0), axis=1, keepdims=True)
        gs_ref[e] = jnp.broadcast_to(g, (cap, LANES))


def _gather(h, pos, gate):
    b, n, d = h.shape
    cap = EC_CAPACITY_FACTOR * n // N_EXPERTS
    vm = 4 * _nbytes((n, d), BF16) + 6 * _nbytes((N_EXPERTS, cap, d), BF16) + 8 * _nbytes((cap, n), F32) + (8 << 20)
    return pl.pallas_call(
        functools.partial(_gather_kernel, cap=cap),
        out_shape=[jax.ShapeDtypeStruct((N_EXPERTS, b * cap, d), BF16),
                   jax.ShapeDtypeStruct((N_EXPERTS, b * cap, LANES), F32)],
        grid=(b,),
        in_specs=[
            pl.BlockSpec((1, n, d), lambda i: (i, 0, 0)),
            pl.BlockSpec((N_EXPERTS, 1, 1, n), lambda i: (0, i, 0, 0)),
            pl.BlockSpec((N_EXPERTS, 1, 1, n), lambda i: (0, i, 0, 0)),
        ],
        out_specs=[pl.BlockSpec((N_EXPERTS, cap, d), lambda i: (0, i, 0)),
                   pl.BlockSpec((N_EXPERTS, cap, LANES), lambda i: (0, i, 0))],
        compiler_params=_params(vm, 1),
        name="moe_gather",
    )(h, pos, gate)


def _ffn_kernel(xs_ref, gs_ref, wg_ref, wu_ref, wd_ref, ye_ref, acc_ref):
    j = pl.program_id(1)

    @pl.when(j == 0)
    def _():
        acc_ref[...] = jnp.zeros_like(acc_ref)

    xs = xs_ref[0]
    a = _dot(xs, wg_ref[...].astype(BF16))
    u = _dot(xs, wu_ref[...].astype(BF16))
    hid = (a * jax.nn.sigmoid(a)) * u
    acc_ref[...] += _dot(hid.astype(BF16), wd_ref[...].astype(BF16))

    @pl.when(j == pl.num_programs(1) - 1)
    def _():
        ye_ref[0] = (acc_ref[...] * gs_ref[0][:, 0:1]).astype(BF16)


def _ffn(xs, gs, w_gate, w_up, w_down, layer):
    e, s, d = xs.shape
    ff = w_gate.shape[-1]
    tf = FF_TILE
    vm = (6 * _nbytes((d, tf), F32) + 3 * _nbytes((d, tf), BF16) + 4 * _nbytes((s, d), BF16)
          + 2 * _nbytes((s, LANES), F32) + _nbytes((s, d), F32) + 4 * _nbytes((s, tf), F32) + (4 << 20))
    return pl.pallas_call(
        _ffn_kernel,
        out_shape=jax.ShapeDtypeStruct((e, s, d), BF16),
        grid=(e, ff // tf),
        in_specs=[
            pl.BlockSpec((1, s, d), lambda i, j: (i, 0, 0)),
            pl.BlockSpec((1, s, LANES), lambda i, j: (i, 0, 0)),
            pl.BlockSpec((None, None, d, tf), lambda i, j: (layer, i, 0, j)),
            pl.BlockSpec((None, None, d, tf), lambda i, j: (layer, i, 0, j)),
            pl.BlockSpec((None, None, tf, d), lambda i, j: (layer, i, j, 0)),
        ],
        out_specs=pl.BlockSpec((1, s, d), lambda i, j: (i, 0, 0)),
        scratch_shapes=[pltpu.VMEM((s, d), F32)],
        compiler_params=_params(vm, 2),
        name="moe_ffn",
    )(xs, gs, w_gate, w_up, w_down)


def _combine_kernel(ye_ref, pos_ref, x_ref, mod_ref, *rest, cap, final):
    if final:
        g_ref, o_ref = rest
    else:
        (o_ref,) = rest
    group = max(1, MXU_DIM // cap)
    y = None
    for e0 in range(0, N_EXPERTS, group):
        oh = jnp.concatenate([jnp.where(_one_hot_rows(pos_ref[e, 0], cap), 1.0, 0.0).astype(BF16)
                              for e in range(e0, e0 + group)], axis=0)
        ye = jnp.concatenate([ye_ref[e] for e in range(e0, e0 + group)], axis=0)
        part = _dot_tn(oh, ye)
        y = part if y is None else y + part
    xn = x_ref[0] + mod_ref[0, 5:6, :] * y
    if final:
        ms = jnp.mean(xn * xn, axis=-1, keepdims=True)
        xn = xn * lax.rsqrt(ms + EPS) * g_ref[...]
    o_ref[0] = xn


def _combine(ye, pos, x, mod, final_norm, slot0):
    b, n, d = x.shape
    cap = EC_CAPACITY_FACTOR * n // N_EXPERTS
    tn = min(n, COMBINE_TILE)
    bm = mod.shape[0]
    mod_map = (lambda i, j: (i, 0, 0)) if bm > 1 else (lambda i, j: (0, 0, 0))
    final = final_norm is not None
    assert slot0 % cap == 0
    blk0 = slot0 // cap
    in_specs = [
        pl.BlockSpec((N_EXPERTS, cap, d), lambda i, j: (0, blk0 + i, 0)),
        pl.BlockSpec((N_EXPERTS, 1, 1, tn), lambda i, j: (0, i, 0, j)),
        pl.BlockSpec((1, tn, d), lambda i, j: (i, j, 0)),
        pl.BlockSpec((1, 6, d), mod_map),
    ]
    args = [ye, pos, x, mod]
    if final:
        in_specs.append(pl.BlockSpec((1, d), lambda i, j: (0, 0)))
        args.append(final_norm)
    vm = (4 * _nbytes((N_EXPERTS, cap, d), BF16) + 10 * _nbytes((tn, d), F32)
          + 6 * _nbytes((max(cap, MXU_DIM), tn), F32) + (8 << 20))
    return pl.pallas_call(
        functools.partial(_combine_kernel, cap=cap, final=final),
        out_shape=jax.ShapeDtypeStruct((b, n, d), F32),
        grid=(b, n // tn),
        in_specs=in_specs,
        out_specs=pl.BlockSpec((1, tn, d), lambda i, j: (i, j, 0)),
        compiler_params=_params(vm, 2),
        name="moe_combine",
    )(*args)


def _rope_tables(n):
    t = jnp.arange(n)
    row = (t // GRID_W).astype(F32)
    col = (t % GRID_W).astype(F32)
    nf = DIFF_QK // 4
    inv = ROPE_BASE ** (-jnp.arange(nf, dtype=F32) / nf)
    lane = jnp.arange(DIFF_WIDTH)
    pos = jnp.where(((lane % DIFF_QK) < DIFF_QK // 2)[None, :], row[:, None], col[:, None])
    ang = pos * inv[lane % nf][None, :]
    first = (lane % (2 * nf)) < nf
    return jnp.cos(ang), jnp.where(first[None, :], -jnp.sin(ang), jnp.sin(ang))


def kernel(x_prompt, x_sample, cache_na_k, cache_na_v, cache_diff_k, cache_diff_v, state_ssm_re, state_ssm_im,
           c, c_ctx, w_ada, b_ada, norm_mix, norm_ffn, w_in, w_out, na_rpb, diff_lambda, diff_subln,
           ssm_a_re, ssm_a_im, ssm_log_dt, ssm_b_re, ssm_b_im, ssm_c_re, ssm_c_im, ssm_d, ssm_w_glu, ssm_b_glu,
           w_router, w_gate, w_up, w_down, final_norm):
    depth = w_in.shape[0]
    bp, sp, d = x_prompt.shape
    bs, ss, _ = x_sample.shape
    assert d == D_MODEL and bs + 1 <= SUBLANES
    past = cache_na_k.shape[2]

    cond = jnp.zeros((SUBLANES, d), F32).at[0].set(c_ctx).at[1:1 + bs].set(c)
    mods = _ada(cond, w_ada, b_ada).reshape(depth, SUBLANES, 6, d)

    rope_tabs = _rope_tables(ss)
    kc_a = cache_na_k.reshape(bs, depth, past, NA_WIDTH)
    vc_a = cache_na_v.reshape(bs, depth, past, NA_WIDTH)
    kc_b = cache_diff_k.reshape(bs, depth, past, DIFF_WIDTH)
    vc_b = cache_diff_v.reshape(bs, depth, past, DIFF_WIDTH)
    subln = jnp.tile(diff_subln, (1, LANES // DIFF_V)).reshape(depth, 1, LANES)
    norm_mix = norm_mix.reshape(depth, 1, d)
    norm_ffn = norm_ffn.reshape(depth, 1, d)
    ssm_d = ssm_d.reshape(depth, 1, SSM_WIDTH)
    ssm_b_glu = ssm_b_glu.reshape(depth, 1, SSM_WIDTH)
    wr_t = jnp.swapaxes(w_router, 1, 2)
    fnorm = final_norm.reshape(1, d)
    zero_state = jnp.zeros((bp, 2, SSM_N), F32)
    cap_p = EC_CAPACITY_FACTOR * sp // N_EXPERTS

    xp, xs = x_prompt, x_sample
    new_ka, new_va, new_kb, new_vb, new_sre, new_sim = [], [], [], [], [], []
    for l in range(depth):
        lam_init = 0.8 - 0.6 * math.exp(-0.3 * l)
        mod_p = mods[l, 0:1]
        mod_s = mods[l, 1:1 + bs]
        tabs = _s5_tables(ssm_a_re[l], ssm_a_im[l], ssm_log_dt[l], ssm_b_re[l], ssm_b_im[l], ssm_c_re[l], ssm_c_im[l])
        bias_tab = _na_bias_table(na_rpb[l])
        diff_params = (diff_lambda, subln)

        qa, ka, va, qb, kb, vb, u = _proj_in(xp, mod_p, norm_mix, w_in, l, None)
        o_a = _attention(qa, ka, va, None, l, None, lam_init)
        o_b = _attention(qb, kb, vb, None, l, diff_params, lam_init)
        o_c, fre, fim = _s5(u, zero_state, zero_state, tabs, ssm_d, ssm_w_glu, ssm_b_glu, l)
        xp = _proj_out(o_a, o_b, o_c, w_out, xp, mod_p, l)
        new_ka.append(ka.reshape(bp, sp, NA_HEADS, HEAD_DIM))
        new_va.append(va.reshape(bp, sp, NA_HEADS, HEAD_DIM))
        new_kb.append(kb.reshape(bp, sp, DIFF_HEADS, DIFF_V))
        new_vb.append(vb.reshape(bp, sp, DIFF_HEADS, DIFF_V))
        new_sre.append(fre.reshape(bp, 2, SSM_GROUPS, SSM_STATE))
        new_sim.append(fim.reshape(bp, 2, SSM_GROUPS, SSM_STATE))

        qa, ka, va, qb, kb, vb, u = _proj_in(xs, mod_s, norm_mix, w_in, l, rope_tabs)
        o_a = _neighbourhood_attention(qa, ka, va, kc_a, vc_a, bias_tab, l)
        o_b = _attention(qb, kb, vb, (kc_b, vc_b), l, diff_params, lam_init)
        h0re = state_ssm_re[:, l].reshape(bs, 2, SSM_N)
        h0im = state_ssm_im[:, l].reshape(bs, 2, SSM_N)
        o_c, _, _ = _s5(u, h0re, h0im, tabs, ssm_d, ssm_w_glu, ssm_b_glu, l)
        xs = _proj_out(o_a, o_b, o_c, w_out, xs, mod_s, l)

        hp, pos_p, gate_p = _route(xp, mod_p, norm_ffn, wr_t, l)
        hs, pos_s, gate_s = _route(xs, mod_s, norm_ffn, wr_t, l)
        xg_p, gs_p = _gather(hp, pos_p, gate_p)
        xg_s, gs_s = _gather(hs, pos_s, gate_s)
        ye = _ffn(jnp.concatenate([xg_p, xg_s], axis=1), jnp.concatenate([gs_p, gs_s], axis=1),
                  w_gate, w_up, w_down, l)
        last = l == depth - 1
        xp = _combine(ye, pos_p, xp, mod_p, fnorm if last else None, 0)
        xs = _combine(ye, pos_s, xs, mod_s, fnorm if last else None, bp * cap_p)

    return (xp, xs, jnp.stack(new_ka, axis=1), jnp.stack(new_va, axis=1),
            jnp.stack(new_kb, axis=1), jnp.stack(new_vb, axis=1),
            jnp.stack(new_sre, axis=1), jnp.stack(new_sim, axis=1))
```

```python
import functools
import math

import jax
import jax.numpy as jnp
from jax import lax
from jax.experimental import pallas as pl
from jax.experimental.pallas import tpu as pltpu

F32 = jnp.float32
BF16 = jnp.bfloat16
I32 = jnp.int32

D_MODEL = 1024
GRID_W = 64
HEAD_DIM = 64
NA_HEADS = 8
NA_WIDTH = NA_HEADS * HEAD_DIM
WIN_R = 8
WIN_C = 16
DIFF_HEADS = 4
DIFF_QK = 32
DIFF_V = 64
DIFF_WIDTH = DIFF_HEADS * DIFF_V
SSM_GROUPS = 16
SSM_GROUP_CH = 16
SSM_WIDTH = SSM_GROUPS * SSM_GROUP_CH
SSM_STATE = 64
SSM_N = SSM_GROUPS * SSM_STATE
IN_WIDTH = 3 * NA_WIDTH + 3 * DIFF_WIDTH + SSM_WIDTH
N_EXPERTS = 16
EXPERT_FF = 2048
EC_CAPACITY_FACTOR = 2
ROPE_BASE = 10000.0
EPS = 1e-6
NEG_INF = -1e30

LANES = 128
SUBLANES = 8
MXU_DIM = 256
VMEM_LIMIT_CAP = 60000 * 1024

ROW_TILE = 256
NA_QROWS = 4
NA_KROWS = 12
SCAN_TILE = 256
FF_TILE = 512
COMBINE_TILE = 512


def _params(vmem_bytes, ndims):
    return pltpu.CompilerParams(
        dimension_semantics=("arbitrary",) * ndims,
        vmem_limit_bytes=VMEM_LIMIT_CAP,
    )


def _nbytes(shape, dtype):
    return math.prod(shape) * jnp.dtype(dtype).itemsize


def _hbm(*arrays):
    return list(arrays)


def _dot(a, b):
    return jnp.dot(a, b, preferred_element_type=F32)


def _dot_nt(a, b):
    return lax.dot_general(a, b, (((1,), (1,)), ((), ())), preferred_element_type=F32)


def _dot_tn(a, b):
    return lax.dot_general(a, b, (((0,), (0,)), ((), ())), preferred_element_type=F32)


def _cast_rows(dst_ref, src_ref, rows):
    n = src_ref.shape[0]
    for r in range(0, n, rows):
        dst_ref[r:r + rows, :] = src_ref[r:r + rows, :].astype(BF16)


def _modulated_norm(x, g, shift, scale):
    ms = jnp.mean(x * x, axis=-1, keepdims=True)
    return (x * lax.rsqrt(ms + EPS) * g) * (1.0 + scale) + shift


def _ada_kernel(c_ref, w_ref, b_ref, o_ref):
    c = c_ref[...]
    s = c * jax.nn.sigmoid(c)
    o_ref[0] = _dot(s.astype(BF16), w_ref[0].astype(BF16)) + b_ref[0]


def _ada(cond, w_ada, b_ada):
    depth = w_ada.shape[0]
    tn = 1536
    nt = 6 * D_MODEL // tn
    vm = 2 * _nbytes((D_MODEL, tn), F32) + _nbytes((D_MODEL, tn), BF16) + (4 << 20)
    return pl.pallas_call(
        _ada_kernel,
        out_shape=jax.ShapeDtypeStruct((depth, SUBLANES, 6 * D_MODEL), F32),
        grid=(depth, nt),
        in_specs=[
            pl.BlockSpec((SUBLANES, D_MODEL), lambda l, j: (0, 0)),
            pl.BlockSpec((1, D_MODEL, tn), lambda l, j: (l, 0, j)),
            pl.BlockSpec((1, 1, tn), lambda l, j: (l, 0, j)),
        ],
        out_specs=pl.BlockSpec((1, SUBLANES, tn), lambda l, j: (l, 0, j)),
        compiler_params=_params(vm, 2),
        name="ada",
    )(*_hbm(cond, w_ada, b_ada.reshape(depth, 1, 6 * D_MODEL)))


def _rope_apply(x, cos, sin_signed):
    lane = lax.broadcasted_iota(I32, (1, LANES), 1)
    first = (lane % 16) < 8
    outs = []
    for t in range(x.shape[1] // LANES):
        xt = x[:, t * LANES:(t + 1) * LANES]
        partner = jnp.where(first, pltpu.roll(xt, LANES - 8, axis=1), pltpu.roll(xt, 8, axis=1))
        outs.append(xt * cos[:, t * LANES:(t + 1) * LANES] + partner * sin_signed[:, t * LANES:(t + 1) * LANES])
    return jnp.concatenate(outs, axis=1)


def _proj_in_kernel(*refs, rope):
    if rope:
        x_ref, mod_ref, g_ref, w_ref, cos_ref, sin_ref = refs[:6]
        outs = refs[6:13]
        wbf_ref = refs[13]
    else:
        x_ref, mod_ref, g_ref, w_ref = refs[:4]
        outs = refs[4:11]
        wbf_ref = refs[11]
    qa_ref, ka_ref, va_ref, qb_ref, kb_ref, vb_ref, u_ref = outs

    @pl.when((pl.program_id(0) == 0) & (pl.program_id(1) == 0))
    def _():
        _cast_rows(wbf_ref, w_ref, 128)

    h = _modulated_norm(x_ref[0], g_ref[...], mod_ref[0, 0:1, :], mod_ref[0, 1:2, :])
    z = _dot(h.astype(BF16), wbf_ref[...])
    i1, i2, i3 = NA_WIDTH, 2 * NA_WIDTH, 3 * NA_WIDTH
    i4, i5, i6 = i3 + DIFF_WIDTH, i3 + 2 * DIFF_WIDTH, i3 + 3 * DIFF_WIDTH
    qa_ref[0] = z[:, :i1]
    ka_ref[0] = z[:, i1:i2]
    va_ref[0] = z[:, i2:i3]
    qb = z[:, i3:i4]
    kb = z[:, i4:i5]
    if rope:
        qb = _rope_apply(qb, cos_ref[...], sin_ref[...])
        kb = _rope_apply(kb, cos_ref[...], sin_ref[...])
    qb_ref[0] = qb
    kb_ref[0] = kb
    vb_ref[0] = z[:, i5:i6]
    u_ref[0] = z[:, i6:]


def _proj_in(x, mod, g_norm, w_in, layer, rope_tabs):
    b, n, d = x.shape
    tm = ROW_TILE
    bm = mod.shape[0]
    rope = rope_tabs is not None
    mod_map = (lambda i, j: (i, 0, 0)) if bm > 1 else (lambda i, j: (0, 0, 0))
    in_specs = [
        pl.BlockSpec((1, tm, d), lambda i, j: (i, j, 0)),
        pl.BlockSpec((1, 6, d), mod_map),
        pl.BlockSpec((None, 1, d), lambda i, j: (layer, 0, 0)),
        pl.BlockSpec((None, d, IN_WIDTH), lambda i, j: (layer, 0, 0), pipeline_mode=pl.Buffered(1)),
    ]
    args = [x, mod, g_norm, w_in]
    if rope:
        in_specs += [pl.BlockSpec((tm, DIFF_WIDTH), lambda i, j: (j, 0))] * 2
        args += list(rope_tabs)
    widths = (NA_WIDTH, NA_WIDTH, NA_WIDTH, DIFF_WIDTH, DIFF_WIDTH, DIFF_WIDTH, SSM_WIDTH)
    vm = (_nbytes((d, IN_WIDTH), F32) + _nbytes((d, IN_WIDTH), BF16)
          + 4 * _nbytes((tm, d), F32) + 6 * _nbytes((tm, IN_WIDTH), F32) + (4 << 20))
    return pl.pallas_call(
        functools.partial(_proj_in_kernel, rope=rope),
        out_shape=[jax.ShapeDtypeStruct((b, n, w), F32) for w in widths],
        grid=(b, n // tm),
        in_specs=in_specs,
        out_specs=[pl.BlockSpec((1, tm, w), lambda i, j: (i, j, 0)) for w in widths],
        scratch_shapes=[pltpu.VMEM((d, IN_WIDTH), BF16)],
        compiler_params=_params(vm, 2),
        name="proj_in",
    )(*_hbm(*args))


def _softmax_av(qm, ks, vs, scale, biases):
    ss = []
    for k, bias in zip(ks, biases):
        s = _dot_nt(qm, k) * scale
        if bias is not None:
            s = bias(s)
        ss.append(s)
    m = ss[0].max(axis=-1, keepdims=True)
    for s in ss[1:]:
        m = jnp.maximum(m, s.max(axis=-1, keepdims=True))
    acc = None
    den = None
    for s, v in zip(ss, vs):
        p = jnp.exp(s - m)
        den = p.sum(axis=-1, keepdims=True) if den is None else den + p.sum(axis=-1, keepdims=True)
        pv = _dot(p.astype(BF16), v)
        acc = pv if acc is None else acc + pv
    return acc / den


def _diff_lambda(lam_ref, lam_init):
    lp = lam_ref[...]
    a = jnp.sum(lp[0:1] * lp[1:2], axis=-1, keepdims=True)
    b = jnp.sum(lp[2:3] * lp[3:4], axis=-1, keepdims=True)
    return jnp.exp(a) - jnp.exp(b) + lam_init


def _attn_kernel(*refs, has_ctx, diff, lam_init):
    refs = list(refs)
    q_ref, k_ref, v_ref = refs[:3]
    pos = 3
    if has_ctx:
        kc_ref, vc_ref = refs[pos:pos + 2]
        pos += 2
    if diff:
        lam_ref, subln_ref = refs[pos:pos + 2]
        pos += 2
    o_ref = refs[pos]
    width = q_ref.shape[2]
    lane = lax.broadcasted_iota(I32, (1, LANES), 1)
    if diff:
        lam = _diff_lambda(lam_ref, lam_init)
        scale = DIFF_QK ** -0.5
    else:
        scale = HEAD_DIM ** -0.5
    for p in range(width // LANES):
        sl = slice(p * LANES, (p + 1) * LANES)
        qp = q_ref[0, :, sl]
        ks = [k_ref[0, :, sl].astype(BF16)]
        vs = [v_ref[0, :, sl].astype(BF16)]
        if has_ctx:
            ks.append(kc_ref[:, sl].astype(BF16))
            vs.append(vc_ref[:, sl].astype(BF16))
        nob = [None] * len(ks)
        pair = None
        for sub in range(LANES // HEAD_DIM):
            head_mask = (lane // HEAD_DIM) == sub
            if diff:
                os_ = []
                for half in range(2):
                    qmask = (lane // DIFF_QK) == (2 * sub + half)
                    qm = jnp.where(qmask, qp, 0.0).astype(BF16)
                    os_.append(_softmax_av(qm, ks, vs, scale, nob))
                o = jnp.where(head_mask, os_[0] - lam * os_[1], 0.0)
                ms = jnp.sum(o * o, axis=-1, keepdims=True) * (1.0 / DIFF_V)
                o = (o * lax.rsqrt(ms + EPS) * subln_ref[...]) * (1.0 - lam_init)
            else:
                qm = jnp.where(head_mask, qp, 0.0).astype(BF16)
                o = jnp.where(head_mask, _softmax_av(qm, ks, vs, scale, nob), 0.0)
            pair = o if pair is None else pair + o
        o_ref[0, :, sl] = pair


def _attention(q, k, v, ctx_kv, layer, diff_params, lam_init):
    b, nq, w = q.shape
    nk = k.shape[1]
    tq = ROW_TILE
    has_ctx = ctx_kv is not None
    diff = diff_params is not None
    in_specs = [
        pl.BlockSpec((1, tq, w), lambda i, j: (i, j, 0)),
        pl.BlockSpec((1, nk, w), lambda i, j: (i, 0, 0)),
        pl.BlockSpec((1, nk, w), lambda i, j: (i, 0, 0)),
    ]
    args = [q, k, v]
    nc = 0
    if has_ctx:
        nc = ctx_kv[0].shape[2]
        in_specs += [pl.BlockSpec((None, None, nc, w), lambda i, j: (i, layer, 0, 0))] * 2
        args += list(ctx_kv)
    if diff:
        lam_p, subln = diff_params
        in_specs += [
            pl.BlockSpec((None, 4, DIFF_QK), lambda i, j: (layer, 0, 0)),
            pl.BlockSpec((None, 1, LANES), lambda i, j: (layer, 0, 0)),
        ]
        args += [lam_p, subln]
    vm = (4 * _nbytes((nk + nc, w), F32) + 2 * _nbytes((nk + nc, LANES), BF16)
          + 8 * _nbytes((tq, nk + nc), F32) + 8 * _nbytes((tq, w), F32) + (4 << 20))
    return pl.pallas_call(
        functools.partial(_attn_kernel, has_ctx=has_ctx, diff=diff, lam_init=lam_init),
        out_shape=jax.ShapeDtypeStruct((b, nq, w), F32),
        grid=(b, nq // tq),
        in_specs=in_specs,
        out_specs=pl.BlockSpec((1, tq, w), lambda i, j: (i, j, 0)),
        compiler_params=_params(vm, 2),
        name="attn_diff" if diff else "attn",
    )(*_hbm(*args))


def _na_kernel(q_ref, k_ref, v_ref, kc_ref, vc_ref, bias_ref, o_ref, *, rows):
    i = pl.program_id(1)
    tq = NA_QROWS * GRID_W
    nkw = NA_KROWS * GRID_W
    ustart = jnp.clip(NA_QROWS * i - WIN_R // 2, 0, rows - NA_KROWS)
    koff = pl.multiple_of(ustart * GRID_W, GRID_W)
    delta = ustart - NA_QROWS * i
    qrow = NA_QROWS * i + lax.broadcasted_iota(I32, (tq, nkw), 0) // GRID_W
    krow = ustart + lax.broadcasted_iota(I32, (tq, nkw), 1) // GRID_W
    wstart = jnp.clip(qrow - WIN_R // 2, 0, rows - WIN_R)
    row_ok = (krow >= wstart) & (krow < wstart + WIN_R)
    lane = lax.broadcasted_iota(I32, (1, LANES), 1)
    scale = HEAD_DIM ** -0.5
    for p in range(NA_WIDTH // LANES):
        sl = slice(p * LANES, (p + 1) * LANES)
        qp = q_ref[0, :, sl]
        ks = [k_ref[0, pl.ds(koff, nkw), sl].astype(BF16), kc_ref[:, sl].astype(BF16)]
        vs = [v_ref[0, pl.ds(koff, nkw), sl].astype(BF16), vc_ref[:, sl].astype(BF16)]
        pair = None
        for sub in range(LANES // HEAD_DIM):
            h = p * (LANES // HEAD_DIM) + sub
            head_mask = (lane // HEAD_DIM) == sub

            def win_bias(s, h=h):
                rows_ = []
                for ri in range(NA_QROWS):
                    tiles = []
                    for pj in range(NA_KROWS // 2):
                        dr = delta + 2 * pj - ri
                        idx = jnp.clip(dr, -WIN_R, WIN_R - 1) + WIN_R
                        tiles.append(bias_ref[h, idx])
                    rows_.append(jnp.concatenate(tiles, axis=1))
                bias = jnp.concatenate(rows_, axis=0)
                return jnp.where(row_ok, s + bias, NEG_INF)

            qm = jnp.where(head_mask, qp, 0.0).astype(BF16)
            o = _softmax_av(qm, ks, vs, scale, [win_bias, None])
            o = jnp.where(head_mask, o, 0.0)
            pair = o if pair is None else pair + o
        o_ref[0, :, sl] = pair


def _na_bias_table(rpb):
    cc = jnp.arange(GRID_W)
    col_start = jnp.clip(cc - WIN_C // 2, 0, GRID_W - WIN_C)
    col_ok = (cc[None, :] >= col_start[:, None]) & (cc[None, :] < col_start[:, None] + WIN_C)
    rpb = rpb.astype(F32)
    lead = rpb.shape[:2]
    lo = GRID_W - WIN_C
    w = jnp.concatenate([jnp.broadcast_to(rpb[..., :1], lead + (lo,)), rpb,
                         jnp.broadcast_to(rpb[..., -1:], lead + (2 * GRID_W - lo - 2 * WIN_C + 1,))], axis=-1)
    flat = jnp.tile(w, (1, 1, GRID_W))
    t = flat[..., GRID_W - 1:GRID_W - 1 + GRID_W * (2 * GRID_W - 1)].reshape(lead + (GRID_W, 2 * GRID_W - 1))
    t = jnp.where(col_ok[None, None], t[..., :GRID_W], NEG_INF)
    zero = jnp.zeros_like(t[:, :1])
    t = jnp.concatenate([zero, t, zero], axis=1)
    return jnp.concatenate([t[:, :-1], t[:, 1:]], axis=-1)


def _neighbourhood_attention(q, k, v, kc, vc, bias_tab, layer):
    b, n, w = q.shape
    rows = n // GRID_W
    tq = NA_QROWS * GRID_W
    nc = kc.shape[2]
    nkw = NA_KROWS * GRID_W
    vm = (4 * _nbytes((n, w), F32) + 4 * _nbytes((nc, w), F32) + 2 * _nbytes(bias_tab.shape, F32)
          + 12 * _nbytes((tq, nkw + nc), F32) + (6 << 20))
    return pl.pallas_call(
        functools.partial(_na_kernel, rows=rows),
        out_shape=jax.ShapeDtypeStruct((b, n, w), F32),
        grid=(b, n // tq),
        in_specs=[
            pl.BlockSpec((1, tq, w), lambda i, j: (i, j, 0)),
            pl.BlockSpec((1, n, w), lambda i, j: (i, 0, 0)),
            pl.BlockSpec((1, n, w), lambda i, j: (i, 0, 0)),
            pl.BlockSpec((None, None, nc, w), lambda i, j: (i, layer, 0, 0)),
            pl.BlockSpec((None, None, nc, w), lambda i, j: (i, layer, 0, 0)),
            pl.BlockSpec(bias_tab.shape, lambda i, j: (0, 0, 0, 0)),
        ],
        out_specs=pl.BlockSpec((1, tq, w), lambda i, j: (i, j, 0)),
        compiler_params=_params(vm, 2),
        name="na_attn",
    )(*_hbm(q, k, v, kc, vc, bias_tab))


def _s5_kernel(u_ref, h0re_ref, h0im_ref, bmat_ref, cmat_ref, apow_ref, ptab_ref, d_ref, wglu_ref, bglu_ref,
               o_ref, fre_ref, fim_ref, bu_ref, hs_ref, y_ref):
    seq = u_ref.shape[1]
    tt = SCAN_TILE
    ntile = seq // tt
    ngrp = tt // SUBLANES
    nlt = SSM_N // LANES
    row = lax.broadcasted_iota(I32, (SUBLANES, LANES), 0)

    y_ref[...] = u_ref[0] * d_ref[...]

    for d in range(2):
        reverse = d == 1

        def group(g, carry, d=d, reverse=reverse):
            gg = (ngrp - 1 - g) if reverse else g
            r0 = pl.multiple_of(gg * SUBLANES, SUBLANES)
            new = []
            for lt in range(nlt):
                lre = slice(lt * LANES, (lt + 1) * LANES)
                lim = slice(SSM_N + lt * LANES, SSM_N + (lt + 1) * LANES)
                xr = bu_ref[pl.ds(r0, SUBLANES), lre]
                xi = bu_ref[pl.ds(r0, SUBLANES), lim]
                for k, s in enumerate((1, 2, 4)):
                    ar = apow_ref[d * 6 + 2 * k:d * 6 + 2 * k + 1, lre]
                    ai = apow_ref[d * 6 + 2 * k + 1:d * 6 + 2 * k + 2, lre]
                    if reverse:
                        keep = row < SUBLANES - s
                        sr = jnp.where(keep, pltpu.roll(xr, SUBLANES - s, axis=0), 0.0)
                        si = jnp.where(keep, pltpu.roll(xi, SUBLANES - s, axis=0), 0.0)
                    else:
                        keep = row >= s
                        sr = jnp.where(keep, pltpu.roll(xr, s, axis=0), 0.0)
                        si = jnp.where(keep, pltpu.roll(xi, s, axis=0), 0.0)
                    xr, xi = xr + (ar * sr - ai * si), xi + (ar * si + ai * sr)
                cr, ci = carry[2 * lt], carry[2 * lt + 1]
                pr = ptab_ref[2 * d, :, lre]
                pi = ptab_ref[2 * d + 1, :, lre]
                hr = xr + (pr * cr - pi * ci)
                hi = xi + (pr * ci + pi * cr)
                hs_ref[pl.ds(r0, SUBLANES), lre] = hr
                hs_ref[pl.ds(r0, SUBLANES), lim] = hi
                edge = 0 if reverse else SUBLANES - 1
                new.append(jnp.broadcast_to(hr[edge:edge + 1, :], (SUBLANES, LANES)))
                new.append(jnp.broadcast_to(hi[edge:edge + 1, :], (SUBLANES, LANES)))
            return tuple(new)

        def tile(t, carry, d=d, reverse=reverse):
            tix = (ntile - 1 - t) if reverse else t
            t0 = pl.multiple_of(tix * tt, tt)
            ub = u_ref[0, pl.ds(t0, tt), :].astype(BF16)
            bu_ref[...] = _dot(ub, bmat_ref[:, d * 2 * SSM_N:(d + 1) * 2 * SSM_N])
            carry = lax.fori_loop(0, ngrp, group, carry)
            y_ref[pl.ds(t0, tt), :] += _dot(hs_ref[...].astype(BF16), cmat_ref[d * 2 * SSM_N:(d + 1) * 2 * SSM_N, :])
            return carry

        init = []
        for lt in range(nlt):
            sl = slice(lt * LANES, (lt + 1) * LANES)
            init.append(jnp.broadcast_to(h0re_ref[0, d:d + 1, sl], (SUBLANES, LANES)))
            init.append(jnp.broadcast_to(h0im_ref[0, d:d + 1, sl], (SUBLANES, LANES)))
        fin = lax.fori_loop(0, ntile, tile, tuple(init))
        for lt in range(nlt):
            sl = slice(lt * LANES, (lt + 1) * LANES)
            fre_ref[0, d:d + 1, sl] = fin[2 * lt][0:1, :]
            fim_ref[0, d:d + 1, sl] = fin[2 * lt + 1][0:1, :]

    wglu = wglu_ref[...].astype(BF16)

    def glu(t, _):
        t0 = pl.multiple_of(t * tt, tt)
        g = jax.nn.gelu(y_ref[pl.ds(t0, tt), :])
        z = _dot(g.astype(BF16), wglu) + bglu_ref[...]
        o_ref[0, pl.ds(t0, tt), :] = g * jax.nn.sigmoid(z)
        return 0

    lax.fori_loop(0, ntile, glu, 0)


def _cmul(ar, ai, br, bi):
    return ar * br - ai * bi, ar * bi + ai * br


def _s5_tables(a_re, a_im, log_dt, b_re, b_im, c_re, c_im):
    g, p, hc = SSM_GROUPS, SSM_STATE, SSM_GROUP_CH
    eye = jnp.eye(g, dtype=F32)
    bcols, crows, apow, ptab = [], [], [], []
    for d in range(2):
        lr, li = a_re[d].astype(F32), a_im[d].astype(F32)
        dt = jnp.exp(log_dt[d].astype(F32))[:, None]
        mag = jnp.exp(lr * dt)
        ar, ai = mag * jnp.cos(li * dt), mag * jnp.sin(li * dt)
        den = lr * lr + li * li
        qr = ((ar - 1.0) * lr + ai * li) / den
        qi = (ai * lr - (ar - 1.0) * li) / den
        br, bi = _cmul(qr[..., None], qi[..., None], b_re[d].astype(F32), b_im[d].astype(F32))
        for part in (br, bi):
            bcols.append((part.transpose(0, 2, 1)[:, :, None, :] * eye[:, None, :, None]).reshape(g * hc, g * p))
        for part in (c_re[d].astype(F32), -c_im[d].astype(F32)):
            crows.append((part.transpose(0, 2, 1)[:, :, None, :] * eye[:, None, :, None]).reshape(g * p, g * hc))
        pw = [(ar.reshape(g * p), ai.reshape(g * p))]
        for _ in range(SUBLANES - 1):
            pw.append(_cmul(*pw[-1], *pw[0]))
        for a in (pw[0], pw[1], pw[3]):
            apow += [a[0], a[1]]
        order = pw[::-1] if d == 1 else pw
        ptab.append(jnp.stack([a[0] for a in order]))
        ptab.append(jnp.stack([a[1] for a in order]))
    return (jnp.concatenate(bcols, axis=1).astype(BF16), jnp.concatenate(crows, axis=0).astype(BF16),
            jnp.stack(apow), jnp.stack(ptab))


def _s5(u, h0re, h0im, tabs, d_skip, w_glu, b_glu, layer):
    b, seq, w = u.shape
    bmat, cmat, apow, ptab = tabs
    vm = (6 * _nbytes((seq, w), F32) + 2 * (bmat.size + cmat.size) * 2 + 4 * _nbytes((SCAN_TILE, 2 * SSM_N), F32)
          + (8 << 20))
    full = lambda a: pl.BlockSpec(a.shape, lambda i: (0,) * a.ndim)
    return pl.pallas_call(
        _s5_kernel,
        out_shape=[jax.ShapeDtypeStruct((b, seq, w), F32),
                   jax.ShapeDtypeStruct((b, 2, SSM_N), F32),
                   jax.ShapeDtypeStruct((b, 2, SSM_N), F32)],
        grid=(b,),
        in_specs=[
            pl.BlockSpec((1, seq, w), lambda i: (i, 0, 0)),
            pl.BlockSpec((1, 2, SSM_N), lambda i: (i, 0, 0)),
            pl.BlockSpec((1, 2, SSM_N), lambda i: (i, 0, 0)),
            full(bmat), full(cmat), full(apow), full(ptab),
            pl.BlockSpec((None, 1, w), lambda i: (layer, 0, 0)),
            pl.BlockSpec((None, w, w), lambda i: (layer, 0, 0)),
            pl.BlockSpec((None, 1, w), lambda i: (layer, 0, 0)),
        ],
        out_specs=[pl.BlockSpec((1, seq, w), lambda i: (i, 0, 0)),
                   pl.BlockSpec((1, 2, SSM_N), lambda i: (i, 0, 0)),
                   pl.BlockSpec((1, 2, SSM_N), lambda i: (i, 0, 0))],
        scratch_shapes=[pltpu.VMEM((SCAN_TILE, 2 * SSM_N), F32),
                        pltpu.VMEM((SCAN_TILE, 2 * SSM_N), F32),
                        pltpu.VMEM((seq, w), F32)],
        compiler_params=_params(vm, 1),
        name="s5",
    )(*_hbm(u, h0re, h0im, bmat, cmat, apow, ptab, d_skip, w_glu, b_glu))


def _proj_out_kernel(oa_ref, ob_ref, oc_ref, w_ref, x_ref, mod_ref, o_ref, wbf_ref):
    @pl.when((pl.program_id(0) == 0) & (pl.program_id(1) == 0))
    def _():
        _cast_rows(wbf_ref, w_ref, 128)

    i1, i2 = NA_WIDTH, NA_WIDTH + DIFF_WIDTH
    y = _dot(oa_ref[0].astype(BF16), wbf_ref[:i1, :])
    y += _dot(ob_ref[0].astype(BF16), wbf_ref[i1:i2, :])
    y += _dot(oc_ref[0].astype(BF16), wbf_ref[i2:, :])
    o_ref[0] = x_ref[0] + mod_ref[0, 2:3, :] * y


def _proj_out(oa, ob, oc, w_out, x, mod, layer):
    b, n, d = x.shape
    tm = ROW_TILE
    bm = mod.shape[0]
    mod_map = (lambda i, j: (i, 0, 0)) if bm > 1 else (lambda i, j: (0, 0, 0))
    kw = w_out.shape[1]
    vm = _nbytes((kw, d), F32) + _nbytes((kw, d), BF16) + 10 * _nbytes((tm, d), F32) + (4 << 20)
    return pl.pallas_call(
        _proj_out_kernel,
        out_shape=jax.ShapeDtypeStruct((b, n, d), F32),
        grid=(b, n // tm),
        in_specs=[
            pl.BlockSpec((1, tm, NA_WIDTH), lambda i, j: (i, j, 0)),
            pl.BlockSpec((1, tm, DIFF_WIDTH), lambda i, j: (i, j, 0)),
            pl.BlockSpec((1, tm, SSM_WIDTH), lambda i, j: (i, j, 0)),
            pl.BlockSpec((None, kw, d), lambda i, j: (layer, 0, 0), pipeline_mode=pl.Buffered(1)),
            pl.BlockSpec((1, tm, d), lambda i, j: (i, j, 0)),
            pl.BlockSpec((1, 6, d), mod_map),
        ],
        out_specs=pl.BlockSpec((1, tm, d), lambda i, j: (i, j, 0)),
        scratch_shapes=[pltpu.VMEM((kw, d), BF16)],
        compiler_params=_params(vm, 2),
        name="proj_out",
    )(*_hbm(oa, ob, oc, w_out, x, mod))


def _excl_cumsum_lanes(m):
    e, n = m.shape
    blk = MXU_DIM
    nb = n // blk
    r = lax.broadcasted_iota(I32, (blk, blk), 0)
    c = lax.broadcasted_iota(I32, (blk, blk), 1)
    tri = jnp.where(r < c, 1.0, 0.0).astype(BF16)
    stacked = jnp.concatenate([m[:, k * blk:(k + 1) * blk] for k in range(nb)], axis=0).astype(BF16)
    within = _dot(stacked, tri)
    outs = []
    off = jnp.zeros((e, 1), F32)
    for k in range(nb):
        outs.append(within[k * e:(k + 1) * e, :] + off)
        off = off + jnp.sum(m[:, k * blk:(k + 1) * blk], axis=1, keepdims=True)
    return jnp.concatenate(outs, axis=1)


def _route_kernel(x_ref, mod_ref, g_ref, wr_ref, h_ref, pos_ref, gate_ref, *, cap):
    n = x_ref.shape[1]
    parts = []
    for r in range(0, n, ROW_TILE):
        h = _modulated_norm(x_ref[0, r:r + ROW_TILE, :], g_ref[...], mod_ref[0, 3:4, :], mod_ref[0, 4:5, :])
        h_ref[0, r:r + ROW_TILE, :] = h.astype(BF16)
        parts.append(lax.dot_general(wr_ref[...], h, (((1,), (1,)), ((), ())),
                                     precision=lax.Precision.HIGHEST, preferred_element_type=F32))
    logits = jnp.concatenate(parts, axis=1)
    mx = logits.max(axis=0, keepdims=True)
    ex = jnp.exp(logits - mx)
    aff = ex / ex.sum(axis=0, keepdims=True)
    def search(it, cur):
        cand = cur | (jnp.int32(1) << (jnp.int32(30) - it))
        cnt = jnp.sum(jnp.where(aff >= pltpu.bitcast(cand, F32), 1.0, 0.0), axis=1, keepdims=True)
        return jnp.where(cnt >= cap, cand, cur)

    thr = pltpu.bitcast(lax.fori_loop(0, 31, search, jnp.zeros((N_EXPERTS, 1), I32)), F32)
    gt = aff > thr
    eq = aff == thr
    need = cap - jnp.sum(jnp.where(gt, 1.0, 0.0), axis=1, keepdims=True)
    eq_rank = _excl_cumsum_lanes(jnp.where(eq, 1.0, 0.0))
    sel = gt | (eq & (eq_rank < need))
    self_ = jnp.where(sel, 1.0, 0.0)
    slot = _excl_cumsum_lanes(self_)
    pos_ref[:, 0, 0, :] = jnp.where(sel, slot, -1.0).astype(I32)
    gate_ref[:, 0, 0, :] = jnp.where(sel, aff, 0.0)


def _route(x, mod, g_norm, wr_t, layer):
    b, n, d = x.shape
    cap = EC_CAPACITY_FACTOR * n // N_EXPERTS
    bm = mod.shape[0]
    mod_map = (lambda i: (i, 0, 0)) if bm > 1 else (lambda i: (0, 0, 0))
    vm = 10 * _nbytes((n, d), F32) + (8 << 20)
    return pl.pallas_call(
        functools.partial(_route_kernel, cap=cap),
        out_shape=[jax.ShapeDtypeStruct((b, n, d), BF16),
                   jax.ShapeDtypeStruct((N_EXPERTS, b, 1, n), I32),
                   jax.ShapeDtypeStruct((N_EXPERTS, b, 1, n), F32)],
        grid=(b,),
        in_specs=[
            pl.BlockSpec((1, n, d), lambda i: (i, 0, 0)),
            pl.BlockSpec((1, 6, d), mod_map),
            pl.BlockSpec((None, 1, d), lambda i: (layer, 0, 0)),
            pl.BlockSpec((None, N_EXPERTS, d), lambda i: (layer, 0, 0)),
        ],
        out_specs=[pl.BlockSpec((1, n, d), lambda i: (i, 0, 0)),
                   pl.BlockSpec((N_EXPERTS, 1, 1, n), lambda i: (0, i, 0, 0)),
                   pl.BlockSpec((N_EXPERTS, 1, 1, n), lambda i: (0, i, 0, 0))],
        compiler_params=_params(vm, 1),
        name="route",
    )(*_hbm(x, mod, g_norm, wr_t))


def _one_hot_rows(pos_row, cap):
    n = pos_row.shape[1]
    slot = lax.broadcasted_iota(I32, (cap, n), 0)
    return slot == pos_row


def _gather_kernel(h_ref, pos_ref, gate_ref, xs_ref, gs_ref, *, cap):
    h = h_ref[0]
    for e in range(N_EXPERTS):
        oh = _one_hot_rows(pos_ref[e, 0], cap)
        xs_ref[e] = _dot(jnp.where(oh, 1.0, 0.0).astype(BF16), h).astype(BF16)
        g = jnp.sum(jnp.where(oh, gate_ref[e, 0], 0.0), axis=1, keepdims=True)
        gs_ref[e] = jnp.broadcast_to(g, (cap, LANES))


def _gather(h, pos, gate):
    b, n, d = h.shape
    cap = EC_CAPACITY_FACTOR * n // N_EXPERTS
    vm = 4 * _nbytes((n, d), BF16) + 6 * _nbytes((N_EXPERTS, cap, d), BF16) + 8 * _nbytes((cap, n), F32) + (8 << 20)
    return pl.pallas_call(
        functools.partial(_gather_kernel, cap=cap),
        out_shape=[jax.ShapeDtypeStruct((N_EXPERTS, b * cap, d), BF16),
                   jax.ShapeDtypeStruct((N_EXPERTS, b * cap, LANES), F32)],
        grid=(b,),
        in_specs=[
            pl.BlockSpec((1, n, d), lambda i: (i, 0, 0)),
            pl.BlockSpec((N_EXPERTS, 1, 1, n), lambda i: (0, i, 0, 0)),
            pl.BlockSpec((N_EXPERTS, 1, 1, n), lambda i: (0, i, 0, 0)),
        ],
        out_specs=[pl.BlockSpec((N_EXPERTS, cap, d), lambda i: (0, i, 0)),
                   pl.BlockSpec((N_EXPERTS, cap, LANES), lambda i: (0, i, 0))],
        compiler_params=_params(vm, 1),
        name="moe_gather",
    )(*_hbm(h, pos, gate))


def _ffn_kernel(xp_ref, gp_ref, xs_ref, gs_ref, wg_ref, wu_ref, wd_ref, yp_ref, ys_ref, acc_ref):
    j = pl.program_id(1)
    sp = xp_ref.shape[1]

    @pl.when(j == 0)
    def _():
        acc_ref[...] = jnp.zeros_like(acc_ref)

    xs = jnp.concatenate([xp_ref[0], xs_ref[0]], axis=0)
    a = _dot(xs, wg_ref[...].astype(BF16))
    u = _dot(xs, wu_ref[...].astype(BF16))
    hid = (a * jax.nn.sigmoid(a)) * u
    acc_ref[...] += _dot(hid.astype(BF16), wd_ref[...].astype(BF16))

    @pl.when(j == pl.num_programs(1) - 1)
    def _():
        yp_ref[0] = (acc_ref[:sp, :] * gp_ref[0][:, 0:1]).astype(BF16)
        ys_ref[0] = (acc_ref[sp:, :] * gs_ref[0][:, 0:1]).astype(BF16)


def _ffn(xp, gp, xs, gs, w_gate, w_up, w_down, layer):
    e, sp, d = xp.shape
    ss = xs.shape[1]
    s = sp + ss
    ff = w_gate.shape[-1]
    tf = FF_TILE
    vm = (6 * _nbytes((d, tf), F32) + 3 * _nbytes((d, tf), BF16) + 5 * _nbytes((s, d), BF16)
          + 2 * _nbytes((s, LANES), F32) + _nbytes((s, d), F32) + 4 * _nbytes((s, tf), F32) + (4 << 20))
    return pl.pallas_call(
        _ffn_kernel,
        out_shape=[jax.ShapeDtypeStruct((e, sp, d), BF16), jax.ShapeDtypeStruct((e, ss, d), BF16)],
        grid=(e, ff // tf),
        in_specs=[
            pl.BlockSpec((1, sp, d), lambda i, j: (i, 0, 0)),
            pl.BlockSpec((1, sp, LANES), lambda i, j: (i, 0, 0)),
            pl.BlockSpec((1, ss, d), lambda i, j: (i, 0, 0)),
            pl.BlockSpec((1, ss, LANES), lambda i, j: (i, 0, 0)),
            pl.BlockSpec((None, None, d, tf), lambda i, j: (layer, i, 0, j)),
            pl.BlockSpec((None, None, d, tf), lambda i, j: (layer, i, 0, j)),
            pl.BlockSpec((None, None, tf, d), lambda i, j: (layer, i, j, 0)),
        ],
        out_specs=[pl.BlockSpec((1, sp, d), lambda i, j: (i, 0, 0)),
                   pl.BlockSpec((1, ss, d), lambda i, j: (i, 0, 0))],
        scratch_shapes=[pltpu.VMEM((s, d), F32)],
        compiler_params=_params(vm, 2),
        name="moe_ffn",
    )(*_hbm(xp, gp, xs, gs, w_gate, w_up, w_down))


def _combine_kernel(ye_ref, pos_ref, x_ref, mod_ref, *rest, cap, final):
    if final:
        g_ref, o_ref = rest
    else:
        (o_ref,) = rest
    group = max(1, MXU_DIM // cap)
    y = None
    for e0 in range(0, N_EXPERTS, group):
        oh = jnp.concatenate([jnp.where(_one_hot_rows(pos_ref[e, 0], cap), 1.0, 0.0).astype(BF16)
                              for e in range(e0, e0 + group)], axis=0)
        ye = jnp.concatenate([ye_ref[e] for e in range(e0, e0 + group)], axis=0)
        part = _dot_tn(oh, ye)
        y = part if y is None else y + part
    xn = x_ref[0] + mod_ref[0, 5:6, :] * y
    if final:
        ms = jnp.mean(xn * xn, axis=-1, keepdims=True)
        xn = xn * lax.rsqrt(ms + EPS) * g_ref[...]
    o_ref[0] = xn


def _combine(ye, pos, x, mod, final_norm):
    b, n, d = x.shape
    cap = EC_CAPACITY_FACTOR * n // N_EXPERTS
    tn = min(n, COMBINE_TILE)
    bm = mod.shape[0]
    mod_map = (lambda i, j: (i, 0, 0)) if bm > 1 else (lambda i, j: (0, 0, 0))
    final = final_norm is not None
    in_specs = [
        pl.BlockSpec((N_EXPERTS, cap, d), lambda i, j: (0, i, 0)),
        pl.BlockSpec((N_EXPERTS, 1, 1, tn), lambda i, j: (0, i, 0, j)),
        pl.BlockSpec((1, tn, d), lambda i, j: (i, j, 0)),
        pl.BlockSpec((1, 6, d), mod_map),
    ]
    args = [ye, pos, x, mod]
    if final:
        in_specs.append(pl.BlockSpec((1, d), lambda i, j: (0, 0)))
        args.append(final_norm)
    vm = (4 * _nbytes((N_EXPERTS, cap, d), BF16) + 10 * _nbytes((tn, d), F32)
          + 6 * _nbytes((max(cap, MXU_DIM), tn), F32) + (8 << 20))
    return pl.pallas_call(
        functools.partial(_combine_kernel, cap=cap, final=final),
        out_shape=jax.ShapeDtypeStruct((b, n, d), F32),
        grid=(b, n // tn),
        in_specs=in_specs,
        out_specs=pl.BlockSpec((1, tn, d), lambda i, j: (i, j, 0)),
        compiler_params=_params(vm, 2),
        name="moe_combine",
    )(*_hbm(*args))


def _rope_tables(n):
    t = jnp.arange(n)
    row = (t // GRID_W).astype(F32)
    col = (t % GRID_W).astype(F32)
    nf = DIFF_QK // 4
    inv = ROPE_BASE ** (-jnp.arange(nf, dtype=F32) / nf)
    lane = jnp.arange(DIFF_WIDTH)
    pos = jnp.where(((lane % DIFF_QK) < DIFF_QK // 2)[None, :], row[:, None], col[:, None])
    ang = pos * inv[lane % nf][None, :]
    first = (lane % (2 * nf)) < nf
    return jnp.cos(ang), jnp.where(first[None, :], -jnp.sin(ang), jnp.sin(ang))


def kernel(x_prompt, x_sample, cache_na_k, cache_na_v, cache_diff_k, cache_diff_v, state_ssm_re, state_ssm_im,
           c, c_ctx, w_ada, b_ada, norm_mix, norm_ffn, w_in, w_out, na_rpb, diff_lambda, diff_subln,
           ssm_a_re, ssm_a_im, ssm_log_dt, ssm_b_re, ssm_b_im, ssm_c_re, ssm_c_im, ssm_d, ssm_w_glu, ssm_b_glu,
           w_router, w_gate, w_up, w_down, final_norm):
    depth = w_in.shape[0]
    bp, sp, d = x_prompt.shape
    bs, ss, _ = x_sample.shape
    assert d == D_MODEL and bs + 1 <= SUBLANES
    past = cache_na_k.shape[2]

    cond = jnp.zeros((SUBLANES, d), F32).at[0].set(c_ctx).at[1:1 + bs].set(c)
    mods = _ada(cond, w_ada, b_ada).reshape(depth, SUBLANES, 6, d)

    rope_tabs = _rope_tables(ss)
    kc_a = cache_na_k.reshape(bs, depth, past, NA_WIDTH)
    vc_a = cache_na_v.reshape(bs, depth, past, NA_WIDTH)
    kc_b = cache_diff_k.reshape(bs, depth, past, DIFF_WIDTH)
    vc_b = cache_diff_v.reshape(bs, depth, past, DIFF_WIDTH)
    subln = jnp.tile(diff_subln, (1, LANES // DIFF_V)).reshape(depth, 1, LANES)
    norm_mix = norm_mix.reshape(depth, 1, d)
    norm_ffn = norm_ffn.reshape(depth, 1, d)
    ssm_d = ssm_d.reshape(depth, 1, SSM_WIDTH)
    ssm_b_glu = ssm_b_glu.reshape(depth, 1, SSM_WIDTH)
    wr_t = jnp.swapaxes(w_router, 1, 2)
    fnorm = final_norm.reshape(1, d)
    zero_state = jnp.zeros((bp, 2, SSM_N), F32)

    xp, xs = x_prompt, x_sample
    new_ka, new_va, new_kb, new_vb, new_sre, new_sim = [], [], [], [], [], []
    for l in range(depth):
        lam_init = 0.8 - 0.6 * math.exp(-0.3 * l)
        mod_p = mods[l, 0:1]
        mod_s = mods[l, 1:1 + bs]
        tabs = _s5_tables(ssm_a_re[l], ssm_a_im[l], ssm_log_dt[l], ssm_b_re[l], ssm_b_im[l], ssm_c_re[l], ssm_c_im[l])
        bias_tab = _na_bias_table(na_rpb[l])
        diff_params = (diff_lambda, subln)

        qa, ka, va, qb, kb, vb, u = _proj_in(xp, mod_p, norm_mix, w_in, l, None)
        o_a = _attention(qa, ka, va, None, l, None, lam_init)
        o_b = _attention(qb, kb, vb, None, l, diff_params, lam_init)
        o_c, fre, fim = _s5(u, zero_state, zero_state, tabs, ssm_d, ssm_w_glu, ssm_b_glu, l)
        xp = _proj_out(o_a, o_b, o_c, w_out, xp, mod_p, l)
        new_ka.append(ka.reshape(bp, sp, NA_HEADS, HEAD_DIM))
        new_va.append(va.reshape(bp, sp, NA_HEADS, HEAD_DIM))
        new_kb.append(kb.reshape(bp, sp, DIFF_HEADS, DIFF_V))
        new_vb.append(vb.reshape(bp, sp, DIFF_HEADS, DIFF_V))
        new_sre.append(fre.reshape(bp, 2, SSM_GROUPS, SSM_STATE))
        new_sim.append(fim.reshape(bp, 2, SSM_GROUPS, SSM_STATE))

        qa, ka, va, qb, kb, vb, u = _proj_in(xs, mod_s, norm_mix, w_in, l, rope_tabs)
        o_a = _neighbourhood_attention(qa, ka, va, kc_a, vc_a, bias_tab, l)
        o_b = _attention(qb, kb, vb, (kc_b, vc_b), l, diff_params, lam_init)
        h0re = state_ssm_re[:, l].reshape(bs, 2, SSM_N)
        h0im = state_ssm_im[:, l].reshape(bs, 2, SSM_N)
        o_c, _, _ = _s5(u, h0re, h0im, tabs, ssm_d, ssm_w_glu, ssm_b_glu, l)
        xs = _proj_out(o_a, o_b, o_c, w_out, xs, mod_s, l)

        hp, pos_p, gate_p = _route(xp, mod_p, norm_ffn, wr_t, l)
        hs, pos_s, gate_s = _route(xs, mod_s, norm_ffn, wr_t, l)
        xg_p, gs_p = _gather(hp, pos_p, gate_p)
        xg_s, gs_s = _gather(hs, pos_s, gate_s)
        ye_p, ye_s = _ffn(xg_p, gs_p, xg_s, gs_s, w_gate, w_up, w_down, l)
        last = l == depth - 1
        xp = _combine(ye_p, pos_p, xp, mod_p, fnorm if last else None)
        xs = _combine(ye_s, pos_s, xs, mod_s, fnorm if last else None)

    return (xp, xs, jnp.stack(new_ka, axis=1), jnp.stack(new_va, axis=1),
            jnp.stack(new_kb, axis=1), jnp.stack(new_vb, axis=1),
            jnp.stack(new_sre, axis=1), jnp.stack(new_sim, axis=1))
```

```python
import functools
import math

import jax
import jax.numpy as jnp
import numpy as np
from jax import lax
from jax.experimental import pallas as pl
from jax.experimental.pallas import tpu as pltpu

F32 = jnp.float32
BF16 = jnp.bfloat16
I32 = jnp.int32

D_MODEL = 1024
GRID_W = 64
HEAD_DIM = 64
NA_HEADS = 8
NA_WIDTH = NA_HEADS * HEAD_DIM
WIN_R = 8
WIN_C = 16
DIFF_HEADS = 4
DIFF_QK = 32
DIFF_V = 64
DIFF_WIDTH = DIFF_HEADS * DIFF_V
SSM_GROUPS = 16
SSM_GROUP_CH = 16
SSM_WIDTH = SSM_GROUPS * SSM_GROUP_CH
SSM_STATE = 64
SSM_N = SSM_GROUPS * SSM_STATE
IN_WIDTH = 3 * NA_WIDTH + 3 * DIFF_WIDTH + SSM_WIDTH
N_EXPERTS = 16
EXPERT_FF = 2048
EC_CAPACITY_FACTOR = 2
ROPE_BASE = 10000.0
EPS = 1e-6
NEG_INF = -1e30

LANES = 128
SUBLANES = 8
MXU_DIM = 256
VMEM_LIMIT_CAP = 60000 * 1024

ROW_TILE = 256
NA_QROWS = 4
NA_KROWS = 12
SCAN_TILE = 256
FF_TILE = 512
COMBINE_TILE = 512


def _params(vmem_bytes, ndims):
    return pltpu.CompilerParams(
        dimension_semantics=("arbitrary",) * ndims,
        vmem_limit_bytes=VMEM_LIMIT_CAP,
    )


def _nbytes(shape, dtype):
    return math.prod(shape) * jnp.dtype(dtype).itemsize


def _hbm(*arrays):
    return list(arrays)


def _dot(a, b):
    return jnp.dot(a, b, preferred_element_type=F32)


def _dot_nt(a, b):
    return lax.dot_general(a, b, (((1,), (1,)), ((), ())), preferred_element_type=F32)


def _dot_tn(a, b):
    return lax.dot_general(a, b, (((0,), (0,)), ((), ())), preferred_element_type=F32)


def _cast_rows(dst_ref, src_ref, rows):
    n = src_ref.shape[0]
    for r in range(0, n, rows):
        dst_ref[r:r + rows, :] = src_ref[r:r + rows, :].astype(BF16)


def _modulated_norm(x, g, shift, scale):
    ms = jnp.mean(x * x, axis=-1, keepdims=True)
    return (x * lax.rsqrt(ms + EPS) * g) * (1.0 + scale) + shift


def _ada_kernel(c_ref, w_ref, b_ref, o_ref):
    c = c_ref[...]
    s = c * jax.nn.sigmoid(c)
    o_ref[0] = _dot(s.astype(BF16), w_ref[0].astype(BF16)) + b_ref[0]


def _ada(cond, w_ada, b_ada):
    depth = w_ada.shape[0]
    tn = 1536
    nt = 6 * D_MODEL // tn
    vm = 2 * _nbytes((D_MODEL, tn), F32) + _nbytes((D_MODEL, tn), BF16) + (4 << 20)
    return pl.pallas_call(
        _ada_kernel,
        out_shape=jax.ShapeDtypeStruct((depth, SUBLANES, 6 * D_MODEL), F32),
        grid=(depth, nt),
        in_specs=[
            pl.BlockSpec((SUBLANES, D_MODEL), lambda l, j: (0, 0)),
            pl.BlockSpec((1, D_MODEL, tn), lambda l, j: (l, 0, j)),
            pl.BlockSpec((1, 1, tn), lambda l, j: (l, 0, j)),
        ],
        out_specs=pl.BlockSpec((1, SUBLANES, tn), lambda l, j: (l, 0, j)),
        compiler_params=_params(vm, 2),
        name="ada",
    )(*_hbm(cond, w_ada, b_ada.reshape(depth, 1, 6 * D_MODEL)))


def _rope_apply(x, cos, sin_signed):
    lane = lax.broadcasted_iota(I32, (1, LANES), 1)
    first = (lane % 16) < 8
    outs = []
    for t in range(x.shape[1] // LANES):
        xt = x[:, t * LANES:(t + 1) * LANES]
        partner = jnp.where(first, pltpu.roll(xt, LANES - 8, axis=1), pltpu.roll(xt, 8, axis=1))
        outs.append(xt * cos[:, t * LANES:(t + 1) * LANES] + partner * sin_signed[:, t * LANES:(t + 1) * LANES])
    return jnp.concatenate(outs, axis=1)


def _proj_in_kernel(*refs, rope):
    if rope:
        x_ref, mod_ref, g_ref, w_ref, cos_ref, sin_ref = refs[:6]
        outs = refs[6:13]
        wbf_ref = refs[13]
    else:
        x_ref, mod_ref, g_ref, w_ref = refs[:4]
        outs = refs[4:11]
        wbf_ref = refs[11]
    qa_ref, ka_ref, va_ref, qb_ref, kb_ref, vb_ref, u_ref = outs

    @pl.when((pl.program_id(0) == 0) & (pl.program_id(1) == 0))
    def _():
        _cast_rows(wbf_ref, w_ref, 128)

    h = _modulated_norm(x_ref[0], g_ref[...], mod_ref[0, 0:1, :], mod_ref[0, 1:2, :])
    z = _dot(h.astype(BF16), wbf_ref[...])
    i1, i2, i3 = NA_WIDTH, 2 * NA_WIDTH, 3 * NA_WIDTH
    i4, i5, i6 = i3 + DIFF_WIDTH, i3 + 2 * DIFF_WIDTH, i3 + 3 * DIFF_WIDTH
    qa_ref[0] = z[:, :i1]
    ka_ref[0] = z[:, i1:i2]
    va_ref[0] = z[:, i2:i3]
    qb = z[:, i3:i4]
    kb = z[:, i4:i5]
    if rope:
        qb = _rope_apply(qb, cos_ref[...], sin_ref[...])
        kb = _rope_apply(kb, cos_ref[...], sin_ref[...])
    qb_ref[0] = qb
    kb_ref[0] = kb
    vb_ref[0] = z[:, i5:i6]
    u_ref[0] = z[:, i6:]


def _proj_in(x, mod, g_norm, w_in, layer, rope_tabs):
    b, n, d = x.shape
    tm = ROW_TILE
    bm = mod.shape[0]
    rope = rope_tabs is not None
    mod_map = (lambda i, j: (i, 0, 0)) if bm > 1 else (lambda i, j: (0, 0, 0))
    in_specs = [
        pl.BlockSpec((1, tm, d), lambda i, j: (i, j, 0)),
        pl.BlockSpec((1, 6, d), mod_map),
        pl.BlockSpec((None, 1, d), lambda i, j: (layer, 0, 0)),
        pl.BlockSpec((None, d, IN_WIDTH), lambda i, j: (layer, 0, 0), pipeline_mode=pl.Buffered(1)),
    ]
    args = [x, mod, g_norm, w_in]
    if rope:
        in_specs += [pl.BlockSpec((tm, DIFF_WIDTH), lambda i, j: (j, 0))] * 2
        args += list(rope_tabs)
    widths = (NA_WIDTH, NA_WIDTH, NA_WIDTH, DIFF_WIDTH, DIFF_WIDTH, DIFF_WIDTH, SSM_WIDTH)
    vm = (_nbytes((d, IN_WIDTH), F32) + _nbytes((d, IN_WIDTH), BF16)
          + 4 * _nbytes((tm, d), F32) + 6 * _nbytes((tm, IN_WIDTH), F32) + (4 << 20))
    return pl.pallas_call(
        functools.partial(_proj_in_kernel, rope=rope),
        out_shape=[jax.ShapeDtypeStruct((b, n, w), F32) for w in widths],
        grid=(b, n // tm),
        in_specs=in_specs,
        out_specs=[pl.BlockSpec((1, tm, w), lambda i, j: (i, j, 0)) for w in widths],
        scratch_shapes=[pltpu.VMEM((d, IN_WIDTH), BF16)],
        compiler_params=_params(vm, 2),
        name="proj_in",
    )(*_hbm(*args))


def _softmax_av(qm, ks, vs, scale, biases):
    ss = []
    for k, bias in zip(ks, biases):
        s = _dot_nt(qm, k) * scale
        if bias is not None:
            s = bias(s)
        ss.append(s)
    m = ss[0].max(axis=-1, keepdims=True)
    for s in ss[1:]:
        m = jnp.maximum(m, s.max(axis=-1, keepdims=True))
    acc = None
    den = None
    for s, v in zip(ss, vs):
        p = jnp.exp(s - m)
        den = p.sum(axis=-1, keepdims=True) if den is None else den + p.sum(axis=-1, keepdims=True)
        pv = _dot(p.astype(BF16), v)
        acc = pv if acc is None else acc + pv
    return acc / den


def _diff_lambda(lam_ref, lam_init):
    lp = lam_ref[...]
    a = jnp.sum(lp[0:1] * lp[1:2], axis=-1, keepdims=True)
    b = jnp.sum(lp[2:3] * lp[3:4], axis=-1, keepdims=True)
    return jnp.exp(a) - jnp.exp(b) + lam_init


def _attn_kernel(*refs, has_ctx, diff, lam_init):
    refs = list(refs)
    q_ref, k_ref, v_ref = refs[:3]
    pos = 3
    if has_ctx:
        kc_ref, vc_ref = refs[pos:pos + 2]
        pos += 2
    if diff:
        lam_ref, subln_ref = refs[pos:pos + 2]
        pos += 2
    o_ref = refs[pos]
    width = q_ref.shape[2]
    lane = lax.broadcasted_iota(I32, (1, LANES), 1)
    if diff:
        lam = _diff_lambda(lam_ref, lam_init)
        scale = DIFF_QK ** -0.5
    else:
        scale = HEAD_DIM ** -0.5
    for p in range(width // LANES):
        sl = slice(p * LANES, (p + 1) * LANES)
        qp = q_ref[0, :, sl]
        ks = [k_ref[0, :, sl].astype(BF16)]
        vs = [v_ref[0, :, sl].astype(BF16)]
        if has_ctx:
            ks.append(kc_ref[:, sl].astype(BF16))
            vs.append(vc_ref[:, sl].astype(BF16))
        nob = [None] * len(ks)
        pair = None
        for sub in range(LANES // HEAD_DIM):
            head_mask = (lane // HEAD_DIM) == sub
            if diff:
                os_ = []
                for half in range(2):
                    qmask = (lane // DIFF_QK) == (2 * sub + half)
                    qm = jnp.where(qmask, qp, 0.0).astype(BF16)
                    os_.append(_softmax_av(qm, ks, vs, scale, nob))
                o = jnp.where(head_mask, os_[0] - lam * os_[1], 0.0)
                ms = jnp.sum(o * o, axis=-1, keepdims=True) * (1.0 / DIFF_V)
                o = (o * lax.rsqrt(ms + EPS) * subln_ref[...]) * (1.0 - lam_init)
            else:
                qm = jnp.where(head_mask, qp, 0.0).astype(BF16)
                o = jnp.where(head_mask, _softmax_av(qm, ks, vs, scale, nob), 0.0)
            pair = o if pair is None else pair + o
        o_ref[0, :, sl] = pair


def _attention(q, k, v, ctx_kv, layer, diff_params, lam_init):
    b, nq, w = q.shape
    nk = k.shape[1]
    tq = ROW_TILE
    has_ctx = ctx_kv is not None
    diff = diff_params is not None
    in_specs = [
        pl.BlockSpec((1, tq, w), lambda i, j: (i, j, 0)),
        pl.BlockSpec((1, nk, w), lambda i, j: (i, 0, 0)),
        pl.BlockSpec((1, nk, w), lambda i, j: (i, 0, 0)),
    ]
    args = [q, k, v]
    nc = 0
    if has_ctx:
        nc = ctx_kv[0].shape[2]
        in_specs += [pl.BlockSpec((None, None, nc, w), lambda i, j: (i, layer, 0, 0))] * 2
        args += list(ctx_kv)
    if diff:
        lam_p, subln = diff_params
        in_specs += [
            pl.BlockSpec((None, 4, DIFF_QK), lambda i, j: (layer, 0, 0)),
            pl.BlockSpec((None, 1, LANES), lambda i, j: (layer, 0, 0)),
        ]
        args += [lam_p, subln]
    vm = (4 * _nbytes((nk + nc, w), F32) + 2 * _nbytes((nk + nc, LANES), BF16)
          + 8 * _nbytes((tq, nk + nc), F32) + 8 * _nbytes((tq, w), F32) + (4 << 20))
    return pl.pallas_call(
        functools.partial(_attn_kernel, has_ctx=has_ctx, diff=diff, lam_init=lam_init),
        out_shape=jax.ShapeDtypeStruct((b, nq, w), F32),
        grid=(b, nq // tq),
        in_specs=in_specs,
        out_specs=pl.BlockSpec((1, tq, w), lambda i, j: (i, j, 0)),
        compiler_params=_params(vm, 2),
        name="attn_diff" if diff else "attn",
    )(*_hbm(*args))


def _na_kernel(q_ref, k_ref, v_ref, kc_ref, vc_ref, bias_ref, o_ref, *, rows):
    i = pl.program_id(1)
    tq = NA_QROWS * GRID_W
    nkw = NA_KROWS * GRID_W
    ustart = jnp.clip(NA_QROWS * i - WIN_R // 2, 0, rows - NA_KROWS)
    koff = pl.multiple_of(ustart * GRID_W, GRID_W)
    delta = ustart - NA_QROWS * i
    qrow = NA_QROWS * i + lax.broadcasted_iota(I32, (tq, nkw), 0) // GRID_W
    krow = ustart + lax.broadcasted_iota(I32, (tq, nkw), 1) // GRID_W
    wstart = jnp.clip(qrow - WIN_R // 2, 0, rows - WIN_R)
    row_ok = (krow >= wstart) & (krow < wstart + WIN_R)
    lane = lax.broadcasted_iota(I32, (1, LANES), 1)
    scale = HEAD_DIM ** -0.5
    for p in range(NA_WIDTH // LANES):
        sl = slice(p * LANES, (p + 1) * LANES)
        qp = q_ref[0, :, sl]
        ks = [k_ref[0, pl.ds(koff, nkw), sl].astype(BF16), kc_ref[:, sl].astype(BF16)]
        vs = [v_ref[0, pl.ds(koff, nkw), sl].astype(BF16), vc_ref[:, sl].astype(BF16)]
        pair = None
        for sub in range(LANES // HEAD_DIM):
            h = p * (LANES // HEAD_DIM) + sub
            head_mask = (lane // HEAD_DIM) == sub

            def win_bias(s, h=h):
                rows_ = []
                for ri in range(NA_QROWS):
                    tiles = []
                    for pj in range(NA_KROWS // 2):
                        dr = delta + 2 * pj - ri
                        idx = jnp.clip(dr, -WIN_R, WIN_R - 1) + WIN_R
                        tiles.append(bias_ref[h, idx])
                    rows_.append(jnp.concatenate(tiles, axis=1))
                bias = jnp.concatenate(rows_, axis=0)
                return jnp.where(row_ok, s + bias, NEG_INF)

            qm = jnp.where(head_mask, qp, 0.0).astype(BF16)
            o = _softmax_av(qm, ks, vs, scale, [win_bias, None])
            o = jnp.where(head_mask, o, 0.0)
            pair = o if pair is None else pair + o
        o_ref[0, :, sl] = pair


def _na_bias_table(rpb):
    cc = jnp.arange(GRID_W)
    col_start = jnp.clip(cc - WIN_C // 2, 0, GRID_W - WIN_C)
    col_ok = (cc[None, :] >= col_start[:, None]) & (cc[None, :] < col_start[:, None] + WIN_C)
    rpb = rpb.astype(F32)
    lead = rpb.shape[:2]
    lo = GRID_W - WIN_C
    w = jnp.concatenate([jnp.broadcast_to(rpb[..., :1], lead + (lo,)), rpb,
                         jnp.broadcast_to(rpb[..., -1:], lead + (2 * GRID_W - lo - 2 * WIN_C + 1,))], axis=-1)
    flat = jnp.tile(w, (1, 1, GRID_W))
    t = flat[..., GRID_W - 1:GRID_W - 1 + GRID_W * (2 * GRID_W - 1)].reshape(lead + (GRID_W, 2 * GRID_W - 1))
    t = jnp.where(col_ok[None, None], t[..., :GRID_W], NEG_INF)
    zero = jnp.zeros_like(t[:, :1])
    t = jnp.concatenate([zero, t, zero], axis=1)
    return jnp.concatenate([t[:, :-1], t[:, 1:]], axis=-1)


def _neighbourhood_attention(q, k, v, kc, vc, bias_tab, layer):
    b, n, w = q.shape
    rows = n // GRID_W
    tq = NA_QROWS * GRID_W
    nc = kc.shape[2]
    nkw = NA_KROWS * GRID_W
    vm = (4 * _nbytes((n, w), F32) + 4 * _nbytes((nc, w), F32) + 2 * _nbytes(bias_tab.shape, F32)
          + 12 * _nbytes((tq, nkw + nc), F32) + (6 << 20))
    return pl.pallas_call(
        functools.partial(_na_kernel, rows=rows),
        out_shape=jax.ShapeDtypeStruct((b, n, w), F32),
        grid=(b, n // tq),
        in_specs=[
            pl.BlockSpec((1, tq, w), lambda i, j: (i, j, 0)),
            pl.BlockSpec((1, n, w), lambda i, j: (i, 0, 0)),
            pl.BlockSpec((1, n, w), lambda i, j: (i, 0, 0)),
            pl.BlockSpec((None, None, nc, w), lambda i, j: (i, layer, 0, 0)),
            pl.BlockSpec((None, None, nc, w), lambda i, j: (i, layer, 0, 0)),
            pl.BlockSpec(bias_tab.shape, lambda i, j: (0, 0, 0, 0)),
        ],
        out_specs=pl.BlockSpec((1, tq, w), lambda i, j: (i, j, 0)),
        compiler_params=_params(vm, 2),
        name="na_attn",
    )(*_hbm(q, k, v, kc, vc, bias_tab))


def _s5_kernel(u_ref, h0re_ref, h0im_ref, bmat_ref, cmat_ref, apow_ref, ptab_ref, d_ref, wglu_ref, bglu_ref,
               o_ref, fre_ref, fim_ref, bu_ref, hs_ref, y_ref):
    seq = u_ref.shape[1]
    tt = SCAN_TILE
    ntile = seq // tt
    ngrp = tt // SUBLANES
    nlt = SSM_N // LANES
    row = lax.broadcasted_iota(I32, (SUBLANES, LANES), 0)

    y_ref[...] = u_ref[0] * d_ref[...]

    for d in range(2):
        reverse = d == 1

        def group(g, carry, d=d, reverse=reverse):
            gg = (ngrp - 1 - g) if reverse else g
            r0 = pl.multiple_of(gg * SUBLANES, SUBLANES)
            new = []
            for lt in range(nlt):
                lre = slice(lt * LANES, (lt + 1) * LANES)
                lim = slice(SSM_N + lt * LANES, SSM_N + (lt + 1) * LANES)
                xr = bu_ref[pl.ds(r0, SUBLANES), lre]
                xi = bu_ref[pl.ds(r0, SUBLANES), lim]
                for k, s in enumerate((1, 2, 4)):
                    ar = apow_ref[d * 6 + 2 * k:d * 6 + 2 * k + 1, lre]
                    ai = apow_ref[d * 6 + 2 * k + 1:d * 6 + 2 * k + 2, lre]
                    if reverse:
                        keep = row < SUBLANES - s
                        sr = jnp.where(keep, pltpu.roll(xr, SUBLANES - s, axis=0), 0.0)
                        si = jnp.where(keep, pltpu.roll(xi, SUBLANES - s, axis=0), 0.0)
                    else:
                        keep = row >= s
                        sr = jnp.where(keep, pltpu.roll(xr, s, axis=0), 0.0)
                        si = jnp.where(keep, pltpu.roll(xi, s, axis=0), 0.0)
                    xr, xi = xr + (ar * sr - ai * si), xi + (ar * si + ai * sr)
                cr, ci = carry[2 * lt], carry[2 * lt + 1]
                pr = ptab_ref[2 * d, :, lre]
                pi = ptab_ref[2 * d + 1, :, lre]
                hr = xr + (pr * cr - pi * ci)
                hi = xi + (pr * ci + pi * cr)
                hs_ref[pl.ds(r0, SUBLANES), lre] = hr
                hs_ref[pl.ds(r0, SUBLANES), lim] = hi
                edge = 0 if reverse else SUBLANES - 1
                new.append(jnp.broadcast_to(hr[edge:edge + 1, :], (SUBLANES, LANES)))
                new.append(jnp.broadcast_to(hi[edge:edge + 1, :], (SUBLANES, LANES)))
            return tuple(new)

        def tile(t, carry, d=d, reverse=reverse):
            tix = (ntile - 1 - t) if reverse else t
            t0 = pl.multiple_of(tix * tt, tt)
            ub = u_ref[0, pl.ds(t0, tt), :].astype(BF16)
            bu_ref[...] = _dot(ub, bmat_ref[:, d * 2 * SSM_N:(d + 1) * 2 * SSM_N])
            carry = lax.fori_loop(0, ngrp, group, carry)
            y_ref[pl.ds(t0, tt), :] += _dot(hs_ref[...].astype(BF16), cmat_ref[d * 2 * SSM_N:(d + 1) * 2 * SSM_N, :])
            return carry

        init = []
        for lt in range(nlt):
            sl = slice(lt * LANES, (lt + 1) * LANES)
            init.append(jnp.broadcast_to(h0re_ref[0, d:d + 1, sl], (SUBLANES, LANES)))
            init.append(jnp.broadcast_to(h0im_ref[0, d:d + 1, sl], (SUBLANES, LANES)))
        fin = lax.fori_loop(0, ntile, tile, tuple(init))
        for lt in range(nlt):
            sl = slice(lt * LANES, (lt + 1) * LANES)
            fre_ref[0, d:d + 1, sl] = fin[2 * lt][0:1, :]
            fim_ref[0, d:d + 1, sl] = fin[2 * lt + 1][0:1, :]

    wglu = wglu_ref[...].astype(BF16)

    def glu(t, _):
        t0 = pl.multiple_of(t * tt, tt)
        g = jax.nn.gelu(y_ref[pl.ds(t0, tt), :])
        z = _dot(g.astype(BF16), wglu) + bglu_ref[...]
        o_ref[0, pl.ds(t0, tt), :] = g * jax.nn.sigmoid(z)
        return 0

    lax.fori_loop(0, ntile, glu, 0)


def _cmul(ar, ai, br, bi):
    return ar * br - ai * bi, ar * bi + ai * br


def _s5_tables(a_re, a_im, log_dt, b_re, b_im, c_re, c_im):
    g, p, hc = SSM_GROUPS, SSM_STATE, SSM_GROUP_CH
    eye = jnp.eye(g, dtype=F32)
    bcols, crows, apow, ptab = [], [], [], []
    for d in range(2):
        lr, li = a_re[d].astype(F32), a_im[d].astype(F32)
        dt = jnp.exp(log_dt[d].astype(F32))[:, None]
        mag = jnp.exp(lr * dt)
        ar, ai = mag * jnp.cos(li * dt), mag * jnp.sin(li * dt)
        den = lr * lr + li * li
        qr = ((ar - 1.0) * lr + ai * li) / den
        qi = (ai * lr - (ar - 1.0) * li) / den
        br, bi = _cmul(qr[..., None], qi[..., None], b_re[d].astype(F32), b_im[d].astype(F32))
        for part in (br, bi):
            bcols.append((part.transpose(0, 2, 1)[:, :, None, :] * eye[:, None, :, None]).reshape(g * hc, g * p))
        for part in (c_re[d].astype(F32), -c_im[d].astype(F32)):
            crows.append((part.transpose(0, 2, 1)[:, :, None, :] * eye[:, None, :, None]).reshape(g * p, g * hc))
        pw = [(ar.reshape(g * p), ai.reshape(g * p))]
        for _ in range(SUBLANES - 1):
            pw.append(_cmul(*pw[-1], *pw[0]))
        for a in (pw[0], pw[1], pw[3]):
            apow += [a[0], a[1]]
        order = pw[::-1] if d == 1 else pw
        ptab.append(jnp.stack([a[0] for a in order]))
        ptab.append(jnp.stack([a[1] for a in order]))
    return (jnp.concatenate(bcols, axis=1).astype(BF16), jnp.concatenate(crows, axis=0).astype(BF16),
            jnp.stack(apow), jnp.stack(ptab))


def _s5(u, h0re, h0im, tabs, d_skip, w_glu, b_glu, layer):
    b, seq, w = u.shape
    bmat, cmat, apow, ptab = tabs
    vm = (6 * _nbytes((seq, w), F32) + 2 * (bmat.size + cmat.size) * 2 + 4 * _nbytes((SCAN_TILE, 2 * SSM_N), F32)
          + (8 << 20))
    full = lambda a: pl.BlockSpec(a.shape, lambda i: (0,) * a.ndim)
    return pl.pallas_call(
        _s5_kernel,
        out_shape=[jax.ShapeDtypeStruct((b, seq, w), F32),
                   jax.ShapeDtypeStruct((b, 2, SSM_N), F32),
                   jax.ShapeDtypeStruct((b, 2, SSM_N), F32)],
        grid=(b,),
        in_specs=[
            pl.BlockSpec((1, seq, w), lambda i: (i, 0, 0)),
            pl.BlockSpec((1, 2, SSM_N), lambda i: (i, 0, 0)),
            pl.BlockSpec((1, 2, SSM_N), lambda i: (i, 0, 0)),
            full(bmat), full(cmat), full(apow), full(ptab),
            pl.BlockSpec((None, 1, w), lambda i: (layer, 0, 0)),
            pl.BlockSpec((None, w, w), lambda i: (layer, 0, 0)),
            pl.BlockSpec((None, 1, w), lambda i: (layer, 0, 0)),
        ],
        out_specs=[pl.BlockSpec((1, seq, w), lambda i: (i, 0, 0)),
                   pl.BlockSpec((1, 2, SSM_N), lambda i: (i, 0, 0)),
                   pl.BlockSpec((1, 2, SSM_N), lambda i: (i, 0, 0))],
        scratch_shapes=[pltpu.VMEM((SCAN_TILE, 2 * SSM_N), F32),
                        pltpu.VMEM((SCAN_TILE, 2 * SSM_N), F32),
                        pltpu.VMEM((seq, w), F32)],
        compiler_params=_params(vm, 1),
        name="s5",
    )(*_hbm(u, h0re, h0im, bmat, cmat, apow, ptab, d_skip, w_glu, b_glu))


def _proj_out_kernel(oa_ref, ob_ref, oc_ref, w_ref, x_ref, mod_ref, g_ref, wr_ref, o_ref, h_ref, lg_ref, wbf_ref):
    @pl.when((pl.program_id(0) == 0) & (pl.program_id(1) == 0))
    def _():
        _cast_rows(wbf_ref, w_ref, 128)

    i1, i2 = NA_WIDTH, NA_WIDTH + DIFF_WIDTH
    y = _dot(oa_ref[0].astype(BF16), wbf_ref[:i1, :])
    y += _dot(ob_ref[0].astype(BF16), wbf_ref[i1:i2, :])
    y += _dot(oc_ref[0].astype(BF16), wbf_ref[i2:, :])
    xn = x_ref[0] + mod_ref[0, 2:3, :] * y
    o_ref[0] = xn
    h = _modulated_norm(xn, g_ref[...], mod_ref[0, 3:4, :], mod_ref[0, 4:5, :])
    h_ref[0] = h.astype(BF16)
    lg_ref[0] = lax.dot_general(wr_ref[...], h, (((1,), (1,)), ((), ())),
                                precision=lax.Precision.HIGHEST, preferred_element_type=F32)


def _proj_out(oa, ob, oc, w_out, x, mod, g_ffn, wr_t, layer):
    b, n, d = x.shape
    tm = ROW_TILE
    bm = mod.shape[0]
    mod_map = (lambda i, j: (i, 0, 0)) if bm > 1 else (lambda i, j: (0, 0, 0))
    kw = w_out.shape[1]
    vm = _nbytes((kw, d), F32) + _nbytes((kw, d), BF16) + 10 * _nbytes((tm, d), F32) + (4 << 20)
    return pl.pallas_call(
        _proj_out_kernel,
        out_shape=[jax.ShapeDtypeStruct((b, n, d), F32),
                   jax.ShapeDtypeStruct((b, n, d), BF16),
                   jax.ShapeDtypeStruct((b, N_EXPERTS, n), F32)],
        grid=(b, n // tm),
        in_specs=[
            pl.BlockSpec((1, tm, NA_WIDTH), lambda i, j: (i, j, 0)),
            pl.BlockSpec((1, tm, DIFF_WIDTH), lambda i, j: (i, j, 0)),
            pl.BlockSpec((1, tm, SSM_WIDTH), lambda i, j: (i, j, 0)),
            pl.BlockSpec((None, kw, d), lambda i, j: (layer, 0, 0), pipeline_mode=pl.Buffered(1)),
            pl.BlockSpec((1, tm, d), lambda i, j: (i, j, 0)),
            pl.BlockSpec((1, 6, d), mod_map),
            pl.BlockSpec((None, 1, d), lambda i, j: (layer, 0, 0)),
            pl.BlockSpec((None, N_EXPERTS, d), lambda i, j: (layer, 0, 0)),
        ],
        out_specs=[pl.BlockSpec((1, tm, d), lambda i, j: (i, j, 0)),
                   pl.BlockSpec((1, tm, d), lambda i, j: (i, j, 0)),
                   pl.BlockSpec((1, N_EXPERTS, tm), lambda i, j: (i, 0, j))],
        scratch_shapes=[pltpu.VMEM((kw, d), BF16)],
        compiler_params=_params(vm, 2),
        name="proj_out",
    )(*_hbm(oa, ob, oc, w_out, x, mod, g_ffn, wr_t))


def _excl_cumsum_lanes(m):
    e, n = m.shape
    blk = MXU_DIM
    nb = n // blk
    r = lax.broadcasted_iota(I32, (blk, blk), 0)
    c = lax.broadcasted_iota(I32, (blk, blk), 1)
    tri = jnp.where(r < c, 1.0, 0.0).astype(BF16)
    stacked = jnp.concatenate([m[:, k * blk:(k + 1) * blk] for k in range(nb)], axis=0).astype(BF16)
    within = _dot(stacked, tri)
    outs = []
    off = jnp.zeros((e, 1), F32)
    for k in range(nb):
        outs.append(within[k * e:(k + 1) * e, :] + off)
        off = off + jnp.sum(m[:, k * blk:(k + 1) * blk], axis=1, keepdims=True)
    return jnp.concatenate(outs, axis=1)


def _route_kernel(lg_ref, pos_ref, gate_ref, *, cap):
    b, e, n = lg_ref.shape
    lg = lg_ref[...]
    mx = lg.max(axis=1, keepdims=True)
    ex = jnp.exp(lg - mx)
    aff = (ex / ex.sum(axis=1, keepdims=True)).reshape(b * e, n)

    def search(it, cur):
        cand = cur | (jnp.int32(1) << (jnp.int32(30) - it))
        cnt = jnp.sum(jnp.where(aff >= pltpu.bitcast(cand, F32), 1.0, 0.0), axis=1, keepdims=True)
        return jnp.where(cnt >= cap, cand, cur)

    thr = pltpu.bitcast(lax.fori_loop(0, 31, search, jnp.zeros((b * e, 1), I32)), F32)
    gt = aff > thr
    eq = aff == thr
    need = cap - jnp.sum(jnp.where(gt, 1.0, 0.0), axis=1, keepdims=True)
    eq_rank = _excl_cumsum_lanes(jnp.where(eq, 1.0, 0.0))
    sel = gt | (eq & (eq_rank < need))
    slot = _excl_cumsum_lanes(jnp.where(sel, 1.0, 0.0))
    pos = jnp.where(sel, slot, -1.0).astype(I32)
    gate = jnp.where(sel, aff, 0.0)
    for i in range(b):
        pos_ref[:, i, 0, :] = pos[i * e:(i + 1) * e, :]
        gate_ref[:, i, 0, :] = gate[i * e:(i + 1) * e, :]


def _route(logits):
    b, e, n = logits.shape
    cap = EC_CAPACITY_FACTOR * n // N_EXPERTS
    return pl.pallas_call(
        functools.partial(_route_kernel, cap=cap),
        out_shape=[jax.ShapeDtypeStruct((e, b, 1, n), I32),
                   jax.ShapeDtypeStruct((e, b, 1, n), F32)],
        compiler_params=_params(0, 0),
        name="route",
    )(logits)


def _one_hot_rows(pos_row, cap):
    n = pos_row.shape[1]
    slot = lax.broadcasted_iota(I32, (cap, n), 0)
    return slot == pos_row


def _gather_kernel(h_ref, pos_ref, gate_ref, xs_ref, gs_ref, *, cap):
    h = h_ref[0]
    for e in range(N_EXPERTS):
        oh = _one_hot_rows(pos_ref[e, 0], cap)
        xs_ref[e] = _dot(jnp.where(oh, 1.0, 0.0).astype(BF16), h).astype(BF16)
        g = jnp.sum(jnp.where(oh, gate_ref[e, 0], 0.0), axis=1, keepdims=True)
        gs_ref[e] = jnp.broadcast_to(g, (cap, LANES))


def _gather(h, pos, gate):
    b, n, d = h.shape
    cap = EC_CAPACITY_FACTOR * n // N_EXPERTS
    vm = 4 * _nbytes((n, d), BF16) + 6 * _nbytes((N_EXPERTS, cap, d), BF16) + 8 * _nbytes((cap, n), F32) + (8 << 20)
    return pl.pallas_call(
        functools.partial(_gather_kernel, cap=cap),
        out_shape=[jax.ShapeDtypeStruct((N_EXPERTS, b * cap, d), BF16),
                   jax.ShapeDtypeStruct((N_EXPERTS, b * cap, LANES), F32)],
        grid=(b,),
        in_specs=[
            pl.BlockSpec((1, n, d), lambda i: (i, 0, 0)),
            pl.BlockSpec((N_EXPERTS, 1, 1, n), lambda i: (0, i, 0, 0)),
            pl.BlockSpec((N_EXPERTS, 1, 1, n), lambda i: (0, i, 0, 0)),
        ],
        out_specs=[pl.BlockSpec((N_EXPERTS, cap, d), lambda i: (0, i, 0)),
                   pl.BlockSpec((N_EXPERTS, cap, LANES), lambda i: (0, i, 0))],
        compiler_params=_params(vm, 1),
        name="moe_gather",
    )(*_hbm(h, pos, gate))


def _ffn_kernel(xp_ref, gp_ref, xs_ref, gs_ref, wg_ref, wu_ref, wd_ref, yp_ref, ys_ref, acc_ref):
    j = pl.program_id(1)
    sp = xp_ref.shape[1]

    @pl.when(j == 0)
    def _():
        acc_ref[...] = jnp.zeros_like(acc_ref)

    xs = jnp.concatenate([xp_ref[0], xs_ref[0]], axis=0)
    a = _dot(xs, wg_ref[...].astype(BF16))
    u = _dot(xs, wu_ref[...].astype(BF16))
    hid = (a * jax.nn.sigmoid(a)) * u
    acc_ref[...] += _dot(hid.astype(BF16), wd_ref[...].astype(BF16))

    @pl.when(j == pl.num_programs(1) - 1)
    def _():
        yp_ref[0] = (acc_ref[:sp, :] * gp_ref[0][:, 0:1]).astype(BF16)
        ys_ref[0] = (acc_ref[sp:, :] * gs_ref[0][:, 0:1]).astype(BF16)


def _ffn(xp, gp, xs, gs, w_gate, w_up, w_down, layer):
    e, sp, d = xp.shape
    ss = xs.shape[1]
    s = sp + ss
    ff = w_gate.shape[-1]
    tf = FF_TILE
    vm = (6 * _nbytes((d, tf), F32) + 3 * _nbytes((d, tf), BF16) + 5 * _nbytes((s, d), BF16)
          + 2 * _nbytes((s, LANES), F32) + _nbytes((s, d), F32) + 4 * _nbytes((s, tf), F32) + (4 << 20))
    return pl.pallas_call(
        _ffn_kernel,
        out_shape=[jax.ShapeDtypeStruct((e, sp, d), BF16), jax.ShapeDtypeStruct((e, ss, d), BF16)],
        grid=(e, ff // tf),
        in_specs=[
            pl.BlockSpec((1, sp, d), lambda i, j: (i, 0, 0)),
            pl.BlockSpec((1, sp, LANES), lambda i, j: (i, 0, 0)),
            pl.BlockSpec((1, ss, d), lambda i, j: (i, 0, 0)),
            pl.BlockSpec((1, ss, LANES), lambda i, j: (i, 0, 0)),
            pl.BlockSpec((None, None, d, tf), lambda i, j: (layer, i, 0, j)),
            pl.BlockSpec((None, None, d, tf), lambda i, j: (layer, i, 0, j)),
            pl.BlockSpec((None, None, tf, d), lambda i, j: (layer, i, j, 0)),
        ],
        out_specs=[pl.BlockSpec((1, sp, d), lambda i, j: (i, 0, 0)),
                   pl.BlockSpec((1, ss, d), lambda i, j: (i, 0, 0))],
        scratch_shapes=[pltpu.VMEM((s, d), F32)],
        compiler_params=_params(vm, 2),
        name="moe_ffn",
    )(*_hbm(xp, gp, xs, gs, w_gate, w_up, w_down))


def _combine_kernel(ye_ref, pos_ref, x_ref, mod_ref, *rest, cap, final):
    if final:
        g_ref, o_ref = rest
    else:
        (o_ref,) = rest
    group = max(1, MXU_DIM // cap)
    y = None
    for e0 in range(0, N_EXPERTS, group):
        oh = jnp.concatenate([jnp.where(_one_hot_rows(pos_ref[e, 0], cap), 1.0, 0.0).astype(BF16)
                              for e in range(e0, e0 + group)], axis=0)
        ye = jnp.concatenate([ye_ref[e] for e in range(e0, e0 + group)], axis=0)
        part = _dot_tn(oh, ye)
        y = part if y is None else y + part
    xn = x_ref[0] + mod_ref[0, 5:6, :] * y
    if final:
        ms = jnp.mean(xn * xn, axis=-1, keepdims=True)
        xn = xn * lax.rsqrt(ms + EPS) * g_ref[...]
    o_ref[0] = xn


def _combine(ye, pos, x, mod, final_norm):
    b, n, d = x.shape
    cap = EC_CAPACITY_FACTOR * n // N_EXPERTS
    tn = min(n, COMBINE_TILE)
    bm = mod.shape[0]
    mod_map = (lambda i, j: (i, 0, 0)) if bm > 1 else (lambda i, j: (0, 0, 0))
    final = final_norm is not None
    in_specs = [
        pl.BlockSpec((N_EXPERTS, cap, d), lambda i, j: (0, i, 0)),
        pl.BlockSpec((N_EXPERTS, 1, 1, tn), lambda i, j: (0, i, 0, j)),
        pl.BlockSpec((1, tn, d), lambda i, j: (i, j, 0)),
        pl.BlockSpec((1, 6, d), mod_map),
    ]
    args = [ye, pos, x, mod]
    if final:
        in_specs.append(pl.BlockSpec((1, d), lambda i, j: (0, 0)))
        args.append(final_norm)
    vm = (4 * _nbytes((N_EXPERTS, cap, d), BF16) + 10 * _nbytes((tn, d), F32)
          + 6 * _nbytes((max(cap, MXU_DIM), tn), F32) + (8 << 20))
    return pl.pallas_call(
        functools.partial(_combine_kernel, cap=cap, final=final),
        out_shape=jax.ShapeDtypeStruct((b, n, d), F32),
        grid=(b, n // tn),
        in_specs=in_specs,
        out_specs=pl.BlockSpec((1, tn, d), lambda i, j: (i, j, 0)),
        compiler_params=_params(vm, 2),
        name="moe_combine",
    )(*_hbm(*args))


def _rope_tables(n):
    t = np.arange(n)
    row = (t // GRID_W).astype(np.float32)
    col = (t % GRID_W).astype(np.float32)
    nf = DIFF_QK // 4
    inv = np.float32(ROPE_BASE) ** (-np.arange(nf, dtype=np.float32) / np.float32(nf))
    lane = np.arange(DIFF_WIDTH)
    pos = np.where(((lane % DIFF_QK) < DIFF_QK // 2)[None, :], row[:, None], col[:, None])
    ang = (pos * inv[lane % nf][None, :]).astype(np.float32)
    first = (lane % (2 * nf)) < nf
    cos, sin = np.cos(ang).astype(np.float32), np.sin(ang).astype(np.float32)
    return jnp.asarray(cos), jnp.asarray(np.where(first[None, :], -sin, sin))


def kernel(x_prompt, x_sample, cache_na_k, cache_na_v, cache_diff_k, cache_diff_v, state_ssm_re, state_ssm_im,
           c, c_ctx, w_ada, b_ada, norm_mix, norm_ffn, w_in, w_out, na_rpb, diff_lambda, diff_subln,
           ssm_a_re, ssm_a_im, ssm_log_dt, ssm_b_re, ssm_b_im, ssm_c_re, ssm_c_im, ssm_d, ssm_w_glu, ssm_b_glu,
           w_router, w_gate, w_up, w_down, final_norm):
    depth = w_in.shape[0]
    bp, sp, d = x_prompt.shape
    bs, ss, _ = x_sample.shape
    assert d == D_MODEL and bs + 1 <= SUBLANES
    past = cache_na_k.shape[2]

    cond = jnp.zeros((SUBLANES, d), F32).at[0].set(c_ctx).at[1:1 + bs].set(c)
    mods = _ada(cond, w_ada, b_ada).reshape(depth, SUBLANES, 6, d)

    rope_tabs = _rope_tables(ss)
    kc_a = cache_na_k.reshape(bs, depth, past, NA_WIDTH)
    vc_a = cache_na_v.reshape(bs, depth, past, NA_WIDTH)
    kc_b = cache_diff_k.reshape(bs, depth, past, DIFF_WIDTH)
    vc_b = cache_diff_v.reshape(bs, depth, past, DIFF_WIDTH)
    subln = jnp.tile(diff_subln, (1, LANES // DIFF_V)).reshape(depth, 1, LANES)
    norm_mix = norm_mix.reshape(depth, 1, d)
    norm_ffn = norm_ffn.reshape(depth, 1, d)
    ssm_d = ssm_d.reshape(depth, 1, SSM_WIDTH)
    ssm_b_glu = ssm_b_glu.reshape(depth, 1, SSM_WIDTH)
    wr_t = jnp.swapaxes(w_router, 1, 2)
    fnorm = final_norm.reshape(1, d)
    zero_state = jnp.zeros((bp, 2, SSM_N), F32)

    xp, xs = x_prompt, x_sample
    new_ka, new_va, new_kb, new_vb, new_sre, new_sim = [], [], [], [], [], []
    for l in range(depth):
        lam_init = 0.8 - 0.6 * math.exp(-0.3 * l)
        mod_p = mods[l, 0:1]
        mod_s = mods[l, 1:1 + bs]
        tabs = _s5_tables(ssm_a_re[l], ssm_a_im[l], ssm_log_dt[l], ssm_b_re[l], ssm_b_im[l], ssm_c_re[l], ssm_c_im[l])
        bias_tab = _na_bias_table(na_rpb[l])
        diff_params = (diff_lambda, subln)

        qa, ka, va, qb, kb, vb, u = _proj_in(xp, mod_p, norm_mix, w_in, l, None)
        o_a = _attention(qa, ka, va, None, l, None, lam_init)
        o_b = _attention(qb, kb, vb, None, l, diff_params, lam_init)
        o_c, fre, fim = _s5(u, zero_state, zero_state, tabs, ssm_d, ssm_w_glu, ssm_b_glu, l)
        xp, hp, lg_p = _proj_out(o_a, o_b, o_c, w_out, xp, mod_p, norm_ffn, wr_t, l)
        new_ka.append(ka.reshape(bp, sp, NA_HEADS, HEAD_DIM))
        new_va.append(va.reshape(bp, sp, NA_HEADS, HEAD_DIM))
        new_kb.append(kb.reshape(bp, sp, DIFF_HEADS, DIFF_V))
        new_vb.append(vb.reshape(bp, sp, DIFF_HEADS, DIFF_V))
        new_sre.append(fre.reshape(bp, 2, SSM_GROUPS, SSM_STATE))
        new_sim.append(fim.reshape(bp, 2, SSM_GROUPS, SSM_STATE))

        qa, ka, va, qb, kb, vb, u = _proj_in(xs, mod_s, norm_mix, w_in, l, rope_tabs)
        o_a = _neighbourhood_attention(qa, ka, va, kc_a, vc_a, bias_tab, l)
        o_b = _attention(qb, kb, vb, (kc_b, vc_b), l, diff_params, lam_init)
        h0re = state_ssm_re[:, l].reshape(bs, 2, SSM_N)
        h0im = state_ssm_im[:, l].reshape(bs, 2, SSM_N)
        o_c, _, _ = _s5(u, h0re, h0im, tabs, ssm_d, ssm_w_glu, ssm_b_glu, l)
        xs, hs, lg_s = _proj_out(o_a, o_b, o_c, w_out, xs, mod_s, norm_ffn, wr_t, l)

        pos_p, gate_p = _route(lg_p)
        pos_s, gate_s = _route(lg_s)
        xg_p, gs_p = _gather(hp, pos_p, gate_p)
        xg_s, gs_s = _gather(hs, pos_s, gate_s)
        ye_p, ye_s = _ffn(xg_p, gs_p, xg_s, gs_s, w_gate, w_up, w_down, l)
        last = l == depth - 1
        xp = _combine(ye_p, pos_p, xp, mod_p, fnorm if last else None)
        xs = _combine(ye_s, pos_s, xs, mod_s, fnorm if last else None)

    return (xp, xs, jnp.stack(new_ka, axis=1), jnp.stack(new_va, axis=1),
            jnp.stack(new_kb, axis=1), jnp.stack(new_vb, axis=1),
            jnp.stack(new_sre, axis=1), jnp.stack(new_sim, axis=1))
```

```python
import functools
import math

import jax
import jax.numpy as jnp
import numpy as np
from jax import lax
from jax.experimental import pallas as pl
from jax.experimental.pallas import tpu as pltpu

F32 = jnp.float32
BF16 = jnp.bfloat16
I32 = jnp.int32

D_MODEL = 1024
GRID_W = 64
HEAD_DIM = 64
NA_HEADS = 8
NA_WIDTH = NA_HEADS * HEAD_DIM
WIN_R = 8
WIN_C = 16
DIFF_HEADS = 4
DIFF_QK = 32
DIFF_V = 64
DIFF_WIDTH = DIFF_HEADS * DIFF_V
SSM_GROUPS = 16
SSM_GROUP_CH = 16
SSM_WIDTH = SSM_GROUPS * SSM_GROUP_CH
SSM_STATE = 64
SSM_N = SSM_GROUPS * SSM_STATE
IN_WIDTH = 3 * NA_WIDTH + 3 * DIFF_WIDTH + SSM_WIDTH
N_EXPERTS = 16
EXPERT_FF = 2048
EC_CAPACITY_FACTOR = 2
ROPE_BASE = 10000.0
EPS = 1e-6
NEG_INF = -1e30

LANES = 128
SUBLANES = 8
MXU_DIM = 256
VMEM_LIMIT_CAP = 60000 * 1024

ROW_TILE = 256
NA_QROWS = 4
NA_KROWS = 12
SCAN_TILE = 256
FF_TILE = 512
COMBINE_TILE = 512


def _params(vmem_bytes, ndims):
    return pltpu.CompilerParams(
        dimension_semantics=("arbitrary",) * ndims,
        vmem_limit_bytes=VMEM_LIMIT_CAP,
    )


def _nbytes(shape, dtype):
    return math.prod(shape) * jnp.dtype(dtype).itemsize


def _hbm(*arrays):
    return list(arrays)


def _dot(a, b):
    return jnp.dot(a, b, preferred_element_type=F32)


def _dot_nt(a, b):
    return lax.dot_general(a, b, (((1,), (1,)), ((), ())), preferred_element_type=F32)


def _dot_tn(a, b):
    return lax.dot_general(a, b, (((0,), (0,)), ((), ())), preferred_element_type=F32)


def _cast_rows(dst_ref, src_ref, rows):
    n = src_ref.shape[0]
    for r in range(0, n, rows):
        dst_ref[r:r + rows, :] = src_ref[r:r + rows, :].astype(BF16)


def _modulated_norm(x, g, shift, scale):
    ms = jnp.mean(x * x, axis=-1, keepdims=True)
    return (x * lax.rsqrt(ms + EPS) * g) * (1.0 + scale) + shift


def _ada_kernel(c_ref, w_ref, b_ref, o_ref):
    c = c_ref[...]
    s = c * jax.nn.sigmoid(c)
    o_ref[0] = _dot(s.astype(BF16), w_ref[0].astype(BF16)) + b_ref[0]


def _ada(cond, w_ada, b_ada):
    depth = w_ada.shape[0]
    tn = 1536
    nt = 6 * D_MODEL // tn
    vm = 2 * _nbytes((D_MODEL, tn), F32) + _nbytes((D_MODEL, tn), BF16) + (4 << 20)
    return pl.pallas_call(
        _ada_kernel,
        out_shape=jax.ShapeDtypeStruct((depth, SUBLANES, 6 * D_MODEL), F32),
        grid=(depth, nt),
        in_specs=[
            pl.BlockSpec((SUBLANES, D_MODEL), lambda l, j: (0, 0)),
            pl.BlockSpec((1, D_MODEL, tn), lambda l, j: (l, 0, j)),
            pl.BlockSpec((1, 1, tn), lambda l, j: (l, 0, j)),
        ],
        out_specs=pl.BlockSpec((1, SUBLANES, tn), lambda l, j: (l, 0, j)),
        compiler_params=_params(vm, 2),
        name="ada",
    )(*_hbm(cond, w_ada, b_ada.reshape(depth, 1, 6 * D_MODEL)))


def _rope_apply(x, cos, sin_signed):
    lane = lax.broadcasted_iota(I32, (1, LANES), 1)
    first = (lane % 16) < 8
    outs = []
    for t in range(x.shape[1] // LANES):
        xt = x[:, t * LANES:(t + 1) * LANES]
        partner = jnp.where(first, pltpu.roll(xt, LANES - 8, axis=1), pltpu.roll(xt, 8, axis=1))
        outs.append(xt * cos[:, t * LANES:(t + 1) * LANES] + partner * sin_signed[:, t * LANES:(t + 1) * LANES])
    return jnp.concatenate(outs, axis=1)


def _proj_in_kernel(*refs, rope):
    if rope:
        x_ref, mod_ref, g_ref, w_ref, cos_ref, sin_ref = refs[:6]
        outs = refs[6:13]
        wbf_ref = refs[13]
    else:
        x_ref, mod_ref, g_ref, w_ref = refs[:4]
        outs = refs[4:11]
        wbf_ref = refs[11]
    qa_ref, ka_ref, va_ref, qb_ref, kb_ref, vb_ref, u_ref = outs

    @pl.when((pl.program_id(0) == 0) & (pl.program_id(1) == 0))
    def _():
        _cast_rows(wbf_ref, w_ref, 128)

    h = _modulated_norm(x_ref[0], g_ref[...], mod_ref[0, 0:1, :], mod_ref[0, 1:2, :])
    z = _dot(h.astype(BF16), wbf_ref[...])
    i1, i2, i3 = NA_WIDTH, 2 * NA_WIDTH, 3 * NA_WIDTH
    i4, i5, i6 = i3 + DIFF_WIDTH, i3 + 2 * DIFF_WIDTH, i3 + 3 * DIFF_WIDTH
    qa_ref[0] = z[:, :i1]
    ka_ref[0] = z[:, i1:i2]
    va_ref[0] = z[:, i2:i3]
    qb = z[:, i3:i4]
    kb = z[:, i4:i5]
    if rope:
        qb = _rope_apply(qb, cos_ref[...], sin_ref[...])
        kb = _rope_apply(kb, cos_ref[...], sin_ref[...])
    qb_ref[0] = qb
    kb_ref[0] = kb
    vb_ref[0] = z[:, i5:i6]
    for hf in range(SSM_WIDTH // LANES):
        u_ref[0, hf] = z[:, i6 + hf * LANES:i6 + (hf + 1) * LANES]


def _proj_in(x, mod, g_norm, w_in, layer, rope_tabs):
    b, n, d = x.shape
    tm = ROW_TILE
    bm = mod.shape[0]
    rope = rope_tabs is not None
    mod_map = (lambda i, j: (i, 0, 0)) if bm > 1 else (lambda i, j: (0, 0, 0))
    in_specs = [
        pl.BlockSpec((1, tm, d), lambda i, j: (i, j, 0)),
        pl.BlockSpec((1, 6, d), mod_map),
        pl.BlockSpec((None, 1, d), lambda i, j: (layer, 0, 0)),
        pl.BlockSpec((None, d, IN_WIDTH), lambda i, j: (layer, 0, 0), pipeline_mode=pl.Buffered(1)),
    ]
    args = [x, mod, g_norm, w_in]
    if rope:
        in_specs += [pl.BlockSpec((tm, DIFF_WIDTH), lambda i, j: (j, 0))] * 2
        args += list(rope_tabs)
    widths = (NA_WIDTH, NA_WIDTH, NA_WIDTH, DIFF_WIDTH, DIFF_WIDTH, DIFF_WIDTH, SSM_WIDTH)
    vm = (_nbytes((d, IN_WIDTH), F32) + _nbytes((d, IN_WIDTH), BF16)
          + 4 * _nbytes((tm, d), F32) + 6 * _nbytes((tm, IN_WIDTH), F32) + (4 << 20))
    nslab = SSM_WIDTH // LANES
    return pl.pallas_call(
        functools.partial(_proj_in_kernel, rope=rope),
        out_shape=([jax.ShapeDtypeStruct((b, n, w), F32) for w in widths[:-1]]
                   + [jax.ShapeDtypeStruct((b, nslab, n, LANES), F32)]),
        grid=(b, n // tm),
        in_specs=in_specs,
        out_specs=([pl.BlockSpec((1, tm, w), lambda i, j: (i, j, 0)) for w in widths[:-1]]
                   + [pl.BlockSpec((1, nslab, tm, LANES), lambda i, j: (i, 0, j, 0))]),
        scratch_shapes=[pltpu.VMEM((d, IN_WIDTH), BF16)],
        compiler_params=_params(vm, 2),
        name="proj_in",
    )(*_hbm(*args))


def _softmax_av(qm, ks, vs, scale, biases):
    ss = []
    for k, bias in zip(ks, biases):
        s = _dot_nt(qm, k) * scale
        if bias is not None:
            s = bias(s)
        ss.append(s)
    m = ss[0].max(axis=-1, keepdims=True)
    for s in ss[1:]:
        m = jnp.maximum(m, s.max(axis=-1, keepdims=True))
    acc = None
    den = None
    for s, v in zip(ss, vs):
        p = jnp.exp(s - m)
        den = p.sum(axis=-1, keepdims=True) if den is None else den + p.sum(axis=-1, keepdims=True)
        pv = _dot(p.astype(BF16), v)
        acc = pv if acc is None else acc + pv
    return acc / den


def _diff_lambda(lam_ref, lam_init):
    lp = lam_ref[...]
    a = jnp.sum(lp[0:1] * lp[1:2], axis=-1, keepdims=True)
    b = jnp.sum(lp[2:3] * lp[3:4], axis=-1, keepdims=True)
    return jnp.exp(a) - jnp.exp(b) + lam_init


def _attn_kernel(*refs, has_ctx, diff, lam_init):
    refs = list(refs)
    q_ref, k_ref, v_ref = refs[:3]
    pos = 3
    if has_ctx:
        kc_ref, vc_ref = refs[pos:pos + 2]
        pos += 2
    if diff:
        lam_ref, subln_ref = refs[pos:pos + 2]
        pos += 2
    o_ref = refs[pos]
    width = q_ref.shape[2]
    lane = lax.broadcasted_iota(I32, (1, LANES), 1)
    if diff:
        lam = _diff_lambda(lam_ref, lam_init)
        scale = DIFF_QK ** -0.5
    else:
        scale = HEAD_DIM ** -0.5
    for p in range(width // LANES):
        sl = slice(p * LANES, (p + 1) * LANES)
        qp = q_ref[0, :, sl]
        ks = [k_ref[0, :, sl].astype(BF16)]
        vs = [v_ref[0, :, sl].astype(BF16)]
        if has_ctx:
            ks.append(kc_ref[:, sl].astype(BF16))
            vs.append(vc_ref[:, sl].astype(BF16))
        nob = [None] * len(ks)
        pair = None
        for sub in range(LANES // HEAD_DIM):
            head_mask = (lane // HEAD_DIM) == sub
            if diff:
                os_ = []
                for half in range(2):
                    qmask = (lane // DIFF_QK) == (2 * sub + half)
                    qm = jnp.where(qmask, qp, 0.0).astype(BF16)
                    os_.append(_softmax_av(qm, ks, vs, scale, nob))
                o = jnp.where(head_mask, os_[0] - lam * os_[1], 0.0)
                ms = jnp.sum(o * o, axis=-1, keepdims=True) * (1.0 / DIFF_V)
                o = (o * lax.rsqrt(ms + EPS) * subln_ref[...]) * (1.0 - lam_init)
            else:
                qm = jnp.where(head_mask, qp, 0.0).astype(BF16)
                o = jnp.where(head_mask, _softmax_av(qm, ks, vs, scale, nob), 0.0)
            pair = o if pair is None else pair + o
        o_ref[0, :, sl] = pair


def _attention(q, k, v, ctx_kv, layer, diff_params, lam_init):
    b, nq, w = q.shape
    nk = k.shape[1]
    tq = ROW_TILE
    has_ctx = ctx_kv is not None
    diff = diff_params is not None
    in_specs = [
        pl.BlockSpec((1, tq, w), lambda i, j: (i, j, 0)),
        pl.BlockSpec((1, nk, w), lambda i, j: (i, 0, 0)),
        pl.BlockSpec((1, nk, w), lambda i, j: (i, 0, 0)),
    ]
    args = [q, k, v]
    nc = 0
    if has_ctx:
        nc = ctx_kv[0].shape[2]
        in_specs += [pl.BlockSpec((None, None, nc, w), lambda i, j: (i, layer, 0, 0))] * 2
        args += list(ctx_kv)
    if diff:
        lam_p, subln = diff_params
        in_specs += [
            pl.BlockSpec((None, 4, DIFF_QK), lambda i, j: (layer, 0, 0)),
            pl.BlockSpec((None, 1, LANES), lambda i, j: (layer, 0, 0)),
        ]
        args += [lam_p, subln]
    vm = (4 * _nbytes((nk + nc, w), F32) + 2 * _nbytes((nk + nc, LANES), BF16)
          + 8 * _nbytes((tq, nk + nc), F32) + 8 * _nbytes((tq, w), F32) + (4 << 20))
    return pl.pallas_call(
        functools.partial(_attn_kernel, has_ctx=has_ctx, diff=diff, lam_init=lam_init),
        out_shape=jax.ShapeDtypeStruct((b, nq, w), F32),
        grid=(b, nq // tq),
        in_specs=in_specs,
        out_specs=pl.BlockSpec((1, tq, w), lambda i, j: (i, j, 0)),
        compiler_params=_params(vm, 2),
        name="attn_diff" if diff else "attn",
    )(*_hbm(*args))


def _na_kernel(q_ref, k_ref, v_ref, kc_ref, vc_ref, bias_ref, o_ref, *, rows):
    i = pl.program_id(1)
    tq = NA_QROWS * GRID_W
    nkw = NA_KROWS * GRID_W
    ustart = jnp.clip(NA_QROWS * i - WIN_R // 2, 0, rows - NA_KROWS)
    koff = pl.multiple_of(ustart * GRID_W, GRID_W)
    delta = ustart - NA_QROWS * i
    qrow = NA_QROWS * i + lax.broadcasted_iota(I32, (tq, nkw), 0) // GRID_W
    krow = ustart + lax.broadcasted_iota(I32, (tq, nkw), 1) // GRID_W
    wstart = jnp.clip(qrow - WIN_R // 2, 0, rows - WIN_R)
    row_ok = (krow >= wstart) & (krow < wstart + WIN_R)
    lane = lax.broadcasted_iota(I32, (1, LANES), 1)
    scale = HEAD_DIM ** -0.5
    for p in range(NA_WIDTH // LANES):
        sl = slice(p * LANES, (p + 1) * LANES)
        qp = q_ref[0, :, sl]
        ks = [k_ref[0, pl.ds(koff, nkw), sl].astype(BF16), kc_ref[:, sl].astype(BF16)]
        vs = [v_ref[0, pl.ds(koff, nkw), sl].astype(BF16), vc_ref[:, sl].astype(BF16)]
        pair = None
        for sub in range(LANES // HEAD_DIM):
            h = p * (LANES // HEAD_DIM) + sub
            head_mask = (lane // HEAD_DIM) == sub

            def win_bias(s, h=h):
                rows_ = []
                for ri in range(NA_QROWS):
                    tiles = []
                    for pj in range(NA_KROWS // 2):
                        dr = delta + 2 * pj - ri
                        idx = jnp.clip(dr, -WIN_R, WIN_R - 1) + WIN_R
                        tiles.append(bias_ref[h, idx])
                    rows_.append(jnp.concatenate(tiles, axis=1))
                bias = jnp.concatenate(rows_, axis=0)
                return jnp.where(row_ok, s + bias, NEG_INF)

            qm = jnp.where(head_mask, qp, 0.0).astype(BF16)
            o = _softmax_av(qm, ks, vs, scale, [win_bias, None])
            o = jnp.where(head_mask, o, 0.0)
            pair = o if pair is None else pair + o
        o_ref[0, :, sl] = pair


def _na_bias_table(rpb):
    cc = jnp.arange(GRID_W)
    col_start = jnp.clip(cc - WIN_C // 2, 0, GRID_W - WIN_C)
    col_ok = (cc[None, :] >= col_start[:, None]) & (cc[None, :] < col_start[:, None] + WIN_C)
    rpb = rpb.astype(F32)
    lead = rpb.shape[:2]
    lo = GRID_W - WIN_C
    w = jnp.concatenate([jnp.broadcast_to(rpb[..., :1], lead + (lo,)), rpb,
                         jnp.broadcast_to(rpb[..., -1:], lead + (2 * GRID_W - lo - 2 * WIN_C + 1,))], axis=-1)
    flat = jnp.tile(w, (1, 1, GRID_W))
    t = flat[..., GRID_W - 1:GRID_W - 1 + GRID_W * (2 * GRID_W - 1)].reshape(lead + (GRID_W, 2 * GRID_W - 1))
    t = jnp.where(col_ok[None, None], t[..., :GRID_W], NEG_INF)
    zero = jnp.zeros_like(t[:, :1])
    t = jnp.concatenate([zero, t, zero], axis=1)
    return jnp.concatenate([t[:, :-1], t[:, 1:]], axis=-1)


def _neighbourhood_attention(q, k, v, kc, vc, bias_tab, layer):
    b, n, w = q.shape
    rows = n // GRID_W
    tq = NA_QROWS * GRID_W
    nc = kc.shape[2]
    nkw = NA_KROWS * GRID_W
    vm = (4 * _nbytes((n, w), F32) + 4 * _nbytes((nc, w), F32) + 2 * _nbytes(bias_tab.shape, F32)
          + 12 * _nbytes((tq, nkw + nc), F32) + (6 << 20))
    return pl.pallas_call(
        functools.partial(_na_kernel, rows=rows),
        out_shape=jax.ShapeDtypeStruct((b, n, w), F32),
        grid=(b, n // tq),
        in_specs=[
            pl.BlockSpec((1, tq, w), lambda i, j: (i, j, 0)),
            pl.BlockSpec((1, n, w), lambda i, j: (i, 0, 0)),
            pl.BlockSpec((1, n, w), lambda i, j: (i, 0, 0)),
            pl.BlockSpec((None, None, nc, w), lambda i, j: (i, layer, 0, 0)),
            pl.BlockSpec((None, None, nc, w), lambda i, j: (i, layer, 0, 0)),
            pl.BlockSpec(bias_tab.shape, lambda i, j: (0, 0, 0, 0)),
        ],
        out_specs=pl.BlockSpec((1, tq, w), lambda i, j: (i, j, 0)),
        compiler_params=_params(vm, 2),
        name="na_attn",
    )(*_hbm(q, k, v, kc, vc, bias_tab))


def _sublane_scan(er, ei, tr, ti, pw_ref, base, lanes, reverse):
    row = lax.broadcasted_iota(I32, (SUBLANES, LANES), 0)
    if reverse:
        first = row == SUBLANES - 1
        xr = jnp.where(first, tr, pltpu.roll(er, SUBLANES - 1, axis=0))
        xi = jnp.where(first, ti, pltpu.roll(ei, SUBLANES - 1, axis=0))
    else:
        first = row == 0
        xr = jnp.where(first, tr, pltpu.roll(er, 1, axis=0))
        xi = jnp.where(first, ti, pltpu.roll(ei, 1, axis=0))
    for k, s in enumerate((1, 2, 4)):
        ar = pw_ref[base + 2 * k:base + 2 * k + 1, lanes]
        ai = pw_ref[base + 2 * k + 1:base + 2 * k + 2, lanes]
        if reverse:
            keep = row < SUBLANES - s
            sr = jnp.where(keep, pltpu.roll(xr, SUBLANES - s, axis=0), 0.0)
            si = jnp.where(keep, pltpu.roll(xi, SUBLANES - s, axis=0), 0.0)
        else:
            keep = row >= s
            sr = jnp.where(keep, pltpu.roll(xr, s, axis=0), 0.0)
            si = jnp.where(keep, pltpu.roll(xi, s, axis=0), 0.0)
        xr, xi = xr + (ar * sr - ai * si), xi + (ar * si + ai * sr)
    return xr, xi


def _s5_kernel(u_ref, h0re_ref, h0im_ref, bmat_ref, cmat_ref, a1_ref, ach_ref, pfix_ref, d_ref, wglu_ref, bglu_ref,
               o_ref, fre_ref, fim_ref, up_ref, yp_ref, buf_ref, car_ref):
    seq = u_ref.shape[2]
    tt = SCAN_TILE
    ch = tt // SUBLANES
    ntile = seq // tt
    nlt = SSM_N // LANES
    nhalf = SSM_WIDTH // LANES

    def permute_in(t, _):
        t0 = pl.multiple_of(t * tt, tt)
        for j in range(0, ch, 2):
            for hf in range(nhalf):
                lanes = slice(hf * LANES, (hf + 1) * LANES)
                rows = jnp.concatenate([u_ref[0, hf, pl.ds(t0 + j + jj, SUBLANES, stride=ch), :] for jj in range(2)],
                                       axis=0)
                r0 = pl.multiple_of(t0 + j * SUBLANES, 2 * SUBLANES)
                up_ref[pl.ds(r0, 2 * SUBLANES), lanes] = rows.astype(BF16)
                yp_ref[pl.ds(r0, 2 * SUBLANES), lanes] = rows * d_ref[:, lanes]
        return 0

    lax.fori_loop(0, ntile, permute_in, 0)

    for d in range(2):
        for lt in range(nlt):
            lanes = slice(lt * LANES, (lt + 1) * LANES)
            car_ref[d, 0, :, lanes] = jnp.broadcast_to(h0re_ref[0, d:d + 1, lanes], (SUBLANES, LANES))
            car_ref[d, 1, :, lanes] = jnp.broadcast_to(h0im_ref[0, d:d + 1, lanes], (SUBLANES, LANES))

    def tile(t, _):
        starts = []
        for d in range(2):
            tix = (ntile - 1 - t) if d == 1 else t
            t0 = pl.multiple_of(tix * tt, tt)
            starts.append(t0)
            buf_ref[d] = _dot(up_ref[pl.ds(t0, tt), :], bmat_ref[:, d * 2 * SSM_N:(d + 1) * 2 * SSM_N])
        for d in range(2):
            reverse = d == 1
            order = range(ch - 1, -1, -1) if reverse else range(ch)
            for lt in range(nlt):
                lre = slice(lt * LANES, (lt + 1) * LANES)
                lim = slice(SSM_N + lt * LANES, SSM_N + (lt + 1) * LANES)
                ar = a1_ref[2 * d:2 * d + 1, lre]
                ai = a1_ref[2 * d + 1:2 * d + 2, lre]
                cr = jnp.zeros((SUBLANES, LANES), F32)
                ci = jnp.zeros((SUBLANES, LANES), F32)
                for j in order:
                    rows = slice(j * SUBLANES, (j + 1) * SUBLANES)
                    cr, ci = (ar * cr - ai * ci) + buf_ref[d, rows, lre], (ar * ci + ai * cr) + buf_ref[d, rows, lim]
                    buf_ref[d, rows, lre] = cr
                    buf_ref[d, rows, lim] = ci
                tr, ti = car_ref[d, 0, :, lre], car_ref[d, 1, :, lre]
                gr, gi = _sublane_scan(cr, ci, tr, ti, ach_ref, 6 * d, lre, reverse)
                a_r, a_i = ach_ref[6 * d:6 * d + 1, lre], ach_ref[6 * d + 1:6 * d + 2, lre]
                nr, ni = (a_r * gr - a_i * gi) + cr, (a_r * gi + a_i * gr) + ci
                edge = 0 if reverse else SUBLANES - 1
                car_ref[d, 0, :, lre] = jnp.broadcast_to(nr[edge:edge + 1, :], (SUBLANES, LANES))
                car_ref[d, 1, :, lre] = jnp.broadcast_to(ni[edge:edge + 1, :], (SUBLANES, LANES))
                for j in range(ch):
                    rows = slice(j * SUBLANES, (j + 1) * SUBLANES)
                    pr = pfix_ref[2 * d, j:j + 1, lre]
                    pi = pfix_ref[2 * d + 1, j:j + 1, lre]
                    buf_ref[d, rows, lre] += pr * gr - pi * gi
                    buf_ref[d, rows, lim] += pr * gi + pi * gr
        for d in range(2):
            yp_ref[pl.ds(starts[d], tt), :] += _dot(buf_ref[d].astype(BF16),
                                                    cmat_ref[d * 2 * SSM_N:(d + 1) * 2 * SSM_N, :])
        return 0

    lax.fori_loop(0, ntile, tile, 0)

    for d in range(2):
        fre_ref[0, d:d + 1, :] = car_ref[d, 0, 0:1, :]
        fim_ref[0, d:d + 1, :] = car_ref[d, 1, 0:1, :]

    wglu = wglu_ref[...].astype(BF16)

    def glu(t, _):
        t0 = pl.multiple_of(t * tt, tt)
        g = jax.nn.gelu(yp_ref[pl.ds(t0, tt), :])
        out = g * jax.nn.sigmoid(_dot(g.astype(BF16), wglu) + bglu_ref[...])
        for j in range(ch):
            for hf in range(nhalf):
                o_ref[0, hf, pl.ds(t0 + j, SUBLANES, stride=ch), :] = out[j * SUBLANES:(j + 1) * SUBLANES,
                                                                          hf * LANES:(hf + 1) * LANES]
        return 0

    lax.fori_loop(0, ntile, glu, 0)


def _cmul(ar, ai, br, bi):
    return ar * br - ai * bi, ar * bi + ai * br


def _s5_tables(a_re, a_im, log_dt, b_re, b_im, c_re, c_im):
    depth = a_re.shape[0]
    g, p, hc = SSM_GROUPS, SSM_STATE, SSM_GROUP_CH
    ch = SCAN_TILE // SUBLANES
    eye = jnp.eye(g, dtype=F32)
    lr, li = a_re.astype(F32), a_im.astype(F32)
    dt = jnp.exp(log_dt.astype(F32))[..., None]
    mag = jnp.exp(lr * dt)
    ar, ai = mag * jnp.cos(li * dt), mag * jnp.sin(li * dt)
    den = lr * lr + li * li
    qr = ((ar - 1.0) * lr + ai * li) / den
    qi = (ai * lr - (ar - 1.0) * li) / den
    br, bi = _cmul(qr[..., None], qi[..., None], b_re.astype(F32), b_im.astype(F32))
    bparts = jnp.stack([br, bi], axis=2)
    bmat = bparts.transpose(0, 3, 5, 1, 2, 4)[:, :, :, :, :, None, :] * eye[None, :, None, None, None, :, None]
    bmat = bmat.reshape(depth, g * hc, 4 * g * p).astype(BF16)
    cparts = jnp.stack([c_re.astype(F32), -c_im.astype(F32)], axis=2)
    cmat = cparts.transpose(0, 1, 2, 3, 5, 4)[:, :, :, :, :, None, :] * eye[None, None, None, :, None, :, None]
    cmat = cmat.reshape(depth, 4 * g * p, g * hc).astype(BF16)

    def power(k):
        kk = k[None, None, :, None, None]
        m = jnp.exp(kk * (lr * dt)[:, :, None])
        th = kk * (li * dt)[:, :, None]
        return (m * jnp.cos(th)).reshape(depth, 2, -1, g * p), (m * jnp.sin(th)).reshape(depth, 2, -1, g * p)

    a1 = jnp.stack([ar.reshape(depth, 2, g * p), ai.reshape(depth, 2, g * p)], axis=2).reshape(depth, 4, g * p)
    cr_, ci_ = power(jnp.asarray([ch, 2 * ch, 4 * ch], F32))
    ach = jnp.stack([cr_, ci_], axis=3).reshape(depth, 12, g * p)
    fr, fi = power(jnp.arange(1, ch + 1, dtype=F32))
    fr = jnp.stack([fr[:, 0], fr[:, 1, ::-1]], axis=1)
    fi = jnp.stack([fi[:, 0], fi[:, 1, ::-1]], axis=1)
    pfix = jnp.stack([fr, fi], axis=2).reshape(depth, 4, ch, g * p)
    return bmat, cmat, a1, ach, pfix


def _s5(u, h0re, h0im, tabs, d_skip, w_glu, b_glu, layer):
    b, nh, seq, _ = u.shape
    w = nh * LANES
    lay = lambda a: pl.BlockSpec((None,) + a.shape[1:], lambda i: (layer,) + (0,) * (a.ndim - 1))
    return pl.pallas_call(
        _s5_kernel,
        out_shape=[jax.ShapeDtypeStruct((b, nh, seq, LANES), F32),
                   jax.ShapeDtypeStruct((b, 2, SSM_N), F32),
                   jax.ShapeDtypeStruct((b, 2, SSM_N), F32)],
        grid=(b,),
        in_specs=[
            pl.BlockSpec((1, nh, seq, LANES), lambda i: (i, 0, 0, 0)),
            pl.BlockSpec((1, 2, SSM_N), lambda i: (i, 0, 0)),
            pl.BlockSpec((1, 2, SSM_N), lambda i: (i, 0, 0)),
            lay(tabs[0]), lay(tabs[1]), lay(tabs[2]), lay(tabs[3]), lay(tabs[4]),
            pl.BlockSpec((None, 1, w), lambda i: (layer, 0, 0)),
            pl.BlockSpec((None, w, w), lambda i: (layer, 0, 0)),
            pl.BlockSpec((None, 1, w), lambda i: (layer, 0, 0)),
        ],
        out_specs=[pl.BlockSpec((1, nh, seq, LANES), lambda i: (i, 0, 0, 0)),
                   pl.BlockSpec((1, 2, SSM_N), lambda i: (i, 0, 0)),
                   pl.BlockSpec((1, 2, SSM_N), lambda i: (i, 0, 0))],
        scratch_shapes=[pltpu.VMEM((seq, w), BF16),
                        pltpu.VMEM((seq, w), F32),
                        pltpu.VMEM((2, SCAN_TILE, 2 * SSM_N), F32),
                        pltpu.VMEM((2, 2, SUBLANES, SSM_N), F32)],
        compiler_params=_params(0, 1),
        name="s5",
    )(u, h0re, h0im, *tabs, d_skip, w_glu, b_glu)


def _proj_out_kernel(oa_ref, ob_ref, oc_ref, w_ref, x_ref, mod_ref, g_ref, wr_ref, o_ref, h_ref, lg_ref, wbf_ref):
    @pl.when((pl.program_id(0) == 0) & (pl.program_id(1) == 0))
    def _():
        _cast_rows(wbf_ref, w_ref, 128)

    i1, i2 = NA_WIDTH, NA_WIDTH + DIFF_WIDTH
    y = _dot(oa_ref[0].astype(BF16), wbf_ref[:i1, :])
    y += _dot(ob_ref[0].astype(BF16), wbf_ref[i1:i2, :])
    for hf in range(SSM_WIDTH // LANES):
        y += _dot(oc_ref[0, hf].astype(BF16), wbf_ref[i2 + hf * LANES:i2 + (hf + 1) * LANES, :])
    xn = x_ref[0] + mod_ref[0, 2:3, :] * y
    o_ref[0] = xn
    h = _modulated_norm(xn, g_ref[...], mod_ref[0, 3:4, :], mod_ref[0, 4:5, :])
    h_hi = h.astype(BF16)
    h_ref[0] = h_hi
    h_lo = (h - h_hi.astype(F32)).astype(BF16)
    t1 = _dot_nt(wr_ref[...], h_hi)
    t2 = _dot_nt(wr_ref[:N_EXPERTS, :], h_lo)
    lg_ref[0] = t1[:N_EXPERTS, :] + t1[N_EXPERTS:, :] + t2


def _proj_out(oa, ob, oc, w_out, x, mod, g_ffn, wr_t, layer):
    b, n, d = x.shape
    tm = ROW_TILE
    bm = mod.shape[0]
    mod_map = (lambda i, j: (i, 0, 0)) if bm > 1 else (lambda i, j: (0, 0, 0))
    kw = w_out.shape[1]
    vm = _nbytes((kw, d), F32) + _nbytes((kw, d), BF16) + 10 * _nbytes((tm, d), F32) + (4 << 20)
    return pl.pallas_call(
        _proj_out_kernel,
        out_shape=[jax.ShapeDtypeStruct((b, n, d), F32),
                   jax.ShapeDtypeStruct((b, n, d), BF16),
                   jax.ShapeDtypeStruct((b, N_EXPERTS, n), F32)],
        grid=(b, n // tm),
        in_specs=[
            pl.BlockSpec((1, tm, NA_WIDTH), lambda i, j: (i, j, 0)),
            pl.BlockSpec((1, tm, DIFF_WIDTH), lambda i, j: (i, j, 0)),
            pl.BlockSpec((1, SSM_WIDTH // LANES, tm, LANES), lambda i, j: (i, 0, j, 0)),
            pl.BlockSpec((None, kw, d), lambda i, j: (layer, 0, 0), pipeline_mode=pl.Buffered(1)),
            pl.BlockSpec((1, tm, d), lambda i, j: (i, j, 0)),
            pl.BlockSpec((1, 6, d), mod_map),
            pl.BlockSpec((None, 1, d), lambda i, j: (layer, 0, 0)),
            pl.BlockSpec((None, 2 * N_EXPERTS, d), lambda i, j: (layer, 0, 0)),
        ],
        out_specs=[pl.BlockSpec((1, tm, d), lambda i, j: (i, j, 0)),
                   pl.BlockSpec((1, tm, d), lambda i, j: (i, j, 0)),
                   pl.BlockSpec((1, N_EXPERTS, tm), lambda i, j: (i, 0, j))],
        scratch_shapes=[pltpu.VMEM((kw, d), BF16)],
        compiler_params=_params(vm, 2),
        name="proj_out",
    )(*_hbm(oa, ob, oc, w_out, x, mod, g_ffn, wr_t))


def _excl_cumsum_lanes(m):
    e, n = m.shape
    blk = MXU_DIM
    nb = n // blk
    r = lax.broadcasted_iota(I32, (blk, blk), 0)
    c = lax.broadcasted_iota(I32, (blk, blk), 1)
    tri = jnp.where(r < c, 1.0, 0.0).astype(BF16)
    stacked = jnp.concatenate([m[:, k * blk:(k + 1) * blk] for k in range(nb)], axis=0).astype(BF16)
    within = _dot(stacked, tri)
    outs = []
    off = jnp.zeros((e, 1), F32)
    for k in range(nb):
        outs.append(within[k * e:(k + 1) * e, :] + off)
        off = off + jnp.sum(m[:, k * blk:(k + 1) * blk], axis=1, keepdims=True)
    return jnp.concatenate(outs, axis=1)


def _route_kernel(lg_ref, pos_ref, gate_ref, *, cap):
    b, e, n = lg_ref.shape
    lg = lg_ref[...]
    mx = lg.max(axis=1, keepdims=True)
    ex = jnp.exp(lg - mx)
    aff = (ex / ex.sum(axis=1, keepdims=True)).reshape(b * e, n)

    def search(it, cur):
        cand = cur | (jnp.int32(1) << (jnp.int32(30) - it))
        cnt = jnp.sum(jnp.where(aff >= pltpu.bitcast(cand, F32), 1.0, 0.0), axis=1, keepdims=True)
        return jnp.where(cnt >= cap, cand, cur)

    thr = pltpu.bitcast(lax.fori_loop(0, 31, search, jnp.zeros((b * e, 1), I32)), F32)
    gt = aff > thr
    eq = aff == thr
    need = cap - jnp.sum(jnp.where(gt, 1.0, 0.0), axis=1, keepdims=True)
    eq_rank = _excl_cumsum_lanes(jnp.where(eq, 1.0, 0.0))
    sel = gt | (eq & (eq_rank < need))
    slot = _excl_cumsum_lanes(jnp.where(sel, 1.0, 0.0))
    pos = jnp.where(sel, slot, -1.0).astype(I32)
    gate = jnp.where(sel, aff, 0.0)
    for i in range(b):
        pos_ref[:, i, 0, :] = pos[i * e:(i + 1) * e, :]
        gate_ref[:, i, 0, :] = gate[i * e:(i + 1) * e, :]


def _route(logits):
    b, e, n = logits.shape
    cap = EC_CAPACITY_FACTOR * n // N_EXPERTS
    return pl.pallas_call(
        functools.partial(_route_kernel, cap=cap),
        out_shape=[jax.ShapeDtypeStruct((e, b, 1, n), I32),
                   jax.ShapeDtypeStruct((e, b, 1, n), F32)],
        compiler_params=_params(0, 0),
        name="route",
    )(logits)


def _one_hot_rows(pos_row, cap):
    n = pos_row.shape[1]
    slot = lax.broadcasted_iota(I32, (cap, n), 0)
    return slot == pos_row


def _gather_kernel(h_ref, pos_ref, gate_ref, xs_ref, gs_ref, *, cap):
    h = h_ref[0]
    for e in range(N_EXPERTS):
        oh = _one_hot_rows(pos_ref[e, 0], cap)
        xs_ref[e] = _dot(jnp.where(oh, 1.0, 0.0).astype(BF16), h).astype(BF16)
        g = jnp.sum(jnp.where(oh, gate_ref[e, 0], 0.0), axis=1, keepdims=True)
        gs_ref[e] = jnp.broadcast_to(g, (cap, LANES))


def _gather(h, pos, gate):
    b, n, d = h.shape
    cap = EC_CAPACITY_FACTOR * n // N_EXPERTS
    vm = 4 * _nbytes((n, d), BF16) + 6 * _nbytes((N_EXPERTS, cap, d), BF16) + 8 * _nbytes((cap, n), F32) + (8 << 20)
    return pl.pallas_call(
        functools.partial(_gather_kernel, cap=cap),
        out_shape=[jax.ShapeDtypeStruct((N_EXPERTS, b * cap, d), BF16),
                   jax.ShapeDtypeStruct((N_EXPERTS, b * cap, LANES), F32)],
        grid=(b,),
        in_specs=[
            pl.BlockSpec((1, n, d), lambda i: (i, 0, 0)),
            pl.BlockSpec((N_EXPERTS, 1, 1, n), lambda i: (0, i, 0, 0)),
            pl.BlockSpec((N_EXPERTS, 1, 1, n), lambda i: (0, i, 0, 0)),
        ],
        out_specs=[pl.BlockSpec((N_EXPERTS, cap, d), lambda i: (0, i, 0)),
                   pl.BlockSpec((N_EXPERTS, cap, LANES), lambda i: (0, i, 0))],
        compiler_params=_params(vm, 1),
        name="moe_gather",
    )(*_hbm(h, pos, gate))


def _ffn_kernel(xp_ref, gp_ref, xs_ref, gs_ref, wg_ref, wu_ref, wd_ref, yp_ref, ys_ref, acc_ref):
    j = pl.program_id(1)
    sp = xp_ref.shape[1]

    @pl.when(j == 0)
    def _():
        acc_ref[...] = jnp.zeros_like(acc_ref)

    xs = jnp.concatenate([xp_ref[0], xs_ref[0]], axis=0)
    a = _dot(xs, wg_ref[...].astype(BF16))
    u = _dot(xs, wu_ref[...].astype(BF16))
    hid = (a * jax.nn.sigmoid(a)) * u
    acc_ref[...] += _dot(hid.astype(BF16), wd_ref[...].astype(BF16))

    @pl.when(j == pl.num_programs(1) - 1)
    def _():
        yp_ref[0] = (acc_ref[:sp, :] * gp_ref[0][:, 0:1]).astype(BF16)
        ys_ref[0] = (acc_ref[sp:, :] * gs_ref[0][:, 0:1]).astype(BF16)


def _ffn(xp, gp, xs, gs, w_gate, w_up, w_down, layer):
    e, sp, d = xp.shape
    ss = xs.shape[1]
    s = sp + ss
    ff = w_gate.shape[-1]
    tf = FF_TILE
    vm = (6 * _nbytes((d, tf), F32) + 3 * _nbytes((d, tf), BF16) + 5 * _nbytes((s, d), BF16)
          + 2 * _nbytes((s, LANES), F32) + _nbytes((s, d), F32) + 4 * _nbytes((s, tf), F32) + (4 << 20))
    return pl.pallas_call(
        _ffn_kernel,
        out_shape=[jax.ShapeDtypeStruct((e, sp, d), BF16), jax.ShapeDtypeStruct((e, ss, d), BF16)],
        grid=(e, ff // tf),
        in_specs=[
            pl.BlockSpec((1, sp, d), lambda i, j: (i, 0, 0)),
            pl.BlockSpec((1, sp, LANES), lambda i, j: (i, 0, 0)),
            pl.BlockSpec((1, ss, d), lambda i, j: (i, 0, 0)),
            pl.BlockSpec((1, ss, LANES), lambda i, j: (i, 0, 0)),
            pl.BlockSpec((None, None, d, tf), lambda i, j: (layer, i, 0, j)),
            pl.BlockSpec((None, None, d, tf), lambda i, j: (layer, i, 0, j)),
            pl.BlockSpec((None, None, tf, d), lambda i, j: (layer, i, j, 0)),
        ],
        out_specs=[pl.BlockSpec((1, sp, d), lambda i, j: (i, 0, 0)),
                   pl.BlockSpec((1, ss, d), lambda i, j: (i, 0, 0))],
        scratch_shapes=[pltpu.VMEM((s, d), F32)],
        compiler_params=_params(vm, 2),
        name="moe_ffn",
    )(*_hbm(xp, gp, xs, gs, w_gate, w_up, w_down))


def _combine_kernel(ye_ref, pos_ref, x_ref, mod_ref, *rest, cap, final):
    if final:
        g_ref, o_ref = rest
    else:
        (o_ref,) = rest
    group = max(1, MXU_DIM // cap)
    y = None
    for e0 in range(0, N_EXPERTS, group):
        oh = jnp.concatenate([jnp.where(_one_hot_rows(pos_ref[e, 0], cap), 1.0, 0.0).astype(BF16)
                              for e in range(e0, e0 + group)], axis=0)
        ye = jnp.concatenate([ye_ref[e] for e in range(e0, e0 + group)], axis=0)
        part = _dot_tn(oh, ye)
        y = part if y is None else y + part
    xn = x_ref[0] + mod_ref[0, 5:6, :] * y
    if final:
        ms = jnp.mean(xn * xn, axis=-1, keepdims=True)
        xn = xn * lax.rsqrt(ms + EPS) * g_ref[...]
    o_ref[0] = xn


def _combine(ye, pos, x, mod, final_norm):
    b, n, d = x.shape
    cap = EC_CAPACITY_FACTOR * n // N_EXPERTS
    tn = min(n, COMBINE_TILE)
    bm = mod.shape[0]
    mod_map = (lambda i, j: (i, 0, 0)) if bm > 1 else (lambda i, j: (0, 0, 0))
    final = final_norm is not None
    in_specs = [
        pl.BlockSpec((N_EXPERTS, cap, d), lambda i, j: (0, i, 0)),
        pl.BlockSpec((N_EXPERTS, 1, 1, tn), lambda i, j: (0, i, 0, j)),
        pl.BlockSpec((1, tn, d), lambda i, j: (i, j, 0)),
        pl.BlockSpec((1, 6, d), mod_map),
    ]
    args = [ye, pos, x, mod]
    if final:
        in_specs.append(pl.BlockSpec((1, d), lambda i, j: (0, 0)))
        args.append(final_norm)
    vm = (4 * _nbytes((N_EXPERTS, cap, d), BF16) + 10 * _nbytes((tn, d), F32)
          + 6 * _nbytes((max(cap, MXU_DIM), tn), F32) + (8 << 20))
    return pl.pallas_call(
        functools.partial(_combine_kernel, cap=cap, final=final),
        out_shape=jax.ShapeDtypeStruct((b, n, d), F32),
        grid=(b, n // tn),
        in_specs=in_specs,
        out_specs=pl.BlockSpec((1, tn, d), lambda i, j: (i, j, 0)),
        compiler_params=_params(vm, 2),
        name="moe_combine",
    )(*_hbm(*args))


def _rope_tables(n):
    t = np.arange(n)
    row = (t // GRID_W).astype(np.float32)
    col = (t % GRID_W).astype(np.float32)
    nf = DIFF_QK // 4
    inv = np.float32(ROPE_BASE) ** (-np.arange(nf, dtype=np.float32) / np.float32(nf))
    lane = np.arange(DIFF_WIDTH)
    pos = np.where(((lane % DIFF_QK) < DIFF_QK // 2)[None, :], row[:, None], col[:, None])
    ang = (pos * inv[lane % nf][None, :]).astype(np.float32)
    first = (lane % (2 * nf)) < nf
    cos, sin = np.cos(ang).astype(np.float32), np.sin(ang).astype(np.float32)
    return jnp.asarray(cos), jnp.asarray(np.where(first[None, :], -sin, sin))


def kernel(x_prompt, x_sample, cache_na_k, cache_na_v, cache_diff_k, cache_diff_v, state_ssm_re, state_ssm_im,
           c, c_ctx, w_ada, b_ada, norm_mix, norm_ffn, w_in, w_out, na_rpb, diff_lambda, diff_subln,
           ssm_a_re, ssm_a_im, ssm_log_dt, ssm_b_re, ssm_b_im, ssm_c_re, ssm_c_im, ssm_d, ssm_w_glu, ssm_b_glu,
           w_router, w_gate, w_up, w_down, final_norm):
    depth = w_in.shape[0]
    bp, sp, d = x_prompt.shape
    bs, ss, _ = x_sample.shape
    assert d == D_MODEL and bs + 1 <= SUBLANES
    past = cache_na_k.shape[2]

    cond = jnp.zeros((SUBLANES, d), F32).at[0].set(c_ctx).at[1:1 + bs].set(c)
    mods = _ada(cond, w_ada, b_ada).reshape(depth, SUBLANES, 6, d)

    rope_tabs = _rope_tables(ss)
    kc_a = cache_na_k.reshape(bs, depth, past, NA_WIDTH)
    vc_a = cache_na_v.reshape(bs, depth, past, NA_WIDTH)
    kc_b = cache_diff_k.reshape(bs, depth, past, DIFF_WIDTH)
    vc_b = cache_diff_v.reshape(bs, depth, past, DIFF_WIDTH)
    subln = jnp.tile(diff_subln, (1, LANES // DIFF_V)).reshape(depth, 1, LANES)
    norm_mix = norm_mix.reshape(depth, 1, d)
    norm_ffn = norm_ffn.reshape(depth, 1, d)
    ssm_d = ssm_d.reshape(depth, 1, SSM_WIDTH)
    ssm_b_glu = ssm_b_glu.reshape(depth, 1, SSM_WIDTH)
    wr_t = jnp.swapaxes(w_router, 1, 2).astype(F32)
    wr_hi = wr_t.astype(BF16)
    wr_t = jnp.concatenate([wr_hi, (wr_t - wr_hi.astype(F32)).astype(BF16)], axis=1)
    fnorm = final_norm.reshape(1, d)
    zero_state = jnp.zeros((bp, 2, SSM_N), F32)
    tabs = _s5_tables(ssm_a_re, ssm_a_im, ssm_log_dt, ssm_b_re, ssm_b_im, ssm_c_re, ssm_c_im)

    xp, xs = x_prompt, x_sample
    new_ka, new_va, new_kb, new_vb, new_sre, new_sim = [], [], [], [], [], []
    for l in range(depth):
        lam_init = 0.8 - 0.6 * math.exp(-0.3 * l)
        mod_p = mods[l, 0:1]
        mod_s = mods[l, 1:1 + bs]
        bias_tab = _na_bias_table(na_rpb[l])
        diff_params = (diff_lambda, subln)

        qa, ka, va, qb, kb, vb, u = _proj_in(xp, mod_p, norm_mix, w_in, l, None)
        o_a = _attention(qa, ka, va, None, l, None, lam_init)
        o_b = _attention(qb, kb, vb, None, l, diff_params, lam_init)
        o_c, fre, fim = _s5(u, zero_state, zero_state, tabs, ssm_d, ssm_w_glu, ssm_b_glu, l)
        xp, hp, lg_p = _proj_out(o_a, o_b, o_c, w_out, xp, mod_p, norm_ffn, wr_t, l)
        new_ka.append(ka.reshape(bp, sp, NA_HEADS, HEAD_DIM))
        new_va.append(va.reshape(bp, sp, NA_HEADS, HEAD_DIM))
        new_kb.append(kb.reshape(bp, sp, DIFF_HEADS, DIFF_V))
        new_vb.append(vb.reshape(bp, sp, DIFF_HEADS, DIFF_V))
        new_sre.append(fre.reshape(bp, 2, SSM_GROUPS, SSM_STATE))
        new_sim.append(fim.reshape(bp, 2, SSM_GROUPS, SSM_STATE))

        qa, ka, va, qb, kb, vb, u = _proj_in(xs, mod_s, norm_mix, w_in, l, rope_tabs)
        o_a = _neighbourhood_attention(qa, ka, va, kc_a, vc_a, bias_tab, l)
        o_b = _attention(qb, kb, vb, (kc_b, vc_b), l, diff_params, lam_init)
        h0re = state_ssm_re[:, l].reshape(bs, 2, SSM_N)
        h0im = state_ssm_im[:, l].reshape(bs, 2, SSM_N)
        o_c, _, _ = _s5(u, h0re, h0im, tabs, ssm_d, ssm_w_glu, ssm_b_glu, l)
        xs, hs, lg_s = _proj_out(o_a, o_b, o_c, w_out, xs, mod_s, norm_ffn, wr_t, l)

        pos_p, gate_p = _route(lg_p)
        pos_s, gate_s = _route(lg_s)
        xg_p, gs_p = _gather(hp, pos_p, gate_p)
        xg_s, gs_s = _gather(hs, pos_s, gate_s)
        ye_p, ye_s = _ffn(xg_p, gs_p, xg_s, gs_s, w_gate, w_up, w_down, l)
        last = l == depth - 1
        xp = _combine(ye_p, pos_p, xp, mod_p, fnorm if last else None)
        xs = _combine(ye_s, pos_s, xs, mod_s, fnorm if last else None)

    return (xp, xs, jnp.stack(new_ka, axis=1), jnp.stack(new_va, axis=1),
            jnp.stack(new_kb, axis=1), jnp.stack(new_vb, axis=1),
            jnp.stack(new_sre, axis=1), jnp.stack(new_sim, axis=1))
```

```python
import functools
import math

import jax
import jax.numpy as jnp
import numpy as np
from jax import lax
from jax.experimental import pallas as pl
from jax.experimental.pallas import tpu as pltpu

F32 = jnp.float32
BF16 = jnp.bfloat16
I32 = jnp.int32

D_MODEL = 1024
GRID_W = 64
HEAD_DIM = 64
NA_HEADS = 8
NA_WIDTH = NA_HEADS * HEAD_DIM
WIN_R = 8
WIN_C = 16
DIFF_HEADS = 4
DIFF_QK = 32
DIFF_V = 64
DIFF_WIDTH = DIFF_HEADS * DIFF_V
SSM_GROUPS = 16
SSM_GROUP_CH = 16
SSM_WIDTH = SSM_GROUPS * SSM_GROUP_CH
SSM_STATE = 64
SSM_N = SSM_GROUPS * SSM_STATE
IN_WIDTH = 3 * NA_WIDTH + 3 * DIFF_WIDTH + SSM_WIDTH
N_EXPERTS = 16
EXPERT_FF = 2048
EC_CAPACITY_FACTOR = 2
ROPE_BASE = 10000.0
EPS = 1e-6
NEG_INF = -1e30

LANES = 128
SUBLANES = 8
MXU_DIM = 256
VMEM_LIMIT_CAP = 60000 * 1024

ROW_TILE = 256
NA_QROWS = 4
NA_KROWS = 12
SCAN_TILE = 256
FF_TILE = 1024
COMBINE_TILE = 512
GATHER_ROWS = 512


def _params(vmem_bytes, ndims):
    return pltpu.CompilerParams(
        dimension_semantics=("arbitrary",) * ndims,
        vmem_limit_bytes=VMEM_LIMIT_CAP,
    )


def _nbytes(shape, dtype):
    return math.prod(shape) * jnp.dtype(dtype).itemsize


def _hbm(*arrays):
    return list(arrays)


def _dot(a, b):
    return jnp.dot(a, b, preferred_element_type=F32)


def _dot_nt(a, b):
    return lax.dot_general(a, b, (((1,), (1,)), ((), ())), preferred_element_type=F32)


def _dot_tn(a, b):
    return lax.dot_general(a, b, (((0,), (0,)), ((), ())), preferred_element_type=F32)


def _cast_rows(dst_ref, src_ref, rows):
    n = src_ref.shape[0]
    for r in range(0, n, rows):
        dst_ref[r:r + rows, :] = src_ref[r:r + rows, :].astype(BF16)


def _modulated_norm(x, g, shift, scale):
    ms = jnp.mean(x * x, axis=-1, keepdims=True)
    return (x * lax.rsqrt(ms + EPS) * g) * (1.0 + scale) + shift


def _ada_kernel(c_ref, w_ref, b_ref, o_ref):
    c = c_ref[...]
    s = c * jax.nn.sigmoid(c)
    o_ref[0] = _dot(s.astype(BF16), w_ref[0].astype(BF16)) + b_ref[0]


def _ada(cond, w_ada, b_ada):
    depth = w_ada.shape[0]
    tn = 1536
    nt = 6 * D_MODEL // tn
    vm = 2 * _nbytes((D_MODEL, tn), F32) + _nbytes((D_MODEL, tn), BF16) + (4 << 20)
    return pl.pallas_call(
        _ada_kernel,
        out_shape=jax.ShapeDtypeStruct((depth, SUBLANES, 6 * D_MODEL), F32),
        grid=(depth, nt),
        in_specs=[
            pl.BlockSpec((SUBLANES, D_MODEL), lambda l, j: (0, 0)),
            pl.BlockSpec((1, D_MODEL, tn), lambda l, j: (l, 0, j)),
            pl.BlockSpec((1, 1, tn), lambda l, j: (l, 0, j)),
        ],
        out_specs=pl.BlockSpec((1, SUBLANES, tn), lambda l, j: (l, 0, j)),
        compiler_params=_params(vm, 2),
        name="ada",
    )(*_hbm(cond, w_ada, b_ada.reshape(depth, 1, 6 * D_MODEL)))


def _rope_apply(x, cos, sin_signed):
    lane = lax.broadcasted_iota(I32, (1, LANES), 1)
    first = (lane % 16) < 8
    outs = []
    for t in range(x.shape[1] // LANES):
        xt = x[:, t * LANES:(t + 1) * LANES]
        partner = jnp.where(first, pltpu.roll(xt, LANES - 8, axis=1), pltpu.roll(xt, 8, axis=1))
        outs.append(xt * cos[:, t * LANES:(t + 1) * LANES] + partner * sin_signed[:, t * LANES:(t + 1) * LANES])
    return jnp.concatenate(outs, axis=1)


def _proj_in_kernel(*refs, rope):
    if rope:
        x_ref, mod_ref, g_ref, w_ref, cos_ref, sin_ref = refs[:6]
        outs = refs[6:13]
        wbf_ref = refs[13]
    else:
        x_ref, mod_ref, g_ref, w_ref = refs[:4]
        outs = refs[4:11]
        wbf_ref = refs[11]
    qa_ref, ka_ref, va_ref, qb_ref, kb_ref, vb_ref, u_ref = outs

    @pl.when((pl.program_id(0) == 0) & (pl.program_id(1) == 0))
    def _():
        _cast_rows(wbf_ref, w_ref, 128)

    h = _modulated_norm(x_ref[0], g_ref[...], mod_ref[0, 0:1, :], mod_ref[0, 1:2, :])
    z = _dot(h.astype(BF16), wbf_ref[...])
    i1, i2, i3 = NA_WIDTH, 2 * NA_WIDTH, 3 * NA_WIDTH
    i4, i5, i6 = i3 + DIFF_WIDTH, i3 + 2 * DIFF_WIDTH, i3 + 3 * DIFF_WIDTH
    qa_ref[0] = z[:, :i1]
    ka_ref[0] = z[:, i1:i2]
    va_ref[0] = z[:, i2:i3]
    qb = z[:, i3:i4]
    kb = z[:, i4:i5]
    if rope:
        qb = _rope_apply(qb, cos_ref[...], sin_ref[...])
        kb = _rope_apply(kb, cos_ref[...], sin_ref[...])
    qb_ref[0] = qb
    kb_ref[0] = kb
    vb_ref[0] = z[:, i5:i6]
    for hf in range(SSM_WIDTH // LANES):
        u_ref[0, hf] = z[:, i6 + hf * LANES:i6 + (hf + 1) * LANES]


def _proj_in(x, mod, g_norm, w_in, layer, rope_tabs):
    b, n, d = x.shape
    tm = ROW_TILE
    bm = mod.shape[0]
    rope = rope_tabs is not None
    mod_map = (lambda i, j: (i, 0, 0)) if bm > 1 else (lambda i, j: (0, 0, 0))
    in_specs = [
        pl.BlockSpec((1, tm, d), lambda i, j: (i, j, 0)),
        pl.BlockSpec((1, 6, d), mod_map),
        pl.BlockSpec((None, 1, d), lambda i, j: (layer, 0, 0)),
        pl.BlockSpec((None, d, IN_WIDTH), lambda i, j: (layer, 0, 0), pipeline_mode=pl.Buffered(1)),
    ]
    args = [x, mod, g_norm, w_in]
    if rope:
        in_specs += [pl.BlockSpec((tm, DIFF_WIDTH), lambda i, j: (j, 0))] * 2
        args += list(rope_tabs)
    widths = (NA_WIDTH, NA_WIDTH, NA_WIDTH, DIFF_WIDTH, DIFF_WIDTH, DIFF_WIDTH, SSM_WIDTH)
    vm = (_nbytes((d, IN_WIDTH), F32) + _nbytes((d, IN_WIDTH), BF16)
          + 4 * _nbytes((tm, d), F32) + 6 * _nbytes((tm, IN_WIDTH), F32) + (4 << 20))
    nslab = SSM_WIDTH // LANES
    return pl.pallas_call(
        functools.partial(_proj_in_kernel, rope=rope),
        out_shape=([jax.ShapeDtypeStruct((b, n, w), F32) for w in widths[:-1]]
                   + [jax.ShapeDtypeStruct((b, nslab, n, LANES), F32)]),
        grid=(b, n // tm),
        in_specs=in_specs,
        out_specs=([pl.BlockSpec((1, tm, w), lambda i, j: (i, j, 0)) for w in widths[:-1]]
                   + [pl.BlockSpec((1, nslab, tm, LANES), lambda i, j: (i, 0, j, 0))]),
        scratch_shapes=[pltpu.VMEM((d, IN_WIDTH), BF16)],
        compiler_params=_params(vm, 2),
        name="proj_in",
    )(*_hbm(*args))


LOG2E = math.log2(math.e)


def _scaled_query(qp, mask, scale):
    return (jnp.where(mask, qp, 0.0) * (scale * LOG2E)).astype(BF16)


def _ones_outside(v, head_mask):
    return jnp.where(head_mask, v, jnp.ones_like(v))


def _softmax_av(qm, ks, vs, biases, den_lane):
    ss = []
    for k, bias in zip(ks, biases):
        s = _dot_nt(qm, k)
        if bias is not None:
            s = bias(s)
        ss.append(s)
    m = ss[0].max(axis=-1, keepdims=True)
    for s in ss[1:]:
        m = jnp.maximum(m, s.max(axis=-1, keepdims=True))
    acc = None
    for s, v in zip(ss, vs):
        pv = _dot(jnp.exp2(s - m).astype(BF16), v)
        acc = pv if acc is None else acc + pv
    return acc / acc[:, den_lane:den_lane + 1]


def _diff_lambda(lam_ref, lam_init):
    lp = lam_ref[...]
    a = jnp.sum(lp[0:1] * lp[1:2], axis=-1, keepdims=True)
    b = jnp.sum(lp[2:3] * lp[3:4], axis=-1, keepdims=True)
    return jnp.exp(a) - jnp.exp(b) + lam_init


def _attn_kernel(*refs, has_ctx, diff, lam_init):
    refs = list(refs)
    q_ref, k_ref, v_ref = refs[:3]
    pos = 3
    if has_ctx:
        kc_ref, vc_ref = refs[pos:pos + 2]
        pos += 2
    if diff:
        lam_ref, subln_ref = refs[pos:pos + 2]
        pos += 2
    o_ref = refs[pos]
    width = q_ref.shape[2]
    lane = lax.broadcasted_iota(I32, (1, LANES), 1)
    if diff:
        lam = _diff_lambda(lam_ref, lam_init)
        scale = DIFF_QK ** -0.5
    else:
        scale = HEAD_DIM ** -0.5
    for p in range(width // LANES):
        sl = slice(p * LANES, (p + 1) * LANES)
        qp = q_ref[0, :, sl]
        ks = [k_ref[0, :, sl].astype(BF16)]
        vs = [v_ref[0, :, sl].astype(BF16)]
        if has_ctx:
            ks.append(kc_ref[:, sl].astype(BF16))
            vs.append(vc_ref[:, sl].astype(BF16))
        nob = [None] * len(ks)
        pair = None
        for sub in range(LANES // HEAD_DIM):
            head_mask = (lane // HEAD_DIM) == sub
            den_lane = (1 - sub) * HEAD_DIM
            vh = [_ones_outside(v, head_mask) for v in vs]
            if diff:
                os_ = []
                for half in range(2):
                    qmask = (lane // DIFF_QK) == (2 * sub + half)
                    os_.append(_softmax_av(_scaled_query(qp, qmask, scale), ks, vh, nob, den_lane))
                o = jnp.where(head_mask, os_[0] - lam * os_[1], 0.0)
                ms = jnp.sum(o * o, axis=-1, keepdims=True) * (1.0 / DIFF_V)
                o = (o * lax.rsqrt(ms + EPS) * subln_ref[...]) * (1.0 - lam_init)
            else:
                o = _softmax_av(_scaled_query(qp, head_mask, scale), ks, vh, nob, den_lane)
                o = jnp.where(head_mask, o, 0.0)
            pair = o if pair is None else pair + o
        o_ref[0, :, sl] = pair


def _attention(q, k, v, ctx_kv, layer, diff_params, lam_init):
    b, nq, w = q.shape
    nk = k.shape[1]
    tq = ROW_TILE
    has_ctx = ctx_kv is not None
    diff = diff_params is not None
    in_specs = [
        pl.BlockSpec((1, tq, w), lambda i, j: (i, j, 0)),
        pl.BlockSpec((1, nk, w), lambda i, j: (i, 0, 0)),
        pl.BlockSpec((1, nk, w), lambda i, j: (i, 0, 0)),
    ]
    args = [q, k, v]
    nc = 0
    if has_ctx:
        nc = ctx_kv[0].shape[2]
        in_specs += [pl.BlockSpec((None, None, nc, w), lambda i, j: (i, layer, 0, 0))] * 2
        args += list(ctx_kv)
    if diff:
        lam_p, subln = diff_params
        in_specs += [
            pl.BlockSpec((None, 4, DIFF_QK), lambda i, j: (layer, 0, 0)),
            pl.BlockSpec((None, 1, LANES), lambda i, j: (layer, 0, 0)),
        ]
        args += [lam_p, subln]
    vm = (4 * _nbytes((nk + nc, w), F32) + 2 * _nbytes((nk + nc, LANES), BF16)
          + 8 * _nbytes((tq, nk + nc), F32) + 8 * _nbytes((tq, w), F32) + (4 << 20))
    return pl.pallas_call(
        functools.partial(_attn_kernel, has_ctx=has_ctx, diff=diff, lam_init=lam_init),
        out_shape=jax.ShapeDtypeStruct((b, nq, w), F32),
        grid=(b, nq // tq),
        in_specs=in_specs,
        out_specs=pl.BlockSpec((1, tq, w), lambda i, j: (i, j, 0)),
        compiler_params=_params(vm, 2),
        name="attn_diff" if diff else "attn",
    )(*_hbm(*args))


def _na_kernel(q_ref, k_ref, v_ref, kc_ref, vc_ref, bias_ref, o_ref, *, rows):
    i = pl.program_id(1)
    tq = NA_QROWS * GRID_W
    nkw = NA_KROWS * GRID_W
    ustart = jnp.clip(NA_QROWS * i - WIN_R // 2, 0, rows - NA_KROWS)
    koff = pl.multiple_of(ustart * GRID_W, GRID_W)
    delta = ustart - NA_QROWS * i
    qrow = NA_QROWS * i + lax.broadcasted_iota(I32, (tq, nkw), 0) // GRID_W
    krow = ustart + lax.broadcasted_iota(I32, (tq, nkw), 1) // GRID_W
    wstart = jnp.clip(qrow - WIN_R // 2, 0, rows - WIN_R)
    row_ok = (krow >= wstart) & (krow < wstart + WIN_R)
    lane = lax.broadcasted_iota(I32, (1, LANES), 1)
    scale = HEAD_DIM ** -0.5
    for p in range(NA_WIDTH // LANES):
        sl = slice(p * LANES, (p + 1) * LANES)
        qp = q_ref[0, :, sl]
        ks = [k_ref[0, pl.ds(koff, nkw), sl].astype(BF16), kc_ref[:, sl].astype(BF16)]
        vs = [v_ref[0, pl.ds(koff, nkw), sl].astype(BF16), vc_ref[:, sl].astype(BF16)]
        pair = None
        for sub in range(LANES // HEAD_DIM):
            h = p * (LANES // HEAD_DIM) + sub
            head_mask = (lane // HEAD_DIM) == sub

            def win_bias(s, h=h):
                rows_ = []
                for ri in range(NA_QROWS):
                    tiles = []
                    for pj in range(NA_KROWS // 2):
                        dr = delta + 2 * pj - ri
                        idx = jnp.clip(dr, -WIN_R, WIN_R - 1) + WIN_R
                        tiles.append(bias_ref[h, idx])
                    rows_.append(jnp.concatenate(tiles, axis=1))
                bias = jnp.concatenate(rows_, axis=0)
                return jnp.where(row_ok, s + bias, NEG_INF)

            vh = [_ones_outside(v, head_mask) for v in vs]
            o = _softmax_av(_scaled_query(qp, head_mask, scale), ks, vh, [win_bias, None], (1 - sub) * HEAD_DIM)
            o = jnp.where(head_mask, o, 0.0)
            pair = o if pair is None else pair + o
        o_ref[0, :, sl] = pair


def _na_table_kernel(v_ref, o_ref):
    nd = v_ref.shape[0]
    q = lax.broadcasted_iota(I32, (GRID_W, 2 * GRID_W), 0)
    k = lax.broadcasted_iota(I32, (GRID_W, 2 * GRID_W), 1)
    kk = k % GRID_W
    cstart = jnp.clip(q - WIN_C // 2, 0, GRID_W - WIN_C)
    col_ok = (kk >= cstart) & (kk < cstart + WIN_C)
    left = k < GRID_W
    tiles = []
    for dr in range(nd):
        x = jnp.broadcast_to(v_ref[dr:dr + 1, :], (GRID_W, 2 * GRID_W))
        tiles.append((pltpu.roll(x, GRID_W + 1, axis=1, stride=1, stride_axis=0),
                      pltpu.roll(x, 1, axis=1, stride=1, stride_axis=0)))
    zero = jnp.zeros((GRID_W, 2 * GRID_W), F32)
    for j in range(nd + 1):
        lt = tiles[j - 1][0] if j >= 1 else zero
        rt = tiles[j][1] if j < nd else zero
        o_ref[j] = jnp.where(col_ok, jnp.where(left, lt, rt) * LOG2E, NEG_INF)


def _na_bias_table(rpb):
    depth, h, nd, nc = rpb.shape
    assert 2 * GRID_W == LANES and nd == 2 * WIN_R - 1 and nc == 2 * WIN_C - 1
    rpb = rpb.astype(F32)
    lo = GRID_W - WIN_C
    v = jnp.concatenate([jnp.broadcast_to(rpb[..., :1], rpb.shape[:3] + (lo,)), rpb,
                         jnp.broadcast_to(rpb[..., -1:], rpb.shape[:3] + (2 * GRID_W - lo - nc,))], axis=-1)
    return pl.pallas_call(
        _na_table_kernel,
        out_shape=jax.ShapeDtypeStruct((depth, h, nd + 1, GRID_W, 2 * GRID_W), F32),
        grid=(depth, h),
        in_specs=[pl.BlockSpec((None, None, nd, 2 * GRID_W), lambda l, i: (l, i, 0, 0))],
        out_specs=pl.BlockSpec((None, None, nd + 1, GRID_W, 2 * GRID_W), lambda l, i: (l, i, 0, 0, 0)),
        compiler_params=_params(0, 2),
        name="na_table",
    )(v)


def _neighbourhood_attention(q, k, v, kc, vc, bias_tab, layer):
    b, n, w = q.shape
    rows = n // GRID_W
    tq = NA_QROWS * GRID_W
    nc = kc.shape[2]
    nkw = NA_KROWS * GRID_W
    vm = (4 * _nbytes((n, w), F32) + 4 * _nbytes((nc, w), F32) + 2 * _nbytes(bias_tab.shape, F32)
          + 12 * _nbytes((tq, nkw + nc), F32) + (6 << 20))
    return pl.pallas_call(
        functools.partial(_na_kernel, rows=rows),
        out_shape=jax.ShapeDtypeStruct((b, n, w), F32),
        grid=(b, n // tq),
        in_specs=[
            pl.BlockSpec((1, tq, w), lambda i, j: (i, j, 0)),
            pl.BlockSpec((1, n, w), lambda i, j: (i, 0, 0)),
            pl.BlockSpec((1, n, w), lambda i, j: (i, 0, 0)),
            pl.BlockSpec((None, None, nc, w), lambda i, j: (i, layer, 0, 0)),
            pl.BlockSpec((None, None, nc, w), lambda i, j: (i, layer, 0, 0)),
            pl.BlockSpec((None,) + bias_tab.shape[1:], lambda i, j: (layer, 0, 0, 0, 0)),
        ],
        out_specs=pl.BlockSpec((1, tq, w), lambda i, j: (i, j, 0)),
        compiler_params=_params(vm, 2),
        name="na_attn",
    )(*_hbm(q, k, v, kc, vc, bias_tab))


def _sublane_scan(er, ei, tr, ti, pw_ref, base, lanes, reverse):
    row = lax.broadcasted_iota(I32, (SUBLANES, LANES), 0)
    if reverse:
        first = row == SUBLANES - 1
        xr = jnp.where(first, tr, pltpu.roll(er, SUBLANES - 1, axis=0))
        xi = jnp.where(first, ti, pltpu.roll(ei, SUBLANES - 1, axis=0))
    else:
        first = row == 0
        xr = jnp.where(first, tr, pltpu.roll(er, 1, axis=0))
        xi = jnp.where(first, ti, pltpu.roll(ei, 1, axis=0))
    for k, s in enumerate((1, 2, 4)):
        ar = pw_ref[base + 2 * k:base + 2 * k + 1, lanes]
        ai = pw_ref[base + 2 * k + 1:base + 2 * k + 2, lanes]
        if reverse:
            keep = row < SUBLANES - s
            sr = jnp.where(keep, pltpu.roll(xr, SUBLANES - s, axis=0), 0.0)
            si = jnp.where(keep, pltpu.roll(xi, SUBLANES - s, axis=0), 0.0)
        else:
            keep = row >= s
            sr = jnp.where(keep, pltpu.roll(xr, s, axis=0), 0.0)
            si = jnp.where(keep, pltpu.roll(xi, s, axis=0), 0.0)
        xr, xi = xr + (ar * sr - ai * si), xi + (ar * si + ai * sr)
    return xr, xi


def _s5_kernel(u_ref, h0re_ref, h0im_ref, bmat_ref, cmat_ref, a1_ref, ach_ref, pfix_ref, d_ref, wglu_ref, bglu_ref,
               o_ref, fre_ref, fim_ref, up_ref, yp_ref, buf_ref, car_ref):
    seq = u_ref.shape[2]
    tt = SCAN_TILE
    ch = tt // SUBLANES
    ntile = seq // tt
    nlt = SSM_N // LANES
    nhalf = SSM_WIDTH // LANES

    def permute_in(t, _):
        t0 = pl.multiple_of(t * tt, tt)
        for j in range(0, ch, 2):
            for hf in range(nhalf):
                lanes = slice(hf * LANES, (hf + 1) * LANES)
                rows = jnp.concatenate([u_ref[0, hf, pl.ds(t0 + j + jj, SUBLANES, stride=ch), :] for jj in range(2)],
                                       axis=0)
                r0 = pl.multiple_of(t0 + j * SUBLANES, 2 * SUBLANES)
                up_ref[pl.ds(r0, 2 * SUBLANES), lanes] = rows.astype(BF16)
                yp_ref[pl.ds(r0, 2 * SUBLANES), lanes] = rows * d_ref[:, lanes]
        return 0

    lax.fori_loop(0, ntile, permute_in, 0)

    for d in range(2):
        for lt in range(nlt):
            lanes = slice(lt * LANES, (lt + 1) * LANES)
            car_ref[d, 0, :, lanes] = jnp.broadcast_to(h0re_ref[0, d:d + 1, lanes], (SUBLANES, LANES))
            car_ref[d, 1, :, lanes] = jnp.broadcast_to(h0im_ref[0, d:d + 1, lanes], (SUBLANES, LANES))

    def tile(t, _):
        starts = []
        for d in range(2):
            tix = (ntile - 1 - t) if d == 1 else t
            t0 = pl.multiple_of(tix * tt, tt)
            starts.append(t0)
            buf_ref[d] = _dot(up_ref[pl.ds(t0, tt), :], bmat_ref[:, d * 2 * SSM_N:(d + 1) * 2 * SSM_N])
        for d in range(2):
            reverse = d == 1
            order = range(ch - 1, -1, -1) if reverse else range(ch)
            for lt in range(nlt):
                lre = slice(lt * LANES, (lt + 1) * LANES)
                lim = slice(SSM_N + lt * LANES, SSM_N + (lt + 1) * LANES)
                ar = a1_ref[2 * d:2 * d + 1, lre]
                ai = a1_ref[2 * d + 1:2 * d + 2, lre]
                cr = jnp.zeros((SUBLANES, LANES), F32)
                ci = jnp.zeros((SUBLANES, LANES), F32)
                for j in order:
                    rows = slice(j * SUBLANES, (j + 1) * SUBLANES)
                    cr, ci = (ar * cr - ai * ci) + buf_ref[d, rows, lre], (ar * ci + ai * cr) + buf_ref[d, rows, lim]
                    buf_ref[d, rows, lre] = cr
                    buf_ref[d, rows, lim] = ci
                tr, ti = car_ref[d, 0, :, lre], car_ref[d, 1, :, lre]
                gr, gi = _sublane_scan(cr, ci, tr, ti, ach_ref, 6 * d, lre, reverse)
                a_r, a_i = ach_ref[6 * d:6 * d + 1, lre], ach_ref[6 * d + 1:6 * d + 2, lre]
                nr, ni = (a_r * gr - a_i * gi) + cr, (a_r * gi + a_i * gr) + ci
                edge = 0 if reverse else SUBLANES - 1
                car_ref[d, 0, :, lre] = jnp.broadcast_to(nr[edge:edge + 1, :], (SUBLANES, LANES))
                car_ref[d, 1, :, lre] = jnp.broadcast_to(ni[edge:edge + 1, :], (SUBLANES, LANES))
                for j in range(ch):
                    rows = slice(j * SUBLANES, (j + 1) * SUBLANES)
                    pr = pfix_ref[2 * d, j:j + 1, lre]
                    pi = pfix_ref[2 * d + 1, j:j + 1, lre]
                    buf_ref[d, rows, lre] += pr * gr - pi * gi
                    buf_ref[d, rows, lim] += pr * gi + pi * gr
        for d in range(2):
            yp_ref[pl.ds(starts[d], tt), :] += _dot(buf_ref[d].astype(BF16),
                                                    cmat_ref[d * 2 * SSM_N:(d + 1) * 2 * SSM_N, :])
        return 0

    lax.fori_loop(0, ntile, tile, 0)

    for d in range(2):
        fre_ref[0, d:d + 1, :] = car_ref[d, 0, 0:1, :]
        fim_ref[0, d:d + 1, :] = car_ref[d, 1, 0:1, :]

    wglu = wglu_ref[...].astype(BF16)

    def glu(t, _):
        t0 = pl.multiple_of(t * tt, tt)
        g = jax.nn.gelu(yp_ref[pl.ds(t0, tt), :])
        out = g * jax.nn.sigmoid(_dot(g.astype(BF16), wglu) + bglu_ref[...])
        for j in range(ch):
            for hf in range(nhalf):
                o_ref[0, hf, pl.ds(t0 + j, SUBLANES, stride=ch), :] = out[j * SUBLANES:(j + 1) * SUBLANES,
                                                                          hf * LANES:(hf + 1) * LANES]
        return 0

    lax.fori_loop(0, ntile, glu, 0)


def _cmul(ar, ai, br, bi):
    return ar * br - ai * bi, ar * bi + ai * br


def _s5_tables(a_re, a_im, log_dt, b_re, b_im, c_re, c_im):
    depth = a_re.shape[0]
    g, p, hc = SSM_GROUPS, SSM_STATE, SSM_GROUP_CH
    ch = SCAN_TILE // SUBLANES
    eye = jnp.eye(g, dtype=F32)
    lr, li = a_re.astype(F32), a_im.astype(F32)
    dt = jnp.exp(log_dt.astype(F32))[..., None]
    mag = jnp.exp(lr * dt)
    ar, ai = mag * jnp.cos(li * dt), mag * jnp.sin(li * dt)
    den = lr * lr + li * li
    qr = ((ar - 1.0) * lr + ai * li) / den
    qi = (ai * lr - (ar - 1.0) * li) / den
    br, bi = _cmul(qr[..., None], qi[..., None], b_re.astype(F32), b_im.astype(F32))
    bparts = jnp.stack([br, bi], axis=2)
    bmat = bparts.transpose(0, 3, 5, 1, 2, 4)[:, :, :, :, :, None, :] * eye[None, :, None, None, None, :, None]
    bmat = bmat.reshape(depth, g * hc, 4 * g * p).astype(BF16)
    cparts = jnp.stack([c_re.astype(F32), -c_im.astype(F32)], axis=2)
    cmat = cparts.transpose(0, 1, 2, 3, 5, 4)[:, :, :, :, :, None, :] * eye[None, None, None, :, None, :, None]
    cmat = cmat.reshape(depth, 4 * g * p, g * hc).astype(BF16)

    def power(k):
        kk = k[None, None, :, None, None]
        m = jnp.exp(kk * (lr * dt)[:, :, None])
        th = kk * (li * dt)[:, :, None]
        return (m * jnp.cos(th)).reshape(depth, 2, -1, g * p), (m * jnp.sin(th)).reshape(depth, 2, -1, g * p)

    a1 = jnp.stack([ar.reshape(depth, 2, g * p), ai.reshape(depth, 2, g * p)], axis=2).reshape(depth, 4, g * p)
    cr_, ci_ = power(jnp.asarray([ch, 2 * ch, 4 * ch], F32))
    ach = jnp.stack([cr_, ci_], axis=3).reshape(depth, 12, g * p)
    fr, fi = power(jnp.arange(1, ch + 1, dtype=F32))
    fr = jnp.stack([fr[:, 0], fr[:, 1, ::-1]], axis=1)
    fi = jnp.stack([fi[:, 0], fi[:, 1, ::-1]], axis=1)
    pfix = jnp.stack([fr, fi], axis=2).reshape(depth, 4, ch, g * p)
    return bmat, cmat, a1, ach, pfix


def _s5(u, h0re, h0im, tabs, d_skip, w_glu, b_glu, layer):
    b, nh, seq, _ = u.shape
    w = nh * LANES
    lay = lambda a: pl.BlockSpec((None,) + a.shape[1:], lambda i: (layer,) + (0,) * (a.ndim - 1))
    return pl.pallas_call(
        _s5_kernel,
        out_shape=[jax.ShapeDtypeStruct((b, nh, seq, LANES), F32),
                   jax.ShapeDtypeStruct((b, 2, SSM_N), F32),
                   jax.ShapeDtypeStruct((b, 2, SSM_N), F32)],
        grid=(b,),
        in_specs=[
            pl.BlockSpec((1, nh, seq, LANES), lambda i: (i, 0, 0, 0)),
            pl.BlockSpec((1, 2, SSM_N), lambda i: (i, 0, 0)),
            pl.BlockSpec((1, 2, SSM_N), lambda i: (i, 0, 0)),
            lay(tabs[0]), lay(tabs[1]), lay(tabs[2]), lay(tabs[3]), lay(tabs[4]),
            pl.BlockSpec((None, 1, w), lambda i: (layer, 0, 0)),
            pl.BlockSpec((None, w, w), lambda i: (layer, 0, 0)),
            pl.BlockSpec((None, 1, w), lambda i: (layer, 0, 0)),
        ],
        out_specs=[pl.BlockSpec((1, nh, seq, LANES), lambda i: (i, 0, 0, 0)),
                   pl.BlockSpec((1, 2, SSM_N), lambda i: (i, 0, 0)),
                   pl.BlockSpec((1, 2, SSM_N), lambda i: (i, 0, 0))],
        scratch_shapes=[pltpu.VMEM((seq, w), BF16),
                        pltpu.VMEM((seq, w), F32),
                        pltpu.VMEM((2, SCAN_TILE, 2 * SSM_N), F32),
                        pltpu.VMEM((2, 2, SUBLANES, SSM_N), F32)],
        compiler_params=_params(0, 1),
        name="s5",
    )(u, h0re, h0im, *tabs, d_skip, w_glu, b_glu)


def _proj_out_kernel(oa_ref, ob_ref, oc_ref, w_ref, x_ref, mod_ref, g_ref, wr_ref, o_ref, h_ref, lg_ref, wbf_ref):
    @pl.when((pl.program_id(0) == 0) & (pl.program_id(1) == 0))
    def _():
        _cast_rows(wbf_ref, w_ref, 128)

    i1, i2 = NA_WIDTH, NA_WIDTH + DIFF_WIDTH
    y = _dot(oa_ref[0].astype(BF16), wbf_ref[:i1, :])
    y += _dot(ob_ref[0].astype(BF16), wbf_ref[i1:i2, :])
    for hf in range(SSM_WIDTH // LANES):
        y += _dot(oc_ref[0, hf].astype(BF16), wbf_ref[i2 + hf * LANES:i2 + (hf + 1) * LANES, :])
    xn = x_ref[0] + mod_ref[0, 2:3, :] * y
    o_ref[0] = xn
    h = _modulated_norm(xn, g_ref[...], mod_ref[0, 3:4, :], mod_ref[0, 4:5, :])
    h_hi = h.astype(BF16)
    h_ref[0] = h_hi
    h_lo = (h - h_hi.astype(F32)).astype(BF16)
    t1 = _dot_nt(wr_ref[...], h_hi)
    t2 = _dot_nt(wr_ref[:N_EXPERTS, :], h_lo)
    lg_ref[0] = t1[:N_EXPERTS, :] + t1[N_EXPERTS:, :] + t2


def _proj_out(oa, ob, oc, w_out, x, mod, g_ffn, wr_t, layer):
    b, n, d = x.shape
    tm = ROW_TILE
    bm = mod.shape[0]
    mod_map = (lambda i, j: (i, 0, 0)) if bm > 1 else (lambda i, j: (0, 0, 0))
    kw = w_out.shape[1]
    vm = _nbytes((kw, d), F32) + _nbytes((kw, d), BF16) + 10 * _nbytes((tm, d), F32) + (4 << 20)
    return pl.pallas_call(
        _proj_out_kernel,
        out_shape=[jax.ShapeDtypeStruct((b, n, d), F32),
                   jax.ShapeDtypeStruct((b, n, d), BF16),
                   jax.ShapeDtypeStruct((b, N_EXPERTS, n), F32)],
        grid=(b, n // tm),
        in_specs=[
            pl.BlockSpec((1, tm, NA_WIDTH), lambda i, j: (i, j, 0)),
            pl.BlockSpec((1, tm, DIFF_WIDTH), lambda i, j: (i, j, 0)),
            pl.BlockSpec((1, SSM_WIDTH // LANES, tm, LANES), lambda i, j: (i, 0, j, 0)),
            pl.BlockSpec((None, kw, d), lambda i, j: (layer, 0, 0), pipeline_mode=pl.Buffered(1)),
            pl.BlockSpec((1, tm, d), lambda i, j: (i, j, 0)),
            pl.BlockSpec((1, 6, d), mod_map),
            pl.BlockSpec((None, 1, d), lambda i, j: (layer, 0, 0)),
            pl.BlockSpec((None, 2 * N_EXPERTS, d), lambda i, j: (layer, 0, 0)),
        ],
        out_specs=[pl.BlockSpec((1, tm, d), lambda i, j: (i, j, 0)),
                   pl.BlockSpec((1, tm, d), lambda i, j: (i, j, 0)),
                   pl.BlockSpec((1, N_EXPERTS, tm), lambda i, j: (i, 0, j))],
        scratch_shapes=[pltpu.VMEM((kw, d), BF16)],
        compiler_params=_params(vm, 2),
        name="proj_out",
    )(*_hbm(oa, ob, oc, w_out, x, mod, g_ffn, wr_t))


def _excl_cumsum_lanes(m):
    e, n = m.shape
    blk = MXU_DIM
    nb = n // blk
    r = lax.broadcasted_iota(I32, (blk, blk), 0)
    c = lax.broadcasted_iota(I32, (blk, blk), 1)
    tri = jnp.where(r < c, 1.0, 0.0).astype(BF16)
    stacked = jnp.concatenate([m[:, k * blk:(k + 1) * blk] for k in range(nb)], axis=0).astype(BF16)
    within = _dot(stacked, tri)
    outs = []
    off = jnp.zeros((e, 1), F32)
    for k in range(nb):
        outs.append(within[k * e:(k + 1) * e, :] + off)
        off = off + jnp.sum(m[:, k * blk:(k + 1) * blk], axis=1, keepdims=True)
    return jnp.concatenate(outs, axis=1)


def _route_kernel(lg_ref, pos_ref, gate_ref, *, cap):
    b, e, n = lg_ref.shape
    lg = lg_ref[...]
    mx = lg.max(axis=1, keepdims=True)
    ex = jnp.exp(lg - mx)
    aff = (ex / ex.sum(axis=1, keepdims=True)).reshape(b * e, n)

    def search(it, cur):
        cand = cur | (jnp.int32(1) << (jnp.int32(30) - it))
        cnt = jnp.sum(jnp.where(aff >= pltpu.bitcast(cand, F32), 1.0, 0.0), axis=1, keepdims=True)
        return jnp.where(cnt >= cap, cand, cur)

    thr = pltpu.bitcast(lax.fori_loop(0, 31, search, jnp.zeros((b * e, 1), I32)), F32)
    gt = aff > thr
    eq = aff == thr
    need = cap - jnp.sum(jnp.where(gt, 1.0, 0.0), axis=1, keepdims=True)
    eq_rank = _excl_cumsum_lanes(jnp.where(eq, 1.0, 0.0))
    sel = gt | (eq & (eq_rank < need))
    slot = _excl_cumsum_lanes(jnp.where(sel, 1.0, 0.0))
    pos = jnp.where(sel, slot, -1.0).astype(I32)
    gate = jnp.where(sel, aff, 0.0)
    for i in range(b):
        pos_ref[:, i, 0, :] = pos[i * e:(i + 1) * e, :]
        gate_ref[:, i, 0, :] = gate[i * e:(i + 1) * e, :]


def _route(logits):
    b, e, n = logits.shape
    cap = EC_CAPACITY_FACTOR * n // N_EXPERTS
    return pl.pallas_call(
        functools.partial(_route_kernel, cap=cap),
        out_shape=[jax.ShapeDtypeStruct((e, b, 1, n), I32),
                   jax.ShapeDtypeStruct((e, b, 1, n), F32)],
        compiler_params=_params(0, 0),
        name="route",
    )(logits)


def _one_hot_rows(pos_row, cap):
    n = pos_row.shape[1]
    slot = lax.broadcasted_iota(I32, (cap, n), 0)
    return slot == pos_row


def _gather_kernel(h_ref, pos_ref, gate_ref, xs_ref, gs_ref, *, cap):
    h = h_ref[0]
    group = max(1, GATHER_ROWS // cap)
    for e0 in range(0, N_EXPERTS, group):
        ohs = [_one_hot_rows(pos_ref[e, 0], cap) for e in range(e0, e0 + group)]
        xs = _dot(jnp.concatenate([jnp.where(oh, 1.0, 0.0).astype(BF16) for oh in ohs], axis=0), h).astype(BF16)
        for i, oh in enumerate(ohs):
            e = e0 + i
            xs_ref[e] = xs[i * cap:(i + 1) * cap, :]
            g = jnp.sum(jnp.where(oh, gate_ref[e, 0], 0.0), axis=1, keepdims=True)
            gs_ref[e] = jnp.broadcast_to(g, (cap, LANES))


def _gather(h, pos, gate):
    b, n, d = h.shape
    cap = EC_CAPACITY_FACTOR * n // N_EXPERTS
    vm = 4 * _nbytes((n, d), BF16) + 6 * _nbytes((N_EXPERTS, cap, d), BF16) + 8 * _nbytes((cap, n), F32) + (8 << 20)
    return pl.pallas_call(
        functools.partial(_gather_kernel, cap=cap),
        out_shape=[jax.ShapeDtypeStruct((N_EXPERTS, b * cap, d), BF16),
                   jax.ShapeDtypeStruct((N_EXPERTS, b * cap, LANES), F32)],
        grid=(b,),
        in_specs=[
            pl.BlockSpec((1, n, d), lambda i: (i, 0, 0)),
            pl.BlockSpec((N_EXPERTS, 1, 1, n), lambda i: (0, i, 0, 0)),
            pl.BlockSpec((N_EXPERTS, 1, 1, n), lambda i: (0, i, 0, 0)),
        ],
        out_specs=[pl.BlockSpec((N_EXPERTS, cap, d), lambda i: (0, i, 0)),
                   pl.BlockSpec((N_EXPERTS, cap, LANES), lambda i: (0, i, 0))],
        compiler_params=_params(vm, 1),
        name="moe_gather",
    )(*_hbm(h, pos, gate))


def _ffn_kernel(xp_ref, gp_ref, xs_ref, gs_ref, wg_ref, wu_ref, wd_ref, yp_ref, ys_ref, acc_ref):
    j = pl.program_id(1)
    sp = xp_ref.shape[1]

    @pl.when(j == 0)
    def _():
        acc_ref[...] = jnp.zeros_like(acc_ref)

    xs = jnp.concatenate([xp_ref[0], xs_ref[0]], axis=0)
    a = _dot(xs, wg_ref[...].astype(BF16))
    u = _dot(xs, wu_ref[...].astype(BF16))
    hid = (a * jax.nn.sigmoid(a)) * u
    acc_ref[...] += _dot(hid.astype(BF16), wd_ref[...].astype(BF16))

    @pl.when(j == pl.num_programs(1) - 1)
    def _():
        yp_ref[0] = (acc_ref[:sp, :] * gp_ref[0][:, 0:1]).astype(BF16)
        ys_ref[0] = (acc_ref[sp:, :] * gs_ref[0][:, 0:1]).astype(BF16)


def _ffn(xp, gp, xs, gs, w_gate, w_up, w_down, layer):
    e, sp, d = xp.shape
    ss = xs.shape[1]
    s = sp + ss
    ff = w_gate.shape[-1]
    tf = FF_TILE
    vm = (6 * _nbytes((d, tf), F32) + 3 * _nbytes((d, tf), BF16) + 5 * _nbytes((s, d), BF16)
          + 2 * _nbytes((s, LANES), F32) + _nbytes((s, d), F32) + 4 * _nbytes((s, tf), F32) + (4 << 20))
    return pl.pallas_call(
        _ffn_kernel,
        out_shape=[jax.ShapeDtypeStruct((e, sp, d), BF16), jax.ShapeDtypeStruct((e, ss, d), BF16)],
        grid=(e, ff // tf),
        in_specs=[
            pl.BlockSpec((1, sp, d), lambda i, j: (i, 0, 0)),
            pl.BlockSpec((1, sp, LANES), lambda i, j: (i, 0, 0)),
            pl.BlockSpec((1, ss, d), lambda i, j: (i, 0, 0)),
            pl.BlockSpec((1, ss, LANES), lambda i, j: (i, 0, 0)),
            pl.BlockSpec((None, None, d, tf), lambda i, j: (layer, i, 0, j)),
            pl.BlockSpec((None, None, d, tf), lambda i, j: (layer, i, 0, j)),
            pl.BlockSpec((None, None, tf, d), lambda i, j: (layer, i, j, 0)),
        ],
        out_specs=[pl.BlockSpec((1, sp, d), lambda i, j: (i, 0, 0)),
                   pl.BlockSpec((1, ss, d), lambda i, j: (i, 0, 0))],
        scratch_shapes=[pltpu.VMEM((s, d), F32)],
        compiler_params=_params(vm, 2),
        name="moe_ffn",
    )(*_hbm(xp, gp, xs, gs, w_gate, w_up, w_down))


def _combine_kernel(ye_ref, pos_ref, x_ref, mod_ref, *rest, cap, final):
    if final:
        g_ref, o_ref = rest
    else:
        (o_ref,) = rest
    group = max(1, MXU_DIM // cap)
    y = None
    for e0 in range(0, N_EXPERTS, group):
        oh = jnp.concatenate([jnp.where(_one_hot_rows(pos_ref[e, 0], cap), 1.0, 0.0).astype(BF16)
                              for e in range(e0, e0 + group)], axis=0)
        ye = jnp.concatenate([ye_ref[e] for e in range(e0, e0 + group)], axis=0)
        part = _dot_tn(oh, ye)
        y = part if y is None else y + part
    xn = x_ref[0] + mod_ref[0, 5:6, :] * y
    if final:
        ms = jnp.mean(xn * xn, axis=-1, keepdims=True)
        xn = xn * lax.rsqrt(ms + EPS) * g_ref[...]
    o_ref[0] = xn


def _combine(ye, pos, x, mod, final_norm):
    b, n, d = x.shape
    cap = EC_CAPACITY_FACTOR * n // N_EXPERTS
    tn = min(n, COMBINE_TILE)
    bm = mod.shape[0]
    mod_map = (lambda i, j: (i, 0, 0)) if bm > 1 else (lambda i, j: (0, 0, 0))
    final = final_norm is not None
    in_specs = [
        pl.BlockSpec((N_EXPERTS, cap, d), lambda i, j: (0, i, 0)),
        pl.BlockSpec((N_EXPERTS, 1, 1, tn), lambda i, j: (0, i, 0, j)),
        pl.BlockSpec((1, tn, d), lambda i, j: (i, j, 0)),
        pl.BlockSpec((1, 6, d), mod_map),
    ]
    args = [ye, pos, x, mod]
    if final:
        in_specs.append(pl.BlockSpec((1, d), lambda i, j: (0, 0)))
        args.append(final_norm)
    vm = (4 * _nbytes((N_EXPERTS, cap, d), BF16) + 10 * _nbytes((tn, d), F32)
          + 6 * _nbytes((max(cap, MXU_DIM), tn), F32) + (8 << 20))
    return pl.pallas_call(
        functools.partial(_combine_kernel, cap=cap, final=final),
        out_shape=jax.ShapeDtypeStruct((b, n, d), F32),
        grid=(b, n // tn),
        in_specs=in_specs,
        out_specs=pl.BlockSpec((1, tn, d), lambda i, j: (i, j, 0)),
        compiler_params=_params(vm, 2),
        name="moe_combine",
    )(*_hbm(*args))


def _rope_tables(n):
    t = np.arange(n)
    row = (t // GRID_W).astype(np.float32)
    col = (t % GRID_W).astype(np.float32)
    nf = DIFF_QK // 4
    inv = np.float32(ROPE_BASE) ** (-np.arange(nf, dtype=np.float32) / np.float32(nf))
    lane = np.arange(DIFF_WIDTH)
    pos = np.where(((lane % DIFF_QK) < DIFF_QK // 2)[None, :], row[:, None], col[:, None])
    ang = (pos * inv[lane % nf][None, :]).astype(np.float32)
    first = (lane % (2 * nf)) < nf
    cos, sin = np.cos(ang).astype(np.float32), np.sin(ang).astype(np.float32)
    return jnp.asarray(cos), jnp.asarray(np.where(first[None, :], -sin, sin))


def kernel(x_prompt, x_sample, cache_na_k, cache_na_v, cache_diff_k, cache_diff_v, state_ssm_re, state_ssm_im,
           c, c_ctx, w_ada, b_ada, norm_mix, norm_ffn, w_in, w_out, na_rpb, diff_lambda, diff_subln,
           ssm_a_re, ssm_a_im, ssm_log_dt, ssm_b_re, ssm_b_im, ssm_c_re, ssm_c_im, ssm_d, ssm_w_glu, ssm_b_glu,
           w_router, w_gate, w_up, w_down, final_norm):
    depth = w_in.shape[0]
    bp, sp, d = x_prompt.shape
    bs, ss, _ = x_sample.shape
    assert d == D_MODEL and bs + 1 <= SUBLANES
    past = cache_na_k.shape[2]

    cond = jnp.zeros((SUBLANES, d), F32).at[0].set(c_ctx).at[1:1 + bs].set(c)
    mods = _ada(cond, w_ada, b_ada).reshape(depth, SUBLANES, 6, d)

    rope_tabs = _rope_tables(ss)
    kc_a = cache_na_k.reshape(bs, depth, past, NA_WIDTH)
    vc_a = cache_na_v.reshape(bs, depth, past, NA_WIDTH)
    kc_b = cache_diff_k.reshape(bs, depth, past, DIFF_WIDTH)
    vc_b = cache_diff_v.reshape(bs, depth, past, DIFF_WIDTH)
    subln = jnp.tile(diff_subln, (1, LANES // DIFF_V)).reshape(depth, 1, LANES)
    norm_mix = norm_mix.reshape(depth, 1, d)
    norm_ffn = norm_ffn.reshape(depth, 1, d)
    ssm_d = ssm_d.reshape(depth, 1, SSM_WIDTH)
    ssm_b_glu = ssm_b_glu.reshape(depth, 1, SSM_WIDTH)
    wr_t = jnp.swapaxes(w_router, 1, 2).astype(F32)
    wr_hi = wr_t.astype(BF16)
    wr_t = jnp.concatenate([wr_hi, (wr_t - wr_hi.astype(F32)).astype(BF16)], axis=1)
    fnorm = final_norm.reshape(1, d)
    zero_state = jnp.zeros((bp, 2, SSM_N), F32)
    tabs = _s5_tables(ssm_a_re, ssm_a_im, ssm_log_dt, ssm_b_re, ssm_b_im, ssm_c_re, ssm_c_im)
    bias_tab = _na_bias_table(na_rpb)

    xp, xs = x_prompt, x_sample
    new_ka, new_va, new_kb, new_vb, new_sre, new_sim = [], [], [], [], [], []
    for l in range(depth):
        lam_init = 0.8 - 0.6 * math.exp(-0.3 * l)
        mod_p = mods[l, 0:1]
        mod_s = mods[l, 1:1 + bs]
        diff_params = (diff_lambda, subln)

        qa, ka, va, qb, kb, vb, u = _proj_in(xp, mod_p, norm_mix, w_in, l, None)
        o_a = _attention(qa, ka, va, None, l, None, lam_init)
        o_b = _attention(qb, kb, vb, None, l, diff_params, lam_init)
        o_c, fre, fim = _s5(u, zero_state, zero_state, tabs, ssm_d, ssm_w_glu, ssm_b_glu, l)
        xp, hp, lg_p = _proj_out(o_a, o_b, o_c, w_out, xp, mod_p, norm_ffn, wr_t, l)
        new_ka.append(ka.reshape(bp, sp, NA_HEADS, HEAD_DIM))
        new_va.append(va.reshape(bp, sp, NA_HEADS, HEAD_DIM))
        new_kb.append(kb.reshape(bp, sp, DIFF_HEADS, DIFF_V))
        new_vb.append(vb.reshape(bp, sp, DIFF_HEADS, DIFF_V))
        new_sre.append(fre.reshape(bp, 2, SSM_GROUPS, SSM_STATE))
        new_sim.append(fim.reshape(bp, 2, SSM_GROUPS, SSM_STATE))

        qa, ka, va, qb, kb, vb, u = _proj_in(xs, mod_s, norm_mix, w_in, l, rope_tabs)
        o_a = _neighbourhood_attention(qa, ka, va, kc_a, vc_a, bias_tab, l)
        o_b = _attention(qb, kb, vb, (kc_b, vc_b), l, diff_params, lam_init)
        h0re = state_ssm_re[:, l].reshape(bs, 2, SSM_N)
        h0im = state_ssm_im[:, l].reshape(bs, 2, SSM_N)
        o_c, _, _ = _s5(u, h0re, h0im, tabs, ssm_d, ssm_w_glu, ssm_b_glu, l)
        xs, hs, lg_s = _proj_out(o_a, o_b, o_c, w_out, xs, mod_s, norm_ffn, wr_t, l)

        pos_p, gate_p = _route(lg_p)
        pos_s, gate_s = _route(lg_s)
        xg_p, gs_p = _gather(hp, pos_p, gate_p)
        xg_s, gs_s = _gather(hs, pos_s, gate_s)
        ye_p, ye_s = _ffn(xg_p, gs_p, xg_s, gs_s, w_gate, w_up, w_down, l)
        last = l == depth - 1
        xp = _combine(ye_p, pos_p, xp, mod_p, fnorm if last else None)
        xs = _combine(ye_s, pos_s, xs, mod_s, fnorm if last else None)

    return (xp, xs, jnp.stack(new_ka, axis=1), jnp.stack(new_va, axis=1),
            jnp.stack(new_kb, axis=1), jnp.stack(new_vb, axis=1),
            jnp.stack(new_sre, axis=1), jnp.stack(new_sim, axis=1))
```

```python
import functools
import math

import jax
import jax.numpy as jnp
import numpy as np
from jax import lax
from jax.experimental import pallas as pl
from jax.experimental.pallas import tpu as pltpu

F32 = jnp.float32
BF16 = jnp.bfloat16
I32 = jnp.int32

D_MODEL = 1024
GRID_W = 64
HEAD_DIM = 64
NA_HEADS = 8
NA_WIDTH = NA_HEADS * HEAD_DIM
WIN_R = 8
WIN_C = 16
DIFF_HEADS = 4
DIFF_QK = 32
DIFF_V = 64
DIFF_WIDTH = DIFF_HEADS * DIFF_V
SSM_GROUPS = 16
SSM_GROUP_CH = 16
SSM_WIDTH = SSM_GROUPS * SSM_GROUP_CH
SSM_STATE = 64
SSM_N = SSM_GROUPS * SSM_STATE
IN_WIDTH = 3 * NA_WIDTH + 3 * DIFF_WIDTH + SSM_WIDTH
N_EXPERTS = 16
EXPERT_FF = 2048
EC_CAPACITY_FACTOR = 2
ROPE_BASE = 10000.0
EPS = 1e-6
NEG_INF = -1e30
LOG2E = math.log2(math.e)

LANES = 128
SUBLANES = 8
MXU_DIM = 256
VMEM_LIMIT_CAP = 60000 * 1024

ROW_TILE = 256
NA_QROWS = 4
NA_KROWS = 12
SCAN_TILE = 256
FF_TILE = 1024
COMBINE_TILE = 512
GATHER_ROWS = 512
DEN_IN_MATMUL_MIN_KEYS = 1024


def _params(vmem_bytes, ndims):
    return pltpu.CompilerParams(
        dimension_semantics=("arbitrary",) * ndims,
        vmem_limit_bytes=VMEM_LIMIT_CAP,
    )


def _nbytes(shape, dtype):
    return math.prod(shape) * jnp.dtype(dtype).itemsize


def _hbm(*arrays):
    return list(arrays)


def _dot(a, b):
    return jnp.dot(a, b, preferred_element_type=F32)


def _dot_nt(a, b):
    return lax.dot_general(a, b, (((1,), (1,)), ((), ())), preferred_element_type=F32)


def _dot_tn(a, b):
    return lax.dot_general(a, b, (((0,), (0,)), ((), ())), preferred_element_type=F32)


def _cast_rows(dst_ref, src_ref, rows):
    n = src_ref.shape[0]
    for r in range(0, n, rows):
        dst_ref[r:r + rows, :] = src_ref[r:r + rows, :].astype(BF16)


def _modulated_norm(x, g, shift, scale):
    ms = jnp.mean(x * x, axis=-1, keepdims=True)
    return (x * lax.rsqrt(ms + EPS) * g) * (1.0 + scale) + shift


def _ada_kernel(c_ref, w_ref, b_ref, o_ref):
    c = c_ref[...]
    s = c * jax.nn.sigmoid(c)
    o_ref[0] = _dot(s.astype(BF16), w_ref[0].astype(BF16)) + b_ref[0]


def _ada(cond, w_ada, b_ada):
    depth = w_ada.shape[0]
    tn = 1536
    nt = 6 * D_MODEL // tn
    vm = 2 * _nbytes((D_MODEL, tn), F32) + _nbytes((D_MODEL, tn), BF16) + (4 << 20)
    return pl.pallas_call(
        _ada_kernel,
        out_shape=jax.ShapeDtypeStruct((depth, SUBLANES, 6 * D_MODEL), F32),
        grid=(depth, nt),
        in_specs=[
            pl.BlockSpec((SUBLANES, D_MODEL), lambda l, j: (0, 0)),
            pl.BlockSpec((1, D_MODEL, tn), lambda l, j: (l, 0, j)),
            pl.BlockSpec((1, 1, tn), lambda l, j: (l, 0, j)),
        ],
        out_specs=pl.BlockSpec((1, SUBLANES, tn), lambda l, j: (l, 0, j)),
        compiler_params=_params(vm, 2),
        name="ada",
    )(*_hbm(cond, w_ada, b_ada.reshape(depth, 1, 6 * D_MODEL)))


def _rope_apply(x, cos, sin_signed):
    lane = lax.broadcasted_iota(I32, (1, LANES), 1)
    first = (lane % 16) < 8
    outs = []
    for t in range(x.shape[1] // LANES):
        xt = x[:, t * LANES:(t + 1) * LANES]
        partner = jnp.where(first, pltpu.roll(xt, LANES - 8, axis=1), pltpu.roll(xt, 8, axis=1))
        outs.append(xt * cos[:, t * LANES:(t + 1) * LANES] + partner * sin_signed[:, t * LANES:(t + 1) * LANES])
    return jnp.concatenate(outs, axis=1)


def _proj_in_kernel(*refs, rope, n_alias):
    x_ref, mod_ref, g_ref, w_ref = refs[:4]
    pos = 4
    if rope:
        cos_ref, sin_ref = refs[pos:pos + 2]
        pos += 2
    pos += n_alias
    qa_ref, ka_ref, va_ref, qb_ref, kb_ref, vb_ref, u_ref, wbf_ref = refs[pos:pos + 8]

    def put(ref, val):
        for s in range(ref.shape[1]):
            ref[0, s] = val.astype(ref.dtype)

    @pl.when((pl.program_id(0) == 0) & (pl.program_id(1) == 0))
    def _():
        _cast_rows(wbf_ref, w_ref, 128)

    h = _modulated_norm(x_ref[0], g_ref[...], mod_ref[0, 0:1, :], mod_ref[0, 1:2, :])
    z = _dot(h.astype(BF16), wbf_ref[...])
    i1, i2, i3 = NA_WIDTH, 2 * NA_WIDTH, 3 * NA_WIDTH
    i4, i5, i6 = i3 + DIFF_WIDTH, i3 + 2 * DIFF_WIDTH, i3 + 3 * DIFF_WIDTH
    qa_ref[0] = (z[:, :i1] * (HEAD_DIM ** -0.5 * LOG2E)).astype(qa_ref.dtype)
    put(ka_ref, z[:, i1:i2])
    put(va_ref, z[:, i2:i3])
    qb = z[:, i3:i4]
    kb = z[:, i4:i5]
    if rope:
        qb = _rope_apply(qb, cos_ref[...], sin_ref[...])
        kb = _rope_apply(kb, cos_ref[...], sin_ref[...])
    qb_ref[0] = (qb * (DIFF_QK ** -0.5 * LOG2E)).astype(qb_ref.dtype)
    put(kb_ref, kb)
    put(vb_ref, z[:, i5:i6])
    for hf in range(SSM_WIDTH // LANES):
        u_ref[0, hf] = z[:, i6 + hf * LANES:i6 + (hf + 1) * LANES]


def _proj_in(x, mod, g_norm, w_in, layer, rope_tabs, cache_slots, caches):
    b, n, d = x.shape
    tm = ROW_TILE
    bm = mod.shape[0]
    rope = rope_tabs is not None
    mod_map = (lambda i, j: (i, 0, 0)) if bm > 1 else (lambda i, j: (0, 0, 0))
    in_specs = [
        pl.BlockSpec((1, tm, d), lambda i, j: (i, j, 0)),
        pl.BlockSpec((1, 6, d), mod_map),
        pl.BlockSpec((None, 1, d), lambda i, j: (layer, 0, 0)),
        pl.BlockSpec((None, d, IN_WIDTH), lambda i, j: (layer, 0, 0), pipeline_mode=pl.Buffered(1)),
    ]
    args = [x, mod, g_norm, w_in]
    if rope:
        in_specs += [pl.BlockSpec((tm, DIFF_WIDTH), lambda i, j: (j, 0))] * 2
        args += list(rope_tabs)
    aliases = {}
    kv_out = (1, 2, 4, 5)
    if caches is not None:
        for c, o in zip(caches, kv_out):
            aliases[len(args)] = o
            in_specs.append(pl.BlockSpec(memory_space=pl.ANY))
            args.append(c)
    widths = (NA_WIDTH, NA_WIDTH, NA_WIDTH, DIFF_WIDTH, DIFF_WIDTH, DIFF_WIDTH)
    nslab = SSM_WIDTH // LANES
    out_shape, out_specs = [], []
    for o, w in enumerate(widths):
        if o not in kv_out:
            out_shape.append(jax.ShapeDtypeStruct((b, n, w), BF16))
            out_specs.append(pl.BlockSpec((1, tm, w), lambda i, j: (i, j, 0)))
        elif cache_slots == 0:
            out_shape.append(jax.ShapeDtypeStruct((b, 1, n, w), BF16))
            out_specs.append(pl.BlockSpec((1, 1, tm, w), lambda i, j: (i, 0, j, 0)))
        elif caches is None:
            out_shape.append(jax.ShapeDtypeStruct((b, cache_slots, n, w), F32))
            out_specs.append(pl.BlockSpec((1, cache_slots, tm, w), lambda i, j: (i, 0, j, 0)))
        else:
            out_shape.append(jax.ShapeDtypeStruct((b, cache_slots, n, w), F32))
            out_specs.append(pl.BlockSpec((1, 1, tm, w), lambda i, j: (i, layer, j, 0)))
    out_shape.append(jax.ShapeDtypeStruct((b, nslab, n, LANES), F32))
    out_specs.append(pl.BlockSpec((1, nslab, tm, LANES), lambda i, j: (i, 0, j, 0)))
    return pl.pallas_call(
        functools.partial(_proj_in_kernel, rope=rope, n_alias=len(aliases)),
        out_shape=out_shape,
        grid=(b, n // tm),
        in_specs=in_specs,
        out_specs=out_specs,
        scratch_shapes=[pltpu.VMEM((d, IN_WIDTH), BF16)],
        input_output_aliases=aliases,
        compiler_params=_params(0, 2),
        name="proj_in",
    )(*args)


def _masked_query(qp, mask):
    return jnp.where(mask, qp, jnp.zeros_like(qp))


def _ones_outside(v, head_mask):
    return jnp.where(head_mask, v, jnp.ones_like(v))


def _softmax_av(qm, ks, vs, biases, den_lane):
    ss = []
    for k, bias in zip(ks, biases):
        s = _dot_nt(qm, k)
        if bias is not None:
            s = bias(s)
        ss.append(s)
    m = ss[0].max(axis=-1, keepdims=True)
    for s in ss[1:]:
        m = jnp.maximum(m, s.max(axis=-1, keepdims=True))
    acc = None
    den = None
    for s, v in zip(ss, vs):
        p = jnp.exp2(s - m)
        if den_lane is None:
            psum = p.sum(axis=-1, keepdims=True)
            den = psum if den is None else den + psum
        pv = _dot(p.astype(BF16), v)
        acc = pv if acc is None else acc + pv
    return acc / (den if den_lane is None else acc[:, den_lane:den_lane + 1])


def _diff_lambda(lam_ref, lam_init):
    lp = lam_ref[...]
    a = jnp.sum(lp[0:1] * lp[1:2], axis=-1, keepdims=True)
    b = jnp.sum(lp[2:3] * lp[3:4], axis=-1, keepdims=True)
    return jnp.exp(a) - jnp.exp(b) + lam_init


def _attn_kernel(*refs, has_ctx, diff, lam_init):
    refs = list(refs)
    q_ref, k_ref, v_ref = refs[:3]
    pos = 3
    if has_ctx:
        kc_ref, vc_ref = refs[pos:pos + 2]
        pos += 2
    if diff:
        lam_ref, subln_ref = refs[pos:pos + 2]
        pos += 2
    o_ref = refs[pos]
    width = q_ref.shape[2]
    lane = lax.broadcasted_iota(I32, (1, LANES), 1)
    if diff:
        lam = _diff_lambda(lam_ref, lam_init)
    for p in range(width // LANES):
        sl = slice(p * LANES, (p + 1) * LANES)
        qp = q_ref[0, :, sl]
        ks = [k_ref[0, :, sl].astype(BF16)]
        vs = [v_ref[0, :, sl].astype(BF16)]
        if has_ctx:
            ks.append(kc_ref[:, sl].astype(BF16))
            vs.append(vc_ref[:, sl].astype(BF16))
        nob = [None] * len(ks)
        pair = None
        for sub in range(LANES // HEAD_DIM):
            head_mask = (lane // HEAD_DIM) == sub
            if sum(k.shape[0] for k in ks) >= DEN_IN_MATMUL_MIN_KEYS:
                den_lane = (1 - sub) * HEAD_DIM
                vh = [_ones_outside(v, head_mask) for v in vs]
            else:
                den_lane, vh = None, vs
            if diff:
                os_ = []
                for half in range(2):
                    qmask = (lane // DIFF_QK) == (2 * sub + half)
                    os_.append(_softmax_av(_masked_query(qp, qmask), ks, vh, nob, den_lane))
                o = jnp.where(head_mask, os_[0] - lam * os_[1], 0.0)
                ms = jnp.sum(o * o, axis=-1, keepdims=True) * (1.0 / DIFF_V)
                o = (o * lax.rsqrt(ms + EPS) * subln_ref[...]) * (1.0 - lam_init)
            else:
                o = _softmax_av(_masked_query(qp, head_mask), ks, vh, nob, den_lane)
                o = jnp.where(head_mask, o, 0.0)
            pair = o if pair is None else pair + o
        o_ref[0, :, sl] = pair.astype(o_ref.dtype)


def _attention(q, k, v, kv_slot, ctx_kv, layer, diff_params, lam_init):
    b, nq, w = q.shape
    nk = k.shape[2]
    tq = ROW_TILE
    has_ctx = ctx_kv is not None
    diff = diff_params is not None
    in_specs = [
        pl.BlockSpec((1, tq, w), lambda i, j: (i, j, 0)),
        pl.BlockSpec((1, None, nk, w), lambda i, j: (i, kv_slot, 0, 0)),
        pl.BlockSpec((1, None, nk, w), lambda i, j: (i, kv_slot, 0, 0)),
    ]
    args = [q, k, v]
    nc = 0
    if has_ctx:
        nc = ctx_kv[0].shape[2]
        in_specs += [pl.BlockSpec((None, None, nc, w), lambda i, j: (i, layer, 0, 0))] * 2
        args += list(ctx_kv)
    if diff:
        lam_p, subln = diff_params
        in_specs += [
            pl.BlockSpec((None, 4, DIFF_QK), lambda i, j: (layer, 0, 0)),
            pl.BlockSpec((None, 1, LANES), lambda i, j: (layer, 0, 0)),
        ]
        args += [lam_p, subln]
    vm = (4 * _nbytes((nk + nc, w), F32) + 2 * _nbytes((nk + nc, LANES), BF16)
          + 8 * _nbytes((tq, nk + nc), F32) + 8 * _nbytes((tq, w), F32) + (4 << 20))
    return pl.pallas_call(
        functools.partial(_attn_kernel, has_ctx=has_ctx, diff=diff, lam_init=lam_init),
        out_shape=jax.ShapeDtypeStruct((b, nq, w), BF16),
        grid=(b, nq // tq),
        in_specs=in_specs,
        out_specs=pl.BlockSpec((1, tq, w), lambda i, j: (i, j, 0)),
        compiler_params=_params(vm, 2),
        name="attn_diff" if diff else "attn",
    )(*_hbm(*args))


def _na_kernel(q_ref, k_ref, v_ref, kc_ref, vc_ref, bias_ref, o_ref, *, rows):
    i = pl.program_id(1)
    tq = NA_QROWS * GRID_W
    nkw = NA_KROWS * GRID_W
    ustart = jnp.clip(NA_QROWS * i - WIN_R // 2, 0, rows - NA_KROWS)
    koff = pl.multiple_of(ustart * GRID_W, GRID_W)
    delta = ustart - NA_QROWS * i
    qrow = NA_QROWS * i + lax.broadcasted_iota(I32, (tq, nkw), 0) // GRID_W
    krow = ustart + lax.broadcasted_iota(I32, (tq, nkw), 1) // GRID_W
    wstart = jnp.clip(qrow - WIN_R // 2, 0, rows - WIN_R)
    row_ok = (krow >= wstart) & (krow < wstart + WIN_R)
    lane = lax.broadcasted_iota(I32, (1, LANES), 1)
    for p in range(NA_WIDTH // LANES):
        sl = slice(p * LANES, (p + 1) * LANES)
        qp = q_ref[0, :, sl]
        ks = [k_ref[0, pl.ds(koff, nkw), sl].astype(BF16), kc_ref[:, sl].astype(BF16)]
        vs = [v_ref[0, pl.ds(koff, nkw), sl].astype(BF16), vc_ref[:, sl].astype(BF16)]
        pair = None
        for sub in range(LANES // HEAD_DIM):
            h = p * (LANES // HEAD_DIM) + sub
            head_mask = (lane // HEAD_DIM) == sub

            def win_bias(s, h=h):
                rows_ = []
                for ri in range(NA_QROWS):
                    tiles = []
                    for pj in range(NA_KROWS // 2):
                        dr = delta + 2 * pj - ri
                        idx = jnp.clip(dr, -WIN_R, WIN_R - 1) + WIN_R
                        tiles.append(bias_ref[h, idx])
                    rows_.append(jnp.concatenate(tiles, axis=1))
                bias = jnp.concatenate(rows_, axis=0)
                return jnp.where(row_ok, s + bias, NEG_INF)

            vh = [_ones_outside(v, head_mask) for v in vs]
            o = _softmax_av(_masked_query(qp, head_mask), ks, vh, [win_bias, None], (1 - sub) * HEAD_DIM)
            o = jnp.where(head_mask, o, 0.0)
            pair = o if pair is None else pair + o
        o_ref[0, :, sl] = pair.astype(o_ref.dtype)


def _na_table_kernel(v_ref, o_ref):
    nd = v_ref.shape[0]
    q = lax.broadcasted_iota(I32, (GRID_W, 2 * GRID_W), 0)
    k = lax.broadcasted_iota(I32, (GRID_W, 2 * GRID_W), 1)
    kk = k % GRID_W
    cstart = jnp.clip(q - WIN_C // 2, 0, GRID_W - WIN_C)
    col_ok = (kk >= cstart) & (kk < cstart + WIN_C)
    left = k < GRID_W
    tiles = []
    for dr in range(nd):
        x = jnp.broadcast_to(v_ref[dr:dr + 1, :], (GRID_W, 2 * GRID_W))
        tiles.append((pltpu.roll(x, GRID_W + 1, axis=1, stride=1, stride_axis=0),
                      pltpu.roll(x, 1, axis=1, stride=1, stride_axis=0)))
    zero = jnp.zeros((GRID_W, 2 * GRID_W), F32)
    for j in range(nd + 1):
        lt = tiles[j - 1][0] if j >= 1 else zero
        rt = tiles[j][1] if j < nd else zero
        o_ref[j] = jnp.where(col_ok, jnp.where(left, lt, rt) * LOG2E, NEG_INF)


def _na_bias_table(rpb):
    depth, h, nd, nc = rpb.shape
    assert 2 * GRID_W == LANES and nd == 2 * WIN_R - 1 and nc == 2 * WIN_C - 1
    rpb = rpb.astype(F32)
    lo = GRID_W - WIN_C
    v = jnp.concatenate([jnp.broadcast_to(rpb[..., :1], rpb.shape[:3] + (lo,)), rpb,
                         jnp.broadcast_to(rpb[..., -1:], rpb.shape[:3] + (2 * GRID_W - lo - nc,))], axis=-1)
    return pl.pallas_call(
        _na_table_kernel,
        out_shape=jax.ShapeDtypeStruct((depth, h, nd + 1, GRID_W, 2 * GRID_W), F32),
        grid=(depth, h),
        in_specs=[pl.BlockSpec((None, None, nd, 2 * GRID_W), lambda l, i: (l, i, 0, 0))],
        out_specs=pl.BlockSpec((None, None, nd + 1, GRID_W, 2 * GRID_W), lambda l, i: (l, i, 0, 0, 0)),
        compiler_params=_params(0, 2),
        name="na_table",
    )(v)


def _neighbourhood_attention(q, k, v, kc, vc, bias_tab, layer):
    b, n, w = q.shape
    rows = n // GRID_W
    tq = NA_QROWS * GRID_W
    nc = kc.shape[2]
    nkw = NA_KROWS * GRID_W
    vm = (4 * _nbytes((n, w), F32) + 4 * _nbytes((nc, w), F32) + 2 * _nbytes(bias_tab.shape, F32)
          + 12 * _nbytes((tq, nkw + nc), F32) + (6 << 20))
    return pl.pallas_call(
        functools.partial(_na_kernel, rows=rows),
        out_shape=jax.ShapeDtypeStruct((b, n, w), BF16),
        grid=(b, n // tq),
        in_specs=[
            pl.BlockSpec((1, tq, w), lambda i, j: (i, j, 0)),
            pl.BlockSpec((1, None, n, w), lambda i, j: (i, 0, 0, 0)),
            pl.BlockSpec((1, None, n, w), lambda i, j: (i, 0, 0, 0)),
            pl.BlockSpec((None, None, nc, w), lambda i, j: (i, layer, 0, 0)),
            pl.BlockSpec((None, None, nc, w), lambda i, j: (i, layer, 0, 0)),
            pl.BlockSpec((None,) + bias_tab.shape[1:], lambda i, j: (layer, 0, 0, 0, 0)),
        ],
        out_specs=pl.BlockSpec((1, tq, w), lambda i, j: (i, j, 0)),
        compiler_params=_params(vm, 2),
        name="na_attn",
    )(*_hbm(q, k, v, kc, vc, bias_tab))


def _sublane_scan(er, ei, tr, ti, pw_ref, base, lanes, reverse):
    row = lax.broadcasted_iota(I32, (SUBLANES, LANES), 0)
    if reverse:
        first = row == SUBLANES - 1
        xr = jnp.where(first, tr, pltpu.roll(er, SUBLANES - 1, axis=0))
        xi = jnp.where(first, ti, pltpu.roll(ei, SUBLANES - 1, axis=0))
    else:
        first = row == 0
        xr = jnp.where(first, tr, pltpu.roll(er, 1, axis=0))
        xi = jnp.where(first, ti, pltpu.roll(ei, 1, axis=0))
    for k, s in enumerate((1, 2, 4)):
        ar = pw_ref[base + 2 * k:base + 2 * k + 1, lanes]
        ai = pw_ref[base + 2 * k + 1:base + 2 * k + 2, lanes]
        if reverse:
            keep = row < SUBLANES - s
            sr = jnp.where(keep, pltpu.roll(xr, SUBLANES - s, axis=0), 0.0)
            si = jnp.where(keep, pltpu.roll(xi, SUBLANES - s, axis=0), 0.0)
        else:
            keep = row >= s
            sr = jnp.where(keep, pltpu.roll(xr, s, axis=0), 0.0)
            si = jnp.where(keep, pltpu.roll(xi, s, axis=0), 0.0)
        xr, xi = xr + (ar * sr - ai * si), xi + (ar * si + ai * sr)
    return xr, xi


def _s5_kernel(u_ref, h0re_ref, h0im_ref, bmat_ref, cmat_ref, a1_ref, ach_ref, pfix_ref, d_ref, wglu_ref, bglu_ref,
               o_ref, fre_ref, fim_ref, up_ref, yp_ref, buf_ref, car_ref):
    seq = u_ref.shape[2]
    tt = SCAN_TILE
    ch = tt // SUBLANES
    ntile = seq // tt
    nlt = SSM_N // LANES
    nhalf = SSM_WIDTH // LANES

    def permute_in(t, _):
        t0 = pl.multiple_of(t * tt, tt)
        for j in range(0, ch, 2):
            for hf in range(nhalf):
                lanes = slice(hf * LANES, (hf + 1) * LANES)
                rows = jnp.concatenate([u_ref[0, hf, pl.ds(t0 + j + jj, SUBLANES, stride=ch), :] for jj in range(2)],
                                       axis=0)
                r0 = pl.multiple_of(t0 + j * SUBLANES, 2 * SUBLANES)
                up_ref[pl.ds(r0, 2 * SUBLANES), lanes] = rows.astype(BF16)
                yp_ref[pl.ds(r0, 2 * SUBLANES), lanes] = rows * d_ref[:, lanes]
        return 0

    lax.fori_loop(0, ntile, permute_in, 0)

    for d in range(2):
        for lt in range(nlt):
            lanes = slice(lt * LANES, (lt + 1) * LANES)
            car_ref[d, 0, :, lanes] = jnp.broadcast_to(h0re_ref[0, d:d + 1, lanes], (SUBLANES, LANES))
            car_ref[d, 1, :, lanes] = jnp.broadcast_to(h0im_ref[0, d:d + 1, lanes], (SUBLANES, LANES))

    def tile(t, _):
        starts = []
        for d in range(2):
            tix = (ntile - 1 - t) if d == 1 else t
            t0 = pl.multiple_of(tix * tt, tt)
            starts.append(t0)
            buf_ref[d] = _dot(up_ref[pl.ds(t0, tt), :], bmat_ref[:, d * 2 * SSM_N:(d + 1) * 2 * SSM_N])
        for d in range(2):
            reverse = d == 1
            order = range(ch - 1, -1, -1) if reverse else range(ch)
            for lt in range(nlt):
                lre = slice(lt * LANES, (lt + 1) * LANES)
                lim = slice(SSM_N + lt * LANES, SSM_N + (lt + 1) * LANES)
                ar = a1_ref[2 * d:2 * d + 1, lre]
                ai = a1_ref[2 * d + 1:2 * d + 2, lre]
                cr = jnp.zeros((SUBLANES, LANES), F32)
                ci = jnp.zeros((SUBLANES, LANES), F32)
                for j in order:
                    rows = slice(j * SUBLANES, (j + 1) * SUBLANES)
                    cr, ci = (ar * cr - ai * ci) + buf_ref[d, rows, lre], (ar * ci + ai * cr) + buf_ref[d, rows, lim]
                    buf_ref[d, rows, lre] = cr
                    buf_ref[d, rows, lim] = ci
                tr, ti = car_ref[d, 0, :, lre], car_ref[d, 1, :, lre]
                gr, gi = _sublane_scan(cr, ci, tr, ti, ach_ref, 6 * d, lre, reverse)
                a_r, a_i = ach_ref[6 * d:6 * d + 1, lre], ach_ref[6 * d + 1:6 * d + 2, lre]
                nr, ni = (a_r * gr - a_i * gi) + cr, (a_r * gi + a_i * gr) + ci
                edge = 0 if reverse else SUBLANES - 1
                car_ref[d, 0, :, lre] = jnp.broadcast_to(nr[edge:edge + 1, :], (SUBLANES, LANES))
                car_ref[d, 1, :, lre] = jnp.broadcast_to(ni[edge:edge + 1, :], (SUBLANES, LANES))
                for j in range(ch):
                    rows = slice(j * SUBLANES, (j + 1) * SUBLANES)
                    pr = pfix_ref[2 * d, j:j + 1, lre]
                    pi = pfix_ref[2 * d + 1, j:j + 1, lre]
                    buf_ref[d, rows, lre] += pr * gr - pi * gi
                    buf_ref[d, rows, lim] += pr * gi + pi * gr
        for d in range(2):
            yp_ref[pl.ds(starts[d], tt), :] += _dot(buf_ref[d].astype(BF16),
                                                    cmat_ref[d * 2 * SSM_N:(d + 1) * 2 * SSM_N, :])
        return 0

    lax.fori_loop(0, ntile, tile, 0)

    for d in range(2):
        fre_ref[0, d:d + 1, :] = car_ref[d, 0, 0:1, :]
        fim_ref[0, d:d + 1, :] = car_ref[d, 1, 0:1, :]

    wglu = wglu_ref[...].astype(BF16)

    def glu(t, _):
        t0 = pl.multiple_of(t * tt, tt)
        g = jax.nn.gelu(yp_ref[pl.ds(t0, tt), :])
        out = g * jax.nn.sigmoid(_dot(g.astype(BF16), wglu) + bglu_ref[...])
        for j in range(ch):
            for hf in range(nhalf):
                o_ref[0, hf, pl.ds(t0 + j, SUBLANES, stride=ch), :] = out[j * SUBLANES:(j + 1) * SUBLANES,
                                                                          hf * LANES:(hf + 1) * LANES]
        return 0

    lax.fori_loop(0, ntile, glu, 0)


def _cmul(ar, ai, br, bi):
    return ar * br - ai * bi, ar * bi + ai * br


def _s5_tables(a_re, a_im, log_dt, b_re, b_im, c_re, c_im):
    depth = a_re.shape[0]
    g, p, hc = SSM_GROUPS, SSM_STATE, SSM_GROUP_CH
    ch = SCAN_TILE // SUBLANES
    eye = jnp.eye(g, dtype=F32)
    lr, li = a_re.astype(F32), a_im.astype(F32)
    dt = jnp.exp(log_dt.astype(F32))[..., None]
    mag = jnp.exp(lr * dt)
    ar, ai = mag * jnp.cos(li * dt), mag * jnp.sin(li * dt)
    den = lr * lr + li * li
    qr = ((ar - 1.0) * lr + ai * li) / den
    qi = (ai * lr - (ar - 1.0) * li) / den
    br, bi = _cmul(qr[..., None], qi[..., None], b_re.astype(F32), b_im.astype(F32))
    bparts = jnp.stack([br, bi], axis=2)
    bmat = bparts.transpose(0, 3, 5, 1, 2, 4)[:, :, :, :, :, None, :] * eye[None, :, None, None, None, :, None]
    bmat = bmat.reshape(depth, g * hc, 4 * g * p).astype(BF16)
    cparts = jnp.stack([c_re.astype(F32), -c_im.astype(F32)], axis=2)
    cmat = cparts.transpose(0, 1, 2, 3, 5, 4)[:, :, :, :, :, None, :] * eye[None, None, None, :, None, :, None]
    cmat = cmat.reshape(depth, 4 * g * p, g * hc).astype(BF16)

    def power(k):
        kk = k[None, None, :, None, None]
        m = jnp.exp(kk * (lr * dt)[:, :, None])
        th = kk * (li * dt)[:, :, None]
        return (m * jnp.cos(th)).reshape(depth, 2, -1, g * p), (m * jnp.sin(th)).reshape(depth, 2, -1, g * p)

    a1 = jnp.stack([ar.reshape(depth, 2, g * p), ai.reshape(depth, 2, g * p)], axis=2).reshape(depth, 4, g * p)
    cr_, ci_ = power(jnp.asarray([ch, 2 * ch, 4 * ch], F32))
    ach = jnp.stack([cr_, ci_], axis=3).reshape(depth, 12, g * p)
    fr, fi = power(jnp.arange(1, ch + 1, dtype=F32))
    fr = jnp.stack([fr[:, 0], fr[:, 1, ::-1]], axis=1)
    fi = jnp.stack([fi[:, 0], fi[:, 1, ::-1]], axis=1)
    pfix = jnp.stack([fr, fi], axis=2).reshape(depth, 4, ch, g * p)
    return bmat, cmat, a1, ach, pfix


def _s5(u, h0re, h0im, tabs, d_skip, w_glu, b_glu, layer):
    b, nh, seq, _ = u.shape
    w = nh * LANES
    lay = lambda a: pl.BlockSpec((None,) + a.shape[1:], lambda i: (layer,) + (0,) * (a.ndim - 1))
    return pl.pallas_call(
        _s5_kernel,
        out_shape=[jax.ShapeDtypeStruct((b, nh, seq, LANES), F32),
                   jax.ShapeDtypeStruct((b, 2, SSM_N), F32),
                   jax.ShapeDtypeStruct((b, 2, SSM_N), F32)],
        grid=(b,),
        in_specs=[
            pl.BlockSpec((1, nh, seq, LANES), lambda i: (i, 0, 0, 0)),
            pl.BlockSpec((1, 2, SSM_N), lambda i: (i, 0, 0)),
            pl.BlockSpec((1, 2, SSM_N), lambda i: (i, 0, 0)),
            lay(tabs[0]), lay(tabs[1]), lay(tabs[2]), lay(tabs[3]), lay(tabs[4]),
            pl.BlockSpec((None, 1, w), lambda i: (layer, 0, 0)),
            pl.BlockSpec((None, w, w), lambda i: (layer, 0, 0)),
            pl.BlockSpec((None, 1, w), lambda i: (layer, 0, 0)),
        ],
        out_specs=[pl.BlockSpec((1, nh, seq, LANES), lambda i: (i, 0, 0, 0)),
                   pl.BlockSpec((1, 2, SSM_N), lambda i: (i, 0, 0)),
                   pl.BlockSpec((1, 2, SSM_N), lambda i: (i, 0, 0))],
        scratch_shapes=[pltpu.VMEM((seq, w), BF16),
                        pltpu.VMEM((seq, w), F32),
                        pltpu.VMEM((2, SCAN_TILE, 2 * SSM_N), F32),
                        pltpu.VMEM((2, 2, SUBLANES, SSM_N), F32)],
        compiler_params=_params(0, 1),
        name="s5",
    )(u, h0re, h0im, *tabs, d_skip, w_glu, b_glu)


def _proj_out_kernel(oa_ref, ob_ref, oc_ref, w_ref, x_ref, mod_ref, g_ref, wr_ref, o_ref, h_ref, lg_ref, wbf_ref):
    @pl.when((pl.program_id(0) == 0) & (pl.program_id(1) == 0))
    def _():
        _cast_rows(wbf_ref, w_ref, 128)

    i1, i2 = NA_WIDTH, NA_WIDTH + DIFF_WIDTH
    y = _dot(oa_ref[0].astype(BF16), wbf_ref[:i1, :])
    y += _dot(ob_ref[0].astype(BF16), wbf_ref[i1:i2, :])
    for hf in range(SSM_WIDTH // LANES):
        y += _dot(oc_ref[0, hf].astype(BF16), wbf_ref[i2 + hf * LANES:i2 + (hf + 1) * LANES, :])
    xn = x_ref[0] + mod_ref[0, 2:3, :] * y
    o_ref[0] = xn
    h = _modulated_norm(xn, g_ref[...], mod_ref[0, 3:4, :], mod_ref[0, 4:5, :])
    h_hi = h.astype(BF16)
    h_ref[0] = h_hi
    h_lo = (h - h_hi.astype(F32)).astype(BF16)
    t1 = _dot_nt(wr_ref[...], h_hi)
    t2 = _dot_nt(wr_ref[:N_EXPERTS, :], h_lo)
    lg_ref[0] = t1[:N_EXPERTS, :] + t1[N_EXPERTS:, :] + t2


def _proj_out(oa, ob, oc, w_out, x, mod, g_ffn, wr_t, layer):
    b, n, d = x.shape
    tm = ROW_TILE
    bm = mod.shape[0]
    mod_map = (lambda i, j: (i, 0, 0)) if bm > 1 else (lambda i, j: (0, 0, 0))
    kw = w_out.shape[1]
    vm = _nbytes((kw, d), F32) + _nbytes((kw, d), BF16) + 10 * _nbytes((tm, d), F32) + (4 << 20)
    return pl.pallas_call(
        _proj_out_kernel,
        out_shape=[jax.ShapeDtypeStruct((b, n, d), F32),
                   jax.ShapeDtypeStruct((b, n, d), BF16),
                   jax.ShapeDtypeStruct((b, N_EXPERTS, n), F32)],
        grid=(b, n // tm),
        in_specs=[
            pl.BlockSpec((1, tm, NA_WIDTH), lambda i, j: (i, j, 0)),
            pl.BlockSpec((1, tm, DIFF_WIDTH), lambda i, j: (i, j, 0)),
            pl.BlockSpec((1, SSM_WIDTH // LANES, tm, LANES), lambda i, j: (i, 0, j, 0)),
            pl.BlockSpec((None, kw, d), lambda i, j: (layer, 0, 0), pipeline_mode=pl.Buffered(1)),
            pl.BlockSpec((1, tm, d), lambda i, j: (i, j, 0)),
            pl.BlockSpec((1, 6, d), mod_map),
            pl.BlockSpec((None, 1, d), lambda i, j: (layer, 0, 0)),
            pl.BlockSpec((None, 2 * N_EXPERTS, d), lambda i, j: (layer, 0, 0)),
        ],
        out_specs=[pl.BlockSpec((1, tm, d), lambda i, j: (i, j, 0)),
                   pl.BlockSpec((1, tm, d), lambda i, j: (i, j, 0)),
                   pl.BlockSpec((1, N_EXPERTS, tm), lambda i, j: (i, 0, j))],
        scratch_shapes=[pltpu.VMEM((kw, d), BF16)],
        compiler_params=_params(vm, 2),
        name="proj_out",
    )(*_hbm(oa, ob, oc, w_out, x, mod, g_ffn, wr_t))


def _excl_cumsum_lanes(m):
    e, n = m.shape
    blk = MXU_DIM
    nb = n // blk
    r = lax.broadcasted_iota(I32, (blk, blk), 0)
    c = lax.broadcasted_iota(I32, (blk, blk), 1)
    tri = jnp.where(r < c, 1.0, 0.0).astype(BF16)
    stacked = jnp.concatenate([m[:, k * blk:(k + 1) * blk] for k in range(nb)], axis=0).astype(BF16)
    within = _dot(stacked, tri)
    outs = []
    off = jnp.zeros((e, 1), F32)
    for k in range(nb):
        outs.append(within[k * e:(k + 1) * e, :] + off)
        off = off + jnp.sum(m[:, k * blk:(k + 1) * blk], axis=1, keepdims=True)
    return jnp.concatenate(outs, axis=1)


def _route_kernel(lg_ref, pos_ref, gate_ref, *, cap):
    b, e, n = lg_ref.shape
    lg = lg_ref[...]
    mx = lg.max(axis=1, keepdims=True)
    ex = jnp.exp(lg - mx)
    aff = (ex / ex.sum(axis=1, keepdims=True)).reshape(b * e, n)

    def search(it, cur):
        cand = cur | (jnp.int32(1) << (jnp.int32(30) - it))
        cnt = jnp.sum(jnp.where(aff >= pltpu.bitcast(cand, F32), 1.0, 0.0), axis=1, keepdims=True)
        return jnp.where(cnt >= cap, cand, cur)

    thr = pltpu.bitcast(lax.fori_loop(0, 31, search, jnp.zeros((b * e, 1), I32)), F32)
    gt = aff > thr
    eq = aff == thr
    need = cap - jnp.sum(jnp.where(gt, 1.0, 0.0), axis=1, keepdims=True)
    eq_rank = _excl_cumsum_lanes(jnp.where(eq, 1.0, 0.0))
    sel = gt | (eq & (eq_rank < need))
    slot = _excl_cumsum_lanes(jnp.where(sel, 1.0, 0.0))
    pos = jnp.where(sel, slot, -1.0).astype(I32)
    gate = jnp.where(sel, aff, 0.0)
    for i in range(b):
        pos_ref[:, i, 0, :] = pos[i * e:(i + 1) * e, :]
        gate_ref[:, i, 0, :] = gate[i * e:(i + 1) * e, :]


def _route(logits):
    b, e, n = logits.shape
    cap = EC_CAPACITY_FACTOR * n // N_EXPERTS
    return pl.pallas_call(
        functools.partial(_route_kernel, cap=cap),
        out_shape=[jax.ShapeDtypeStruct((e, b, 1, n), I32),
                   jax.ShapeDtypeStruct((e, b, 1, n), F32)],
        compiler_params=_params(0, 0),
        name="route",
    )(logits)


def _one_hot_rows(pos_row, cap):
    n = pos_row.shape[1]
    slot = lax.broadcasted_iota(I32, (cap, n), 0)
    return slot == pos_row


def _gather_kernel(h_ref, pos_ref, gate_ref, xs_ref, gs_ref, *, cap):
    h = h_ref[0]
    group = max(1, GATHER_ROWS // cap)
    for e0 in range(0, N_EXPERTS, group):
        ohs = [_one_hot_rows(pos_ref[e, 0], cap) for e in range(e0, e0 + group)]
        xs = _dot(jnp.concatenate([jnp.where(oh, 1.0, 0.0).astype(BF16) for oh in ohs], axis=0), h).astype(BF16)
        for i, oh in enumerate(ohs):
            e = e0 + i
            xs_ref[e] = xs[i * cap:(i + 1) * cap, :]
            g = jnp.sum(jnp.where(oh, gate_ref[e, 0], 0.0), axis=1, keepdims=True)
            gs_ref[e] = jnp.broadcast_to(g, (cap, LANES))


def _gather(h, pos, gate):
    b, n, d = h.shape
    cap = EC_CAPACITY_FACTOR * n // N_EXPERTS
    vm = 4 * _nbytes((n, d), BF16) + 6 * _nbytes((N_EXPERTS, cap, d), BF16) + 8 * _nbytes((cap, n), F32) + (8 << 20)
    return pl.pallas_call(
        functools.partial(_gather_kernel, cap=cap),
        out_shape=[jax.ShapeDtypeStruct((N_EXPERTS, b * cap, d), BF16),
                   jax.ShapeDtypeStruct((N_EXPERTS, b * cap, LANES), F32)],
        grid=(b,),
        in_specs=[
            pl.BlockSpec((1, n, d), lambda i: (i, 0, 0)),
            pl.BlockSpec((N_EXPERTS, 1, 1, n), lambda i: (0, i, 0, 0)),
            pl.BlockSpec((N_EXPERTS, 1, 1, n), lambda i: (0, i, 0, 0)),
        ],
        out_specs=[pl.BlockSpec((N_EXPERTS, cap, d), lambda i: (0, i, 0)),
                   pl.BlockSpec((N_EXPERTS, cap, LANES), lambda i: (0, i, 0))],
        compiler_params=_params(vm, 1),
        name="moe_gather",
    )(*_hbm(h, pos, gate))


def _ffn_kernel(xp_ref, gp_ref, xs_ref, gs_ref, wg_ref, wu_ref, wd_ref, yp_ref, ys_ref, acc_ref):
    j = pl.program_id(1)
    sp = xp_ref.shape[1]

    @pl.when(j == 0)
    def _():
        acc_ref[...] = jnp.zeros_like(acc_ref)

    xs = jnp.concatenate([xp_ref[0], xs_ref[0]], axis=0)
    a = _dot(xs, wg_ref[...].astype(BF16))
    u = _dot(xs, wu_ref[...].astype(BF16))
    hid = (a * jax.nn.sigmoid(a)) * u
    acc_ref[...] += _dot(hid.astype(BF16), wd_ref[...].astype(BF16))

    @pl.when(j == pl.num_programs(1) - 1)
    def _():
        yp_ref[0] = (acc_ref[:sp, :] * gp_ref[0][:, 0:1]).astype(BF16)
        ys_ref[0] = (acc_ref[sp:, :] * gs_ref[0][:, 0:1]).astype(BF16)


def _ffn(xp, gp, xs, gs, w_gate, w_up, w_down, layer):
    e, sp, d = xp.shape
    ss = xs.shape[1]
    s = sp + ss
    ff = w_gate.shape[-1]
    tf = FF_TILE
    vm = (6 * _nbytes((d, tf), F32) + 3 * _nbytes((d, tf), BF16) + 5 * _nbytes((s, d), BF16)
          + 2 * _nbytes((s, LANES), F32) + _nbytes((s, d), F32) + 4 * _nbytes((s, tf), F32) + (4 << 20))
    return pl.pallas_call(
        _ffn_kernel,
        out_shape=[jax.ShapeDtypeStruct((e, sp, d), BF16), jax.ShapeDtypeStruct((e, ss, d), BF16)],
        grid=(e, ff // tf),
        in_specs=[
            pl.BlockSpec((1, sp, d), lambda i, j: (i, 0, 0)),
            pl.BlockSpec((1, sp, LANES), lambda i, j: (i, 0, 0)),
            pl.BlockSpec((1, ss, d), lambda i, j: (i, 0, 0)),
            pl.BlockSpec((1, ss, LANES), lambda i, j: (i, 0, 0)),
            pl.BlockSpec((None, None, d, tf), lambda i, j: (layer, i, 0, j)),
            pl.BlockSpec((None, None, d, tf), lambda i, j: (layer, i, 0, j)),
            pl.BlockSpec((None, None, tf, d), lambda i, j: (layer, i, j, 0)),
        ],
        out_specs=[pl.BlockSpec((1, sp, d), lambda i, j: (i, 0, 0)),
                   pl.BlockSpec((1, ss, d), lambda i, j: (i, 0, 0))],
        scratch_shapes=[pltpu.VMEM((s, d), F32)],
        compiler_params=_params(vm, 2),
        name="moe_ffn",
    )(*_hbm(xp, gp, xs, gs, w_gate, w_up, w_down))


def _combine_kernel(ye_ref, pos_ref, x_ref, mod_ref, *rest, cap, final):
    if final:
        g_ref, o_ref = rest
    else:
        (o_ref,) = rest
    group = max(1, MXU_DIM // cap)
    y = None
    for e0 in range(0, N_EXPERTS, group):
        oh = jnp.concatenate([jnp.where(_one_hot_rows(pos_ref[e, 0], cap), 1.0, 0.0).astype(BF16)
                              for e in range(e0, e0 + group)], axis=0)
        ye = jnp.concatenate([ye_ref[e] for e in range(e0, e0 + group)], axis=0)
        part = _dot_tn(oh, ye)
        y = part if y is None else y + part
    xn = x_ref[0] + mod_ref[0, 5:6, :] * y
    if final:
        ms = jnp.mean(xn * xn, axis=-1, keepdims=True)
        xn = xn * lax.rsqrt(ms + EPS) * g_ref[...]
    o_ref[0] = xn


def _combine(ye, pos, x, mod, final_norm):
    b, n, d = x.shape
    cap = EC_CAPACITY_FACTOR * n // N_EXPERTS
    tn = min(n, COMBINE_TILE)
    bm = mod.shape[0]
    mod_map = (lambda i, j: (i, 0, 0)) if bm > 1 else (lambda i, j: (0, 0, 0))
    final = final_norm is not None
    in_specs = [
        pl.BlockSpec((N_EXPERTS, cap, d), lambda i, j: (0, i, 0)),
        pl.BlockSpec((N_EXPERTS, 1, 1, tn), lambda i, j: (0, i, 0, j)),
        pl.BlockSpec((1, tn, d), lambda i, j: (i, j, 0)),
        pl.BlockSpec((1, 6, d), mod_map),
    ]
    args = [ye, pos, x, mod]
    if final:
        in_specs.append(pl.BlockSpec((1, d), lambda i, j: (0, 0)))
        args.append(final_norm)
    vm = (4 * _nbytes((N_EXPERTS, cap, d), BF16) + 10 * _nbytes((tn, d), F32)
          + 6 * _nbytes((max(cap, MXU_DIM), tn), F32) + (8 << 20))
    return pl.pallas_call(
        functools.partial(_combine_kernel, cap=cap, final=final),
        out_shape=jax.ShapeDtypeStruct((b, n, d), F32),
        grid=(b, n // tn),
        in_specs=in_specs,
        out_specs=pl.BlockSpec((1, tn, d), lambda i, j: (i, j, 0)),
        compiler_params=_params(vm, 2),
        name="moe_combine",
    )(*_hbm(*args))


def _rope_tables(n):
    t = np.arange(n)
    row = (t // GRID_W).astype(np.float32)
    col = (t % GRID_W).astype(np.float32)
    nf = DIFF_QK // 4
    inv = np.float32(ROPE_BASE) ** (-np.arange(nf, dtype=np.float32) / np.float32(nf))
    lane = np.arange(DIFF_WIDTH)
    pos = np.where(((lane % DIFF_QK) < DIFF_QK // 2)[None, :], row[:, None], col[:, None])
    ang = (pos * inv[lane % nf][None, :]).astype(np.float32)
    first = (lane % (2 * nf)) < nf
    cos, sin = np.cos(ang).astype(np.float32), np.sin(ang).astype(np.float32)
    return jnp.asarray(cos), jnp.asarray(np.where(first[None, :], -sin, sin))


def kernel(x_prompt, x_sample, cache_na_k, cache_na_v, cache_diff_k, cache_diff_v, state_ssm_re, state_ssm_im,
           c, c_ctx, w_ada, b_ada, norm_mix, norm_ffn, w_in, w_out, na_rpb, diff_lambda, diff_subln,
           ssm_a_re, ssm_a_im, ssm_log_dt, ssm_b_re, ssm_b_im, ssm_c_re, ssm_c_im, ssm_d, ssm_w_glu, ssm_b_glu,
           w_router, w_gate, w_up, w_down, final_norm):
    depth = w_in.shape[0]
    bp, sp, d = x_prompt.shape
    bs, ss, _ = x_sample.shape
    assert d == D_MODEL and bs + 1 <= SUBLANES
    past = cache_na_k.shape[2]

    cond = jnp.zeros((SUBLANES, d), F32).at[0].set(c_ctx).at[1:1 + bs].set(c)
    mods = _ada(cond, w_ada, b_ada).reshape(depth, SUBLANES, 6, d)

    rope_tabs = _rope_tables(ss)
    kc_a = cache_na_k.reshape(bs, depth, past, NA_WIDTH)
    vc_a = cache_na_v.reshape(bs, depth, past, NA_WIDTH)
    kc_b = cache_diff_k.reshape(bs, depth, past, DIFF_WIDTH)
    vc_b = cache_diff_v.reshape(bs, depth, past, DIFF_WIDTH)
    subln = jnp.tile(diff_subln, (1, LANES // DIFF_V)).reshape(depth, 1, LANES)
    norm_mix = norm_mix.reshape(depth, 1, d)
    norm_ffn = norm_ffn.reshape(depth, 1, d)
    ssm_d = ssm_d.reshape(depth, 1, SSM_WIDTH)
    ssm_b_glu = ssm_b_glu.reshape(depth, 1, SSM_WIDTH)
    wr_t = jnp.swapaxes(w_router, 1, 2).astype(F32)
    wr_hi = wr_t.astype(BF16)
    wr_t = jnp.concatenate([wr_hi, (wr_t - wr_hi.astype(F32)).astype(BF16)], axis=1)
    fnorm = final_norm.reshape(1, d)
    zero_state = jnp.zeros((bp, 2, SSM_N), F32)
    tabs = _s5_tables(ssm_a_re, ssm_a_im, ssm_log_dt, ssm_b_re, ssm_b_im, ssm_c_re, ssm_c_im)
    bias_tab = _na_bias_table(na_rpb)

    xp, xs = x_prompt, x_sample
    new_sre, new_sim = [], []
    caches = None
    for l in range(depth):
        lam_init = 0.8 - 0.6 * math.exp(-0.3 * l)
        mod_p = mods[l, 0:1]
        mod_s = mods[l, 1:1 + bs]
        diff_params = (diff_lambda, subln)

        qa, ka, va, qb, kb, vb, u = _proj_in(xp, mod_p, norm_mix, w_in, l, None, depth, caches)
        caches = (ka, va, kb, vb)
        o_a = _attention(qa, ka, va, l, None, l, None, lam_init)
        o_b = _attention(qb, kb, vb, l, None, l, diff_params, lam_init)
        o_c, fre, fim = _s5(u, zero_state, zero_state, tabs, ssm_d, ssm_w_glu, ssm_b_glu, l)
        xp, hp, lg_p = _proj_out(o_a, o_b, o_c, w_out, xp, mod_p, norm_ffn, wr_t, l)
        new_sre.append(fre.reshape(bp, 2, SSM_GROUPS, SSM_STATE))
        new_sim.append(fim.reshape(bp, 2, SSM_GROUPS, SSM_STATE))

        qa, ka, va, qb, kb, vb, u = _proj_in(xs, mod_s, norm_mix, w_in, l, rope_tabs, 0, None)
        o_a = _neighbourhood_attention(qa, ka, va, kc_a, vc_a, bias_tab, l)
        o_b = _attention(qb, kb, vb, 0, (kc_b, vc_b), l, diff_params, lam_init)
        h0re = state_ssm_re[:, l].reshape(bs, 2, SSM_N)
        h0im = state_ssm_im[:, l].reshape(bs, 2, SSM_N)
        o_c, _, _ = _s5(u, h0re, h0im, tabs, ssm_d, ssm_w_glu, ssm_b_glu, l)
        xs, hs, lg_s = _proj_out(o_a, o_b, o_c, w_out, xs, mod_s, norm_ffn, wr_t, l)

        pos_p, gate_p = _route(lg_p)
        pos_s, gate_s = _route(lg_s)
        xg_p, gs_p = _gather(hp, pos_p, gate_p)
        xg_s, gs_s = _gather(hs, pos_s, gate_s)
        ye_p, ye_s = _ffn(xg_p, gs_p, xg_s, gs_s, w_gate, w_up, w_down, l)
        last = l == depth - 1
        xp = _combine(ye_p, pos_p, xp, mod_p, fnorm if last else None)
        xs = _combine(ye_s, pos_s, xs, mod_s, fnorm if last else None)

    ka, va, kb, vb = caches
    return (xp, xs, ka.reshape(bp, depth, sp, NA_HEADS, HEAD_DIM), va.reshape(bp, depth, sp, NA_HEADS, HEAD_DIM),
            kb.reshape(bp, depth, sp, DIFF_HEADS, DIFF_V), vb.reshape(bp, depth, sp, DIFF_HEADS, DIFF_V),
            jnp.stack(new_sre, axis=1), jnp.stack(new_sim, axis=1))
```

```python
import functools
import math

import jax
import jax.numpy as jnp
import numpy as np
from jax import lax
from jax.experimental import pallas as pl
from jax.experimental.pallas import tpu as pltpu

F32 = jnp.float32
BF16 = jnp.bfloat16
I32 = jnp.int32

D_MODEL = 1024
GRID_W = 64
HEAD_DIM = 64
NA_HEADS = 8
NA_WIDTH = NA_HEADS * HEAD_DIM
WIN_R = 8
WIN_C = 16
DIFF_HEADS = 4
DIFF_QK = 32
DIFF_V = 64
DIFF_WIDTH = DIFF_HEADS * DIFF_V
SSM_GROUPS = 16
SSM_GROUP_CH = 16
SSM_WIDTH = SSM_GROUPS * SSM_GROUP_CH
SSM_STATE = 64
SSM_N = SSM_GROUPS * SSM_STATE
IN_WIDTH = 3 * NA_WIDTH + 3 * DIFF_WIDTH + SSM_WIDTH
N_EXPERTS = 16
EXPERT_FF = 2048
EC_CAPACITY_FACTOR = 2
ROPE_BASE = 10000.0
EPS = 1e-6
NEG_INF = -1e30
LOG2E = math.log2(math.e)

LANES = 128
SUBLANES = 8
MXU_DIM = 256
VMEM_LIMIT_CAP = 60000 * 1024

ROW_TILE = 256
STEP_CHUNKS = 2
ATTN_REQUESTS_PER_STEP = 2
NA_QROWS = 4
NA_KROWS = 12
SCAN_TILE = 256
FF_TILE = 1024
COMBINE_TILE = 512
GATHER_ROWS = 512
DEN_IN_MATMUL_MIN_KEYS = 1024


def _params(ndims):
    return pltpu.CompilerParams(
        dimension_semantics=("arbitrary",) * ndims,
        vmem_limit_bytes=VMEM_LIMIT_CAP,
    )


def _step_tile(b, n, n_mod):
    tm = min(n, ROW_TILE * STEP_CHUNKS)
    bb = max(1, ROW_TILE * STEP_CHUNKS // tm)
    if n_mod > 1 or b % bb:
        bb = 1
    return bb, tm


def _dot(a, b):
    return jnp.dot(a, b, preferred_element_type=F32)


def _dot_nt(a, b):
    return lax.dot_general(a, b, (((1,), (1,)), ((), ())), preferred_element_type=F32)


def _dot_tn(a, b):
    return lax.dot_general(a, b, (((0,), (0,)), ((), ())), preferred_element_type=F32)


def _cast_rows(dst_ref, src_ref, rows):
    n = src_ref.shape[0]
    for r in range(0, n, rows):
        dst_ref[r:r + rows, :] = src_ref[r:r + rows, :].astype(BF16)


def _modulated_norm(x, g, shift, scale):
    ms = jnp.mean(x * x, axis=-1, keepdims=True)
    return (x * lax.rsqrt(ms + EPS) * g) * (1.0 + scale) + shift


def _ada_kernel(c_ref, w_ref, b_ref, o_ref):
    c = c_ref[...]
    s = c * jax.nn.sigmoid(c)
    o_ref[0] = _dot(s.astype(BF16), w_ref[0].astype(BF16)) + b_ref[0]


def _ada(cond, w_ada, b_ada):
    depth = w_ada.shape[0]
    tn = 1536
    nt = 6 * D_MODEL // tn
    return pl.pallas_call(
        _ada_kernel,
        out_shape=jax.ShapeDtypeStruct((depth, SUBLANES, 6 * D_MODEL), F32),
        grid=(depth, nt),
        in_specs=[
            pl.BlockSpec((SUBLANES, D_MODEL), lambda l, j: (0, 0)),
            pl.BlockSpec((1, D_MODEL, tn), lambda l, j: (l, 0, j)),
            pl.BlockSpec((1, 1, tn), lambda l, j: (l, 0, j)),
        ],
        out_specs=pl.BlockSpec((1, SUBLANES, tn), lambda l, j: (l, 0, j)),
        compiler_params=_params(2),
        name="ada",
    )(cond, w_ada, b_ada.reshape(depth, 1, 6 * D_MODEL))


def _rope_apply(x, cos, sin_signed):
    lane = lax.broadcasted_iota(I32, (1, LANES), 1)
    first = (lane % 16) < 8
    outs = []
    for t in range(x.shape[1] // LANES):
        xt = x[:, t * LANES:(t + 1) * LANES]
        partner = jnp.where(first, pltpu.roll(xt, LANES - 8, axis=1), pltpu.roll(xt, 8, axis=1))
        outs.append(xt * cos[:, t * LANES:(t + 1) * LANES] + partner * sin_signed[:, t * LANES:(t + 1) * LANES])
    return jnp.concatenate(outs, axis=1)


def _proj_in_kernel(*refs, rope, n_alias):
    x_ref, mod_ref, g_ref, w_ref = refs[:4]
    pos = 4
    if rope:
        cos_ref, sin_ref = refs[pos:pos + 2]
        pos += 2
    pos += n_alias
    qa_ref, ka_ref, va_ref, qb_ref, kb_ref, vb_ref, u_ref, wbf_ref = refs[pos:pos + 8]

    @pl.when((pl.program_id(0) == 0) & (pl.program_id(1) == 0))
    def _():
        _cast_rows(wbf_ref, w_ref, 128)

    i1, i2, i3 = NA_WIDTH, 2 * NA_WIDTH, 3 * NA_WIDTH
    i4, i5, i6 = i3 + DIFF_WIDTH, i3 + 2 * DIFF_WIDTH, i3 + 3 * DIFF_WIDTH
    for bi in range(x_ref.shape[0]):
        for r0 in range(0, x_ref.shape[1], ROW_TILE):
            rows = slice(r0, r0 + ROW_TILE)

            def put(ref, val):
                for s in range(ref.shape[1]):
                    ref[bi, s, rows, :] = val.astype(ref.dtype)

            h = _modulated_norm(x_ref[bi, rows, :], g_ref[...], mod_ref[0, 0:1, :], mod_ref[0, 1:2, :])
            z = _dot(h.astype(BF16), wbf_ref[...])
            qa_ref[bi, rows, :] = (z[:, :i1] * (HEAD_DIM ** -0.5 * LOG2E)).astype(qa_ref.dtype)
            put(ka_ref, z[:, i1:i2])
            put(va_ref, z[:, i2:i3])
            qb = z[:, i3:i4]
            kb = z[:, i4:i5]
            if rope:
                qb = _rope_apply(qb, cos_ref[rows, :], sin_ref[rows, :])
                kb = _rope_apply(kb, cos_ref[rows, :], sin_ref[rows, :])
            qb_ref[bi, rows, :] = (qb * (DIFF_QK ** -0.5 * LOG2E)).astype(qb_ref.dtype)
            put(kb_ref, kb)
            put(vb_ref, z[:, i5:i6])
            for hf in range(SSM_WIDTH // LANES):
                u_ref[bi, hf, rows, :] = z[:, i6 + hf * LANES:i6 + (hf + 1) * LANES]


def _proj_in(x, mod, g_norm, w_in, layer, rope_tabs, cache_slots, caches):
    b, n, d = x.shape
    bb, tm = _step_tile(b, n, mod.shape[0])
    rope = rope_tabs is not None
    mod_map = (lambda i, j: (i, 0, 0)) if mod.shape[0] > 1 else (lambda i, j: (0, 0, 0))
    in_specs = [
        pl.BlockSpec((bb, tm, d), lambda i, j: (i, j, 0)),
        pl.BlockSpec((1, 6, d), mod_map),
        pl.BlockSpec((None, 1, d), lambda i, j: (layer, 0, 0)),
        pl.BlockSpec((None, d, IN_WIDTH), lambda i, j: (layer, 0, 0), pipeline_mode=pl.Buffered(1)),
    ]
    args = [x, mod, g_norm, w_in]
    if rope:
        in_specs += [pl.BlockSpec((tm, DIFF_WIDTH), lambda i, j: (j, 0))] * 2
        args += list(rope_tabs)
    aliases = {}
    kv_out = (1, 2, 4, 5)
    if caches is not None:
        for c, o in zip(caches, kv_out):
            aliases[len(args)] = o
            in_specs.append(pl.BlockSpec(memory_space=pl.ANY))
            args.append(c)
    widths = (NA_WIDTH, NA_WIDTH, NA_WIDTH, DIFF_WIDTH, DIFF_WIDTH, DIFF_WIDTH)
    nslab = SSM_WIDTH // LANES
    out_shape, out_specs = [], []
    for o, w in enumerate(widths):
        if o not in kv_out:
            out_shape.append(jax.ShapeDtypeStruct((b, n, w), BF16))
            out_specs.append(pl.BlockSpec((bb, tm, w), lambda i, j: (i, j, 0)))
        elif cache_slots == 0:
            out_shape.append(jax.ShapeDtypeStruct((b, 1, n, w), BF16))
            out_specs.append(pl.BlockSpec((bb, 1, tm, w), lambda i, j: (i, 0, j, 0)))
        elif caches is None:
            out_shape.append(jax.ShapeDtypeStruct((b, cache_slots, n, w), F32))
            out_specs.append(pl.BlockSpec((bb, cache_slots, tm, w), lambda i, j: (i, 0, j, 0)))
        else:
            out_shape.append(jax.ShapeDtypeStruct((b, cache_slots, n, w), F32))
            out_specs.append(pl.BlockSpec((bb, 1, tm, w), lambda i, j: (i, layer, j, 0)))
    out_shape.append(jax.ShapeDtypeStruct((b, nslab, n, LANES), F32))
    out_specs.append(pl.BlockSpec((bb, nslab, tm, LANES), lambda i, j: (i, 0, j, 0)))
    return pl.pallas_call(
        functools.partial(_proj_in_kernel, rope=rope, n_alias=len(aliases)),
        out_shape=out_shape,
        grid=(b // bb, n // tm),
        in_specs=in_specs,
        out_specs=out_specs,
        scratch_shapes=[pltpu.VMEM((d, IN_WIDTH), BF16)],
        input_output_aliases=aliases,
        compiler_params=_params(2),
        name="proj_in",
    )(*args)


def _masked_query(qp, mask):
    return jnp.where(mask, qp, jnp.zeros_like(qp))


def _ones_outside(v, head_mask):
    return jnp.where(head_mask, v, jnp.ones_like(v))


def _softmax_av(qm, ks, vs, biases, den_lane):
    ss = []
    for k, bias in zip(ks, biases):
        s = _dot_nt(qm, k)
        if bias is not None:
            s = bias(s)
        ss.append(s)
    m = ss[0].max(axis=-1, keepdims=True)
    for s in ss[1:]:
        m = jnp.maximum(m, s.max(axis=-1, keepdims=True))
    acc = None
    den = None
    for s, v in zip(ss, vs):
        p = jnp.exp2(s - m)
        if den_lane is None:
            psum = p.sum(axis=-1, keepdims=True)
            den = psum if den is None else den + psum
        pv = _dot(p.astype(BF16), v)
        acc = pv if acc is None else acc + pv
    return acc / (den if den_lane is None else acc[:, den_lane:den_lane + 1])


def _diff_lambda(lam_ref, lam_init):
    lp = lam_ref[...]
    a = jnp.sum(lp[0:1] * lp[1:2], axis=-1, keepdims=True)
    b = jnp.sum(lp[2:3] * lp[3:4], axis=-1, keepdims=True)
    return jnp.exp(a) - jnp.exp(b) + lam_init


def _attn_kernel(*refs, has_ctx, diff, lam_init):
    refs = list(refs)
    q_ref, k_ref, v_ref = refs[:3]
    pos = 3
    if has_ctx:
        kc_ref, vc_ref = refs[pos:pos + 2]
        pos += 2
    if diff:
        lam_ref, subln_ref = refs[pos:pos + 2]
        pos += 2
    o_ref = refs[pos]
    width = q_ref.shape[2]
    lane = lax.broadcasted_iota(I32, (1, LANES), 1)
    if diff:
        lam = _diff_lambda(lam_ref, lam_init)
    for bi, p in [(bi, p) for bi in range(q_ref.shape[0]) for p in range(width // LANES)]:
        sl = slice(p * LANES, (p + 1) * LANES)
        qp = q_ref[bi, :, sl]
        ks = [k_ref[bi, :, sl].astype(BF16)]
        vs = [v_ref[bi, :, sl].astype(BF16)]
        if has_ctx:
            ks.append(kc_ref[:, sl].astype(BF16))
            vs.append(vc_ref[:, sl].astype(BF16))
        nob = [None] * len(ks)
        pair = None
        for sub in range(LANES // HEAD_DIM):
            head_mask = (lane // HEAD_DIM) == sub
            if sum(k.shape[0] for k in ks) >= DEN_IN_MATMUL_MIN_KEYS:
                den_lane = (1 - sub) * HEAD_DIM
                vh = [_ones_outside(v, head_mask) for v in vs]
            else:
                den_lane, vh = None, vs
            if diff:
                os_ = []
                for half in range(2):
                    qmask = (lane // DIFF_QK) == (2 * sub + half)
                    os_.append(_softmax_av(_masked_query(qp, qmask), ks, vh, nob, den_lane))
                o = jnp.where(head_mask, os_[0] - lam * os_[1], 0.0)
                ms = jnp.sum(o * o, axis=-1, keepdims=True) * (1.0 / DIFF_V)
                o = (o * lax.rsqrt(ms + EPS) * subln_ref[...]) * (1.0 - lam_init)
            else:
                o = _softmax_av(_masked_query(qp, head_mask), ks, vh, nob, den_lane)
                o = jnp.where(head_mask, o, 0.0)
            pair = o if pair is None else pair + o
        o_ref[bi, :, sl] = pair.astype(o_ref.dtype)


def _attention(q, k, v, kv_slot, ctx_kv, layer, diff_params, lam_init):
    b, nq, w = q.shape
    nk = k.shape[2]
    tq = ROW_TILE
    has_ctx = ctx_kv is not None
    diff = diff_params is not None
    bb = ATTN_REQUESTS_PER_STEP if (nq == tq and not has_ctx and b % ATTN_REQUESTS_PER_STEP == 0) else 1
    in_specs = [
        pl.BlockSpec((bb, tq, w), lambda i, j: (i, j, 0)),
        pl.BlockSpec((bb, None, nk, w), lambda i, j: (i, kv_slot, 0, 0)),
        pl.BlockSpec((bb, None, nk, w), lambda i, j: (i, kv_slot, 0, 0)),
    ]
    args = [q, k, v]
    nc = 0
    if has_ctx:
        nc = ctx_kv[0].shape[2]
        in_specs += [pl.BlockSpec((None, None, nc, w), lambda i, j: (i, layer, 0, 0))] * 2
        args += list(ctx_kv)
    if diff:
        lam_p, subln = diff_params
        in_specs += [
            pl.BlockSpec((None, 4, DIFF_QK), lambda i, j: (layer, 0, 0)),
            pl.BlockSpec((None, 1, LANES), lambda i, j: (layer, 0, 0)),
        ]
        args += [lam_p, subln]
    return pl.pallas_call(
        functools.partial(_attn_kernel, has_ctx=has_ctx, diff=diff, lam_init=lam_init),
        out_shape=jax.ShapeDtypeStruct((b, nq, w), BF16),
        grid=(b // bb, nq // tq),
        in_specs=in_specs,
        out_specs=pl.BlockSpec((bb, tq, w), lambda i, j: (i, j, 0)),
        compiler_params=_params(2),
        name="attn_diff" if diff else "attn",
    )(*args)


def _na_kernel(q_ref, k_ref, v_ref, kc_ref, vc_ref, bias_ref, o_ref, *, rows):
    i = pl.program_id(1)
    tq = NA_QROWS * GRID_W
    nkw = NA_KROWS * GRID_W
    ustart = jnp.clip(NA_QROWS * i - WIN_R // 2, 0, rows - NA_KROWS)
    koff = pl.multiple_of(ustart * GRID_W, GRID_W)
    delta = ustart - NA_QROWS * i
    qrow = NA_QROWS * i + lax.broadcasted_iota(I32, (tq, nkw), 0) // GRID_W
    krow = ustart + lax.broadcasted_iota(I32, (tq, nkw), 1) // GRID_W
    wstart = jnp.clip(qrow - WIN_R // 2, 0, rows - WIN_R)
    row_ok = (krow >= wstart) & (krow < wstart + WIN_R)
    lane = lax.broadcasted_iota(I32, (1, LANES), 1)
    for p in range(NA_WIDTH // LANES):
        sl = slice(p * LANES, (p + 1) * LANES)
        qp = q_ref[0, :, sl]
        ks = [k_ref[0, pl.ds(koff, nkw), sl].astype(BF16), kc_ref[:, sl].astype(BF16)]
        vs = [v_ref[0, pl.ds(koff, nkw), sl].astype(BF16), vc_ref[:, sl].astype(BF16)]
        pair = None
        for sub in range(LANES // HEAD_DIM):
            h = p * (LANES // HEAD_DIM) + sub
            head_mask = (lane // HEAD_DIM) == sub

            def win_bias(s, h=h):
                rows_ = []
                for ri in range(NA_QROWS):
                    tiles = []
                    for pj in range(NA_KROWS // 2):
                        dr = delta + 2 * pj - ri
                        idx = jnp.clip(dr, -WIN_R, WIN_R - 1) + WIN_R
                        tiles.append(bias_ref[h, idx])
                    rows_.append(jnp.concatenate(tiles, axis=1))
                bias = jnp.concatenate(rows_, axis=0)
                return jnp.where(row_ok, s + bias, NEG_INF)

            vh = [_ones_outside(v, head_mask) for v in vs]
            o = _softmax_av(_masked_query(qp, head_mask), ks, vh, [win_bias, None], (1 - sub) * HEAD_DIM)
            o = jnp.where(head_mask, o, 0.0)
            pair = o if pair is None else pair + o
        o_ref[0, :, sl] = pair.astype(o_ref.dtype)


def _na_table_kernel(v_ref, o_ref):
    nd = v_ref.shape[0]
    q = lax.broadcasted_iota(I32, (GRID_W, 2 * GRID_W), 0)
    k = lax.broadcasted_iota(I32, (GRID_W, 2 * GRID_W), 1)
    kk = k % GRID_W
    cstart = jnp.clip(q - WIN_C // 2, 0, GRID_W - WIN_C)
    col_ok = (kk >= cstart) & (kk < cstart + WIN_C)
    left = k < GRID_W
    tiles = []
    for dr in range(nd):
        x = jnp.broadcast_to(v_ref[dr:dr + 1, :], (GRID_W, 2 * GRID_W))
        tiles.append((pltpu.roll(x, GRID_W + 1, axis=1, stride=1, stride_axis=0),
                      pltpu.roll(x, 1, axis=1, stride=1, stride_axis=0)))
    zero = jnp.zeros((GRID_W, 2 * GRID_W), F32)
    for j in range(nd + 1):
        lt = tiles[j - 1][0] if j >= 1 else zero
        rt = tiles[j][1] if j < nd else zero
        o_ref[j] = jnp.where(col_ok, jnp.where(left, lt, rt) * LOG2E, NEG_INF)


def _na_bias_table(rpb):
    depth, h, nd, nc = rpb.shape
    assert 2 * GRID_W == LANES and nd == 2 * WIN_R - 1 and nc == 2 * WIN_C - 1
    rpb = rpb.astype(F32)
    lo = GRID_W - WIN_C
    v = jnp.concatenate([jnp.broadcast_to(rpb[..., :1], rpb.shape[:3] + (lo,)), rpb,
                         jnp.broadcast_to(rpb[..., -1:], rpb.shape[:3] + (2 * GRID_W - lo - nc,))], axis=-1)
    return pl.pallas_call(
        _na_table_kernel,
        out_shape=jax.ShapeDtypeStruct((depth, h, nd + 1, GRID_W, 2 * GRID_W), F32),
        grid=(depth, h),
        in_specs=[pl.BlockSpec((None, None, nd, 2 * GRID_W), lambda l, i: (l, i, 0, 0))],
        out_specs=pl.BlockSpec((None, None, nd + 1, GRID_W, 2 * GRID_W), lambda l, i: (l, i, 0, 0, 0)),
        compiler_params=_params(2),
        name="na_table",
    )(v)


def _neighbourhood_attention(q, k, v, kc, vc, bias_tab, layer):
    b, n, w = q.shape
    rows = n // GRID_W
    tq = NA_QROWS * GRID_W
    nc = kc.shape[2]
    nkw = NA_KROWS * GRID_W
    return pl.pallas_call(
        functools.partial(_na_kernel, rows=rows),
        out_shape=jax.ShapeDtypeStruct((b, n, w), BF16),
        grid=(b, n // tq),
        in_specs=[
            pl.BlockSpec((1, tq, w), lambda i, j: (i, j, 0)),
            pl.BlockSpec((1, None, n, w), lambda i, j: (i, 0, 0, 0)),
            pl.BlockSpec((1, None, n, w), lambda i, j: (i, 0, 0, 0)),
            pl.BlockSpec((None, None, nc, w), lambda i, j: (i, layer, 0, 0)),
            pl.BlockSpec((None, None, nc, w), lambda i, j: (i, layer, 0, 0)),
            pl.BlockSpec((None,) + bias_tab.shape[1:], lambda i, j: (layer, 0, 0, 0, 0)),
        ],
        out_specs=pl.BlockSpec((1, tq, w), lambda i, j: (i, j, 0)),
        compiler_params=_params(2),
        name="na_attn",
    )(q, k, v, kc, vc, bias_tab)


def _sublane_scan(er, ei, tr, ti, pw_ref, base, lanes, reverse):
    row = lax.broadcasted_iota(I32, (SUBLANES, LANES), 0)
    if reverse:
        first = row == SUBLANES - 1
        xr = jnp.where(first, tr, pltpu.roll(er, SUBLANES - 1, axis=0))
        xi = jnp.where(first, ti, pltpu.roll(ei, SUBLANES - 1, axis=0))
    else:
        first = row == 0
        xr = jnp.where(first, tr, pltpu.roll(er, 1, axis=0))
        xi = jnp.where(first, ti, pltpu.roll(ei, 1, axis=0))
    for k, s in enumerate((1, 2, 4)):
        ar = pw_ref[base + 2 * k:base + 2 * k + 1, lanes]
        ai = pw_ref[base + 2 * k + 1:base + 2 * k + 2, lanes]
        if reverse:
            keep = row < SUBLANES - s
            sr = jnp.where(keep, pltpu.roll(xr, SUBLANES - s, axis=0), 0.0)
            si = jnp.where(keep, pltpu.roll(xi, SUBLANES - s, axis=0), 0.0)
        else:
            keep = row >= s
            sr = jnp.where(keep, pltpu.roll(xr, s, axis=0), 0.0)
            si = jnp.where(keep, pltpu.roll(xi, s, axis=0), 0.0)
        xr, xi = xr + (ar * sr - ai * si), xi + (ar * si + ai * sr)
    return xr, xi


def _s5_kernel(u_ref, h0re_ref, h0im_ref, bt_ref, ct_ref, a1_ref, ach_ref, pfix_ref, d_ref, wglu_ref, bglu_ref,
               o_ref, fre_ref, fim_ref, up_ref, yp_ref, buf_ref, car_ref, bmat_ref, cmat_ref):
    seq = u_ref.shape[2]
    tt = SCAN_TILE
    ch = tt // SUBLANES
    ntile = seq // tt
    nlt = SSM_N // LANES
    nhalf = SSM_WIDTH // LANES

    @pl.when(pl.program_id(0) == 0)
    def _():
        g, p, hc = SSM_GROUPS, SSM_STATE, SSM_GROUP_CH
        btb = bt_ref[...].astype(BF16)
        ctb = ct_ref[...].astype(BF16)
        for dc in range(4):
            r = lax.broadcasted_iota(I32, (4 * p, g * p), 0)
            c = lax.broadcasted_iota(I32, (4 * p, g * p), 1)
            sel = jnp.where(r == dc * p + c % p, 1.0, 0.0).astype(BF16)
            r = lax.broadcasted_iota(I32, (g * hc, g * p), 0)
            c = lax.broadcasted_iota(I32, (g * hc, g * p), 1)
            blk = jnp.where(r // hc == c // p, _dot(btb, sel), 0.0)
            bmat_ref[:, dc * g * p:(dc + 1) * g * p] = blk.astype(BF16)
            r = lax.broadcasted_iota(I32, (g * p, 4 * p), 0)
            c = lax.broadcasted_iota(I32, (g * p, 4 * p), 1)
            sel = jnp.where(c == dc * p + r % p, 1.0, 0.0).astype(BF16)
            r = lax.broadcasted_iota(I32, (g * p, g * hc), 0)
            c = lax.broadcasted_iota(I32, (g * p, g * hc), 1)
            blk = jnp.where(r // p == c // hc, _dot(sel, ctb), 0.0)
            cmat_ref[dc * g * p:(dc + 1) * g * p, :] = blk.astype(BF16)

    def permute_in(t, _):
        t0 = pl.multiple_of(t * tt, tt)
        for j in range(0, ch, 2):
            for hf in range(nhalf):
                lanes = slice(hf * LANES, (hf + 1) * LANES)
                rows = jnp.concatenate([u_ref[0, hf, pl.ds(t0 + j + jj, SUBLANES, stride=ch), :] for jj in range(2)],
                                       axis=0)
                r0 = pl.multiple_of(t0 + j * SUBLANES, 2 * SUBLANES)
                up_ref[pl.ds(r0, 2 * SUBLANES), lanes] = rows.astype(BF16)
                yp_ref[pl.ds(r0, 2 * SUBLANES), lanes] = rows * d_ref[:, lanes]
        return 0

    lax.fori_loop(0, ntile, permute_in, 0)

    for d in range(2):
        for lt in range(nlt):
            lanes = slice(lt * LANES, (lt + 1) * LANES)
            car_ref[d, 0, :, lanes] = jnp.broadcast_to(h0re_ref[0, d:d + 1, lanes], (SUBLANES, LANES))
            car_ref[d, 1, :, lanes] = jnp.broadcast_to(h0im_ref[0, d:d + 1, lanes], (SUBLANES, LANES))

    def tile(t, _):
        starts = []
        for d in range(2):
            tix = (ntile - 1 - t) if d == 1 else t
            t0 = pl.multiple_of(tix * tt, tt)
            starts.append(t0)
            buf_ref[d] = _dot(up_ref[pl.ds(t0, tt), :], bmat_ref[:, d * 2 * SSM_N:(d + 1) * 2 * SSM_N])
        for d in range(2):
            reverse = d == 1
            order = range(ch - 1, -1, -1) if reverse else range(ch)
            for lt in range(nlt):
                lre = slice(lt * LANES, (lt + 1) * LANES)
                lim = slice(SSM_N + lt * LANES, SSM_N + (lt + 1) * LANES)
                ar = a1_ref[2 * d:2 * d + 1, lre]
                ai = a1_ref[2 * d + 1:2 * d + 2, lre]
                cr = jnp.zeros((SUBLANES, LANES), F32)
                ci = jnp.zeros((SUBLANES, LANES), F32)
                for j in order:
                    rows = slice(j * SUBLANES, (j + 1) * SUBLANES)
                    cr, ci = (ar * cr - ai * ci) + buf_ref[d, rows, lre], (ar * ci + ai * cr) + buf_ref[d, rows, lim]
                    buf_ref[d, rows, lre] = cr
                    buf_ref[d, rows, lim] = ci
                tr, ti = car_ref[d, 0, :, lre], car_ref[d, 1, :, lre]
                gr, gi = _sublane_scan(cr, ci, tr, ti, ach_ref, 6 * d, lre, reverse)
                a_r, a_i = ach_ref[6 * d:6 * d + 1, lre], ach_ref[6 * d + 1:6 * d + 2, lre]
                nr, ni = (a_r * gr - a_i * gi) + cr, (a_r * gi + a_i * gr) + ci
                edge = 0 if reverse else SUBLANES - 1
                car_ref[d, 0, :, lre] = jnp.broadcast_to(nr[edge:edge + 1, :], (SUBLANES, LANES))
                car_ref[d, 1, :, lre] = jnp.broadcast_to(ni[edge:edge + 1, :], (SUBLANES, LANES))
                for j in range(ch):
                    rows = slice(j * SUBLANES, (j + 1) * SUBLANES)
                    pr = pfix_ref[2 * d, j:j + 1, lre]
                    pi = pfix_ref[2 * d + 1, j:j + 1, lre]
                    buf_ref[d, rows, lre] += pr * gr - pi * gi
                    buf_ref[d, rows, lim] += pr * gi + pi * gr
        for d in range(2):
            yp_ref[pl.ds(starts[d], tt), :] += _dot(buf_ref[d].astype(BF16),
                                                    cmat_ref[d * 2 * SSM_N:(d + 1) * 2 * SSM_N, :])
        return 0

    lax.fori_loop(0, ntile, tile, 0)

    for d in range(2):
        fre_ref[0, d:d + 1, :] = car_ref[d, 0, 0:1, :]
        fim_ref[0, d:d + 1, :] = car_ref[d, 1, 0:1, :]

    wglu = wglu_ref[...].astype(BF16)

    def glu(t, _):
        t0 = pl.multiple_of(t * tt, tt)
        g = jax.nn.gelu(yp_ref[pl.ds(t0, tt), :])
        out = g * jax.nn.sigmoid(_dot(g.astype(BF16), wglu) + bglu_ref[...])
        for j in range(ch):
            for hf in range(nhalf):
                o_ref[0, hf, pl.ds(t0 + j, SUBLANES, stride=ch), :] = out[j * SUBLANES:(j + 1) * SUBLANES,
                                                                          hf * LANES:(hf + 1) * LANES]
        return 0

    lax.fori_loop(0, ntile, glu, 0)


def _cmul(ar, ai, br, bi):
    return ar * br - ai * bi, ar * bi + ai * br


def _s5_tables(a_re, a_im, log_dt, b_re, b_im, c_re, c_im):
    depth = a_re.shape[0]
    g, p, hc = SSM_GROUPS, SSM_STATE, SSM_GROUP_CH
    ch = SCAN_TILE // SUBLANES
    lr, li = a_re.astype(F32), a_im.astype(F32)
    dt = jnp.exp(log_dt.astype(F32))[..., None]
    mag = jnp.exp(lr * dt)
    ar, ai = mag * jnp.cos(li * dt), mag * jnp.sin(li * dt)
    den = lr * lr + li * li
    qr = ((ar - 1.0) * lr + ai * li) / den
    qi = (ai * lr - (ar - 1.0) * li) / den
    br, bi = _cmul(qr[..., None], qi[..., None], b_re.astype(F32), b_im.astype(F32))
    bparts = jnp.stack([br, bi], axis=2)
    bmat = bparts.transpose(0, 3, 5, 1, 2, 4).reshape(depth, g * hc, 4 * p)
    cparts = jnp.stack([c_re.astype(F32), -c_im.astype(F32)], axis=2)
    cmat = cparts.transpose(0, 1, 2, 5, 3, 4).reshape(depth, 4 * p, g * hc)

    def power(k):
        kk = k[None, None, :, None, None]
        m = jnp.exp(kk * (lr * dt)[:, :, None])
        th = kk * (li * dt)[:, :, None]
        return (m * jnp.cos(th)).reshape(depth, 2, -1, g * p), (m * jnp.sin(th)).reshape(depth, 2, -1, g * p)

    a1 = jnp.stack([ar.reshape(depth, 2, g * p), ai.reshape(depth, 2, g * p)], axis=2).reshape(depth, 4, g * p)
    cr_, ci_ = power(jnp.asarray([ch, 2 * ch, 4 * ch], F32))
    ach = jnp.stack([cr_, ci_], axis=3).reshape(depth, 12, g * p)
    fr, fi = power(jnp.arange(1, ch + 1, dtype=F32))
    fr = jnp.stack([fr[:, 0], fr[:, 1, ::-1]], axis=1)
    fi = jnp.stack([fi[:, 0], fi[:, 1, ::-1]], axis=1)
    pfix = jnp.stack([fr, fi], axis=2).reshape(depth, 4, ch, g * p)
    return bmat, cmat, a1, ach, pfix


def _s5(u, h0re, h0im, tabs, d_skip, w_glu, b_glu, layer):
    b, nh, seq, _ = u.shape
    w = nh * LANES
    lay = lambda a: pl.BlockSpec((None,) + a.shape[1:], lambda i: (layer,) + (0,) * (a.ndim - 1))
    return pl.pallas_call(
        _s5_kernel,
        out_shape=[jax.ShapeDtypeStruct((b, nh, seq, LANES), F32),
                   jax.ShapeDtypeStruct((b, 2, SSM_N), F32),
                   jax.ShapeDtypeStruct((b, 2, SSM_N), F32)],
        grid=(b,),
        in_specs=[
            pl.BlockSpec((1, nh, seq, LANES), lambda i: (i, 0, 0, 0)),
            pl.BlockSpec((1, 2, SSM_N), lambda i: (i, 0, 0)),
            pl.BlockSpec((1, 2, SSM_N), lambda i: (i, 0, 0)),
            lay(tabs[0]), lay(tabs[1]), lay(tabs[2]), lay(tabs[3]), lay(tabs[4]),
            pl.BlockSpec((None, 1, w), lambda i: (layer, 0, 0)),
            pl.BlockSpec((None, w, w), lambda i: (layer, 0, 0)),
            pl.BlockSpec((None, 1, w), lambda i: (layer, 0, 0)),
        ],
        out_specs=[pl.BlockSpec((1, nh, seq, LANES), lambda i: (i, 0, 0, 0)),
                   pl.BlockSpec((1, 2, SSM_N), lambda i: (i, 0, 0)),
                   pl.BlockSpec((1, 2, SSM_N), lambda i: (i, 0, 0))],
        scratch_shapes=[pltpu.VMEM((seq, w), BF16),
                        pltpu.VMEM((seq, w), F32),
                        pltpu.VMEM((2, SCAN_TILE, 2 * SSM_N), F32),
                        pltpu.VMEM((2, 2, SUBLANES, SSM_N), F32),
                        pltpu.VMEM((w, 4 * SSM_N), BF16),
                        pltpu.VMEM((4 * SSM_N, w), BF16)],
        compiler_params=_params(1),
        name="s5",
    )(u, h0re, h0im, *tabs, d_skip, w_glu, b_glu)


def _proj_out_kernel(oa_ref, ob_ref, oc_ref, w_ref, x_ref, mod_ref, g_ref, wr_ref, o_ref, h_ref, lg_ref, wbf_ref):
    @pl.when((pl.program_id(0) == 0) & (pl.program_id(1) == 0))
    def _():
        _cast_rows(wbf_ref, w_ref, 128)

    i1, i2 = NA_WIDTH, NA_WIDTH + DIFF_WIDTH
    for bi in range(x_ref.shape[0]):
        for r0 in range(0, x_ref.shape[1], ROW_TILE):
            rows = slice(r0, r0 + ROW_TILE)
            y = _dot(oa_ref[bi, rows, :].astype(BF16), wbf_ref[:i1, :])
            y += _dot(ob_ref[bi, rows, :].astype(BF16), wbf_ref[i1:i2, :])
            for hf in range(SSM_WIDTH // LANES):
                y += _dot(oc_ref[bi, hf, rows, :].astype(BF16), wbf_ref[i2 + hf * LANES:i2 + (hf + 1) * LANES, :])
            xn = x_ref[bi, rows, :] + mod_ref[0, 2:3, :] * y
            o_ref[bi, rows, :] = xn
            h = _modulated_norm(xn, g_ref[...], mod_ref[0, 3:4, :], mod_ref[0, 4:5, :])
            h_hi = h.astype(BF16)
            h_ref[bi, rows, :] = h_hi
            h_lo = (h - h_hi.astype(F32)).astype(BF16)
            t1 = _dot_nt(wr_ref[...], h_hi)
            t2 = _dot_nt(wr_ref[:N_EXPERTS, :], h_lo)
            lg_ref[bi, :, rows] = t1[:N_EXPERTS, :] + t1[N_EXPERTS:, :] + t2


def _proj_out(oa, ob, oc, w_out, x, mod, g_ffn, wr_t, layer):
    b, n, d = x.shape
    bb, tm = _step_tile(b, n, mod.shape[0])
    mod_map = (lambda i, j: (i, 0, 0)) if mod.shape[0] > 1 else (lambda i, j: (0, 0, 0))
    kw = w_out.shape[1]
    return pl.pallas_call(
        _proj_out_kernel,
        out_shape=[jax.ShapeDtypeStruct((b, n, d), F32),
                   jax.ShapeDtypeStruct((b, n, d), BF16),
                   jax.ShapeDtypeStruct((b, N_EXPERTS, n), F32)],
        grid=(b // bb, n // tm),
        in_specs=[
            pl.BlockSpec((bb, tm, NA_WIDTH), lambda i, j: (i, j, 0)),
            pl.BlockSpec((bb, tm, DIFF_WIDTH), lambda i, j: (i, j, 0)),
            pl.BlockSpec((bb, SSM_WIDTH // LANES, tm, LANES), lambda i, j: (i, 0, j, 0)),
            pl.BlockSpec((None, kw, d), lambda i, j: (layer, 0, 0), pipeline_mode=pl.Buffered(1)),
            pl.BlockSpec((bb, tm, d), lambda i, j: (i, j, 0)),
            pl.BlockSpec((1, 6, d), mod_map),
            pl.BlockSpec((None, 1, d), lambda i, j: (layer, 0, 0)),
            pl.BlockSpec((None, 2 * N_EXPERTS, d), lambda i, j: (layer, 0, 0)),
        ],
        out_specs=[pl.BlockSpec((bb, tm, d), lambda i, j: (i, j, 0)),
                   pl.BlockSpec((bb, tm, d), lambda i, j: (i, j, 0)),
                   pl.BlockSpec((bb, N_EXPERTS, tm), lambda i, j: (i, 0, j))],
        scratch_shapes=[pltpu.VMEM((kw, d), BF16)],
        compiler_params=_params(2),
        name="proj_out",
    )(oa, ob, oc, w_out, x, mod, g_ffn, wr_t)


def _excl_cumsum_lanes(m):
    e, n = m.shape
    blk = MXU_DIM
    nb = n // blk
    r = lax.broadcasted_iota(I32, (blk, blk), 0)
    c = lax.broadcasted_iota(I32, (blk, blk), 1)
    tri = jnp.where(r < c, 1.0, 0.0).astype(BF16)
    stacked = jnp.concatenate([m[:, k * blk:(k + 1) * blk] for k in range(nb)], axis=0).astype(BF16)
    within = _dot(stacked, tri)
    outs = []
    off = jnp.zeros((e, 1), F32)
    for k in range(nb):
        outs.append(within[k * e:(k + 1) * e, :] + off)
        off = off + jnp.sum(m[:, k * blk:(k + 1) * blk], axis=1, keepdims=True)
    return jnp.concatenate(outs, axis=1)


def _route_kernel(lg_ref, pos_ref, gate_ref, *, cap):
    b, e, n = lg_ref.shape
    lg = lg_ref[...]
    mx = lg.max(axis=1, keepdims=True)
    ex = jnp.exp(lg - mx)
    aff = (ex / ex.sum(axis=1, keepdims=True)).reshape(b * e, n)

    def search(it, cur):
        cand = cur | (jnp.int32(1) << (jnp.int32(30) - it))
        cnt = jnp.sum(jnp.where(aff >= pltpu.bitcast(cand, F32), 1.0, 0.0), axis=1, keepdims=True)
        return jnp.where(cnt >= cap, cand, cur)

    thr = pltpu.bitcast(lax.fori_loop(0, 31, search, jnp.zeros((b * e, 1), I32)), F32)
    gt = aff > thr
    eq = aff == thr
    need = cap - jnp.sum(jnp.where(gt, 1.0, 0.0), axis=1, keepdims=True)
    eq_rank = _excl_cumsum_lanes(jnp.where(eq, 1.0, 0.0))
    sel = gt | (eq & (eq_rank < need))
    slot = _excl_cumsum_lanes(jnp.where(sel, 1.0, 0.0))
    pos = jnp.where(sel, slot, -1.0).astype(I32)
    gate = jnp.where(sel, aff, 0.0)
    for i in range(b):
        pos_ref[:, i, 0, :] = pos[i * e:(i + 1) * e, :]
        gate_ref[:, i, 0, :] = gate[i * e:(i + 1) * e, :]


def _route(logits):
    b, e, n = logits.shape
    cap = EC_CAPACITY_FACTOR * n // N_EXPERTS
    return pl.pallas_call(
        functools.partial(_route_kernel, cap=cap),
        out_shape=[jax.ShapeDtypeStruct((e, b, 1, n), I32),
                   jax.ShapeDtypeStruct((e, b, 1, n), F32)],
        compiler_params=_params(0),
        name="route",
    )(logits)


def _one_hot_rows(pos_row, cap):
    n = pos_row.shape[1]
    slot = lax.broadcasted_iota(I32, (cap, n), 0)
    return slot == pos_row


def _gather_kernel(h_ref, pos_ref, gate_ref, xs_ref, gs_ref, *, cap):
    h = h_ref[0]
    group = max(1, GATHER_ROWS // cap)
    for e0 in range(0, N_EXPERTS, group):
        ohs = [_one_hot_rows(pos_ref[e, 0], cap) for e in range(e0, e0 + group)]
        xs = _dot(jnp.concatenate([jnp.where(oh, 1.0, 0.0).astype(BF16) for oh in ohs], axis=0), h).astype(BF16)
        for i, oh in enumerate(ohs):
            e = e0 + i
            xs_ref[e] = xs[i * cap:(i + 1) * cap, :]
            g = jnp.sum(jnp.where(oh, gate_ref[e, 0], 0.0), axis=1, keepdims=True)
            gs_ref[e] = jnp.broadcast_to(g, (cap, LANES))


def _gather(h, pos, gate):
    b, n, d = h.shape
    cap = EC_CAPACITY_FACTOR * n // N_EXPERTS
    return pl.pallas_call(
        functools.partial(_gather_kernel, cap=cap),
        out_shape=[jax.ShapeDtypeStruct((N_EXPERTS, b * cap, d), BF16),
                   jax.ShapeDtypeStruct((N_EXPERTS, b * cap, LANES), F32)],
        grid=(b,),
        in_specs=[
            pl.BlockSpec((1, n, d), lambda i: (i, 0, 0)),
            pl.BlockSpec((N_EXPERTS, 1, 1, n), lambda i: (0, i, 0, 0)),
            pl.BlockSpec((N_EXPERTS, 1, 1, n), lambda i: (0, i, 0, 0)),
        ],
        out_specs=[pl.BlockSpec((N_EXPERTS, cap, d), lambda i: (0, i, 0)),
                   pl.BlockSpec((N_EXPERTS, cap, LANES), lambda i: (0, i, 0))],
        compiler_params=_params(1),
        name="moe_gather",
    )(h, pos, gate)


def _ffn_kernel(xp_ref, gp_ref, xs_ref, gs_ref, wg_ref, wu_ref, wd_ref, yp_ref, ys_ref, acc_ref):
    j = pl.program_id(1)
    sp = xp_ref.shape[1]

    @pl.when(j == 0)
    def _():
        acc_ref[...] = jnp.zeros_like(acc_ref)

    xs = jnp.concatenate([xp_ref[0], xs_ref[0]], axis=0)
    a = _dot(xs, wg_ref[...].astype(BF16))
    u = _dot(xs, wu_ref[...].astype(BF16))
    hid = (a * jax.nn.sigmoid(a)) * u
    acc_ref[...] += _dot(hid.astype(BF16), wd_ref[...].astype(BF16))

    @pl.when(j == pl.num_programs(1) - 1)
    def _():
        yp_ref[0] = (acc_ref[:sp, :] * gp_ref[0][:, 0:1]).astype(BF16)
        ys_ref[0] = (acc_ref[sp:, :] * gs_ref[0][:, 0:1]).astype(BF16)


def _ffn(xp, gp, xs, gs, w_gate, w_up, w_down, layer):
    e, sp, d = xp.shape
    ss = xs.shape[1]
    s = sp + ss
    ff = w_gate.shape[-1]
    tf = FF_TILE
    return pl.pallas_call(
        _ffn_kernel,
        out_shape=[jax.ShapeDtypeStruct((e, sp, d), BF16), jax.ShapeDtypeStruct((e, ss, d), BF16)],
        grid=(e, ff // tf),
        in_specs=[
            pl.BlockSpec((1, sp, d), lambda i, j: (i, 0, 0)),
            pl.BlockSpec((1, sp, LANES), lambda i, j: (i, 0, 0)),
            pl.BlockSpec((1, ss, d), lambda i, j: (i, 0, 0)),
            pl.BlockSpec((1, ss, LANES), lambda i, j: (i, 0, 0)),
            pl.BlockSpec((None, None, d, tf), lambda i, j: (layer, i, 0, j)),
            pl.BlockSpec((None, None, d, tf), lambda i, j: (layer, i, 0, j)),
            pl.BlockSpec((None, None, tf, d), lambda i, j: (layer, i, j, 0)),
        ],
        out_specs=[pl.BlockSpec((1, sp, d), lambda i, j: (i, 0, 0)),
                   pl.BlockSpec((1, ss, d), lambda i, j: (i, 0, 0))],
        scratch_shapes=[pltpu.VMEM((s, d), F32)],
        compiler_params=_params(2),
        name="moe_ffn",
    )(xp, gp, xs, gs, w_gate, w_up, w_down)


def _combine_kernel(ye_ref, pos_ref, x_ref, mod_ref, *rest, cap, final):
    if final:
        g_ref, o_ref = rest
    else:
        (o_ref,) = rest
    group = max(1, MXU_DIM // cap)
    y = None
    for e0 in range(0, N_EXPERTS, group):
        oh = jnp.concatenate([jnp.where(_one_hot_rows(pos_ref[e, 0], cap), 1.0, 0.0).astype(BF16)
                              for e in range(e0, e0 + group)], axis=0)
        ye = jnp.concatenate([ye_ref[e] for e in range(e0, e0 + group)], axis=0)
        part = _dot_tn(oh, ye)
        y = part if y is None else y + part
    xn = x_ref[0] + mod_ref[0, 5:6, :] * y
    if final:
        ms = jnp.mean(xn * xn, axis=-1, keepdims=True)
        xn = xn * lax.rsqrt(ms + EPS) * g_ref[...]
    o_ref[0] = xn


def _combine(ye, pos, x, mod, final_norm):
    b, n, d = x.shape
    cap = EC_CAPACITY_FACTOR * n // N_EXPERTS
    tn = min(n, COMBINE_TILE)
    bm = mod.shape[0]
    mod_map = (lambda i, j: (i, 0, 0)) if bm > 1 else (lambda i, j: (0, 0, 0))
    final = final_norm is not None
    in_specs = [
        pl.BlockSpec((N_EXPERTS, cap, d), lambda i, j: (0, i, 0)),
        pl.BlockSpec((N_EXPERTS, 1, 1, tn), lambda i, j: (0, i, 0, j)),
        pl.BlockSpec((1, tn, d), lambda i, j: (i, j, 0)),
        pl.BlockSpec((1, 6, d), mod_map),
    ]
    args = [ye, pos, x, mod]
    if final:
        in_specs.append(pl.BlockSpec((1, d), lambda i, j: (0, 0)))
        args.append(final_norm)
    return pl.pallas_call(
        functools.partial(_combine_kernel, cap=cap, final=final),
        out_shape=jax.ShapeDtypeStruct((b, n, d), F32),
        grid=(b, n // tn),
        in_specs=in_specs,
        out_specs=pl.BlockSpec((1, tn, d), lambda i, j: (i, j, 0)),
        compiler_params=_params(2),
        name="moe_combine",
    )(*args)


def _rope_tables(n):
    t = np.arange(n)
    row = (t // GRID_W).astype(np.float32)
    col = (t % GRID_W).astype(np.float32)
    nf = DIFF_QK // 4
    inv = np.float32(ROPE_BASE) ** (-np.arange(nf, dtype=np.float32) / np.float32(nf))
    lane = np.arange(DIFF_WIDTH)
    pos = np.where(((lane % DIFF_QK) < DIFF_QK // 2)[None, :], row[:, None], col[:, None])
    ang = (pos * inv[lane % nf][None, :]).astype(np.float32)
    first = (lane % (2 * nf)) < nf
    cos, sin = np.cos(ang).astype(np.float32), np.sin(ang).astype(np.float32)
    return jnp.asarray(cos), jnp.asarray(np.where(first[None, :], -sin, sin))


def kernel(x_prompt, x_sample, cache_na_k, cache_na_v, cache_diff_k, cache_diff_v, state_ssm_re, state_ssm_im,
           c, c_ctx, w_ada, b_ada, norm_mix, norm_ffn, w_in, w_out, na_rpb, diff_lambda, diff_subln,
           ssm_a_re, ssm_a_im, ssm_log_dt, ssm_b_re, ssm_b_im, ssm_c_re, ssm_c_im, ssm_d, ssm_w_glu, ssm_b_glu,
           w_router, w_gate, w_up, w_down, final_norm):
    depth = w_in.shape[0]
    bp, sp, d = x_prompt.shape
    bs, ss, _ = x_sample.shape
    assert d == D_MODEL and bs + 1 <= SUBLANES
    past = cache_na_k.shape[2]

    cond = jnp.zeros((SUBLANES, d), F32).at[0].set(c_ctx).at[1:1 + bs].set(c)
    mods = _ada(cond, w_ada, b_ada).reshape(depth, SUBLANES, 6, d)

    rope_tabs = _rope_tables(ss)
    kc_a = cache_na_k.reshape(bs, depth, past, NA_WIDTH)
    vc_a = cache_na_v.reshape(bs, depth, past, NA_WIDTH)
    kc_b = cache_diff_k.reshape(bs, depth, past, DIFF_WIDTH)
    vc_b = cache_diff_v.reshape(bs, depth, past, DIFF_WIDTH)
    subln = jnp.tile(diff_subln, (1, LANES // DIFF_V)).reshape(depth, 1, LANES)
    norm_mix = norm_mix.reshape(depth, 1, d)
    norm_ffn = norm_ffn.reshape(depth, 1, d)
    ssm_d = ssm_d.reshape(depth, 1, SSM_WIDTH)
    ssm_b_glu = ssm_b_glu.reshape(depth, 1, SSM_WIDTH)
    wr_t = jnp.swapaxes(w_router, 1, 2).astype(F32)
    wr_hi = wr_t.astype(BF16)
    wr_t = jnp.concatenate([wr_hi, (wr_t - wr_hi.astype(F32)).astype(BF16)], axis=1)
    fnorm = final_norm.reshape(1, d)
    zero_state = jnp.zeros((bp, 2, SSM_N), F32)
    tabs = _s5_tables(ssm_a_re, ssm_a_im, ssm_log_dt, ssm_b_re, ssm_b_im, ssm_c_re, ssm_c_im)
    bias_tab = _na_bias_table(na_rpb)

    xp, xs = x_prompt, x_sample
    new_sre, new_sim = [], []
    caches = None
    for l in range(depth):
        lam_init = 0.8 - 0.6 * math.exp(-0.3 * l)
        mod_p = mods[l, 0:1]
        mod_s = mods[l, 1:1 + bs]
        diff_params = (diff_lambda, subln)

        qa, ka, va, qb, kb, vb, u = _proj_in(xp, mod_p, norm_mix, w_in, l, None, depth, caches)
        caches = (ka, va, kb, vb)
        o_a = _attention(qa, ka, va, l, None, l, None, lam_init)
        o_b = _attention(qb, kb, vb, l, None, l, diff_params, lam_init)
        o_c, fre, fim = _s5(u, zero_state, zero_state, tabs, ssm_d, ssm_w_glu, ssm_b_glu, l)
        xp, hp, lg_p = _proj_out(o_a, o_b, o_c, w_out, xp, mod_p, norm_ffn, wr_t, l)
        new_sre.append(fre.reshape(bp, 2, SSM_GROUPS, SSM_STATE))
        new_sim.append(fim.reshape(bp, 2, SSM_GROUPS, SSM_STATE))

        qa, ka, va, qb, kb, vb, u = _proj_in(xs, mod_s, norm_mix, w_in, l, rope_tabs, 0, None)
        o_a = _neighbourhood_attention(qa, ka, va, kc_a, vc_a, bias_tab, l)
        o_b = _attention(qb, kb, vb, 0, (kc_b, vc_b), l, diff_params, lam_init)
        h0re = state_ssm_re[:, l].reshape(bs, 2, SSM_N)
        h0im = state_ssm_im[:, l].reshape(bs, 2, SSM_N)
        o_c, _, _ = _s5(u, h0re, h0im, tabs, ssm_d, ssm_w_glu, ssm_b_glu, l)
        xs, hs, lg_s = _proj_out(o_a, o_b, o_c, w_out, xs, mod_s, norm_ffn, wr_t, l)

        pos_p, gate_p = _route(lg_p)
        pos_s, gate_s = _route(lg_s)
        xg_p, gs_p = _gather(hp, pos_p, gate_p)
        xg_s, gs_s = _gather(hs, pos_s, gate_s)
        ye_p, ye_s = _ffn(xg_p, gs_p, xg_s, gs_s, w_gate, w_up, w_down, l)
        last = l == depth - 1
        xp = _combine(ye_p, pos_p, xp, mod_p, fnorm if last else None)
        xs = _combine(ye_s, pos_s, xs, mod_s, fnorm if last else None)

    ka, va, kb, vb = caches
    return (xp, xs, ka.reshape(bp, depth, sp, NA_HEADS, HEAD_DIM), va.reshape(bp, depth, sp, NA_HEADS, HEAD_DIM),
            kb.reshape(bp, depth, sp, DIFF_HEADS, DIFF_V), vb.reshape(bp, depth, sp, DIFF_HEADS, DIFF_V),
            jnp.stack(new_sre, axis=1), jnp.stack(new_sim, axis=1))
```

```python
import functools
import math

import jax
import jax.numpy as jnp
import numpy as np
from jax import lax
from jax.experimental import pallas as pl
from jax.experimental.pallas import tpu as pltpu

F32 = jnp.float32
BF16 = jnp.bfloat16
I32 = jnp.int32

D_MODEL = 1024
GRID_W = 64
HEAD_DIM = 64
NA_HEADS = 8
NA_WIDTH = NA_HEADS * HEAD_DIM
WIN_R = 8
WIN_C = 16
DIFF_HEADS = 4
DIFF_QK = 32
DIFF_V = 64
DIFF_WIDTH = DIFF_HEADS * DIFF_V
SSM_GROUPS = 16
SSM_GROUP_CH = 16
SSM_WIDTH = SSM_GROUPS * SSM_GROUP_CH
SSM_STATE = 64
SSM_N = SSM_GROUPS * SSM_STATE
IN_WIDTH = 3 * NA_WIDTH + 3 * DIFF_WIDTH + SSM_WIDTH
N_EXPERTS = 16
EXPERT_FF = 2048
EC_CAPACITY_FACTOR = 2
ROPE_BASE = 10000.0
EPS = 1e-6
NEG_INF = -1e30
LOG2E = math.log2(math.e)

LANES = 128
SUBLANES = 8
MXU_DIM = 256
VMEM_LIMIT_CAP = 60000 * 1024

ROW_TILE = 256
STEP_CHUNKS = 2
ATTN_REQUESTS_PER_STEP = 2
NA_QROWS = 4
NA_KROWS = 12
SCAN_TILE = 256
FF_TILE = 1024
COMBINE_TILE = 512
GATHER_ROWS = 512
DEN_IN_MATMUL_MIN_KEYS = 1024


def _params(ndims):
    return pltpu.CompilerParams(
        dimension_semantics=("arbitrary",) * ndims,
        vmem_limit_bytes=VMEM_LIMIT_CAP,
    )


def _step_tile(b, n, n_mod):
    tm = min(n, ROW_TILE * STEP_CHUNKS)
    bb = max(1, ROW_TILE * STEP_CHUNKS // tm)
    if n_mod > 1 or b % bb:
        bb = 1
    return bb, tm


def _dot(a, b):
    return jnp.dot(a, b, preferred_element_type=F32)


def _dot_nt(a, b):
    return lax.dot_general(a, b, (((1,), (1,)), ((), ())), preferred_element_type=F32)


def _dot_tn(a, b):
    return lax.dot_general(a, b, (((0,), (0,)), ((), ())), preferred_element_type=F32)


def _cast_rows(dst_ref, src_ref, rows):
    n = src_ref.shape[0]
    for r in range(0, n, rows):
        dst_ref[r:r + rows, :] = src_ref[r:r + rows, :].astype(BF16)


def _modulated_norm(x, g, shift, scale):
    ms = jnp.mean(x * x, axis=-1, keepdims=True)
    return (x * lax.rsqrt(ms + EPS) * g) * (1.0 + scale) + shift


def _ada_kernel(c_ref, w_ref, b_ref, o_ref):
    c = c_ref[...]
    s = c * jax.nn.sigmoid(c)
    o_ref[0] = _dot(s.astype(BF16), w_ref[0].astype(BF16)) + b_ref[0]


def _ada(cond, w_ada, b_ada):
    depth = w_ada.shape[0]
    tn = 1536
    nt = 6 * D_MODEL // tn
    return pl.pallas_call(
        _ada_kernel,
        out_shape=jax.ShapeDtypeStruct((depth, SUBLANES, 6 * D_MODEL), F32),
        grid=(depth, nt),
        in_specs=[
            pl.BlockSpec((SUBLANES, D_MODEL), lambda l, j: (0, 0)),
            pl.BlockSpec((1, D_MODEL, tn), lambda l, j: (l, 0, j)),
            pl.BlockSpec((1, 1, tn), lambda l, j: (l, 0, j)),
        ],
        out_specs=pl.BlockSpec((1, SUBLANES, tn), lambda l, j: (l, 0, j)),
        compiler_params=_params(2),
        name="ada",
    )(cond, w_ada, b_ada.reshape(depth, 1, 6 * D_MODEL))


def _rope_apply(x, cos, sin_signed):
    lane = lax.broadcasted_iota(I32, (1, LANES), 1)
    first = (lane % 16) < 8
    outs = []
    for t in range(x.shape[1] // LANES):
        xt = x[:, t * LANES:(t + 1) * LANES]
        partner = jnp.where(first, pltpu.roll(xt, LANES - 8, axis=1), pltpu.roll(xt, 8, axis=1))
        outs.append(xt * cos[:, t * LANES:(t + 1) * LANES] + partner * sin_signed[:, t * LANES:(t + 1) * LANES])
    return jnp.concatenate(outs, axis=1)


def _proj_in_kernel(*refs, rope, n_alias):
    x_ref, mod_ref, g_ref, w_ref = refs[:4]
    pos = 4
    if rope:
        cos_ref, sin_ref = refs[pos:pos + 2]
        pos += 2
    pos += n_alias
    qa_ref, ka_ref, va_ref, qb_ref, kb_ref, vb_ref, u_ref, wbf_ref = refs[pos:pos + 8]

    @pl.when((pl.program_id(0) == 0) & (pl.program_id(1) == 0))
    def _():
        _cast_rows(wbf_ref, w_ref, 128)

    i1, i2, i3 = NA_WIDTH, 2 * NA_WIDTH, 3 * NA_WIDTH
    i4, i5, i6 = i3 + DIFF_WIDTH, i3 + 2 * DIFF_WIDTH, i3 + 3 * DIFF_WIDTH
    for bi in range(x_ref.shape[0]):
        for r0 in range(0, x_ref.shape[1], ROW_TILE):
            rows = slice(r0, r0 + ROW_TILE)

            def put(ref, val):
                for s in range(ref.shape[1]):
                    ref[bi, s, rows, :] = val.astype(ref.dtype)

            h = _modulated_norm(x_ref[bi, rows, :], g_ref[...], mod_ref[0, 0:1, :], mod_ref[0, 1:2, :])
            z = _dot(h.astype(BF16), wbf_ref[...])
            qa_ref[bi, rows, :] = (z[:, :i1] * (HEAD_DIM ** -0.5 * LOG2E)).astype(qa_ref.dtype)
            put(ka_ref, z[:, i1:i2])
            put(va_ref, z[:, i2:i3])
            qb = z[:, i3:i4]
            kb = z[:, i4:i5]
            if rope:
                qb = _rope_apply(qb, cos_ref[rows, :], sin_ref[rows, :])
                kb = _rope_apply(kb, cos_ref[rows, :], sin_ref[rows, :])
            qb_ref[bi, rows, :] = (qb * (DIFF_QK ** -0.5 * LOG2E)).astype(qb_ref.dtype)
            put(kb_ref, kb)
            put(vb_ref, z[:, i5:i6])
            for hf in range(SSM_WIDTH // LANES):
                u_ref[bi, hf, rows, :] = z[:, i6 + hf * LANES:i6 + (hf + 1) * LANES]


def _proj_in(x, mod, g_norm, w_in, layer, rope_tabs, cache_slots, caches):
    b, n, d = x.shape
    bb, tm = _step_tile(b, n, mod.shape[0])
    rope = rope_tabs is not None
    mod_map = (lambda i, j: (i, 0, 0)) if mod.shape[0] > 1 else (lambda i, j: (0, 0, 0))
    in_specs = [
        pl.BlockSpec((bb, tm, d), lambda i, j: (i, j, 0)),
        pl.BlockSpec((1, 6, d), mod_map),
        pl.BlockSpec((None, 1, d), lambda i, j: (layer, 0, 0)),
        pl.BlockSpec((None, d, IN_WIDTH), lambda i, j: (layer, 0, 0), pipeline_mode=pl.Buffered(1)),
    ]
    args = [x, mod, g_norm, w_in]
    if rope:
        in_specs += [pl.BlockSpec((tm, DIFF_WIDTH), lambda i, j: (j, 0))] * 2
        args += list(rope_tabs)
    aliases = {}
    kv_out = (1, 2, 4, 5)
    if caches is not None:
        for c, o in zip(caches, kv_out):
            aliases[len(args)] = o
            in_specs.append(pl.BlockSpec(memory_space=pl.ANY))
            args.append(c)
    widths = (NA_WIDTH, NA_WIDTH, NA_WIDTH, DIFF_WIDTH, DIFF_WIDTH, DIFF_WIDTH)
    nslab = SSM_WIDTH // LANES
    out_shape, out_specs = [], []
    for o, w in enumerate(widths):
        if o not in kv_out:
            out_shape.append(jax.ShapeDtypeStruct((b, n, w), BF16))
            out_specs.append(pl.BlockSpec((bb, tm, w), lambda i, j: (i, j, 0)))
        elif cache_slots == 0:
            out_shape.append(jax.ShapeDtypeStruct((b, 1, n, w), BF16))
            out_specs.append(pl.BlockSpec((bb, 1, tm, w), lambda i, j: (i, 0, j, 0)))
        elif caches is None:
            out_shape.append(jax.ShapeDtypeStruct((b, cache_slots, n, w), F32))
            out_specs.append(pl.BlockSpec((bb, cache_slots, tm, w), lambda i, j: (i, 0, j, 0)))
        else:
            out_shape.append(jax.ShapeDtypeStruct((b, cache_slots, n, w), F32))
            out_specs.append(pl.BlockSpec((bb, 1, tm, w), lambda i, j: (i, layer, j, 0)))
    out_shape.append(jax.ShapeDtypeStruct((b, nslab, n, LANES), F32))
    out_specs.append(pl.BlockSpec((bb, nslab, tm, LANES), lambda i, j: (i, 0, j, 0)))
    return pl.pallas_call(
        functools.partial(_proj_in_kernel, rope=rope, n_alias=len(aliases)),
        out_shape=out_shape,
        grid=(b // bb, n // tm),
        in_specs=in_specs,
        out_specs=out_specs,
        scratch_shapes=[pltpu.VMEM((d, IN_WIDTH), BF16)],
        input_output_aliases=aliases,
        compiler_params=_params(2),
        name="proj_in",
    )(*args)


def _masked_query(qp, mask):
    return jnp.where(mask, qp, jnp.zeros_like(qp))


def _with_ones(v):
    return jnp.concatenate([v, jnp.ones_like(v)], axis=1)


def _softmax_av_wide(qm, ks, vs, biases):
    ss = []
    for k, bias in zip(ks, biases):
        s = _dot_nt(qm, k)
        if bias is not None:
            s = bias(s)
        ss.append(s)
    m = ss[0].max(axis=-1, keepdims=True)
    for s in ss[1:]:
        m = jnp.maximum(m, s.max(axis=-1, keepdims=True))
    acc = None
    for s, v in zip(ss, vs):
        pv = _dot(jnp.exp2(s - m).astype(BF16), v)
        acc = pv if acc is None else acc + pv
    return acc[:, :LANES] / acc[:, LANES:]


def _softmax_av(qm, ks, vs, biases):
    ss = []
    for k, bias in zip(ks, biases):
        s = _dot_nt(qm, k)
        if bias is not None:
            s = bias(s)
        ss.append(s)
    m = ss[0].max(axis=-1, keepdims=True)
    for s in ss[1:]:
        m = jnp.maximum(m, s.max(axis=-1, keepdims=True))
    acc = None
    den = None
    for s, v in zip(ss, vs):
        p = jnp.exp2(s - m)
        psum = p.sum(axis=-1, keepdims=True)
        den = psum if den is None else den + psum
        pv = _dot(p.astype(BF16), v)
        acc = pv if acc is None else acc + pv
    return acc / den


def _diff_lambda(lam_ref, lam_init):
    lp = lam_ref[...]
    a = jnp.sum(lp[0:1] * lp[1:2], axis=-1, keepdims=True)
    b = jnp.sum(lp[2:3] * lp[3:4], axis=-1, keepdims=True)
    return jnp.exp(a) - jnp.exp(b) + lam_init


def _attn_kernel(*refs, has_ctx, diff, lam_init):
    refs = list(refs)
    q_ref, k_ref, v_ref = refs[:3]
    pos = 3
    if has_ctx:
        kc_ref, vc_ref = refs[pos:pos + 2]
        pos += 2
    if diff:
        lam_ref, subln_ref = refs[pos:pos + 2]
        pos += 2
    o_ref = refs[pos]
    width = q_ref.shape[2]
    lane = lax.broadcasted_iota(I32, (1, LANES), 1)
    if diff:
        lam = _diff_lambda(lam_ref, lam_init)
    for bi, p in [(bi, p) for bi in range(q_ref.shape[0]) for p in range(width // LANES)]:
        sl = slice(p * LANES, (p + 1) * LANES)
        qp = q_ref[bi, :, sl]
        ks = [k_ref[bi, :, sl].astype(BF16)]
        vs = [v_ref[bi, :, sl].astype(BF16)]
        if has_ctx:
            ks.append(kc_ref[:, sl].astype(BF16))
            vs.append(vc_ref[:, sl].astype(BF16))
        nob = [None] * len(ks)
        tq = qp.shape[0]
        wide = sum(k.shape[0] for k in ks) >= DEN_IN_MATMUL_MIN_KEYS
        if wide:
            vs = [_with_ones(v) for v in vs]
        pair = None
        for sub in range(LANES // HEAD_DIM):
            head_mask = (lane // HEAD_DIM) == sub
            if diff:
                qmasks = [(lane // DIFF_QK) == (2 * sub + half) for half in range(2)]
                av = _softmax_av_wide if wide else _softmax_av
                os_ = [av(_masked_query(qp, mk), ks, vs, nob) for mk in qmasks]
                o = jnp.where(head_mask, os_[0] - lam * os_[1], 0.0)
                ms = jnp.sum(o * o, axis=-1, keepdims=True) * (1.0 / DIFF_V)
                o = (o * lax.rsqrt(ms + EPS) * subln_ref[...]) * (1.0 - lam_init)
            else:
                qm = _masked_query(qp, head_mask)
                o = _softmax_av_wide(qm, ks, vs, nob) if wide else _softmax_av(qm, ks, vs, nob)
                o = jnp.where(head_mask, o, 0.0)
            pair = o if pair is None else pair + o
        o_ref[bi, :, sl] = pair.astype(o_ref.dtype)


def _attention(q, k, v, kv_slot, ctx_kv, layer, diff_params, lam_init):
    b, nq, w = q.shape
    nk = k.shape[2]
    tq = ROW_TILE
    has_ctx = ctx_kv is not None
    diff = diff_params is not None
    bb = ATTN_REQUESTS_PER_STEP if (nq == tq and not has_ctx and b % ATTN_REQUESTS_PER_STEP == 0) else 1
    in_specs = [
        pl.BlockSpec((bb, tq, w), lambda i, j: (i, j, 0)),
        pl.BlockSpec((bb, None, nk, w), lambda i, j: (i, kv_slot, 0, 0)),
        pl.BlockSpec((bb, None, nk, w), lambda i, j: (i, kv_slot, 0, 0)),
    ]
    args = [q, k, v]
    nc = 0
    if has_ctx:
        nc = ctx_kv[0].shape[2]
        in_specs += [pl.BlockSpec((None, None, nc, w), lambda i, j: (i, layer, 0, 0))] * 2
        args += list(ctx_kv)
    if diff:
        lam_p, subln = diff_params
        in_specs += [
            pl.BlockSpec((None, 4, DIFF_QK), lambda i, j: (layer, 0, 0)),
            pl.BlockSpec((None, 1, LANES), lambda i, j: (layer, 0, 0)),
        ]
        args += [lam_p, subln]
    return pl.pallas_call(
        functools.partial(_attn_kernel, has_ctx=has_ctx, diff=diff, lam_init=lam_init),
        out_shape=jax.ShapeDtypeStruct((b, nq, w), BF16),
        grid=(b // bb, nq // tq),
        in_specs=in_specs,
        out_specs=pl.BlockSpec((bb, tq, w), lambda i, j: (i, j, 0)),
        compiler_params=_params(2),
        name="attn_diff" if diff else "attn",
    )(*args)


def _na_kernel(q_ref, k_ref, v_ref, kc_ref, vc_ref, bias_ref, o_ref, *, rows):
    i = pl.program_id(1)
    tq = NA_QROWS * GRID_W
    nkw = NA_KROWS * GRID_W
    ustart = jnp.clip(NA_QROWS * i - WIN_R // 2, 0, rows - NA_KROWS)
    koff = pl.multiple_of(ustart * GRID_W, GRID_W)
    delta = ustart - NA_QROWS * i
    qrow = NA_QROWS * i + lax.broadcasted_iota(I32, (tq, nkw), 0) // GRID_W
    krow = ustart + lax.broadcasted_iota(I32, (tq, nkw), 1) // GRID_W
    wstart = jnp.clip(qrow - WIN_R // 2, 0, rows - WIN_R)
    row_ok = (krow >= wstart) & (krow < wstart + WIN_R)
    nsub = LANES // HEAD_DIM
    lane = lax.broadcasted_iota(I32, (1, LANES), 1)
    for p in range(NA_WIDTH // LANES):
        sl = slice(p * LANES, (p + 1) * LANES)
        qp = q_ref[0, :, sl]
        ks = [k_ref[0, pl.ds(koff, nkw), sl].astype(BF16), kc_ref[:, sl].astype(BF16)]
        vs = [_with_ones(v_ref[0, pl.ds(koff, nkw), sl].astype(BF16)), _with_ones(vc_ref[:, sl].astype(BF16))]
        outs = []
        for sub in range(nsub):
            def win_bias(s, h=p * nsub + sub):
                rows_ = []
                for ri in range(NA_QROWS):
                    tiles = []
                    for pj in range(NA_KROWS // 2):
                        dr = delta + 2 * pj - ri
                        idx = jnp.clip(dr, -WIN_R, WIN_R - 1) + WIN_R
                        tiles.append(bias_ref[h, idx])
                    rows_.append(jnp.concatenate(tiles, axis=1))
                bias = jnp.concatenate(rows_, axis=0)
                return jnp.where(row_ok, s + bias, NEG_INF)

            outs.append(_softmax_av_wide(_masked_query(qp, (lane // HEAD_DIM) == sub), ks, vs, [win_bias, None]))
        pair = jnp.where((lane // HEAD_DIM) == 0, outs[0], outs[1])
        o_ref[0, :, sl] = pair.astype(o_ref.dtype)


def _na_table_kernel(v_ref, o_ref):
    nd = v_ref.shape[0]
    q = lax.broadcasted_iota(I32, (GRID_W, 2 * GRID_W), 0)
    k = lax.broadcasted_iota(I32, (GRID_W, 2 * GRID_W), 1)
    kk = k % GRID_W
    cstart = jnp.clip(q - WIN_C // 2, 0, GRID_W - WIN_C)
    col_ok = (kk >= cstart) & (kk < cstart + WIN_C)
    left = k < GRID_W
    tiles = []
    for dr in range(nd):
        x = jnp.broadcast_to(v_ref[dr:dr + 1, :], (GRID_W, 2 * GRID_W))
        tiles.append((pltpu.roll(x, GRID_W + 1, axis=1, stride=1, stride_axis=0),
                      pltpu.roll(x, 1, axis=1, stride=1, stride_axis=0)))
    zero = jnp.zeros((GRID_W, 2 * GRID_W), F32)
    for j in range(nd + 1):
        lt = tiles[j - 1][0] if j >= 1 else zero
        rt = tiles[j][1] if j < nd else zero
        o_ref[j] = jnp.where(col_ok, jnp.where(left, lt, rt) * LOG2E, NEG_INF)


def _na_bias_table(rpb):
    depth, h, nd, nc = rpb.shape
    assert 2 * GRID_W == LANES and nd == 2 * WIN_R - 1 and nc == 2 * WIN_C - 1
    rpb = rpb.astype(F32)
    lo = GRID_W - WIN_C
    v = jnp.concatenate([jnp.broadcast_to(rpb[..., :1], rpb.shape[:3] + (lo,)), rpb,
                         jnp.broadcast_to(rpb[..., -1:], rpb.shape[:3] + (2 * GRID_W - lo - nc,))], axis=-1)
    return pl.pallas_call(
        _na_table_kernel,
        out_shape=jax.ShapeDtypeStruct((depth, h, nd + 1, GRID_W, 2 * GRID_W), F32),
        grid=(depth, h),
        in_specs=[pl.BlockSpec((None, None, nd, 2 * GRID_W), lambda l, i: (l, i, 0, 0))],
        out_specs=pl.BlockSpec((None, None, nd + 1, GRID_W, 2 * GRID_W), lambda l, i: (l, i, 0, 0, 0)),
        compiler_params=_params(2),
        name="na_table",
    )(v)


def _neighbourhood_attention(q, k, v, kc, vc, bias_tab, layer):
    b, n, w = q.shape
    rows = n // GRID_W
    tq = NA_QROWS * GRID_W
    nc = kc.shape[2]
    nkw = NA_KROWS * GRID_W
    return pl.pallas_call(
        functools.partial(_na_kernel, rows=rows),
        out_shape=jax.ShapeDtypeStruct((b, n, w), BF16),
        grid=(b, n // tq),
        in_specs=[
            pl.BlockSpec((1, tq, w), lambda i, j: (i, j, 0)),
            pl.BlockSpec((1, None, n, w), lambda i, j: (i, 0, 0, 0)),
            pl.BlockSpec((1, None, n, w), lambda i, j: (i, 0, 0, 0)),
            pl.BlockSpec((None, None, nc, w), lambda i, j: (i, layer, 0, 0)),
            pl.BlockSpec((None, None, nc, w), lambda i, j: (i, layer, 0, 0)),
            pl.BlockSpec((None,) + bias_tab.shape[1:], lambda i, j: (layer, 0, 0, 0, 0)),
        ],
        out_specs=pl.BlockSpec((1, tq, w), lambda i, j: (i, j, 0)),
        compiler_params=_params(2),
        name="na_attn",
    )(q, k, v, kc, vc, bias_tab)


def _sublane_scan(er, ei, tr, ti, pw_ref, base, lanes, reverse):
    row = lax.broadcasted_iota(I32, (SUBLANES, LANES), 0)
    if reverse:
        first = row == SUBLANES - 1
        xr = jnp.where(first, tr, pltpu.roll(er, SUBLANES - 1, axis=0))
        xi = jnp.where(first, ti, pltpu.roll(ei, SUBLANES - 1, axis=0))
    else:
        first = row == 0
        xr = jnp.where(first, tr, pltpu.roll(er, 1, axis=0))
        xi = jnp.where(first, ti, pltpu.roll(ei, 1, axis=0))
    for k, s in enumerate((1, 2, 4)):
        ar = pw_ref[base + 2 * k:base + 2 * k + 1, lanes]
        ai = pw_ref[base + 2 * k + 1:base + 2 * k + 2, lanes]
        if reverse:
            keep = row < SUBLANES - s
            sr = jnp.where(keep, pltpu.roll(xr, SUBLANES - s, axis=0), 0.0)
            si = jnp.where(keep, pltpu.roll(xi, SUBLANES - s, axis=0), 0.0)
        else:
            keep = row >= s
            sr = jnp.where(keep, pltpu.roll(xr, s, axis=0), 0.0)
            si = jnp.where(keep, pltpu.roll(xi, s, axis=0), 0.0)
        xr, xi = xr + (ar * sr - ai * si), xi + (ar * si + ai * sr)
    return xr, xi


def _s5_kernel(u_ref, h0re_ref, h0im_ref, bt_ref, ct_ref, a1_ref, ach_ref, pfix_ref, d_ref, wglu_ref, bglu_ref,
               o_ref, fre_ref, fim_ref, up_ref, yp_ref, buf_ref, car_ref, bmat_ref, cmat_ref):
    seq = u_ref.shape[2]
    tt = SCAN_TILE
    ch = tt // SUBLANES
    ntile = seq // tt
    nlt = SSM_N // LANES
    nhalf = SSM_WIDTH // LANES

    @pl.when(pl.program_id(0) == 0)
    def _():
        g, p, hc = SSM_GROUPS, SSM_STATE, SSM_GROUP_CH
        btb = bt_ref[...].astype(BF16)
        ctb = ct_ref[...].astype(BF16)
        for dc in range(4):
            r = lax.broadcasted_iota(I32, (4 * p, g * p), 0)
            c = lax.broadcasted_iota(I32, (4 * p, g * p), 1)
            sel = jnp.where(r == dc * p + c % p, 1.0, 0.0).astype(BF16)
            r = lax.broadcasted_iota(I32, (g * hc, g * p), 0)
            c = lax.broadcasted_iota(I32, (g * hc, g * p), 1)
            blk = jnp.where(r // hc == c // p, _dot(btb, sel), 0.0)
            bmat_ref[:, dc * g * p:(dc + 1) * g * p] = blk.astype(BF16)
            r = lax.broadcasted_iota(I32, (g * p, 4 * p), 0)
            c = lax.broadcasted_iota(I32, (g * p, 4 * p), 1)
            sel = jnp.where(c == dc * p + r % p, 1.0, 0.0).astype(BF16)
            r = lax.broadcasted_iota(I32, (g * p, g * hc), 0)
            c = lax.broadcasted_iota(I32, (g * p, g * hc), 1)
            blk = jnp.where(r // p == c // hc, _dot(sel, ctb), 0.0)
            cmat_ref[dc * g * p:(dc + 1) * g * p, :] = blk.astype(BF16)

    def permute_in(t, _):
        t0 = pl.multiple_of(t * tt, tt)
        for j in range(0, ch, 2):
            for hf in range(nhalf):
                lanes = slice(hf * LANES, (hf + 1) * LANES)
                rows = jnp.concatenate([u_ref[0, hf, pl.ds(t0 + j + jj, SUBLANES, stride=ch), :] for jj in range(2)],
                                       axis=0)
                r0 = pl.multiple_of(t0 + j * SUBLANES, 2 * SUBLANES)
                up_ref[pl.ds(r0, 2 * SUBLANES), lanes] = rows.astype(BF16)
                yp_ref[pl.ds(r0, 2 * SUBLANES), lanes] = rows * d_ref[:, lanes]
        return 0

    lax.fori_loop(0, ntile, permute_in, 0)

    for d in range(2):
        for lt in range(nlt):
            lanes = slice(lt * LANES, (lt + 1) * LANES)
            car_ref[d, 0, :, lanes] = jnp.broadcast_to(h0re_ref[0, d:d + 1, lanes], (SUBLANES, LANES))
            car_ref[d, 1, :, lanes] = jnp.broadcast_to(h0im_ref[0, d:d + 1, lanes], (SUBLANES, LANES))

    def tile(t, _):
        starts = []
        for d in range(2):
            tix = (ntile - 1 - t) if d == 1 else t
            t0 = pl.multiple_of(tix * tt, tt)
            starts.append(t0)
            buf_ref[d] = _dot(up_ref[pl.ds(t0, tt), :], bmat_ref[:, d * 2 * SSM_N:(d + 1) * 2 * SSM_N])
        for d in range(2):
            reverse = d == 1
            order = range(ch - 1, -1, -1) if reverse else range(ch)
            for lt in range(nlt):
                lre = slice(lt * LANES, (lt + 1) * LANES)
                lim = slice(SSM_N + lt * LANES, SSM_N + (lt + 1) * LANES)
                ar = a1_ref[2 * d:2 * d + 1, lre]
                ai = a1_ref[2 * d + 1:2 * d + 2, lre]
                cr = jnp.zeros((SUBLANES, LANES), F32)
                ci = jnp.zeros((SUBLANES, LANES), F32)
                for j in order:
                    rows = slice(j * SUBLANES, (j + 1) * SUBLANES)
                    cr, ci = (ar * cr - ai * ci) + buf_ref[d, rows, lre], (ar * ci + ai * cr) + buf_ref[d, rows, lim]
                    buf_ref[d, rows, lre] = cr
                    buf_ref[d, rows, lim] = ci
                tr, ti = car_ref[d, 0, :, lre], car_ref[d, 1, :, lre]
                gr, gi = _sublane_scan(cr, ci, tr, ti, ach_ref, 6 * d, lre, reverse)
                a_r, a_i = ach_ref[6 * d:6 * d + 1, lre], ach_ref[6 * d + 1:6 * d + 2, lre]
                nr, ni = (a_r * gr - a_i * gi) + cr, (a_r * gi + a_i * gr) + ci
                edge = 0 if reverse else SUBLANES - 1
                car_ref[d, 0, :, lre] = jnp.broadcast_to(nr[edge:edge + 1, :], (SUBLANES, LANES))
                car_ref[d, 1, :, lre] = jnp.broadcast_to(ni[edge:edge + 1, :], (SUBLANES, LANES))
                for j in range(ch):
                    rows = slice(j * SUBLANES, (j + 1) * SUBLANES)
                    pr = pfix_ref[2 * d, j:j + 1, lre]
                    pi = pfix_ref[2 * d + 1, j:j + 1, lre]
                    buf_ref[d, rows, lre] += pr * gr - pi * gi
                    buf_ref[d, rows, lim] += pr * gi + pi * gr
        for d in range(2):
            yp_ref[pl.ds(starts[d], tt), :] += _dot(buf_ref[d].astype(BF16),
                                                    cmat_ref[d * 2 * SSM_N:(d + 1) * 2 * SSM_N, :])
        return 0

    lax.fori_loop(0, ntile, tile, 0)

    for d in range(2):
        fre_ref[0, d:d + 1, :] = car_ref[d, 0, 0:1, :]
        fim_ref[0, d:d + 1, :] = car_ref[d, 1, 0:1, :]

    wglu = wglu_ref[...].astype(BF16)

    def glu(t, _):
        t0 = pl.multiple_of(t * tt, tt)
        g = jax.nn.gelu(yp_ref[pl.ds(t0, tt), :])
        out = g * jax.nn.sigmoid(_dot(g.astype(BF16), wglu) + bglu_ref[...])
        for j in range(ch):
            for hf in range(nhalf):
                o_ref[0, hf, pl.ds(t0 + j, SUBLANES, stride=ch), :] = out[j * SUBLANES:(j + 1) * SUBLANES,
                                                                          hf * LANES:(hf + 1) * LANES]
        return 0

    lax.fori_loop(0, ntile, glu, 0)


def _cmul(ar, ai, br, bi):
    return ar * br - ai * bi, ar * bi + ai * br


def _s5_tables(a_re, a_im, log_dt, b_re, b_im, c_re, c_im):
    depth = a_re.shape[0]
    g, p, hc = SSM_GROUPS, SSM_STATE, SSM_GROUP_CH
    ch = SCAN_TILE // SUBLANES
    lr, li = a_re.astype(F32), a_im.astype(F32)
    dt = jnp.exp(log_dt.astype(F32))[..., None]
    mag = jnp.exp(lr * dt)
    ar, ai = mag * jnp.cos(li * dt), mag * jnp.sin(li * dt)
    den = lr * lr + li * li
    qr = ((ar - 1.0) * lr + ai * li) / den
    qi = (ai * lr - (ar - 1.0) * li) / den
    br, bi = _cmul(qr[..., None], qi[..., None], b_re.astype(F32), b_im.astype(F32))
    bparts = jnp.stack([br, bi], axis=2)
    bmat = bparts.transpose(0, 3, 5, 1, 2, 4).reshape(depth, g * hc, 4 * p)
    cparts = jnp.stack([c_re.astype(F32), -c_im.astype(F32)], axis=2)
    cmat = cparts.transpose(0, 1, 2, 5, 3, 4).reshape(depth, 4 * p, g * hc)

    def power(k):
        kk = k[None, None, :, None, None]
        m = jnp.exp(kk * (lr * dt)[:, :, None])
        th = kk * (li * dt)[:, :, None]
        return (m * jnp.cos(th)).reshape(depth, 2, -1, g * p), (m * jnp.sin(th)).reshape(depth, 2, -1, g * p)

    a1 = jnp.stack([ar.reshape(depth, 2, g * p), ai.reshape(depth, 2, g * p)], axis=2).reshape(depth, 4, g * p)
    cr_, ci_ = power(jnp.asarray([ch, 2 * ch, 4 * ch], F32))
    ach = jnp.stack([cr_, ci_], axis=3).reshape(depth, 12, g * p)
    fr, fi = power(jnp.arange(1, ch + 1, dtype=F32))
    fr = jnp.stack([fr[:, 0], fr[:, 1, ::-1]], axis=1)
    fi = jnp.stack([fi[:, 0], fi[:, 1, ::-1]], axis=1)
    pfix = jnp.stack([fr, fi], axis=2).reshape(depth, 4, ch, g * p)
    return bmat, cmat, a1, ach, pfix


def _s5(u, h0re, h0im, tabs, d_skip, w_glu, b_glu, layer):
    b, nh, seq, _ = u.shape
    w = nh * LANES
    lay = lambda a: pl.BlockSpec((None,) + a.shape[1:], lambda i: (layer,) + (0,) * (a.ndim - 1))
    return pl.pallas_call(
        _s5_kernel,
        out_shape=[jax.ShapeDtypeStruct((b, nh, seq, LANES), F32),
                   jax.ShapeDtypeStruct((b, 2, SSM_N), F32),
                   jax.ShapeDtypeStruct((b, 2, SSM_N), F32)],
        grid=(b,),
        in_specs=[
            pl.BlockSpec((1, nh, seq, LANES), lambda i: (i, 0, 0, 0)),
            pl.BlockSpec((1, 2, SSM_N), lambda i: (i, 0, 0)),
            pl.BlockSpec((1, 2, SSM_N), lambda i: (i, 0, 0)),
            lay(tabs[0]), lay(tabs[1]), lay(tabs[2]), lay(tabs[3]), lay(tabs[4]),
            pl.BlockSpec((None, 1, w), lambda i: (layer, 0, 0)),
            pl.BlockSpec((None, w, w), lambda i: (layer, 0, 0)),
            pl.BlockSpec((None, 1, w), lambda i: (layer, 0, 0)),
        ],
        out_specs=[pl.BlockSpec((1, nh, seq, LANES), lambda i: (i, 0, 0, 0)),
                   pl.BlockSpec((1, 2, SSM_N), lambda i: (i, 0, 0)),
                   pl.BlockSpec((1, 2, SSM_N), lambda i: (i, 0, 0))],
        scratch_shapes=[pltpu.VMEM((seq, w), BF16),
                        pltpu.VMEM((seq, w), F32),
                        pltpu.VMEM((2, SCAN_TILE, 2 * SSM_N), F32),
                        pltpu.VMEM((2, 2, SUBLANES, SSM_N), F32),
                        pltpu.VMEM((w, 4 * SSM_N), BF16),
                        pltpu.VMEM((4 * SSM_N, w), BF16)],
        compiler_params=_params(1),
        name="s5",
    )(u, h0re, h0im, *tabs, d_skip, w_glu, b_glu)


def _proj_out_kernel(oa_ref, ob_ref, oc_ref, w_ref, x_ref, mod_ref, g_ref, wr_ref, o_ref, h_ref, lg_ref, wbf_ref):
    @pl.when((pl.program_id(0) == 0) & (pl.program_id(1) == 0))
    def _():
        _cast_rows(wbf_ref, w_ref, 128)

    i1, i2 = NA_WIDTH, NA_WIDTH + DIFF_WIDTH
    for bi in range(x_ref.shape[0]):
        for r0 in range(0, x_ref.shape[1], ROW_TILE):
            rows = slice(r0, r0 + ROW_TILE)
            y = _dot(oa_ref[bi, rows, :].astype(BF16), wbf_ref[:i1, :])
            y += _dot(ob_ref[bi, rows, :].astype(BF16), wbf_ref[i1:i2, :])
            for hf in range(SSM_WIDTH // LANES):
                y += _dot(oc_ref[bi, hf, rows, :].astype(BF16), wbf_ref[i2 + hf * LANES:i2 + (hf + 1) * LANES, :])
            xn = x_ref[bi, rows, :] + mod_ref[0, 2:3, :] * y
            o_ref[bi, rows, :] = xn
            h = _modulated_norm(xn, g_ref[...], mod_ref[0, 3:4, :], mod_ref[0, 4:5, :])
            h_hi = h.astype(BF16)
            h_ref[bi, rows, :] = h_hi
            h_lo = (h - h_hi.astype(F32)).astype(BF16)
            t1 = _dot_nt(wr_ref[...], h_hi)
            t2 = _dot_nt(wr_ref[:N_EXPERTS, :], h_lo)
            lg_ref[bi, :, rows] = t1[:N_EXPERTS, :] + t1[N_EXPERTS:, :] + t2


def _proj_out(oa, ob, oc, w_out, x, mod, g_ffn, wr_t, layer):
    b, n, d = x.shape
    bb, tm = _step_tile(b, n, mod.shape[0])
    mod_map = (lambda i, j: (i, 0, 0)) if mod.shape[0] > 1 else (lambda i, j: (0, 0, 0))
    kw = w_out.shape[1]
    return pl.pallas_call(
        _proj_out_kernel,
        out_shape=[jax.ShapeDtypeStruct((b, n, d), F32),
                   jax.ShapeDtypeStruct((b, n, d), BF16),
                   jax.ShapeDtypeStruct((b, N_EXPERTS, n), F32)],
        grid=(b // bb, n // tm),
        in_specs=[
            pl.BlockSpec((bb, tm, NA_WIDTH), lambda i, j: (i, j, 0)),
            pl.BlockSpec((bb, tm, DIFF_WIDTH), lambda i, j: (i, j, 0)),
            pl.BlockSpec((bb, SSM_WIDTH // LANES, tm, LANES), lambda i, j: (i, 0, j, 0)),
            pl.BlockSpec((None, kw, d), lambda i, j: (layer, 0, 0), pipeline_mode=pl.Buffered(1)),
            pl.BlockSpec((bb, tm, d), lambda i, j: (i, j, 0)),
            pl.BlockSpec((1, 6, d), mod_map),
            pl.BlockSpec((None, 1, d), lambda i, j: (layer, 0, 0)),
            pl.BlockSpec((None, 2 * N_EXPERTS, d), lambda i, j: (layer, 0, 0)),
        ],
        out_specs=[pl.BlockSpec((bb, tm, d), lambda i, j: (i, j, 0)),
                   pl.BlockSpec((bb, tm, d), lambda i, j: (i, j, 0)),
                   pl.BlockSpec((bb, N_EXPERTS, tm), lambda i, j: (i, 0, j))],
        scratch_shapes=[pltpu.VMEM((kw, d), BF16)],
        compiler_params=_params(2),
        name="proj_out",
    )(oa, ob, oc, w_out, x, mod, g_ffn, wr_t)


def _excl_cumsum_lanes(m):
    e, n = m.shape
    blk = MXU_DIM
    nb = n // blk
    r = lax.broadcasted_iota(I32, (blk, blk), 0)
    c = lax.broadcasted_iota(I32, (blk, blk), 1)
    tri = jnp.where(r < c, 1.0, 0.0).astype(BF16)
    stacked = jnp.concatenate([m[:, k * blk:(k + 1) * blk] for k in range(nb)], axis=0).astype(BF16)
    within = _dot(stacked, tri)
    outs = []
    off = jnp.zeros((e, 1), F32)
    for k in range(nb):
        outs.append(within[k * e:(k + 1) * e, :] + off)
        off = off + jnp.sum(m[:, k * blk:(k + 1) * blk], axis=1, keepdims=True)
    return jnp.concatenate(outs, axis=1)


def _route_kernel(lg_ref, pos_ref, gate_ref, *, cap):
    b, e, n = lg_ref.shape
    lg = lg_ref[...]
    mx = lg.max(axis=1, keepdims=True)
    ex = jnp.exp(lg - mx)
    aff = (ex / ex.sum(axis=1, keepdims=True)).reshape(b * e, n)

    def search(it, cur):
        cand = cur | (jnp.int32(1) << (jnp.int32(30) - it))
        cnt = jnp.sum(jnp.where(aff >= pltpu.bitcast(cand, F32), 1.0, 0.0), axis=1, keepdims=True)
        return jnp.where(cnt >= cap, cand, cur)

    thr = pltpu.bitcast(lax.fori_loop(0, 31, search, jnp.zeros((b * e, 1), I32)), F32)
    gt = aff > thr
    eq = aff == thr
    need = cap - jnp.sum(jnp.where(gt, 1.0, 0.0), axis=1, keepdims=True)
    eq_rank = _excl_cumsum_lanes(jnp.where(eq, 1.0, 0.0))
    sel = gt | (eq & (eq_rank < need))
    slot = _excl_cumsum_lanes(jnp.where(sel, 1.0, 0.0))
    pos = jnp.where(sel, slot, -1.0).astype(I32)
    gate = jnp.where(sel, aff, 0.0)
    for i in range(b):
        pos_ref[:, i, 0, :] = pos[i * e:(i + 1) * e, :]
        gate_ref[:, i, 0, :] = gate[i * e:(i + 1) * e, :]


def _route(logits):
    b, e, n = logits.shape
    cap = EC_CAPACITY_FACTOR * n // N_EXPERTS
    return pl.pallas_call(
        functools.partial(_route_kernel, cap=cap),
        out_shape=[jax.ShapeDtypeStruct((e, b, 1, n), I32),
                   jax.ShapeDtypeStruct((e, b, 1, n), F32)],
        compiler_params=_params(0),
        name="route",
    )(logits)


def _one_hot_rows(pos_row, cap):
    n = pos_row.shape[1]
    slot = lax.broadcasted_iota(I32, (cap, n), 0)
    return slot == pos_row


def _gather_kernel(h_ref, pos_ref, gate_ref, xs_ref, gs_ref, *, cap):
    h = h_ref[0]
    group = max(1, GATHER_ROWS // cap)
    for e0 in range(0, N_EXPERTS, group):
        ohs = [_one_hot_rows(pos_ref[e, 0], cap) for e in range(e0, e0 + group)]
        xs = _dot(jnp.concatenate([jnp.where(oh, 1.0, 0.0).astype(BF16) for oh in ohs], axis=0), h).astype(BF16)
        for i, oh in enumerate(ohs):
            e = e0 + i
            xs_ref[e] = xs[i * cap:(i + 1) * cap, :]
            g = jnp.sum(jnp.where(oh, gate_ref[e, 0], 0.0), axis=1, keepdims=True)
            gs_ref[e] = jnp.broadcast_to(g, (cap, LANES))


def _gather(h, pos, gate):
    b, n, d = h.shape
    cap = EC_CAPACITY_FACTOR * n // N_EXPERTS
    return pl.pallas_call(
        functools.partial(_gather_kernel, cap=cap),
        out_shape=[jax.ShapeDtypeStruct((N_EXPERTS, b * cap, d), BF16),
                   jax.ShapeDtypeStruct((N_EXPERTS, b * cap, LANES), F32)],
        grid=(b,),
        in_specs=[
            pl.BlockSpec((1, n, d), lambda i: (i, 0, 0)),
            pl.BlockSpec((N_EXPERTS, 1, 1, n), lambda i: (0, i, 0, 0)),
            pl.BlockSpec((N_EXPERTS, 1, 1, n), lambda i: (0, i, 0, 0)),
        ],
        out_specs=[pl.BlockSpec((N_EXPERTS, cap, d), lambda i: (0, i, 0)),
                   pl.BlockSpec((N_EXPERTS, cap, LANES), lambda i: (0, i, 0))],
        compiler_params=_params(1),
        name="moe_gather",
    )(h, pos, gate)


def _ffn_kernel(xp_ref, gp_ref, xs_ref, gs_ref, wg_ref, wu_ref, wd_ref, yp_ref, ys_ref, acc_ref):
    j = pl.program_id(1)
    sp = xp_ref.shape[1]

    @pl.when(j == 0)
    def _():
        acc_ref[...] = jnp.zeros_like(acc_ref)

    xs = jnp.concatenate([xp_ref[0], xs_ref[0]], axis=0)
    a = _dot(xs, wg_ref[...].astype(BF16))
    u = _dot(xs, wu_ref[...].astype(BF16))
    hid = (a * jax.nn.sigmoid(a)) * u
    acc_ref[...] += _dot(hid.astype(BF16), wd_ref[...].astype(BF16))

    @pl.when(j == pl.num_programs(1) - 1)
    def _():
        yp_ref[0] = (acc_ref[:sp, :] * gp_ref[0][:, 0:1]).astype(BF16)
        ys_ref[0] = (acc_ref[sp:, :] * gs_ref[0][:, 0:1]).astype(BF16)


def _ffn(xp, gp, xs, gs, w_gate, w_up, w_down, layer):
    e, sp, d = xp.shape
    ss = xs.shape[1]
    s = sp + ss
    ff = w_gate.shape[-1]
    tf = FF_TILE
    return pl.pallas_call(
        _ffn_kernel,
        out_shape=[jax.ShapeDtypeStruct((e, sp, d), BF16), jax.ShapeDtypeStruct((e, ss, d), BF16)],
        grid=(e, ff // tf),
        in_specs=[
            pl.BlockSpec((1, sp, d), lambda i, j: (i, 0, 0)),
            pl.BlockSpec((1, sp, LANES), lambda i, j: (i, 0, 0)),
            pl.BlockSpec((1, ss, d), lambda i, j: (i, 0, 0)),
            pl.BlockSpec((1, ss, LANES), lambda i, j: (i, 0, 0)),
            pl.BlockSpec((None, None, d, tf), lambda i, j: (layer, i, 0, j)),
            pl.BlockSpec((None, None, d, tf), lambda i, j: (layer, i, 0, j)),
            pl.BlockSpec((None, None, tf, d), lambda i, j: (layer, i, j, 0)),
        ],
        out_specs=[pl.BlockSpec((1, sp, d), lambda i, j: (i, 0, 0)),
                   pl.BlockSpec((1, ss, d), lambda i, j: (i, 0, 0))],
        scratch_shapes=[pltpu.VMEM((s, d), F32)],
        compiler_params=_params(2),
        name="moe_ffn",
    )(xp, gp, xs, gs, w_gate, w_up, w_down)


def _combine_kernel(ye_ref, pos_ref, x_ref, mod_ref, *rest, cap, final):
    if final:
        g_ref, o_ref = rest
    else:
        (o_ref,) = rest
    group = max(1, MXU_DIM // cap)
    y = None
    for e0 in range(0, N_EXPERTS, group):
        oh = jnp.concatenate([jnp.where(_one_hot_rows(pos_ref[e, 0], cap), 1.0, 0.0).astype(BF16)
                              for e in range(e0, e0 + group)], axis=0)
        ye = jnp.concatenate([ye_ref[e] for e in range(e0, e0 + group)], axis=0)
        part = _dot_tn(oh, ye)
        y = part if y is None else y + part
    xn = x_ref[0] + mod_ref[0, 5:6, :] * y
    if final:
        ms = jnp.mean(xn * xn, axis=-1, keepdims=True)
        xn = xn * lax.rsqrt(ms + EPS) * g_ref[...]
    o_ref[0] = xn


def _combine(ye, pos, x, mod, final_norm):
    b, n, d = x.shape
    cap = EC_CAPACITY_FACTOR * n // N_EXPERTS
    tn = min(n, COMBINE_TILE)
    bm = mod.shape[0]
    mod_map = (lambda i, j: (i, 0, 0)) if bm > 1 else (lambda i, j: (0, 0, 0))
    final = final_norm is not None
    in_specs = [
        pl.BlockSpec((N_EXPERTS, cap, d), lambda i, j: (0, i, 0)),
        pl.BlockSpec((N_EXPERTS, 1, 1, tn), lambda i, j: (0, i, 0, j)),
        pl.BlockSpec((1, tn, d), lambda i, j: (i, j, 0)),
        pl.BlockSpec((1, 6, d), mod_map),
    ]
    args = [ye, pos, x, mod]
    if final:
        in_specs.append(pl.BlockSpec((1, d), lambda i, j: (0, 0)))
        args.append(final_norm)
    return pl.pallas_call(
        functools.partial(_combine_kernel, cap=cap, final=final),
        out_shape=jax.ShapeDtypeStruct((b, n, d), F32),
        grid=(b, n // tn),
        in_specs=in_specs,
        out_specs=pl.BlockSpec((1, tn, d), lambda i, j: (i, j, 0)),
        compiler_params=_params(2),
        name="moe_combine",
    )(*args)


def _rope_tables(n):
    t = np.arange(n)
    row = (t // GRID_W).astype(np.float32)
    col = (t % GRID_W).astype(np.float32)
    nf = DIFF_QK // 4
    inv = np.float32(ROPE_BASE) ** (-np.arange(nf, dtype=np.float32) / np.float32(nf))
    lane = np.arange(DIFF_WIDTH)
    pos = np.where(((lane % DIFF_QK) < DIFF_QK // 2)[None, :], row[:, None], col[:, None])
    ang = (pos * inv[lane % nf][None, :]).astype(np.float32)
    first = (lane % (2 * nf)) < nf
    cos, sin = np.cos(ang).astype(np.float32), np.sin(ang).astype(np.float32)
    return jnp.asarray(cos), jnp.asarray(np.where(first[None, :], -sin, sin))


def kernel(x_prompt, x_sample, cache_na_k, cache_na_v, cache_diff_k, cache_diff_v, state_ssm_re, state_ssm_im,
           c, c_ctx, w_ada, b_ada, norm_mix, norm_ffn, w_in, w_out, na_rpb, diff_lambda, diff_subln,
           ssm_a_re, ssm_a_im, ssm_log_dt, ssm_b_re, ssm_b_im, ssm_c_re, ssm_c_im, ssm_d, ssm_w_glu, ssm_b_glu,
           w_router, w_gate, w_up, w_down, final_norm):
    depth = w_in.shape[0]
    bp, sp, d = x_prompt.shape
    bs, ss, _ = x_sample.shape
    assert d == D_MODEL and bs + 1 <= SUBLANES
    past = cache_na_k.shape[2]

    cond = jnp.zeros((SUBLANES, d), F32).at[0].set(c_ctx).at[1:1 + bs].set(c)
    mods = _ada(cond, w_ada, b_ada).reshape(depth, SUBLANES, 6, d)

    rope_tabs = _rope_tables(ss)
    kc_a = cache_na_k.reshape(bs, depth, past, NA_WIDTH)
    vc_a = cache_na_v.reshape(bs, depth, past, NA_WIDTH)
    kc_b = cache_diff_k.reshape(bs, depth, past, DIFF_WIDTH)
    vc_b = cache_diff_v.reshape(bs, depth, past, DIFF_WIDTH)
    subln = jnp.tile(diff_subln, (1, LANES // DIFF_V)).reshape(depth, 1, LANES)
    norm_mix = norm_mix.reshape(depth, 1, d)
    norm_ffn = norm_ffn.reshape(depth, 1, d)
    ssm_d = ssm_d.reshape(depth, 1, SSM_WIDTH)
    ssm_b_glu = ssm_b_glu.reshape(depth, 1, SSM_WIDTH)
    wr_t = jnp.swapaxes(w_router, 1, 2).astype(F32)
    wr_hi = wr_t.astype(BF16)
    wr_t = jnp.concatenate([wr_hi, (wr_t - wr_hi.astype(F32)).astype(BF16)], axis=1)
    fnorm = final_norm.reshape(1, d)
    zero_state = jnp.zeros((bp, 2, SSM_N), F32)
    tabs = _s5_tables(ssm_a_re, ssm_a_im, ssm_log_dt, ssm_b_re, ssm_b_im, ssm_c_re, ssm_c_im)
    bias_tab = _na_bias_table(na_rpb)

    xp, xs = x_prompt, x_sample
    new_sre, new_sim = [], []
    caches = None
    for l in range(depth):
        lam_init = 0.8 - 0.6 * math.exp(-0.3 * l)
        mod_p = mods[l, 0:1]
        mod_s = mods[l, 1:1 + bs]
        diff_params = (diff_lambda, subln)

        qa, ka, va, qb, kb, vb, u = _proj_in(xp, mod_p, norm_mix, w_in, l, None, depth, caches)
        caches = (ka, va, kb, vb)
        o_a = _attention(qa, ka, va, l, None, l, None, lam_init)
        o_b = _attention(qb, kb, vb, l, None, l, diff_params, lam_init)
        o_c, fre, fim = _s5(u, zero_state, zero_state, tabs, ssm_d, ssm_w_glu, ssm_b_glu, l)
        xp, hp, lg_p = _proj_out(o_a, o_b, o_c, w_out, xp, mod_p, norm_ffn, wr_t, l)
        new_sre.append(fre.reshape(bp, 2, SSM_GROUPS, SSM_STATE))
        new_sim.append(fim.reshape(bp, 2, SSM_GROUPS, SSM_STATE))

        qa, ka, va, qb, kb, vb, u = _proj_in(xs, mod_s, norm_mix, w_in, l, rope_tabs, 0, None)
        o_a = _neighbourhood_attention(qa, ka, va, kc_a, vc_a, bias_tab, l)
        o_b = _attention(qb, kb, vb, 0, (kc_b, vc_b), l, diff_params, lam_init)
        h0re = state_ssm_re[:, l].reshape(bs, 2, SSM_N)
        h0im = state_ssm_im[:, l].reshape(bs, 2, SSM_N)
        o_c, _, _ = _s5(u, h0re, h0im, tabs, ssm_d, ssm_w_glu, ssm_b_glu, l)
        xs, hs, lg_s = _proj_out(o_a, o_b, o_c, w_out, xs, mod_s, norm_ffn, wr_t, l)

        pos_p, gate_p = _route(lg_p)
        pos_s, gate_s = _route(lg_s)
        xg_p, gs_p = _gather(hp, pos_p, gate_p)
        xg_s, gs_s = _gather(hs, pos_s, gate_s)
        ye_p, ye_s = _ffn(xg_p, gs_p, xg_s, gs_s, w_gate, w_up, w_down, l)
        last = l == depth - 1
        xp = _combine(ye_p, pos_p, xp, mod_p, fnorm if last else None)
        xs = _combine(ye_s, pos_s, xs, mod_s, fnorm if last else None)

    ka, va, kb, vb = caches
    return (xp, xs, ka.reshape(bp, depth, sp, NA_HEADS, HEAD_DIM), va.reshape(bp, depth, sp, NA_HEADS, HEAD_DIM),
            kb.reshape(bp, depth, sp, DIFF_HEADS, DIFF_V), vb.reshape(bp, depth, sp, DIFF_HEADS, DIFF_V),
            jnp.stack(new_sre, axis=1), jnp.stack(new_sim, axis=1))
```

```python
import functools
import math

import jax
import jax.numpy as jnp
import numpy as np
from jax import lax
from jax.experimental import pallas as pl
from jax.experimental.pallas import tpu as pltpu

F32 = jnp.float32
BF16 = jnp.bfloat16
I32 = jnp.int32

D_MODEL = 1024
GRID_W = 64
HEAD_DIM = 64
NA_HEADS = 8
NA_WIDTH = NA_HEADS * HEAD_DIM
WIN_R = 8
WIN_C = 16
DIFF_HEADS = 4
DIFF_QK = 32
DIFF_V = 64
DIFF_WIDTH = DIFF_HEADS * DIFF_V
SSM_GROUPS = 16
SSM_GROUP_CH = 16
SSM_WIDTH = SSM_GROUPS * SSM_GROUP_CH
SSM_STATE = 64
SSM_N = SSM_GROUPS * SSM_STATE
IN_WIDTH = 3 * NA_WIDTH + 3 * DIFF_WIDTH + SSM_WIDTH
N_EXPERTS = 16
EXPERT_FF = 2048
EC_CAPACITY_FACTOR = 2
ROPE_BASE = 10000.0
EPS = 1e-6
NEG_INF = -1e30
LOG2E = math.log2(math.e)

LANES = 128
SUBLANES = 8
MXU_DIM = 256
VMEM_LIMIT_CAP = 60000 * 1024

ROW_TILE = 256
PROJ_IN_CHUNKS = 4
PROJ_OUT_CHUNKS = 2
ATTN_Q_TILE = 512
ATTN_REQUESTS_PER_STEP = 2
NA_QROWS = 4
NA_KROWS = 12
SCAN_TILE = 256
FF_TILE = 1024
COMBINE_TILE = 512
GATHER_ROWS = 512
DEN_IN_MATMUL_MIN_KEYS = 1024


def _params(ndims):
    return pltpu.CompilerParams(
        dimension_semantics=("arbitrary",) * ndims,
        vmem_limit_bytes=VMEM_LIMIT_CAP,
    )


def _step_tile(b, n, n_mod, chunks):
    tm = min(n, ROW_TILE * chunks)
    bb = max(1, ROW_TILE * chunks // tm)
    if n_mod > 1 or b % bb:
        bb = 1
    return bb, tm


def _dot(a, b):
    return jnp.dot(a, b, preferred_element_type=F32)


def _dot_nt(a, b):
    return lax.dot_general(a, b, (((1,), (1,)), ((), ())), preferred_element_type=F32)


def _dot_tn(a, b):
    return lax.dot_general(a, b, (((0,), (0,)), ((), ())), preferred_element_type=F32)


def _cast_rows(dst_ref, src_ref, rows):
    n = src_ref.shape[0]
    for r in range(0, n, rows):
        dst_ref[r:r + rows, :] = src_ref[r:r + rows, :].astype(BF16)


def _modulated_norm(x, g, shift, scale):
    ms = jnp.mean(x * x, axis=-1, keepdims=True)
    return (x * lax.rsqrt(ms + EPS) * g) * (1.0 + scale) + shift


def _ada_kernel(c_ref, w_ref, b_ref, o_ref):
    c = c_ref[...]
    s = c * jax.nn.sigmoid(c)
    o_ref[0] = _dot(s.astype(BF16), w_ref[0].astype(BF16)) + b_ref[0]


def _ada(cond, w_ada, b_ada):
    depth = w_ada.shape[0]
    tn = 1536
    nt = 6 * D_MODEL // tn
    return pl.pallas_call(
        _ada_kernel,
        out_shape=jax.ShapeDtypeStruct((depth, SUBLANES, 6 * D_MODEL), F32),
        grid=(depth, nt),
        in_specs=[
            pl.BlockSpec((SUBLANES, D_MODEL), lambda l, j: (0, 0)),
            pl.BlockSpec((1, D_MODEL, tn), lambda l, j: (l, 0, j)),
            pl.BlockSpec((1, 1, tn), lambda l, j: (l, 0, j)),
        ],
        out_specs=pl.BlockSpec((1, SUBLANES, tn), lambda l, j: (l, 0, j)),
        compiler_params=_params(2),
        name="ada",
    )(cond, w_ada, b_ada.reshape(depth, 1, 6 * D_MODEL))


def _rope_apply(x, cos, sin_signed):
    lane = lax.broadcasted_iota(I32, (1, LANES), 1)
    first = (lane % 16) < 8
    outs = []
    for t in range(x.shape[1] // LANES):
        xt = x[:, t * LANES:(t + 1) * LANES]
        partner = jnp.where(first, pltpu.roll(xt, LANES - 8, axis=1), pltpu.roll(xt, 8, axis=1))
        outs.append(xt * cos[:, t * LANES:(t + 1) * LANES] + partner * sin_signed[:, t * LANES:(t + 1) * LANES])
    return jnp.concatenate(outs, axis=1)


def _proj_in_kernel(*refs, rope, n_alias):
    x_ref, mod_ref, g_ref, w_ref = refs[:4]
    pos = 4
    if rope:
        cos_ref, sin_ref = refs[pos:pos + 2]
        pos += 2
    pos += n_alias
    qa_ref, ka_ref, va_ref, qb_ref, kb_ref, vb_ref, u_ref, wbf_ref = refs[pos:pos + 8]

    @pl.when((pl.program_id(0) == 0) & (pl.program_id(1) == 0))
    def _():
        _cast_rows(wbf_ref, w_ref, 128)

    i1, i2, i3 = NA_WIDTH, 2 * NA_WIDTH, 3 * NA_WIDTH
    i4, i5, i6 = i3 + DIFF_WIDTH, i3 + 2 * DIFF_WIDTH, i3 + 3 * DIFF_WIDTH
    for bi in range(x_ref.shape[0]):
        for r0 in range(0, x_ref.shape[1], ROW_TILE):
            rows = slice(r0, r0 + ROW_TILE)

            def put(ref, val):
                for s in range(ref.shape[1]):
                    ref[bi, s, rows, :] = val.astype(ref.dtype)

            h = _modulated_norm(x_ref[bi, rows, :], g_ref[...], mod_ref[0, 0:1, :], mod_ref[0, 1:2, :])
            z = _dot(h.astype(BF16), wbf_ref[...])
            qa_ref[bi, rows, :] = (z[:, :i1] * (HEAD_DIM ** -0.5 * LOG2E)).astype(qa_ref.dtype)
            put(ka_ref, z[:, i1:i2])
            put(va_ref, z[:, i2:i3])
            qb = z[:, i3:i4]
            kb = z[:, i4:i5]
            if rope:
                qb = _rope_apply(qb, cos_ref[rows, :], sin_ref[rows, :])
                kb = _rope_apply(kb, cos_ref[rows, :], sin_ref[rows, :])
            qb_ref[bi, rows, :] = (qb * (DIFF_QK ** -0.5 * LOG2E)).astype(qb_ref.dtype)
            put(kb_ref, kb)
            put(vb_ref, z[:, i5:i6])
            for hf in range(SSM_WIDTH // LANES):
                u_ref[bi, hf, rows, :] = z[:, i6 + hf * LANES:i6 + (hf + 1) * LANES]


def _proj_in(x, mod, g_norm, w_in, layer, rope_tabs, cache_slots, caches):
    b, n, d = x.shape
    bb, tm = _step_tile(b, n, mod.shape[0], PROJ_IN_CHUNKS)
    rope = rope_tabs is not None
    mod_map = (lambda i, j: (i, 0, 0)) if mod.shape[0] > 1 else (lambda i, j: (0, 0, 0))
    in_specs = [
        pl.BlockSpec((bb, tm, d), lambda i, j: (i, j, 0)),
        pl.BlockSpec((1, 6, d), mod_map),
        pl.BlockSpec((None, 1, d), lambda i, j: (layer, 0, 0)),
        pl.BlockSpec((None, d, IN_WIDTH), lambda i, j: (layer, 0, 0), pipeline_mode=pl.Buffered(1)),
    ]
    args = [x, mod, g_norm, w_in]
    if rope:
        in_specs += [pl.BlockSpec((tm, DIFF_WIDTH), lambda i, j: (j, 0))] * 2
        args += list(rope_tabs)
    aliases = {}
    kv_out = (1, 2, 4, 5)
    if caches is not None:
        for c, o in zip(caches, kv_out):
            aliases[len(args)] = o
            in_specs.append(pl.BlockSpec(memory_space=pl.ANY))
            args.append(c)
    widths = (NA_WIDTH, NA_WIDTH, NA_WIDTH, DIFF_WIDTH, DIFF_WIDTH, DIFF_WIDTH)
    nslab = SSM_WIDTH // LANES
    out_shape, out_specs = [], []
    for o, w in enumerate(widths):
        if o not in kv_out:
            out_shape.append(jax.ShapeDtypeStruct((b, n, w), BF16))
            out_specs.append(pl.BlockSpec((bb, tm, w), lambda i, j: (i, j, 0)))
        elif cache_slots == 0:
            out_shape.append(jax.ShapeDtypeStruct((b, 1, n, w), BF16))
            out_specs.append(pl.BlockSpec((bb, 1, tm, w), lambda i, j: (i, 0, j, 0)))
        elif caches is None:
            out_shape.append(jax.ShapeDtypeStruct((b, cache_slots, n, w), F32))
            out_specs.append(pl.BlockSpec((bb, cache_slots, tm, w), lambda i, j: (i, 0, j, 0)))
        else:
            out_shape.append(jax.ShapeDtypeStruct((b, cache_slots, n, w), F32))
            out_specs.append(pl.BlockSpec((bb, 1, tm, w), lambda i, j: (i, layer, j, 0)))
    out_shape.append(jax.ShapeDtypeStruct((b, nslab, n, LANES), F32))
    out_specs.append(pl.BlockSpec((bb, nslab, tm, LANES), lambda i, j: (i, 0, j, 0)))
    return pl.pallas_call(
        functools.partial(_proj_in_kernel, rope=rope, n_alias=len(aliases)),
        out_shape=out_shape,
        grid=(b // bb, n // tm),
        in_specs=in_specs,
        out_specs=out_specs,
        scratch_shapes=[pltpu.VMEM((d, IN_WIDTH), BF16)],
        input_output_aliases=aliases,
        compiler_params=_params(2),
        name="proj_in",
    )(*args)


def _masked_query(qp, mask):
    return jnp.where(mask, qp, jnp.zeros_like(qp))


def _with_ones(v):
    return jnp.concatenate([v, jnp.ones_like(v)], axis=1)


def _softmax_av_wide(qm, ks, vs, biases):
    ss = []
    for k, bias in zip(ks, biases):
        s = _dot_nt(qm, k)
        if bias is not None:
            s = bias(s)
        ss.append(s)
    m = ss[0].max(axis=-1, keepdims=True)
    for s in ss[1:]:
        m = jnp.maximum(m, s.max(axis=-1, keepdims=True))
    acc = None
    for s, v in zip(ss, vs):
        pv = _dot(jnp.exp2(s - m).astype(BF16), v)
        acc = pv if acc is None else acc + pv
    return acc[:, :LANES] / acc[:, LANES:]


def _softmax_av(qm, ks, vs, biases):
    ss = []
    for k, bias in zip(ks, biases):
        s = _dot_nt(qm, k)
        if bias is not None:
            s = bias(s)
        ss.append(s)
    m = ss[0].max(axis=-1, keepdims=True)
    for s in ss[1:]:
        m = jnp.maximum(m, s.max(axis=-1, keepdims=True))
    acc = None
    den = None
    for s, v in zip(ss, vs):
        p = jnp.exp2(s - m)
        psum = p.sum(axis=-1, keepdims=True)
        den = psum if den is None else den + psum
        pv = _dot(p.astype(BF16), v)
        acc = pv if acc is None else acc + pv
    return acc / den


def _diff_lambda(lam_ref, lam_init):
    lp = lam_ref[...]
    a = jnp.sum(lp[0:1] * lp[1:2], axis=-1, keepdims=True)
    b = jnp.sum(lp[2:3] * lp[3:4], axis=-1, keepdims=True)
    return jnp.exp(a) - jnp.exp(b) + lam_init


def _attn_kernel(*refs, has_ctx, diff, lam_init):
    refs = list(refs)
    q_ref, k_ref, v_ref = refs[:3]
    pos = 3
    if has_ctx:
        kc_ref, vc_ref = refs[pos:pos + 2]
        pos += 2
    if diff:
        lam_ref, subln_ref = refs[pos:pos + 2]
        pos += 2
    o_ref = refs[pos]
    width = q_ref.shape[2]
    lane = lax.broadcasted_iota(I32, (1, LANES), 1)
    if diff:
        lam = _diff_lambda(lam_ref, lam_init)
    for bi, p in [(bi, p) for bi in range(q_ref.shape[0]) for p in range(width // LANES)]:
        sl = slice(p * LANES, (p + 1) * LANES)
        qp = q_ref[bi, :, sl]
        ks = [k_ref[bi, :, sl].astype(BF16)]
        vs = [v_ref[bi, :, sl].astype(BF16)]
        if has_ctx:
            ks.append(kc_ref[:, sl].astype(BF16))
            vs.append(vc_ref[:, sl].astype(BF16))
        nob = [None] * len(ks)
        tq = qp.shape[0]
        wide = sum(k.shape[0] for k in ks) >= DEN_IN_MATMUL_MIN_KEYS
        if wide:
            vs = [_with_ones(v) for v in vs]
        pair = None
        for sub in range(LANES // HEAD_DIM):
            head_mask = (lane // HEAD_DIM) == sub
            if diff:
                qmasks = [(lane // DIFF_QK) == (2 * sub + half) for half in range(2)]
                av = _softmax_av_wide if wide else _softmax_av
                os_ = [av(_masked_query(qp, mk), ks, vs, nob) for mk in qmasks]
                o = jnp.where(head_mask, os_[0] - lam * os_[1], 0.0)
                ms = jnp.sum(o * o, axis=-1, keepdims=True) * (1.0 / DIFF_V)
                o = (o * lax.rsqrt(ms + EPS) * subln_ref[...]) * (1.0 - lam_init)
            else:
                qm = _masked_query(qp, head_mask)
                o = _softmax_av_wide(qm, ks, vs, nob) if wide else _softmax_av(qm, ks, vs, nob)
                o = jnp.where(head_mask, o, 0.0)
            pair = o if pair is None else pair + o
        o_ref[bi, :, sl] = pair.astype(o_ref.dtype)


def _attention(q, k, v, kv_slot, ctx_kv, layer, diff_params, lam_init):
    b, nq, w = q.shape
    nk = k.shape[2]
    tq = min(nq, ATTN_Q_TILE)
    has_ctx = ctx_kv is not None
    diff = diff_params is not None
    bb = ATTN_REQUESTS_PER_STEP if (nq == tq and not has_ctx and b % ATTN_REQUESTS_PER_STEP == 0) else 1
    in_specs = [
        pl.BlockSpec((bb, tq, w), lambda i, j: (i, j, 0)),
        pl.BlockSpec((bb, None, nk, w), lambda i, j: (i, kv_slot, 0, 0)),
        pl.BlockSpec((bb, None, nk, w), lambda i, j: (i, kv_slot, 0, 0)),
    ]
    args = [q, k, v]
    nc = 0
    if has_ctx:
        nc = ctx_kv[0].shape[2]
        in_specs += [pl.BlockSpec((None, None, nc, w), lambda i, j: (i, layer, 0, 0))] * 2
        args += list(ctx_kv)
    if diff:
        lam_p, subln = diff_params
        in_specs += [
            pl.BlockSpec((None, 4, DIFF_QK), lambda i, j: (layer, 0, 0)),
            pl.BlockSpec((None, 1, LANES), lambda i, j: (layer, 0, 0)),
        ]
        args += [lam_p, subln]
    return pl.pallas_call(
        functools.partial(_attn_kernel, has_ctx=has_ctx, diff=diff, lam_init=lam_init),
        out_shape=jax.ShapeDtypeStruct((b, nq, w), BF16),
        grid=(b // bb, nq // tq),
        in_specs=in_specs,
        out_specs=pl.BlockSpec((bb, tq, w), lambda i, j: (i, j, 0)),
        compiler_params=_params(2),
        name="attn_diff" if diff else "attn",
    )(*args)


def _na_kernel(q_ref, k_ref, v_ref, kc_ref, vc_ref, bias_ref, o_ref, *, rows):
    i = pl.program_id(1)
    tq = NA_QROWS * GRID_W
    nkw = NA_KROWS * GRID_W
    ustart = jnp.clip(NA_QROWS * i - WIN_R // 2, 0, rows - NA_KROWS)
    koff = pl.multiple_of(ustart * GRID_W, GRID_W)
    delta = ustart - NA_QROWS * i
    qrow = NA_QROWS * i + lax.broadcasted_iota(I32, (tq, nkw), 0) // GRID_W
    krow = ustart + lax.broadcasted_iota(I32, (tq, nkw), 1) // GRID_W
    wstart = jnp.clip(qrow - WIN_R // 2, 0, rows - WIN_R)
    row_ok = (krow >= wstart) & (krow < wstart + WIN_R)
    nsub = LANES // HEAD_DIM
    lane = lax.broadcasted_iota(I32, (1, LANES), 1)
    for p in range(NA_WIDTH // LANES):
        sl = slice(p * LANES, (p + 1) * LANES)
        qp = q_ref[0, :, sl]
        ks = [k_ref[0, pl.ds(koff, nkw), sl].astype(BF16), kc_ref[:, sl].astype(BF16)]
        vs = [_with_ones(v_ref[0, pl.ds(koff, nkw), sl].astype(BF16)), _with_ones(vc_ref[:, sl].astype(BF16))]
        outs = []
        for sub in range(nsub):
            def win_bias(s, h=p * nsub + sub):
                rows_ = []
                for ri in range(NA_QROWS):
                    tiles = []
                    for pj in range(NA_KROWS // 2):
                        dr = delta + 2 * pj - ri
                        idx = jnp.clip(dr, -WIN_R, WIN_R - 1) + WIN_R
                        tiles.append(bias_ref[h, idx])
                    rows_.append(jnp.concatenate(tiles, axis=1))
                bias = jnp.concatenate(rows_, axis=0)
                return jnp.where(row_ok, s + bias, NEG_INF)

            outs.append(_softmax_av_wide(_masked_query(qp, (lane // HEAD_DIM) == sub), ks, vs, [win_bias, None]))
        pair = jnp.where((lane // HEAD_DIM) == 0, outs[0], outs[1])
        o_ref[0, :, sl] = pair.astype(o_ref.dtype)


def _na_table_kernel(v_ref, o_ref):
    nd = v_ref.shape[0]
    q = lax.broadcasted_iota(I32, (GRID_W, 2 * GRID_W), 0)
    k = lax.broadcasted_iota(I32, (GRID_W, 2 * GRID_W), 1)
    kk = k % GRID_W
    cstart = jnp.clip(q - WIN_C // 2, 0, GRID_W - WIN_C)
    col_ok = (kk >= cstart) & (kk < cstart + WIN_C)
    left = k < GRID_W
    tiles = []
    for dr in range(nd):
        x = jnp.broadcast_to(v_ref[dr:dr + 1, :], (GRID_W, 2 * GRID_W))
        tiles.append((pltpu.roll(x, GRID_W + 1, axis=1, stride=1, stride_axis=0),
                      pltpu.roll(x, 1, axis=1, stride=1, stride_axis=0)))
    zero = jnp.zeros((GRID_W, 2 * GRID_W), F32)
    for j in range(nd + 1):
        lt = tiles[j - 1][0] if j >= 1 else zero
        rt = tiles[j][1] if j < nd else zero
        o_ref[j] = jnp.where(col_ok, jnp.where(left, lt, rt) * LOG2E, NEG_INF)


def _na_bias_table(rpb):
    depth, h, nd, nc = rpb.shape
    assert 2 * GRID_W == LANES and nd == 2 * WIN_R - 1 and nc == 2 * WIN_C - 1
    rpb = rpb.astype(F32)
    lo = GRID_W - WIN_C
    v = jnp.concatenate([jnp.broadcast_to(rpb[..., :1], rpb.shape[:3] + (lo,)), rpb,
                         jnp.broadcast_to(rpb[..., -1:], rpb.shape[:3] + (2 * GRID_W - lo - nc,))], axis=-1)
    return pl.pallas_call(
        _na_table_kernel,
        out_shape=jax.ShapeDtypeStruct((depth, h, nd + 1, GRID_W, 2 * GRID_W), F32),
        grid=(depth, h),
        in_specs=[pl.BlockSpec((None, None, nd, 2 * GRID_W), lambda l, i: (l, i, 0, 0))],
        out_specs=pl.BlockSpec((None, None, nd + 1, GRID_W, 2 * GRID_W), lambda l, i: (l, i, 0, 0, 0)),
        compiler_params=_params(2),
        name="na_table",
    )(v)


def _neighbourhood_attention(q, k, v, kc, vc, bias_tab, layer):
    b, n, w = q.shape
    rows = n // GRID_W
    tq = NA_QROWS * GRID_W
    nc = kc.shape[2]
    nkw = NA_KROWS * GRID_W
    return pl.pallas_call(
        functools.partial(_na_kernel, rows=rows),
        out_shape=jax.ShapeDtypeStruct((b, n, w), BF16),
        grid=(b, n // tq),
        in_specs=[
            pl.BlockSpec((1, tq, w), lambda i, j: (i, j, 0)),
            pl.BlockSpec((1, None, n, w), lambda i, j: (i, 0, 0, 0)),
            pl.BlockSpec((1, None, n, w), lambda i, j: (i, 0, 0, 0)),
            pl.BlockSpec((None, None, nc, w), lambda i, j: (i, layer, 0, 0)),
            pl.BlockSpec((None, None, nc, w), lambda i, j: (i, layer, 0, 0)),
            pl.BlockSpec((None,) + bias_tab.shape[1:], lambda i, j: (layer, 0, 0, 0, 0)),
        ],
        out_specs=pl.BlockSpec((1, tq, w), lambda i, j: (i, j, 0)),
        compiler_params=_params(2),
        name="na_attn",
    )(q, k, v, kc, vc, bias_tab)


def _sublane_scan(er, ei, tr, ti, pw_ref, base, lanes, reverse):
    row = lax.broadcasted_iota(I32, (SUBLANES, LANES), 0)
    if reverse:
        first = row == SUBLANES - 1
        xr = jnp.where(first, tr, pltpu.roll(er, SUBLANES - 1, axis=0))
        xi = jnp.where(first, ti, pltpu.roll(ei, SUBLANES - 1, axis=0))
    else:
        first = row == 0
        xr = jnp.where(first, tr, pltpu.roll(er, 1, axis=0))
        xi = jnp.where(first, ti, pltpu.roll(ei, 1, axis=0))
    for k, s in enumerate((1, 2, 4)):
        ar = pw_ref[base + 2 * k:base + 2 * k + 1, lanes]
        ai = pw_ref[base + 2 * k + 1:base + 2 * k + 2, lanes]
        if reverse:
            keep = row < SUBLANES - s
            sr = jnp.where(keep, pltpu.roll(xr, SUBLANES - s, axis=0), 0.0)
            si = jnp.where(keep, pltpu.roll(xi, SUBLANES - s, axis=0), 0.0)
        else:
            keep = row >= s
            sr = jnp.where(keep, pltpu.roll(xr, s, axis=0), 0.0)
            si = jnp.where(keep, pltpu.roll(xi, s, axis=0), 0.0)
        xr, xi = xr + (ar * sr - ai * si), xi + (ar * si + ai * sr)
    return xr, xi


def _s5_kernel(u_ref, h0re_ref, h0im_ref, bt_ref, ct_ref, a1_ref, ach_ref, pfix_ref, d_ref, wglu_ref, bglu_ref,
               o_ref, fre_ref, fim_ref, up_ref, yp_ref, buf_ref, car_ref, bmat_ref, cmat_ref):
    seq = u_ref.shape[2]
    tt = SCAN_TILE
    ch = tt // SUBLANES
    ntile = seq // tt
    nlt = SSM_N // LANES
    nhalf = SSM_WIDTH // LANES

    @pl.when(pl.program_id(0) == 0)
    def _():
        g, p, hc = SSM_GROUPS, SSM_STATE, SSM_GROUP_CH
        btb = bt_ref[...].astype(BF16)
        ctb = ct_ref[...].astype(BF16)
        for dc in range(4):
            r = lax.broadcasted_iota(I32, (4 * p, g * p), 0)
            c = lax.broadcasted_iota(I32, (4 * p, g * p), 1)
            sel = jnp.where(r == dc * p + c % p, 1.0, 0.0).astype(BF16)
            r = lax.broadcasted_iota(I32, (g * hc, g * p), 0)
            c = lax.broadcasted_iota(I32, (g * hc, g * p), 1)
            blk = jnp.where(r // hc == c // p, _dot(btb, sel), 0.0)
            bmat_ref[:, dc * g * p:(dc + 1) * g * p] = blk.astype(BF16)
            r = lax.broadcasted_iota(I32, (g * p, 4 * p), 0)
            c = lax.broadcasted_iota(I32, (g * p, 4 * p), 1)
            sel = jnp.where(c == dc * p + r % p, 1.0, 0.0).astype(BF16)
            r = lax.broadcasted_iota(I32, (g * p, g * hc), 0)
            c = lax.broadcasted_iota(I32, (g * p, g * hc), 1)
            blk = jnp.where(r // p == c // hc, _dot(sel, ctb), 0.0)
            cmat_ref[dc * g * p:(dc + 1) * g * p, :] = blk.astype(BF16)

    def permute_in(t, _):
        t0 = pl.multiple_of(t * tt, tt)
        for j in range(0, ch, 2):
            for hf in range(nhalf):
                lanes = slice(hf * LANES, (hf + 1) * LANES)
                rows = jnp.concatenate([u_ref[0, hf, pl.ds(t0 + j + jj, SUBLANES, stride=ch), :] for jj in range(2)],
                                       axis=0)
                r0 = pl.multiple_of(t0 + j * SUBLANES, 2 * SUBLANES)
                up_ref[pl.ds(r0, 2 * SUBLANES), lanes] = rows.astype(BF16)
                yp_ref[pl.ds(r0, 2 * SUBLANES), lanes] = rows * d_ref[:, lanes]
        return 0

    lax.fori_loop(0, ntile, permute_in, 0)

    for d in range(2):
        for lt in range(nlt):
            lanes = slice(lt * LANES, (lt + 1) * LANES)
            car_ref[d, 0, :, lanes] = jnp.broadcast_to(h0re_ref[0, d:d + 1, lanes], (SUBLANES, LANES))
            car_ref[d, 1, :, lanes] = jnp.broadcast_to(h0im_ref[0, d:d + 1, lanes], (SUBLANES, LANES))

    def tile(t, _):
        starts = []
        for d in range(2):
            tix = (ntile - 1 - t) if d == 1 else t
            t0 = pl.multiple_of(tix * tt, tt)
            starts.append(t0)
            buf_ref[d] = _dot(up_ref[pl.ds(t0, tt), :], bmat_ref[:, d * 2 * SSM_N:(d + 1) * 2 * SSM_N])
        for d in range(2):
            reverse = d == 1
            order = range(ch - 1, -1, -1) if reverse else range(ch)
            for lt in range(nlt):
                lre = slice(lt * LANES, (lt + 1) * LANES)
                lim = slice(SSM_N + lt * LANES, SSM_N + (lt + 1) * LANES)
                ar = a1_ref[2 * d:2 * d + 1, lre]
                ai = a1_ref[2 * d + 1:2 * d + 2, lre]
                cr = jnp.zeros((SUBLANES, LANES), F32)
                ci = jnp.zeros((SUBLANES, LANES), F32)
                for j in order:
                    rows = slice(j * SUBLANES, (j + 1) * SUBLANES)
                    cr, ci = (ar * cr - ai * ci) + buf_ref[d, rows, lre], (ar * ci + ai * cr) + buf_ref[d, rows, lim]
                    buf_ref[d, rows, lre] = cr
                    buf_ref[d, rows, lim] = ci
                tr, ti = car_ref[d, 0, :, lre], car_ref[d, 1, :, lre]
                gr, gi = _sublane_scan(cr, ci, tr, ti, ach_ref, 6 * d, lre, reverse)
                a_r, a_i = ach_ref[6 * d:6 * d + 1, lre], ach_ref[6 * d + 1:6 * d + 2, lre]
                nr, ni = (a_r * gr - a_i * gi) + cr, (a_r * gi + a_i * gr) + ci
                edge = 0 if reverse else SUBLANES - 1
                car_ref[d, 0, :, lre] = jnp.broadcast_to(nr[edge:edge + 1, :], (SUBLANES, LANES))
                car_ref[d, 1, :, lre] = jnp.broadcast_to(ni[edge:edge + 1, :], (SUBLANES, LANES))
                for j in range(ch):
                    rows = slice(j * SUBLANES, (j + 1) * SUBLANES)
                    pr = pfix_ref[2 * d, j:j + 1, lre]
                    pi = pfix_ref[2 * d + 1, j:j + 1, lre]
                    buf_ref[d, rows, lre] += pr * gr - pi * gi
                    buf_ref[d, rows, lim] += pr * gi + pi * gr
        for d in range(2):
            yp_ref[pl.ds(starts[d], tt), :] += _dot(buf_ref[d].astype(BF16),
                                                    cmat_ref[d * 2 * SSM_N:(d + 1) * 2 * SSM_N, :])
        return 0

    lax.fori_loop(0, ntile, tile, 0)

    for d in range(2):
        fre_ref[0, d:d + 1, :] = car_ref[d, 0, 0:1, :]
        fim_ref[0, d:d + 1, :] = car_ref[d, 1, 0:1, :]

    wglu = wglu_ref[...].astype(BF16)

    def glu(t, _):
        t0 = pl.multiple_of(t * tt, tt)
        g = jax.nn.gelu(yp_ref[pl.ds(t0, tt), :])
        out = g * jax.nn.sigmoid(_dot(g.astype(BF16), wglu) + bglu_ref[...])
        for j in range(ch):
            for hf in range(nhalf):
                o_ref[0, hf, pl.ds(t0 + j, SUBLANES, stride=ch), :] = out[j * SUBLANES:(j + 1) * SUBLANES,
                                                                          hf * LANES:(hf + 1) * LANES]
        return 0

    lax.fori_loop(0, ntile, glu, 0)


def _cmul(ar, ai, br, bi):
    return ar * br - ai * bi, ar * bi + ai * br


def _s5_tables(a_re, a_im, log_dt, b_re, b_im, c_re, c_im):
    depth = a_re.shape[0]
    g, p, hc = SSM_GROUPS, SSM_STATE, SSM_GROUP_CH
    ch = SCAN_TILE // SUBLANES
    lr, li = a_re.astype(F32), a_im.astype(F32)
    dt = jnp.exp(log_dt.astype(F32))[..., None]
    mag = jnp.exp(lr * dt)
    ar, ai = mag * jnp.cos(li * dt), mag * jnp.sin(li * dt)
    den = lr * lr + li * li
    qr = ((ar - 1.0) * lr + ai * li) / den
    qi = (ai * lr - (ar - 1.0) * li) / den
    br, bi = _cmul(qr[..., None], qi[..., None], b_re.astype(F32), b_im.astype(F32))
    bparts = jnp.stack([br, bi], axis=2)
    bmat = bparts.transpose(0, 3, 5, 1, 2, 4).reshape(depth, g * hc, 4 * p)
    cparts = jnp.stack([c_re.astype(F32), -c_im.astype(F32)], axis=2)
    cmat = cparts.transpose(0, 1, 2, 5, 3, 4).reshape(depth, 4 * p, g * hc)

    def power(k):
        kk = k[None, None, :, None, None]
        m = jnp.exp(kk * (lr * dt)[:, :, None])
        th = kk * (li * dt)[:, :, None]
        return (m * jnp.cos(th)).reshape(depth, 2, -1, g * p), (m * jnp.sin(th)).reshape(depth, 2, -1, g * p)

    a1 = jnp.stack([ar.reshape(depth, 2, g * p), ai.reshape(depth, 2, g * p)], axis=2).reshape(depth, 4, g * p)
    cr_, ci_ = power(jnp.asarray([ch, 2 * ch, 4 * ch], F32))
    ach = jnp.stack([cr_, ci_], axis=3).reshape(depth, 12, g * p)
    fr, fi = power(jnp.arange(1, ch + 1, dtype=F32))
    fr = jnp.stack([fr[:, 0], fr[:, 1, ::-1]], axis=1)
    fi = jnp.stack([fi[:, 0], fi[:, 1, ::-1]], axis=1)
    pfix = jnp.stack([fr, fi], axis=2).reshape(depth, 4, ch, g * p)
    return bmat, cmat, a1, ach, pfix


def _s5(u, h0re, h0im, tabs, d_skip, w_glu, b_glu, layer):
    b, nh, seq, _ = u.shape
    w = nh * LANES
    lay = lambda a: pl.BlockSpec((None,) + a.shape[1:], lambda i: (layer,) + (0,) * (a.ndim - 1))
    return pl.pallas_call(
        _s5_kernel,
        out_shape=[jax.ShapeDtypeStruct((b, nh, seq, LANES), F32),
                   jax.ShapeDtypeStruct((b, 2, SSM_N), F32),
                   jax.ShapeDtypeStruct((b, 2, SSM_N), F32)],
        grid=(b,),
        in_specs=[
            pl.BlockSpec((1, nh, seq, LANES), lambda i: (i, 0, 0, 0)),
            pl.BlockSpec((1, 2, SSM_N), lambda i: (i, 0, 0)),
            pl.BlockSpec((1, 2, SSM_N), lambda i: (i, 0, 0)),
            lay(tabs[0]), lay(tabs[1]), lay(tabs[2]), lay(tabs[3]), lay(tabs[4]),
            pl.BlockSpec((None, 1, w), lambda i: (layer, 0, 0)),
            pl.BlockSpec((None, w, w), lambda i: (layer, 0, 0)),
            pl.BlockSpec((None, 1, w), lambda i: (layer, 0, 0)),
        ],
        out_specs=[pl.BlockSpec((1, nh, seq, LANES), lambda i: (i, 0, 0, 0)),
                   pl.BlockSpec((1, 2, SSM_N), lambda i: (i, 0, 0)),
                   pl.BlockSpec((1, 2, SSM_N), lambda i: (i, 0, 0))],
        scratch_shapes=[pltpu.VMEM((seq, w), BF16),
                        pltpu.VMEM((seq, w), F32),
                        pltpu.VMEM((2, SCAN_TILE, 2 * SSM_N), F32),
                        pltpu.VMEM((2, 2, SUBLANES, SSM_N), F32),
                        pltpu.VMEM((w, 4 * SSM_N), BF16),
                        pltpu.VMEM((4 * SSM_N, w), BF16)],
        compiler_params=_params(1),
        name="s5",
    )(u, h0re, h0im, *tabs, d_skip, w_glu, b_glu)


def _proj_out_kernel(oa_ref, ob_ref, oc_ref, w_ref, x_ref, mod_ref, g_ref, wr_ref, o_ref, h_ref, lg_ref, wbf_ref):
    @pl.when((pl.program_id(0) == 0) & (pl.program_id(1) == 0))
    def _():
        _cast_rows(wbf_ref, w_ref, 128)

    i1, i2 = NA_WIDTH, NA_WIDTH + DIFF_WIDTH
    for bi in range(x_ref.shape[0]):
        for r0 in range(0, x_ref.shape[1], ROW_TILE):
            rows = slice(r0, r0 + ROW_TILE)
            y = _dot(oa_ref[bi, rows, :].astype(BF16), wbf_ref[:i1, :])
            y += _dot(ob_ref[bi, rows, :].astype(BF16), wbf_ref[i1:i2, :])
            for hf in range(SSM_WIDTH // LANES):
                y += _dot(oc_ref[bi, hf, rows, :].astype(BF16), wbf_ref[i2 + hf * LANES:i2 + (hf + 1) * LANES, :])
            xn = x_ref[bi, rows, :] + mod_ref[0, 2:3, :] * y
            o_ref[bi, rows, :] = xn
            h = _modulated_norm(xn, g_ref[...], mod_ref[0, 3:4, :], mod_ref[0, 4:5, :])
            h_hi = h.astype(BF16)
            h_ref[bi, rows, :] = h_hi
            h_lo = (h - h_hi.astype(F32)).astype(BF16)
            t1 = _dot_nt(wr_ref[...], h_hi)
            t2 = _dot_nt(wr_ref[:N_EXPERTS, :], h_lo)
            lg_ref[bi, :, rows] = t1[:N_EXPERTS, :] + t1[N_EXPERTS:, :] + t2


def _proj_out(oa, ob, oc, w_out, x, mod, g_ffn, wr_t, layer):
    b, n, d = x.shape
    bb, tm = _step_tile(b, n, mod.shape[0], PROJ_OUT_CHUNKS)
    mod_map = (lambda i, j: (i, 0, 0)) if mod.shape[0] > 1 else (lambda i, j: (0, 0, 0))
    kw = w_out.shape[1]
    return pl.pallas_call(
        _proj_out_kernel,
        out_shape=[jax.ShapeDtypeStruct((b, n, d), F32),
                   jax.ShapeDtypeStruct((b, n, d), BF16),
                   jax.ShapeDtypeStruct((b, N_EXPERTS, n), F32)],
        grid=(b // bb, n // tm),
        in_specs=[
            pl.BlockSpec((bb, tm, NA_WIDTH), lambda i, j: (i, j, 0)),
            pl.BlockSpec((bb, tm, DIFF_WIDTH), lambda i, j: (i, j, 0)),
            pl.BlockSpec((bb, SSM_WIDTH // LANES, tm, LANES), lambda i, j: (i, 0, j, 0)),
            pl.BlockSpec((None, kw, d), lambda i, j: (layer, 0, 0), pipeline_mode=pl.Buffered(1)),
            pl.BlockSpec((bb, tm, d), lambda i, j: (i, j, 0)),
            pl.BlockSpec((1, 6, d), mod_map),
            pl.BlockSpec((None, 1, d), lambda i, j: (layer, 0, 0)),
            pl.BlockSpec((None, 2 * N_EXPERTS, d), lambda i, j: (layer, 0, 0)),
        ],
        out_specs=[pl.BlockSpec((bb, tm, d), lambda i, j: (i, j, 0)),
                   pl.BlockSpec((bb, tm, d), lambda i, j: (i, j, 0)),
                   pl.BlockSpec((bb, N_EXPERTS, tm), lambda i, j: (i, 0, j))],
        scratch_shapes=[pltpu.VMEM((kw, d), BF16)],
        compiler_params=_params(2),
        name="proj_out",
    )(oa, ob, oc, w_out, x, mod, g_ffn, wr_t)


def _excl_cumsum_lanes(m):
    e, n = m.shape
    blk = MXU_DIM
    nb = n // blk
    r = lax.broadcasted_iota(I32, (blk, blk), 0)
    c = lax.broadcasted_iota(I32, (blk, blk), 1)
    tri = jnp.where(r < c, 1.0, 0.0).astype(BF16)
    stacked = jnp.concatenate([m[:, k * blk:(k + 1) * blk] for k in range(nb)], axis=0).astype(BF16)
    within = _dot(stacked, tri)
    outs = []
    off = jnp.zeros((e, 1), F32)
    for k in range(nb):
        outs.append(within[k * e:(k + 1) * e, :] + off)
        off = off + jnp.sum(m[:, k * blk:(k + 1) * blk], axis=1, keepdims=True)
    return jnp.concatenate(outs, axis=1)


def _route_kernel(lg_ref, pos_ref, gate_ref, *, cap):
    b, e, n = lg_ref.shape
    lg = lg_ref[...]
    mx = lg.max(axis=1, keepdims=True)
    ex = jnp.exp(lg - mx)
    aff = (ex / ex.sum(axis=1, keepdims=True)).reshape(b * e, n)

    def search(it, cur):
        cand = cur | (jnp.int32(1) << (jnp.int32(30) - it))
        cnt = jnp.sum(jnp.where(aff >= pltpu.bitcast(cand, F32), 1.0, 0.0), axis=1, keepdims=True)
        return jnp.where(cnt >= cap, cand, cur)

    thr = pltpu.bitcast(lax.fori_loop(0, 31, search, jnp.zeros((b * e, 1), I32)), F32)
    gt = aff > thr
    eq = aff == thr
    need = cap - jnp.sum(jnp.where(gt, 1.0, 0.0), axis=1, keepdims=True)
    eq_rank = _excl_cumsum_lanes(jnp.where(eq, 1.0, 0.0))
    sel = gt | (eq & (eq_rank < need))
    slot = _excl_cumsum_lanes(jnp.where(sel, 1.0, 0.0))
    pos = jnp.where(sel, slot, -1.0).astype(I32)
    gate = jnp.where(sel, aff, 0.0)
    for i in range(b):
        pos_ref[:, i, 0, :] = pos[i * e:(i + 1) * e, :]
        gate_ref[:, i, 0, :] = gate[i * e:(i + 1) * e, :]


def _route(logits):
    b, e, n = logits.shape
    cap = EC_CAPACITY_FACTOR * n // N_EXPERTS
    return pl.pallas_call(
        functools.partial(_route_kernel, cap=cap),
        out_shape=[jax.ShapeDtypeStruct((e, b, 1, n), I32),
                   jax.ShapeDtypeStruct((e, b, 1, n), F32)],
        compiler_params=_params(0),
        name="route",
    )(logits)


def _one_hot_rows(pos_row, cap):
    n = pos_row.shape[1]
    slot = lax.broadcasted_iota(I32, (cap, n), 0)
    return slot == pos_row


def _gather_kernel(h_ref, pos_ref, gate_ref, xs_ref, gs_ref, *, cap):
    h = h_ref[0]
    group = max(1, GATHER_ROWS // cap)
    for e0 in range(0, N_EXPERTS, group):
        ohs = [_one_hot_rows(pos_ref[e, 0], cap) for e in range(e0, e0 + group)]
        xs = _dot(jnp.concatenate([jnp.where(oh, 1.0, 0.0).astype(BF16) for oh in ohs], axis=0), h).astype(BF16)
        for i, oh in enumerate(ohs):
            e = e0 + i
            xs_ref[e] = xs[i * cap:(i + 1) * cap, :]
            g = jnp.sum(jnp.where(oh, gate_ref[e, 0], 0.0), axis=1, keepdims=True)
            gs_ref[e] = jnp.broadcast_to(g, (cap, LANES))


def _gather(h, pos, gate):
    b, n, d = h.shape
    cap = EC_CAPACITY_FACTOR * n // N_EXPERTS
    return pl.pallas_call(
        functools.partial(_gather_kernel, cap=cap),
        out_shape=[jax.ShapeDtypeStruct((N_EXPERTS, b * cap, d), BF16),
                   jax.ShapeDtypeStruct((N_EXPERTS, b * cap, LANES), F32)],
        grid=(b,),
        in_specs=[
            pl.BlockSpec((1, n, d), lambda i: (i, 0, 0)),
            pl.BlockSpec((N_EXPERTS, 1, 1, n), lambda i: (0, i, 0, 0)),
            pl.BlockSpec((N_EXPERTS, 1, 1, n), lambda i: (0, i, 0, 0)),
        ],
        out_specs=[pl.BlockSpec((N_EXPERTS, cap, d), lambda i: (0, i, 0)),
                   pl.BlockSpec((N_EXPERTS, cap, LANES), lambda i: (0, i, 0))],
        compiler_params=_params(1),
        name="moe_gather",
    )(h, pos, gate)


def _ffn_kernel(xp_ref, gp_ref, xs_ref, gs_ref, wg_ref, wu_ref, wd_ref, yp_ref, ys_ref, acc_ref):
    j = pl.program_id(1)
    sp = xp_ref.shape[1]

    @pl.when(j == 0)
    def _():
        acc_ref[...] = jnp.zeros_like(acc_ref)

    xs = jnp.concatenate([xp_ref[0], xs_ref[0]], axis=0)
    a = _dot(xs, wg_ref[...].astype(BF16))
    u = _dot(xs, wu_ref[...].astype(BF16))
    hid = (a * jax.nn.sigmoid(a)) * u
    acc_ref[...] += _dot(hid.astype(BF16), wd_ref[...].astype(BF16))

    @pl.when(j == pl.num_programs(1) - 1)
    def _():
        yp_ref[0] = (acc_ref[:sp, :] * gp_ref[0][:, 0:1]).astype(BF16)
        ys_ref[0] = (acc_ref[sp:, :] * gs_ref[0][:, 0:1]).astype(BF16)


def _ffn(xp, gp, xs, gs, w_gate, w_up, w_down, layer):
    e, sp, d = xp.shape
    ss = xs.shape[1]
    s = sp + ss
    ff = w_gate.shape[-1]
    tf = FF_TILE
    return pl.pallas_call(
        _ffn_kernel,
        out_shape=[jax.ShapeDtypeStruct((e, sp, d), BF16), jax.ShapeDtypeStruct((e, ss, d), BF16)],
        grid=(e, ff // tf),
        in_specs=[
            pl.BlockSpec((1, sp, d), lambda i, j: (i, 0, 0)),
            pl.BlockSpec((1, sp, LANES), lambda i, j: (i, 0, 0)),
            pl.BlockSpec((1, ss, d), lambda i, j: (i, 0, 0)),
            pl.BlockSpec((1, ss, LANES), lambda i, j: (i, 0, 0)),
            pl.BlockSpec((None, None, d, tf), lambda i, j: (layer, i, 0, j)),
            pl.BlockSpec((None, None, d, tf), lambda i, j: (layer, i, 0, j)),
            pl.BlockSpec((None, None, tf, d), lambda i, j: (layer, i, j, 0)),
        ],
        out_specs=[pl.BlockSpec((1, sp, d), lambda i, j: (i, 0, 0)),
                   pl.BlockSpec((1, ss, d), lambda i, j: (i, 0, 0))],
        scratch_shapes=[pltpu.VMEM((s, d), F32)],
        compiler_params=_params(2),
        name="moe_ffn",
    )(xp, gp, xs, gs, w_gate, w_up, w_down)


def _combine_kernel(ye_ref, pos_ref, x_ref, mod_ref, *rest, cap, final):
    if final:
        g_ref, o_ref = rest
    else:
        (o_ref,) = rest
    group = max(1, MXU_DIM // cap)
    y = None
    for e0 in range(0, N_EXPERTS, group):
        oh = jnp.concatenate([jnp.where(_one_hot_rows(pos_ref[e, 0], cap), 1.0, 0.0).astype(BF16)
                              for e in range(e0, e0 + group)], axis=0)
        ye = jnp.concatenate([ye_ref[e] for e in range(e0, e0 + group)], axis=0)
        part = _dot_tn(oh, ye)
        y = part if y is None else y + part
    xn = x_ref[0] + mod_ref[0, 5:6, :] * y
    if final:
        ms = jnp.mean(xn * xn, axis=-1, keepdims=True)
        xn = xn * lax.rsqrt(ms + EPS) * g_ref[...]
    o_ref[0] = xn


def _combine(ye, pos, x, mod, final_norm):
    b, n, d = x.shape
    cap = EC_CAPACITY_FACTOR * n // N_EXPERTS
    tn = min(n, COMBINE_TILE)
    bm = mod.shape[0]
    mod_map = (lambda i, j: (i, 0, 0)) if bm > 1 else (lambda i, j: (0, 0, 0))
    final = final_norm is not None
    in_specs = [
        pl.BlockSpec((N_EXPERTS, cap, d), lambda i, j: (0, i, 0)),
        pl.BlockSpec((N_EXPERTS, 1, 1, tn), lambda i, j: (0, i, 0, j)),
        pl.BlockSpec((1, tn, d), lambda i, j: (i, j, 0)),
        pl.BlockSpec((1, 6, d), mod_map),
    ]
    args = [ye, pos, x, mod]
    if final:
        in_specs.append(pl.BlockSpec((1, d), lambda i, j: (0, 0)))
        args.append(final_norm)
    return pl.pallas_call(
        functools.partial(_combine_kernel, cap=cap, final=final),
        out_shape=jax.ShapeDtypeStruct((b, n, d), F32),
        grid=(b, n // tn),
        in_specs=in_specs,
        out_specs=pl.BlockSpec((1, tn, d), lambda i, j: (i, j, 0)),
        compiler_params=_params(2),
        name="moe_combine",
    )(*args)


def _rope_tables(n):
    t = np.arange(n)
    row = (t // GRID_W).astype(np.float32)
    col = (t % GRID_W).astype(np.float32)
    nf = DIFF_QK // 4
    inv = np.float32(ROPE_BASE) ** (-np.arange(nf, dtype=np.float32) / np.float32(nf))
    lane = np.arange(DIFF_WIDTH)
    pos = np.where(((lane % DIFF_QK) < DIFF_QK // 2)[None, :], row[:, None], col[:, None])
    ang = (pos * inv[lane % nf][None, :]).astype(np.float32)
    first = (lane % (2 * nf)) < nf
    cos, sin = np.cos(ang).astype(np.float32), np.sin(ang).astype(np.float32)
    return jnp.asarray(cos), jnp.asarray(np.where(first[None, :], -sin, sin))


def kernel(x_prompt, x_sample, cache_na_k, cache_na_v, cache_diff_k, cache_diff_v, state_ssm_re, state_ssm_im,
           c, c_ctx, w_ada, b_ada, norm_mix, norm_ffn, w_in, w_out, na_rpb, diff_lambda, diff_subln,
           ssm_a_re, ssm_a_im, ssm_log_dt, ssm_b_re, ssm_b_im, ssm_c_re, ssm_c_im, ssm_d, ssm_w_glu, ssm_b_glu,
           w_router, w_gate, w_up, w_down, final_norm):
    depth = w_in.shape[0]
    bp, sp, d = x_prompt.shape
    bs, ss, _ = x_sample.shape
    assert d == D_MODEL and bs + 1 <= SUBLANES
    past = cache_na_k.shape[2]

    cond = jnp.zeros((SUBLANES, d), F32).at[0].set(c_ctx).at[1:1 + bs].set(c)
    mods = _ada(cond, w_ada, b_ada).reshape(depth, SUBLANES, 6, d)

    rope_tabs = _rope_tables(ss)
    kc_a = cache_na_k.reshape(bs, depth, past, NA_WIDTH)
    vc_a = cache_na_v.reshape(bs, depth, past, NA_WIDTH)
    kc_b = cache_diff_k.reshape(bs, depth, past, DIFF_WIDTH)
    vc_b = cache_diff_v.reshape(bs, depth, past, DIFF_WIDTH)
    subln = jnp.tile(diff_subln, (1, LANES // DIFF_V)).reshape(depth, 1, LANES)
    norm_mix = norm_mix.reshape(depth, 1, d)
    norm_ffn = norm_ffn.reshape(depth, 1, d)
    ssm_d = ssm_d.reshape(depth, 1, SSM_WIDTH)
    ssm_b_glu = ssm_b_glu.reshape(depth, 1, SSM_WIDTH)
    wr_t = jnp.swapaxes(w_router, 1, 2).astype(F32)
    wr_hi = wr_t.astype(BF16)
    wr_t = jnp.concatenate([wr_hi, (wr_t - wr_hi.astype(F32)).astype(BF16)], axis=1)
    fnorm = final_norm.reshape(1, d)
    zero_state = jnp.zeros((bp, 2, SSM_N), F32)
    tabs = _s5_tables(ssm_a_re, ssm_a_im, ssm_log_dt, ssm_b_re, ssm_b_im, ssm_c_re, ssm_c_im)
    bias_tab = _na_bias_table(na_rpb)

    xp, xs = x_prompt, x_sample
    new_sre, new_sim = [], []
    caches = None
    for l in range(depth):
        lam_init = 0.8 - 0.6 * math.exp(-0.3 * l)
        mod_p = mods[l, 0:1]
        mod_s = mods[l, 1:1 + bs]
        diff_params = (diff_lambda, subln)

        qa, ka, va, qb, kb, vb, u = _proj_in(xp, mod_p, norm_mix, w_in, l, None, depth, caches)
        caches = (ka, va, kb, vb)
        o_a = _attention(qa, ka, va, l, None, l, None, lam_init)
        o_b = _attention(qb, kb, vb, l, None, l, diff_params, lam_init)
        o_c, fre, fim = _s5(u, zero_state, zero_state, tabs, ssm_d, ssm_w_glu, ssm_b_glu, l)
        xp, hp, lg_p = _proj_out(o_a, o_b, o_c, w_out, xp, mod_p, norm_ffn, wr_t, l)
        new_sre.append(fre.reshape(bp, 2, SSM_GROUPS, SSM_STATE))
        new_sim.append(fim.reshape(bp, 2, SSM_GROUPS, SSM_STATE))

        qa, ka, va, qb, kb, vb, u = _proj_in(xs, mod_s, norm_mix, w_in, l, rope_tabs, 0, None)
        o_a = _neighbourhood_attention(qa, ka, va, kc_a, vc_a, bias_tab, l)
        o_b = _attention(qb, kb, vb, 0, (kc_b, vc_b), l, diff_params, lam_init)
        h0re = state_ssm_re[:, l].reshape(bs, 2, SSM_N)
        h0im = state_ssm_im[:, l].reshape(bs, 2, SSM_N)
        o_c, _, _ = _s5(u, h0re, h0im, tabs, ssm_d, ssm_w_glu, ssm_b_glu, l)
        xs, hs, lg_s = _proj_out(o_a, o_b, o_c, w_out, xs, mod_s, norm_ffn, wr_t, l)

        pos_p, gate_p = _route(lg_p)
        pos_s, gate_s = _route(lg_s)
        xg_p, gs_p = _gather(hp, pos_p, gate_p)
        xg_s, gs_s = _gather(hs, pos_s, gate_s)
        ye_p, ye_s = _ffn(xg_p, gs_p, xg_s, gs_s, w_gate, w_up, w_down, l)
        last = l == depth - 1
        xp = _combine(ye_p, pos_p, xp, mod_p, fnorm if last else None)
        xs = _combine(ye_s, pos_s, xs, mod_s, fnorm if last else None)

    ka, va, kb, vb = caches
    return (xp, xs, ka.reshape(bp, depth, sp, NA_HEADS, HEAD_DIM), va.reshape(bp, depth, sp, NA_HEADS, HEAD_DIM),
            kb.reshape(bp, depth, sp, DIFF_HEADS, DIFF_V), vb.reshape(bp, depth, sp, DIFF_HEADS, DIFF_V),
            jnp.stack(new_sre, axis=1), jnp.stack(new_sim, axis=1))
```

```python
import functools
import math

import jax
import jax.numpy as jnp
import numpy as np
from jax import lax
from jax.experimental import pallas as pl
from jax.experimental.pallas import tpu as pltpu

F32 = jnp.float32
BF16 = jnp.bfloat16
I32 = jnp.int32

D_MODEL = 1024
GRID_W = 64
HEAD_DIM = 64
NA_HEADS = 8
NA_WIDTH = NA_HEADS * HEAD_DIM
WIN_R = 8
WIN_C = 16
DIFF_HEADS = 4
DIFF_QK = 32
DIFF_V = 64
DIFF_WIDTH = DIFF_HEADS * DIFF_V
SSM_GROUPS = 16
SSM_GROUP_CH = 16
SSM_WIDTH = SSM_GROUPS * SSM_GROUP_CH
SSM_STATE = 64
SSM_N = SSM_GROUPS * SSM_STATE
IN_WIDTH = 3 * NA_WIDTH + 3 * DIFF_WIDTH + SSM_WIDTH
N_EXPERTS = 16
EXPERT_FF = 2048
EC_CAPACITY_FACTOR = 2
ROPE_BASE = 10000.0
EPS = 1e-6
NEG_INF = -1e30
LOG2E = math.log2(math.e)

LANES = 128
SUBLANES = 8
MXU_DIM = 256
VMEM_LIMIT_CAP = 60000 * 1024

ROW_TILE = 256
PROJ_IN_CHUNKS = 4
PROJ_OUT_CHUNKS = 2
ATTN_Q_TILE = 512
ATTN_REQUESTS_PER_STEP = 2
NA_QROWS = 4
NA_KROWS = 12
SCAN_TILE = 256
FF_TILE = 1024
COMBINE_TILE = 512
GATHER_ROWS = 512
DEN_IN_MATMUL_MIN_KEYS = 1024


def _params(ndims):
    return pltpu.CompilerParams(
        dimension_semantics=("arbitrary",) * ndims,
        vmem_limit_bytes=VMEM_LIMIT_CAP,
    )


def _step_tile(b, n, n_mod, chunks):
    tm = min(n, ROW_TILE * chunks)
    bb = max(1, ROW_TILE * chunks // tm)
    if n_mod > 1 or b % bb:
        bb = 1
    return bb, tm


def _dot(a, b):
    return jnp.dot(a, b, preferred_element_type=F32)


def _dot_nt(a, b):
    return lax.dot_general(a, b, (((1,), (1,)), ((), ())), preferred_element_type=F32)


def _dot_tn(a, b):
    return lax.dot_general(a, b, (((0,), (0,)), ((), ())), preferred_element_type=F32)


def _cast_rows(dst_ref, src_ref, rows):
    n = src_ref.shape[0]
    for r in range(0, n, rows):
        dst_ref[r:r + rows, :] = src_ref[r:r + rows, :].astype(BF16)


def _modulated_norm(x, g, shift, scale):
    ms = jnp.mean(x * x, axis=-1, keepdims=True)
    return (x * lax.rsqrt(ms + EPS) * g) * (1.0 + scale) + shift


def _ada_kernel(c_ref, w_ref, b_ref, o_ref):
    c = c_ref[...]
    s = c * jax.nn.sigmoid(c)
    o_ref[0] = _dot(s.astype(BF16), w_ref[0].astype(BF16)) + b_ref[0]


def _ada(cond, w_ada, b_ada):
    depth = w_ada.shape[0]
    tn = 1536
    nt = 6 * D_MODEL // tn
    return pl.pallas_call(
        _ada_kernel,
        out_shape=jax.ShapeDtypeStruct((depth, SUBLANES, 6 * D_MODEL), F32),
        grid=(depth, nt),
        in_specs=[
            pl.BlockSpec((SUBLANES, D_MODEL), lambda l, j: (0, 0)),
            pl.BlockSpec((1, D_MODEL, tn), lambda l, j: (l, 0, j)),
            pl.BlockSpec((1, 1, tn), lambda l, j: (l, 0, j)),
        ],
        out_specs=pl.BlockSpec((1, SUBLANES, tn), lambda l, j: (l, 0, j)),
        compiler_params=_params(2),
        name="ada",
    )(cond, w_ada, b_ada.reshape(depth, 1, 6 * D_MODEL))


def _rope_apply(x, cos, sin_signed):
    lane = lax.broadcasted_iota(I32, (1, LANES), 1)
    first = (lane % 16) < 8
    outs = []
    for t in range(x.shape[1] // LANES):
        xt = x[:, t * LANES:(t + 1) * LANES]
        partner = jnp.where(first, pltpu.roll(xt, LANES - 8, axis=1), pltpu.roll(xt, 8, axis=1))
        outs.append(xt * cos[:, t * LANES:(t + 1) * LANES] + partner * sin_signed[:, t * LANES:(t + 1) * LANES])
    return jnp.concatenate(outs, axis=1)


def _proj_in_kernel(*refs, rope, n_alias):
    x_ref, mod_ref, g_ref, w_ref = refs[:4]
    pos = 4
    if rope:
        cos_ref, sin_ref = refs[pos:pos + 2]
        pos += 2
    pos += n_alias
    qa_ref, ka_ref, va_ref, qb_ref, kb_ref, vb_ref, u_ref, wbf_ref = refs[pos:pos + 8]

    @pl.when((pl.program_id(0) == 0) & (pl.program_id(1) == 0))
    def _():
        _cast_rows(wbf_ref, w_ref, 128)

    i1, i2, i3 = NA_WIDTH, 2 * NA_WIDTH, 3 * NA_WIDTH
    i4, i5, i6 = i3 + DIFF_WIDTH, i3 + 2 * DIFF_WIDTH, i3 + 3 * DIFF_WIDTH
    for bi in range(x_ref.shape[0]):
        for r0 in range(0, x_ref.shape[1], ROW_TILE):
            rows = slice(r0, r0 + ROW_TILE)

            def put(ref, val):
                for s in range(ref.shape[1]):
                    ref[bi, s, rows, :] = val.astype(ref.dtype)

            h = _modulated_norm(x_ref[bi, rows, :], g_ref[...], mod_ref[0, 0:1, :], mod_ref[0, 1:2, :])
            z = _dot(h.astype(BF16), wbf_ref[...])
            qa_ref[bi, rows, :] = (z[:, :i1] * (HEAD_DIM ** -0.5 * LOG2E)).astype(qa_ref.dtype)
            put(ka_ref, z[:, i1:i2])
            put(va_ref, z[:, i2:i3])
            qb = z[:, i3:i4]
            kb = z[:, i4:i5]
            if rope:
                qb = _rope_apply(qb, cos_ref[rows, :], sin_ref[rows, :])
                kb = _rope_apply(kb, cos_ref[rows, :], sin_ref[rows, :])
            qb_ref[bi, rows, :] = (qb * (DIFF_QK ** -0.5 * LOG2E)).astype(qb_ref.dtype)
            put(kb_ref, kb)
            put(vb_ref, z[:, i5:i6])
            for hf in range(SSM_WIDTH // LANES):
                u_ref[bi, hf, rows, :] = z[:, i6 + hf * LANES:i6 + (hf + 1) * LANES]


def _proj_in(x, mod, g_norm, w_in, layer, rope_tabs, cache_slots, caches):
    b, n, d = x.shape
    bb, tm = _step_tile(b, n, mod.shape[0], PROJ_IN_CHUNKS)
    rope = rope_tabs is not None
    mod_map = (lambda i, j: (i, 0, 0)) if mod.shape[0] > 1 else (lambda i, j: (0, 0, 0))
    in_specs = [
        pl.BlockSpec((bb, tm, d), lambda i, j: (i, j, 0)),
        pl.BlockSpec((1, 6, d), mod_map),
        pl.BlockSpec((None, 1, d), lambda i, j: (layer, 0, 0)),
        pl.BlockSpec((None, d, IN_WIDTH), lambda i, j: (layer, 0, 0), pipeline_mode=pl.Buffered(1)),
    ]
    args = [x, mod, g_norm, w_in]
    if rope:
        in_specs += [pl.BlockSpec((tm, DIFF_WIDTH), lambda i, j: (j, 0))] * 2
        args += list(rope_tabs)
    aliases = {}
    kv_out = (1, 2, 4, 5)
    if caches is not None:
        for c, o in zip(caches, kv_out):
            aliases[len(args)] = o
            in_specs.append(pl.BlockSpec(memory_space=pl.ANY))
            args.append(c)
    widths = (NA_WIDTH, NA_WIDTH, NA_WIDTH, DIFF_WIDTH, DIFF_WIDTH, DIFF_WIDTH)
    nslab = SSM_WIDTH // LANES
    out_shape, out_specs = [], []
    for o, w in enumerate(widths):
        if o not in kv_out:
            out_shape.append(jax.ShapeDtypeStruct((b, n, w), BF16))
            out_specs.append(pl.BlockSpec((bb, tm, w), lambda i, j: (i, j, 0)))
        elif cache_slots == 0:
            out_shape.append(jax.ShapeDtypeStruct((b, 1, n, w), BF16))
            out_specs.append(pl.BlockSpec((bb, 1, tm, w), lambda i, j: (i, 0, j, 0)))
        elif caches is None:
            out_shape.append(jax.ShapeDtypeStruct((b, cache_slots, n, w), F32))
            out_specs.append(pl.BlockSpec((bb, cache_slots, tm, w), lambda i, j: (i, 0, j, 0)))
        else:
            out_shape.append(jax.ShapeDtypeStruct((b, cache_slots, n, w), F32))
            out_specs.append(pl.BlockSpec((bb, 1, tm, w), lambda i, j: (i, layer, j, 0)))
    out_shape.append(jax.ShapeDtypeStruct((b, nslab, n, LANES), F32))
    out_specs.append(pl.BlockSpec((bb, nslab, tm, LANES), lambda i, j: (i, 0, j, 0)))
    return pl.pallas_call(
        functools.partial(_proj_in_kernel, rope=rope, n_alias=len(aliases)),
        out_shape=out_shape,
        grid=(b // bb, n // tm),
        in_specs=in_specs,
        out_specs=out_specs,
        scratch_shapes=[pltpu.VMEM((d, IN_WIDTH), BF16)],
        input_output_aliases=aliases,
        compiler_params=_params(2),
        name="proj_in",
    )(*args)


def _masked_query(qp, mask):
    return jnp.where(mask, qp, jnp.zeros_like(qp))


def _with_ones(v):
    return jnp.concatenate([v, jnp.ones_like(v)], axis=1)


def _softmax_av_wide(qm, ks, vs, biases):
    ss = []
    for k, bias in zip(ks, biases):
        s = _dot_nt(qm, k)
        if bias is not None:
            s = bias(s)
        ss.append(s)
    m = ss[0].max(axis=-1, keepdims=True)
    for s in ss[1:]:
        m = jnp.maximum(m, s.max(axis=-1, keepdims=True))
    acc = None
    for s, v in zip(ss, vs):
        pv = _dot(jnp.exp2(s - m).astype(BF16), v)
        acc = pv if acc is None else acc + pv
    return acc[:, :LANES] / acc[:, LANES:]


def _softmax_av(qm, ks, vs, biases):
    ss = []
    for k, bias in zip(ks, biases):
        s = _dot_nt(qm, k)
        if bias is not None:
            s = bias(s)
        ss.append(s)
    m = ss[0].max(axis=-1, keepdims=True)
    for s in ss[1:]:
        m = jnp.maximum(m, s.max(axis=-1, keepdims=True))
    acc = None
    den = None
    for s, v in zip(ss, vs):
        p = jnp.exp2(s - m)
        psum = p.sum(axis=-1, keepdims=True)
        den = psum if den is None else den + psum
        pv = _dot(p.astype(BF16), v)
        acc = pv if acc is None else acc + pv
    return acc / den


def _diff_lambda(lam_ref, lam_init):
    lp = lam_ref[...]
    a = jnp.sum(lp[0:1] * lp[1:2], axis=-1, keepdims=True)
    b = jnp.sum(lp[2:3] * lp[3:4], axis=-1, keepdims=True)
    return jnp.exp(a) - jnp.exp(b) + lam_init


def _attn_kernel(*refs, has_ctx, diff, lam_init):
    refs = list(refs)
    q_ref, k_ref, v_ref = refs[:3]
    pos = 3
    if has_ctx:
        kc_ref, vc_ref = refs[pos:pos + 2]
        pos += 2
    if diff:
        lam_ref, subln_ref = refs[pos:pos + 2]
        pos += 2
    o_ref = refs[pos]
    width = q_ref.shape[2]
    lane = lax.broadcasted_iota(I32, (1, LANES), 1)
    if diff:
        lam = _diff_lambda(lam_ref, lam_init)
    for bi, p in [(bi, p) for bi in range(q_ref.shape[0]) for p in range(width // LANES)]:
        sl = slice(p * LANES, (p + 1) * LANES)
        qp = q_ref[bi, :, sl]
        ks = [k_ref[bi, :, sl].astype(BF16)]
        vs = [v_ref[bi, :, sl].astype(BF16)]
        if has_ctx:
            ks.append(kc_ref[:, sl].astype(BF16))
            vs.append(vc_ref[:, sl].astype(BF16))
        nob = [None] * len(ks)
        tq = qp.shape[0]
        wide = sum(k.shape[0] for k in ks) >= DEN_IN_MATMUL_MIN_KEYS
        if wide:
            vs = [_with_ones(v) for v in vs]
        pair = None
        for sub in range(LANES // HEAD_DIM):
            head_mask = (lane // HEAD_DIM) == sub
            if diff:
                qmasks = [(lane // DIFF_QK) == (2 * sub + half) for half in range(2)]
                av = _softmax_av_wide if wide else _softmax_av
                os_ = [av(_masked_query(qp, mk), ks, vs, nob) for mk in qmasks]
                o = jnp.where(head_mask, os_[0] - lam * os_[1], 0.0)
                ms = jnp.sum(o * o, axis=-1, keepdims=True) * (1.0 / DIFF_V)
                o = (o * lax.rsqrt(ms + EPS) * subln_ref[...]) * (1.0 - lam_init)
            else:
                qm = _masked_query(qp, head_mask)
                o = _softmax_av_wide(qm, ks, vs, nob) if wide else _softmax_av(qm, ks, vs, nob)
                o = jnp.where(head_mask, o, 0.0)
            pair = o if pair is None else pair + o
        o_ref[bi, :, sl] = pair.astype(o_ref.dtype)


def _attention(q, k, v, kv_slot, ctx_kv, layer, diff_params, lam_init):
    b, nq, w = q.shape
    nk = k.shape[2]
    tq = min(nq, ATTN_Q_TILE)
    has_ctx = ctx_kv is not None
    diff = diff_params is not None
    bb = ATTN_REQUESTS_PER_STEP if (nq == tq and not has_ctx and b % ATTN_REQUESTS_PER_STEP == 0) else 1
    in_specs = [
        pl.BlockSpec((bb, tq, w), lambda i, j: (i, j, 0)),
        pl.BlockSpec((bb, None, nk, w), lambda i, j: (i, kv_slot, 0, 0)),
        pl.BlockSpec((bb, None, nk, w), lambda i, j: (i, kv_slot, 0, 0)),
    ]
    args = [q, k, v]
    nc = 0
    if has_ctx:
        nc = ctx_kv[0].shape[2]
        in_specs += [pl.BlockSpec((None, None, nc, w), lambda i, j: (i, layer, 0, 0))] * 2
        args += list(ctx_kv)
    if diff:
        lam_p, subln = diff_params
        in_specs += [
            pl.BlockSpec((None, 4, DIFF_QK), lambda i, j: (layer, 0, 0)),
            pl.BlockSpec((None, 1, LANES), lambda i, j: (layer, 0, 0)),
        ]
        args += [lam_p, subln]
    return pl.pallas_call(
        functools.partial(_attn_kernel, has_ctx=has_ctx, diff=diff, lam_init=lam_init),
        out_shape=jax.ShapeDtypeStruct((b, nq, w), BF16),
        grid=(b // bb, nq // tq),
        in_specs=in_specs,
        out_specs=pl.BlockSpec((bb, tq, w), lambda i, j: (i, j, 0)),
        compiler_params=_params(2),
        name="attn_diff" if diff else "attn",
    )(*args)


def _na_kernel(q_ref, k_ref, v_ref, kc_ref, vc_ref, bias_ref, o_ref, *, rows):
    i = pl.program_id(1)
    tq = NA_QROWS * GRID_W
    nkw = NA_KROWS * GRID_W
    ustart = jnp.clip(NA_QROWS * i - WIN_R // 2, 0, rows - NA_KROWS)
    koff = pl.multiple_of(ustart * GRID_W, GRID_W)
    delta = ustart - NA_QROWS * i
    qrow = NA_QROWS * i + lax.broadcasted_iota(I32, (tq, nkw), 0) // GRID_W
    krow = ustart + lax.broadcasted_iota(I32, (tq, nkw), 1) // GRID_W
    wstart = jnp.clip(qrow - WIN_R // 2, 0, rows - WIN_R)
    row_ok = (krow >= wstart) & (krow < wstart + WIN_R)
    nsub = LANES // HEAD_DIM
    lane = lax.broadcasted_iota(I32, (1, LANES), 1)
    for p in range(NA_WIDTH // LANES):
        sl = slice(p * LANES, (p + 1) * LANES)
        qp = q_ref[0, :, sl]
        ks = [k_ref[0, pl.ds(koff, nkw), sl].astype(BF16), kc_ref[:, sl].astype(BF16)]
        vs = [_with_ones(v_ref[0, pl.ds(koff, nkw), sl].astype(BF16)), _with_ones(vc_ref[:, sl].astype(BF16))]
        outs = []
        for sub in range(nsub):
            def win_bias(s, h=p * nsub + sub):
                rows_ = []
                for ri in range(NA_QROWS):
                    tiles = []
                    for pj in range(NA_KROWS // 2):
                        dr = delta + 2 * pj - ri
                        idx = jnp.clip(dr, -WIN_R, WIN_R - 1) + WIN_R
                        tiles.append(bias_ref[h, idx])
                    rows_.append(jnp.concatenate(tiles, axis=1))
                bias = jnp.concatenate(rows_, axis=0)
                return jnp.where(row_ok, s + bias, NEG_INF)

            outs.append(_softmax_av_wide(_masked_query(qp, (lane // HEAD_DIM) == sub), ks, vs, [win_bias, None]))
        pair = jnp.where((lane // HEAD_DIM) == 0, outs[0], outs[1])
        o_ref[0, :, sl] = pair.astype(o_ref.dtype)


def _na_table_kernel(v_ref, o_ref):
    nd = v_ref.shape[0]
    q = lax.broadcasted_iota(I32, (GRID_W, 2 * GRID_W), 0)
    k = lax.broadcasted_iota(I32, (GRID_W, 2 * GRID_W), 1)
    kk = k % GRID_W
    cstart = jnp.clip(q - WIN_C // 2, 0, GRID_W - WIN_C)
    col_ok = (kk >= cstart) & (kk < cstart + WIN_C)
    left = k < GRID_W
    tiles = []
    for dr in range(nd):
        x = jnp.broadcast_to(v_ref[dr:dr + 1, :], (GRID_W, 2 * GRID_W))
        tiles.append((pltpu.roll(x, GRID_W + 1, axis=1, stride=1, stride_axis=0),
                      pltpu.roll(x, 1, axis=1, stride=1, stride_axis=0)))
    zero = jnp.zeros((GRID_W, 2 * GRID_W), F32)
    for j in range(nd + 1):
        lt = tiles[j - 1][0] if j >= 1 else zero
        rt = tiles[j][1] if j < nd else zero
        o_ref[j] = jnp.where(col_ok, jnp.where(left, lt, rt) * LOG2E, NEG_INF)


def _na_bias_table(rpb):
    depth, h, nd, nc = rpb.shape
    assert 2 * GRID_W == LANES and nd == 2 * WIN_R - 1 and nc == 2 * WIN_C - 1
    rpb = rpb.astype(F32)
    lo = GRID_W - WIN_C
    v = jnp.concatenate([jnp.broadcast_to(rpb[..., :1], rpb.shape[:3] + (lo,)), rpb,
                         jnp.broadcast_to(rpb[..., -1:], rpb.shape[:3] + (2 * GRID_W - lo - nc,))], axis=-1)
    return pl.pallas_call(
        _na_table_kernel,
        out_shape=jax.ShapeDtypeStruct((depth, h, nd + 1, GRID_W, 2 * GRID_W), F32),
        grid=(depth, h),
        in_specs=[pl.BlockSpec((None, None, nd, 2 * GRID_W), lambda l, i: (l, i, 0, 0))],
        out_specs=pl.BlockSpec((None, None, nd + 1, GRID_W, 2 * GRID_W), lambda l, i: (l, i, 0, 0, 0)),
        compiler_params=_params(2),
        name="na_table",
    )(v)


def _neighbourhood_attention(q, k, v, kc, vc, bias_tab, layer):
    b, n, w = q.shape
    rows = n // GRID_W
    tq = NA_QROWS * GRID_W
    nc = kc.shape[2]
    nkw = NA_KROWS * GRID_W
    return pl.pallas_call(
        functools.partial(_na_kernel, rows=rows),
        out_shape=jax.ShapeDtypeStruct((b, n, w), BF16),
        grid=(b, n // tq),
        in_specs=[
            pl.BlockSpec((1, tq, w), lambda i, j: (i, j, 0)),
            pl.BlockSpec((1, None, n, w), lambda i, j: (i, 0, 0, 0)),
            pl.BlockSpec((1, None, n, w), lambda i, j: (i, 0, 0, 0)),
            pl.BlockSpec((None, None, nc, w), lambda i, j: (i, layer, 0, 0)),
            pl.BlockSpec((None, None, nc, w), lambda i, j: (i, layer, 0, 0)),
            pl.BlockSpec((None,) + bias_tab.shape[1:], lambda i, j: (layer, 0, 0, 0, 0)),
        ],
        out_specs=pl.BlockSpec((1, tq, w), lambda i, j: (i, j, 0)),
        compiler_params=_params(2),
        name="na_attn",
    )(q, k, v, kc, vc, bias_tab)


def _sublane_scan(er, ei, tr, ti, pw_ref, base, lanes, reverse):
    row = lax.broadcasted_iota(I32, (SUBLANES, LANES), 0)
    if reverse:
        first = row == SUBLANES - 1
        xr = jnp.where(first, tr, pltpu.roll(er, SUBLANES - 1, axis=0))
        xi = jnp.where(first, ti, pltpu.roll(ei, SUBLANES - 1, axis=0))
    else:
        first = row == 0
        xr = jnp.where(first, tr, pltpu.roll(er, 1, axis=0))
        xi = jnp.where(first, ti, pltpu.roll(ei, 1, axis=0))
    for k, s in enumerate((1, 2, 4)):
        ar = pw_ref[base + 2 * k:base + 2 * k + 1, lanes]
        ai = pw_ref[base + 2 * k + 1:base + 2 * k + 2, lanes]
        if reverse:
            keep = row < SUBLANES - s
            sr = jnp.where(keep, pltpu.roll(xr, SUBLANES - s, axis=0), 0.0)
            si = jnp.where(keep, pltpu.roll(xi, SUBLANES - s, axis=0), 0.0)
        else:
            keep = row >= s
            sr = jnp.where(keep, pltpu.roll(xr, s, axis=0), 0.0)
            si = jnp.where(keep, pltpu.roll(xi, s, axis=0), 0.0)
        xr, xi = xr + (ar * sr - ai * si), xi + (ar * si + ai * sr)
    return xr, xi


def _s5_kernel(u_ref, h0re_ref, h0im_ref, bt_ref, ct_ref, a1_ref, ach_ref, pfix_ref, d_ref, wglu_ref, bglu_ref,
               o_ref, fre_ref, fim_ref, up_ref, yp_ref, buf_ref, car_ref, bmat_ref, cmat_ref):
    seq = u_ref.shape[2]
    tt = SCAN_TILE
    ch = tt // SUBLANES
    ntile = seq // tt
    nlt = SSM_N // LANES
    nhalf = SSM_WIDTH // LANES

    @pl.when(pl.program_id(0) == 0)
    def _():
        g, p, hc = SSM_GROUPS, SSM_STATE, SSM_GROUP_CH
        btb = bt_ref[...].astype(BF16)
        ctb = ct_ref[...].astype(BF16)
        for dc in range(4):
            r = lax.broadcasted_iota(I32, (4 * p, g * p), 0)
            c = lax.broadcasted_iota(I32, (4 * p, g * p), 1)
            sel = jnp.where(r == dc * p + c % p, 1.0, 0.0).astype(BF16)
            r = lax.broadcasted_iota(I32, (g * hc, g * p), 0)
            c = lax.broadcasted_iota(I32, (g * hc, g * p), 1)
            blk = jnp.where(r // hc == c // p, _dot(btb, sel), 0.0)
            bmat_ref[:, dc * g * p:(dc + 1) * g * p] = blk.astype(BF16)
            r = lax.broadcasted_iota(I32, (g * p, 4 * p), 0)
            c = lax.broadcasted_iota(I32, (g * p, 4 * p), 1)
            sel = jnp.where(c == dc * p + r % p, 1.0, 0.0).astype(BF16)
            r = lax.broadcasted_iota(I32, (g * p, g * hc), 0)
            c = lax.broadcasted_iota(I32, (g * p, g * hc), 1)
            blk = jnp.where(r // p == c // hc, _dot(sel, ctb), 0.0)
            cmat_ref[dc * g * p:(dc + 1) * g * p, :] = blk.astype(BF16)

    def permute_in(t, _):
        t0 = pl.multiple_of(t * tt, tt)
        for j in range(0, ch, 2):
            for hf in range(nhalf):
                lanes = slice(hf * LANES, (hf + 1) * LANES)
                rows = jnp.concatenate([u_ref[0, hf, pl.ds(t0 + j + jj, SUBLANES, stride=ch), :] for jj in range(2)],
                                       axis=0)
                r0 = pl.multiple_of(t0 + j * SUBLANES, 2 * SUBLANES)
                up_ref[pl.ds(r0, 2 * SUBLANES), lanes] = rows.astype(BF16)
                yp_ref[pl.ds(r0, 2 * SUBLANES), lanes] = rows * d_ref[:, lanes]
        return 0

    lax.fori_loop(0, ntile, permute_in, 0)

    for d in range(2):
        for lt in range(nlt):
            lanes = slice(lt * LANES, (lt + 1) * LANES)
            car_ref[d, 0, :, lanes] = jnp.broadcast_to(h0re_ref[0, d:d + 1, lanes], (SUBLANES, LANES))
            car_ref[d, 1, :, lanes] = jnp.broadcast_to(h0im_ref[0, d:d + 1, lanes], (SUBLANES, LANES))

    def tile(t, _):
        starts = []
        for d in range(2):
            tix = (ntile - 1 - t) if d == 1 else t
            t0 = pl.multiple_of(tix * tt, tt)
            starts.append(t0)
            buf_ref[d] = _dot(up_ref[pl.ds(t0, tt), :], bmat_ref[:, d * 2 * SSM_N:(d + 1) * 2 * SSM_N])
        for d in range(2):
            reverse = d == 1
            order = range(ch - 1, -1, -1) if reverse else range(ch)
            for lt in range(nlt):
                lre = slice(lt * LANES, (lt + 1) * LANES)
                lim = slice(SSM_N + lt * LANES, SSM_N + (lt + 1) * LANES)
                ar = a1_ref[2 * d:2 * d + 1, lre]
                ai = a1_ref[2 * d + 1:2 * d + 2, lre]
                cr = jnp.zeros((SUBLANES, LANES), F32)
                ci = jnp.zeros((SUBLANES, LANES), F32)
                for j in order:
                    rows = slice(j * SUBLANES, (j + 1) * SUBLANES)
                    cr, ci = (ar * cr - ai * ci) + buf_ref[d, rows, lre], (ar * ci + ai * cr) + buf_ref[d, rows, lim]
                    buf_ref[d, rows, lre] = cr
                    buf_ref[d, rows, lim] = ci
                tr, ti = car_ref[d, 0, :, lre], car_ref[d, 1, :, lre]
                gr, gi = _sublane_scan(cr, ci, tr, ti, ach_ref, 6 * d, lre, reverse)
                a_r, a_i = ach_ref[6 * d:6 * d + 1, lre], ach_ref[6 * d + 1:6 * d + 2, lre]
                nr, ni = (a_r * gr - a_i * gi) + cr, (a_r * gi + a_i * gr) + ci
                edge = 0 if reverse else SUBLANES - 1
                car_ref[d, 0, :, lre] = jnp.broadcast_to(nr[edge:edge + 1, :], (SUBLANES, LANES))
                car_ref[d, 1, :, lre] = jnp.broadcast_to(ni[edge:edge + 1, :], (SUBLANES, LANES))
                for j in range(ch):
                    rows = slice(j * SUBLANES, (j + 1) * SUBLANES)
                    pr = pfix_ref[2 * d, j:j + 1, lre]
                    pi = pfix_ref[2 * d + 1, j:j + 1, lre]
                    buf_ref[d, rows, lre] += pr * gr - pi * gi
                    buf_ref[d, rows, lim] += pr * gi + pi * gr
        for d in range(2):
            yp_ref[pl.ds(starts[d], tt), :] += _dot(buf_ref[d].astype(BF16),
                                                    cmat_ref[d * 2 * SSM_N:(d + 1) * 2 * SSM_N, :])
        return 0

    lax.fori_loop(0, ntile, tile, 0)

    for d in range(2):
        fre_ref[0, d:d + 1, :] = car_ref[d, 0, 0:1, :]
        fim_ref[0, d:d + 1, :] = car_ref[d, 1, 0:1, :]

    wglu = wglu_ref[...].astype(BF16)

    def glu(t, _):
        t0 = pl.multiple_of(t * tt, tt)
        g = jax.nn.gelu(yp_ref[pl.ds(t0, tt), :])
        out = g * jax.nn.sigmoid(_dot(g.astype(BF16), wglu) + bglu_ref[...])
        for j in range(ch):
            for hf in range(nhalf):
                o_ref[0, hf, pl.ds(t0 + j, SUBLANES, stride=ch), :] = out[j * SUBLANES:(j + 1) * SUBLANES,
                                                                          hf * LANES:(hf + 1) * LANES]
        return 0

    lax.fori_loop(0, ntile, glu, 0)


def _cmul(ar, ai, br, bi):
    return ar * br - ai * bi, ar * bi + ai * br


def _s5_tables(a_re, a_im, log_dt, b_re, b_im, c_re, c_im):
    depth = a_re.shape[0]
    g, p, hc = SSM_GROUPS, SSM_STATE, SSM_GROUP_CH
    ch = SCAN_TILE // SUBLANES
    lr, li = a_re.astype(F32), a_im.astype(F32)
    dt = jnp.exp(log_dt.astype(F32))[..., None]
    mag = jnp.exp(lr * dt)
    ar, ai = mag * jnp.cos(li * dt), mag * jnp.sin(li * dt)
    den = lr * lr + li * li
    qr = ((ar - 1.0) * lr + ai * li) / den
    qi = (ai * lr - (ar - 1.0) * li) / den
    br, bi = _cmul(qr[..., None], qi[..., None], b_re.astype(F32), b_im.astype(F32))
    bparts = jnp.stack([br, bi], axis=2)
    bmat = bparts.transpose(0, 3, 5, 1, 2, 4).reshape(depth, g * hc, 4 * p)
    cparts = jnp.stack([c_re.astype(F32), -c_im.astype(F32)], axis=2)
    cmat = cparts.transpose(0, 1, 2, 5, 3, 4).reshape(depth, 4 * p, g * hc)

    def power(k):
        kk = k[None, None, :, None, None]
        m = jnp.exp(kk * (lr * dt)[:, :, None])
        th = kk * (li * dt)[:, :, None]
        return (m * jnp.cos(th)).reshape(depth, 2, -1, g * p), (m * jnp.sin(th)).reshape(depth, 2, -1, g * p)

    a1 = jnp.stack([ar.reshape(depth, 2, g * p), ai.reshape(depth, 2, g * p)], axis=2).reshape(depth, 4, g * p)
    cr_, ci_ = power(jnp.asarray([ch, 2 * ch, 4 * ch], F32))
    ach = jnp.stack([cr_, ci_], axis=3).reshape(depth, 12, g * p)
    fr, fi = power(jnp.arange(1, ch + 1, dtype=F32))
    fr = jnp.stack([fr[:, 0], fr[:, 1, ::-1]], axis=1)
    fi = jnp.stack([fi[:, 0], fi[:, 1, ::-1]], axis=1)
    pfix = jnp.stack([fr, fi], axis=2).reshape(depth, 4, ch, g * p)
    return bmat, cmat, a1, ach, pfix


def _s5(u, h0re, h0im, tabs, d_skip, w_glu, b_glu, layer):
    b, nh, seq, _ = u.shape
    w = nh * LANES
    lay = lambda a: pl.BlockSpec((None,) + a.shape[1:], lambda i: (layer,) + (0,) * (a.ndim - 1))
    return pl.pallas_call(
        _s5_kernel,
        out_shape=[jax.ShapeDtypeStruct((b, nh, seq, LANES), F32),
                   jax.ShapeDtypeStruct((b, 2, SSM_N), F32),
                   jax.ShapeDtypeStruct((b, 2, SSM_N), F32)],
        grid=(b,),
        in_specs=[
            pl.BlockSpec((1, nh, seq, LANES), lambda i: (i, 0, 0, 0)),
            pl.BlockSpec((1, 2, SSM_N), lambda i: (i, 0, 0)),
            pl.BlockSpec((1, 2, SSM_N), lambda i: (i, 0, 0)),
            lay(tabs[0]), lay(tabs[1]), lay(tabs[2]), lay(tabs[3]), lay(tabs[4]),
            pl.BlockSpec((None, 1, w), lambda i: (layer, 0, 0)),
            pl.BlockSpec((None, w, w), lambda i: (layer, 0, 0)),
            pl.BlockSpec((None, 1, w), lambda i: (layer, 0, 0)),
        ],
        out_specs=[pl.BlockSpec((1, nh, seq, LANES), lambda i: (i, 0, 0, 0)),
                   pl.BlockSpec((1, 2, SSM_N), lambda i: (i, 0, 0)),
                   pl.BlockSpec((1, 2, SSM_N), lambda i: (i, 0, 0))],
        scratch_shapes=[pltpu.VMEM((seq, w), BF16),
                        pltpu.VMEM((seq, w), F32),
                        pltpu.VMEM((2, SCAN_TILE, 2 * SSM_N), F32),
                        pltpu.VMEM((2, 2, SUBLANES, SSM_N), F32),
                        pltpu.VMEM((w, 4 * SSM_N), BF16),
                        pltpu.VMEM((4 * SSM_N, w), BF16)],
        compiler_params=_params(1),
        name="s5",
    )(u, h0re, h0im, *tabs, d_skip, w_glu, b_glu)


def _proj_out_kernel(oa_ref, ob_ref, oc_ref, w_ref, x_ref, mod_ref, g_ref, wr_ref, o_ref, h_ref, lg_ref, wbf_ref):
    @pl.when((pl.program_id(0) == 0) & (pl.program_id(1) == 0))
    def _():
        _cast_rows(wbf_ref, w_ref, 128)

    i1, i2 = NA_WIDTH, NA_WIDTH + DIFF_WIDTH
    for bi in range(x_ref.shape[0]):
        for r0 in range(0, x_ref.shape[1], ROW_TILE):
            rows = slice(r0, r0 + ROW_TILE)
            y = _dot(oa_ref[bi, rows, :].astype(BF16), wbf_ref[:i1, :])
            y += _dot(ob_ref[bi, rows, :].astype(BF16), wbf_ref[i1:i2, :])
            for hf in range(SSM_WIDTH // LANES):
                y += _dot(oc_ref[bi, hf, rows, :].astype(BF16), wbf_ref[i2 + hf * LANES:i2 + (hf + 1) * LANES, :])
            xn = x_ref[bi, rows, :] + mod_ref[0, 2:3, :] * y
            o_ref[bi, rows, :] = xn
            h = _modulated_norm(xn, g_ref[...], mod_ref[0, 3:4, :], mod_ref[0, 4:5, :])
            h_hi = h.astype(BF16)
            h_ref[bi, rows, :] = h_hi
            h_lo = (h - h_hi.astype(F32)).astype(BF16)
            t1 = _dot_nt(wr_ref[...], h_hi)
            t2 = _dot_nt(wr_ref[:N_EXPERTS, :], h_lo)
            lg_ref[bi, :, rows] = t1[:N_EXPERTS, :] + t1[N_EXPERTS:, :] + t2


def _proj_out(oa, ob, oc, w_out, x, mod, g_ffn, wr_t, layer):
    b, n, d = x.shape
    bb, tm = _step_tile(b, n, mod.shape[0], PROJ_OUT_CHUNKS)
    mod_map = (lambda i, j: (i, 0, 0)) if mod.shape[0] > 1 else (lambda i, j: (0, 0, 0))
    kw = w_out.shape[1]
    return pl.pallas_call(
        _proj_out_kernel,
        out_shape=[jax.ShapeDtypeStruct((b, n, d), F32),
                   jax.ShapeDtypeStruct((b, n, d), BF16),
                   jax.ShapeDtypeStruct((b, N_EXPERTS, n), F32)],
        grid=(b // bb, n // tm),
        in_specs=[
            pl.BlockSpec((bb, tm, NA_WIDTH), lambda i, j: (i, j, 0)),
            pl.BlockSpec((bb, tm, DIFF_WIDTH), lambda i, j: (i, j, 0)),
            pl.BlockSpec((bb, SSM_WIDTH // LANES, tm, LANES), lambda i, j: (i, 0, j, 0)),
            pl.BlockSpec((None, kw, d), lambda i, j: (layer, 0, 0), pipeline_mode=pl.Buffered(1)),
            pl.BlockSpec((bb, tm, d), lambda i, j: (i, j, 0)),
            pl.BlockSpec((1, 6, d), mod_map),
            pl.BlockSpec((None, 1, d), lambda i, j: (layer, 0, 0)),
            pl.BlockSpec((None, 2 * N_EXPERTS, d), lambda i, j: (layer, 0, 0)),
        ],
        out_specs=[pl.BlockSpec((bb, tm, d), lambda i, j: (i, j, 0)),
                   pl.BlockSpec((bb, tm, d), lambda i, j: (i, j, 0)),
                   pl.BlockSpec((bb, N_EXPERTS, tm), lambda i, j: (i, 0, j))],
        scratch_shapes=[pltpu.VMEM((kw, d), BF16)],
        compiler_params=_params(2),
        name="proj_out",
    )(oa, ob, oc, w_out, x, mod, g_ffn, wr_t)


def _excl_cumsum_lanes(m):
    e, n = m.shape
    blk = MXU_DIM
    nb = n // blk
    r = lax.broadcasted_iota(I32, (blk, blk), 0)
    c = lax.broadcasted_iota(I32, (blk, blk), 1)
    tri = jnp.where(r < c, 1.0, 0.0).astype(BF16)
    stacked = jnp.concatenate([m[:, k * blk:(k + 1) * blk] for k in range(nb)], axis=0).astype(BF16)
    within = _dot(stacked, tri)
    outs = []
    off = jnp.zeros((e, 1), F32)
    for k in range(nb):
        outs.append(within[k * e:(k + 1) * e, :] + off)
        off = off + jnp.sum(m[:, k * blk:(k + 1) * blk], axis=1, keepdims=True)
    return jnp.concatenate(outs, axis=1)


def _route_kernel(lg_ref, pos_ref, gate_ref, *, cap):
    b, e, n = lg_ref.shape
    lg = lg_ref[...]
    mx = lg.max(axis=1, keepdims=True)
    ex = jnp.exp(lg - mx)
    aff = (ex / ex.sum(axis=1, keepdims=True)).reshape(b * e, n)

    def search(it, cur):
        cand = cur | (jnp.int32(1) << (jnp.int32(30) - it))
        cnt = jnp.sum(jnp.where(aff >= pltpu.bitcast(cand, F32), 1.0, 0.0), axis=1, keepdims=True)
        return jnp.where(cnt >= cap, cand, cur)

    thr = pltpu.bitcast(lax.fori_loop(0, 31, search, jnp.zeros((b * e, 1), I32)), F32)
    gt = aff > thr
    eq = aff == thr
    need = cap - jnp.sum(jnp.where(gt, 1.0, 0.0), axis=1, keepdims=True)
    eq_rank = _excl_cumsum_lanes(jnp.where(eq, 1.0, 0.0))
    sel = gt | (eq & (eq_rank < need))
    slot = _excl_cumsum_lanes(jnp.where(sel, 1.0, 0.0))
    pos = jnp.where(sel, slot, -1.0).astype(I32)
    gate = jnp.where(sel, aff, 0.0)
    for i in range(b):
        pos_ref[:, i, 0, :] = pos[i * e:(i + 1) * e, :]
        gate_ref[:, i, 0, :] = gate[i * e:(i + 1) * e, :]


def _route(logits):
    b, e, n = logits.shape
    cap = EC_CAPACITY_FACTOR * n // N_EXPERTS
    return pl.pallas_call(
        functools.partial(_route_kernel, cap=cap),
        out_shape=[jax.ShapeDtypeStruct((e, b, 1, n), I32),
                   jax.ShapeDtypeStruct((e, b, 1, n), F32)],
        compiler_params=_params(0),
        name="route",
    )(logits)


def _one_hot_rows(pos_row, cap):
    n = pos_row.shape[1]
    slot = lax.broadcasted_iota(I32, (cap, n), 0)
    return slot == pos_row


def _gather_kernel(h_ref, pos_ref, gate_ref, xs_ref, gs_ref, *, cap):
    h = h_ref[0]
    group = max(1, GATHER_ROWS // cap)
    for e0 in range(0, N_EXPERTS, group):
        ohs = [_one_hot_rows(pos_ref[e, 0], cap) for e in range(e0, e0 + group)]
        xs = _dot(jnp.concatenate([jnp.where(oh, 1.0, 0.0).astype(BF16) for oh in ohs], axis=0), h).astype(BF16)
        for i, oh in enumerate(ohs):
            e = e0 + i
            xs_ref[e] = xs[i * cap:(i + 1) * cap, :]
            g = jnp.sum(jnp.where(oh, gate_ref[e, 0], 0.0), axis=1, keepdims=True)
            gs_ref[e] = jnp.broadcast_to(g, (cap, LANES))


def _gather(h, pos, gate):
    b, n, d = h.shape
    cap = EC_CAPACITY_FACTOR * n // N_EXPERTS
    return pl.pallas_call(
        functools.partial(_gather_kernel, cap=cap),
        out_shape=[jax.ShapeDtypeStruct((N_EXPERTS, b * cap, d), BF16),
                   jax.ShapeDtypeStruct((N_EXPERTS, b * cap, LANES), F32)],
        grid=(b,),
        in_specs=[
            pl.BlockSpec((1, n, d), lambda i: (i, 0, 0)),
            pl.BlockSpec((N_EXPERTS, 1, 1, n), lambda i: (0, i, 0, 0)),
            pl.BlockSpec((N_EXPERTS, 1, 1, n), lambda i: (0, i, 0, 0)),
        ],
        out_specs=[pl.BlockSpec((N_EXPERTS, cap, d), lambda i: (0, i, 0)),
                   pl.BlockSpec((N_EXPERTS, cap, LANES), lambda i: (0, i, 0))],
        compiler_params=_params(1),
        name="moe_gather",
    )(h, pos, gate)


def _ffn_kernel(xp_ref, gp_ref, xs_ref, gs_ref, wg_ref, wu_ref, wd_ref, yp_ref, ys_ref, acc_ref):
    j = pl.program_id(1)
    sp = xp_ref.shape[1]

    @pl.when((pl.program_id(0) == 0) & (j == 0))
    def _():
        acc_ref[...] = jnp.zeros_like(acc_ref)

    wg = wg_ref[...].astype(BF16)
    wu = wu_ref[...].astype(BF16)
    wd = wd_ref[...].astype(BF16)
    for x_ref, g_ref, y_ref, rows in ((xp_ref, gp_ref, yp_ref, slice(0, sp)),
                                      (xs_ref, gs_ref, ys_ref, slice(sp, acc_ref.shape[0]))):
        x = x_ref[0]
        a = _dot(x, wg)
        u = _dot(x, wu)
        hid = (a * jax.nn.sigmoid(a)) * u
        acc = jnp.where(j > 0, acc_ref[rows, :], 0.0) + _dot(hid.astype(BF16), wd)
        acc_ref[rows, :] = acc
        y_ref[0] = (acc * g_ref[0][:, 0:1]).astype(BF16)


def _ffn(xp, gp, xs, gs, w_gate, w_up, w_down, layer):
    e, sp, d = xp.shape
    ss = xs.shape[1]
    s = sp + ss
    ff = w_gate.shape[-1]
    tf = FF_TILE
    return pl.pallas_call(
        _ffn_kernel,
        out_shape=[jax.ShapeDtypeStruct((e, sp, d), BF16), jax.ShapeDtypeStruct((e, ss, d), BF16)],
        grid=(e, ff // tf),
        in_specs=[
            pl.BlockSpec((1, sp, d), lambda i, j: (i, 0, 0)),
            pl.BlockSpec((1, sp, LANES), lambda i, j: (i, 0, 0)),
            pl.BlockSpec((1, ss, d), lambda i, j: (i, 0, 0)),
            pl.BlockSpec((1, ss, LANES), lambda i, j: (i, 0, 0)),
            pl.BlockSpec((None, None, d, tf), lambda i, j: (layer, i, 0, j)),
            pl.BlockSpec((None, None, d, tf), lambda i, j: (layer, i, 0, j)),
            pl.BlockSpec((None, None, tf, d), lambda i, j: (layer, i, j, 0)),
        ],
        out_specs=[pl.BlockSpec((1, sp, d), lambda i, j: (i, 0, 0)),
                   pl.BlockSpec((1, ss, d), lambda i, j: (i, 0, 0))],
        scratch_shapes=[pltpu.VMEM((s, d), F32)],
        compiler_params=_params(2),
        name="moe_ffn",
    )(xp, gp, xs, gs, w_gate, w_up, w_down)


def _combine_kernel(ye_ref, pos_ref, x_ref, mod_ref, *rest, cap, final):
    if final:
        g_ref, o_ref = rest
    else:
        (o_ref,) = rest
    group = max(1, MXU_DIM // cap)
    y = None
    for e0 in range(0, N_EXPERTS, group):
        oh = jnp.concatenate([jnp.where(_one_hot_rows(pos_ref[e, 0], cap), 1.0, 0.0).astype(BF16)
                              for e in range(e0, e0 + group)], axis=0)
        ye = jnp.concatenate([ye_ref[e] for e in range(e0, e0 + group)], axis=0)
        part = _dot_tn(oh, ye)
        y = part if y is None else y + part
    xn = x_ref[0] + mod_ref[0, 5:6, :] * y
    if final:
        ms = jnp.mean(xn * xn, axis=-1, keepdims=True)
        xn = xn * lax.rsqrt(ms + EPS) * g_ref[...]
    o_ref[0] = xn


def _combine(ye, pos, x, mod, final_norm):
    b, n, d = x.shape
    cap = EC_CAPACITY_FACTOR * n // N_EXPERTS
    tn = min(n, COMBINE_TILE)
    bm = mod.shape[0]
    mod_map = (lambda i, j: (i, 0, 0)) if bm > 1 else (lambda i, j: (0, 0, 0))
    final = final_norm is not None
    in_specs = [
        pl.BlockSpec((N_EXPERTS, cap, d), lambda i, j: (0, i, 0)),
        pl.BlockSpec((N_EXPERTS, 1, 1, tn), lambda i, j: (0, i, 0, j)),
        pl.BlockSpec((1, tn, d), lambda i, j: (i, j, 0)),
        pl.BlockSpec((1, 6, d), mod_map),
    ]
    args = [ye, pos, x, mod]
    if final:
        in_specs.append(pl.BlockSpec((1, d), lambda i, j: (0, 0)))
        args.append(final_norm)
    return pl.pallas_call(
        functools.partial(_combine_kernel, cap=cap, final=final),
        out_shape=jax.ShapeDtypeStruct((b, n, d), F32),
        grid=(b, n // tn),
        in_specs=in_specs,
        out_specs=pl.BlockSpec((1, tn, d), lambda i, j: (i, j, 0)),
        compiler_params=_params(2),
        name="moe_combine",
    )(*args)


def _rope_tables(n):
    t = np.arange(n)
    row = (t // GRID_W).astype(np.float32)
    col = (t % GRID_W).astype(np.float32)
    nf = DIFF_QK // 4
    inv = np.float32(ROPE_BASE) ** (-np.arange(nf, dtype=np.float32) / np.float32(nf))
    lane = np.arange(DIFF_WIDTH)
    pos = np.where(((lane % DIFF_QK) < DIFF_QK // 2)[None, :], row[:, None], col[:, None])
    ang = (pos * inv[lane % nf][None, :]).astype(np.float32)
    first = (lane % (2 * nf)) < nf
    cos, sin = np.cos(ang).astype(np.float32), np.sin(ang).astype(np.float32)
    return jnp.asarray(cos), jnp.asarray(np.where(first[None, :], -sin, sin))


def kernel(x_prompt, x_sample, cache_na_k, cache_na_v, cache_diff_k, cache_diff_v, state_ssm_re, state_ssm_im,
           c, c_ctx, w_ada, b_ada, norm_mix, norm_ffn, w_in, w_out, na_rpb, diff_lambda, diff_subln,
           ssm_a_re, ssm_a_im, ssm_log_dt, ssm_b_re, ssm_b_im, ssm_c_re, ssm_c_im, ssm_d, ssm_w_glu, ssm_b_glu,
           w_router, w_gate, w_up, w_down, final_norm):
    depth = w_in.shape[0]
    bp, sp, d = x_prompt.shape
    bs, ss, _ = x_sample.shape
    assert d == D_MODEL and bs + 1 <= SUBLANES
    past = cache_na_k.shape[2]

    cond = jnp.zeros((SUBLANES, d), F32).at[0].set(c_ctx).at[1:1 + bs].set(c)
    mods = _ada(cond, w_ada, b_ada).reshape(depth, SUBLANES, 6, d)

    rope_tabs = _rope_tables(ss)
    kc_a = cache_na_k.reshape(bs, depth, past, NA_WIDTH)
    vc_a = cache_na_v.reshape(bs, depth, past, NA_WIDTH)
    kc_b = cache_diff_k.reshape(bs, depth, past, DIFF_WIDTH)
    vc_b = cache_diff_v.reshape(bs, depth, past, DIFF_WIDTH)
    subln = jnp.tile(diff_subln, (1, LANES // DIFF_V)).reshape(depth, 1, LANES)
    norm_mix = norm_mix.reshape(depth, 1, d)
    norm_ffn = norm_ffn.reshape(depth, 1, d)
    ssm_d = ssm_d.reshape(depth, 1, SSM_WIDTH)
    ssm_b_glu = ssm_b_glu.reshape(depth, 1, SSM_WIDTH)
    wr_t = jnp.swapaxes(w_router, 1, 2).astype(F32)
    wr_hi = wr_t.astype(BF16)
    wr_t = jnp.concatenate([wr_hi, (wr_t - wr_hi.astype(F32)).astype(BF16)], axis=1)
    fnorm = final_norm.reshape(1, d)
    zero_state = jnp.zeros((bp, 2, SSM_N), F32)
    tabs = _s5_tables(ssm_a_re, ssm_a_im, ssm_log_dt, ssm_b_re, ssm_b_im, ssm_c_re, ssm_c_im)
    bias_tab = _na_bias_table(na_rpb)

    xp, xs = x_prompt, x_sample
    new_sre, new_sim = [], []
    caches = None
    for l in range(depth):
        lam_init = 0.8 - 0.6 * math.exp(-0.3 * l)
        mod_p = mods[l, 0:1]
        mod_s = mods[l, 1:1 + bs]
        diff_params = (diff_lambda, subln)

        qa, ka, va, qb, kb, vb, u = _proj_in(xp, mod_p, norm_mix, w_in, l, None, depth, caches)
        caches = (ka, va, kb, vb)
        o_a = _attention(qa, ka, va, l, None, l, None, lam_init)
        o_b = _attention(qb, kb, vb, l, None, l, diff_params, lam_init)
        o_c, fre, fim = _s5(u, zero_state, zero_state, tabs, ssm_d, ssm_w_glu, ssm_b_glu, l)
        xp, hp, lg_p = _proj_out(o_a, o_b, o_c, w_out, xp, mod_p, norm_ffn, wr_t, l)
        new_sre.append(fre.reshape(bp, 2, SSM_GROUPS, SSM_STATE))
        new_sim.append(fim.reshape(bp, 2, SSM_GROUPS, SSM_STATE))

        qa, ka, va, qb, kb, vb, u = _proj_in(xs, mod_s, norm_mix, w_in, l, rope_tabs, 0, None)
        o_a = _neighbourhood_attention(qa, ka, va, kc_a, vc_a, bias_tab, l)
        o_b = _attention(qb, kb, vb, 0, (kc_b, vc_b), l, diff_params, lam_init)
        h0re = state_ssm_re[:, l].reshape(bs, 2, SSM_N)
        h0im = state_ssm_im[:, l].reshape(bs, 2, SSM_N)
        o_c, _, _ = _s5(u, h0re, h0im, tabs, ssm_d, ssm_w_glu, ssm_b_glu, l)
        xs, hs, lg_s = _proj_out(o_a, o_b, o_c, w_out, xs, mod_s, norm_ffn, wr_t, l)

        pos_p, gate_p = _route(lg_p)
        pos_s, gate_s = _route(lg_s)
        xg_p, gs_p = _gather(hp, pos_p, gate_p)
        xg_s, gs_s = _gather(hs, pos_s, gate_s)
        ye_p, ye_s = _ffn(xg_p, gs_p, xg_s, gs_s, w_gate, w_up, w_down, l)
        last = l == depth - 1
        xp = _combine(ye_p, pos_p, xp, mod_p, fnorm if last else None)
        xs = _combine(ye_s, pos_s, xs, mod_s, fnorm if last else None)

    ka, va, kb, vb = caches
    return (xp, xs, ka.reshape(bp, depth, sp, NA_HEADS, HEAD_DIM), va.reshape(bp, depth, sp, NA_HEADS, HEAD_DIM),
            kb.reshape(bp, depth, sp, DIFF_HEADS, DIFF_V), vb.reshape(bp, depth, sp, DIFF_HEADS, DIFF_V),
            jnp.stack(new_sre, axis=1), jnp.stack(new_sim, axis=1))
```

```python
import functools
import math

import jax
import jax.numpy as jnp
import numpy as np
from jax import lax
from jax.experimental import pallas as pl
from jax.experimental.pallas import tpu as pltpu

F32 = jnp.float32
BF16 = jnp.bfloat16
I32 = jnp.int32

D_MODEL = 1024
GRID_W = 64
HEAD_DIM = 64
NA_HEADS = 8
NA_WIDTH = NA_HEADS * HEAD_DIM
WIN_R = 8
WIN_C = 16
DIFF_HEADS = 4
DIFF_QK = 32
DIFF_V = 64
DIFF_WIDTH = DIFF_HEADS * DIFF_V
SSM_GROUPS = 16
SSM_GROUP_CH = 16
SSM_WIDTH = SSM_GROUPS * SSM_GROUP_CH
SSM_STATE = 64
SSM_N = SSM_GROUPS * SSM_STATE
IN_WIDTH = 3 * NA_WIDTH + 3 * DIFF_WIDTH + SSM_WIDTH
N_EXPERTS = 16
EXPERT_FF = 2048
EC_CAPACITY_FACTOR = 2
ROPE_BASE = 10000.0
EPS = 1e-6
NEG_INF = -1e30
LOG2E = math.log2(math.e)

LANES = 128
SUBLANES = 8
MXU_DIM = 256
VMEM_LIMIT_CAP = 60000 * 1024

ROW_TILE = 256
PROJ_IN_CHUNKS = 4
PROJ_OUT_CHUNKS = 2
ATTN_Q_TILE = 512
ATTN_REQUESTS_PER_STEP = 4
NA_QROWS = 4
NA_KROWS = 12
SCAN_TILE = 512
FF_TILE = 1024
COMBINE_TILE = 512
GATHER_ROWS = 512
DEN_IN_MATMUL_MIN_KEYS = 1024


def _params(ndims):
    return pltpu.CompilerParams(
        dimension_semantics=("arbitrary",) * ndims,
        vmem_limit_bytes=VMEM_LIMIT_CAP,
    )


def _step_tile(b, n, n_mod, chunks):
    tm = min(n, ROW_TILE * chunks)
    bb = max(1, ROW_TILE * chunks // tm)
    if n_mod > 1 or b % bb:
        bb = 1
    return bb, tm


def _dot(a, b):
    return jnp.dot(a, b, preferred_element_type=F32)


def _dot_nt(a, b):
    return lax.dot_general(a, b, (((1,), (1,)), ((), ())), preferred_element_type=F32)


def _dot_tn(a, b):
    return lax.dot_general(a, b, (((0,), (0,)), ((), ())), preferred_element_type=F32)


def _cast_rows(dst_ref, src_ref, rows):
    n = src_ref.shape[0]
    for r in range(0, n, rows):
        dst_ref[r:r + rows, :] = src_ref[r:r + rows, :].astype(BF16)


def _modulated_norm(x, g, shift, scale):
    ms = jnp.mean(x * x, axis=-1, keepdims=True)
    return (x * lax.rsqrt(ms + EPS) * g) * (1.0 + scale) + shift


def _ada_kernel(c_ref, w_ref, b_ref, o_ref):
    c = c_ref[...]
    s = c * jax.nn.sigmoid(c)
    o_ref[0] = _dot(s.astype(BF16), w_ref[0].astype(BF16)) + b_ref[0]


def _ada(cond, w_ada, b_ada):
    depth = w_ada.shape[0]
    tn = 1536
    nt = 6 * D_MODEL // tn
    return pl.pallas_call(
        _ada_kernel,
        out_shape=jax.ShapeDtypeStruct((depth, SUBLANES, 6 * D_MODEL), F32),
        grid=(depth, nt),
        in_specs=[
            pl.BlockSpec((SUBLANES, D_MODEL), lambda l, j: (0, 0)),
            pl.BlockSpec((1, D_MODEL, tn), lambda l, j: (l, 0, j)),
            pl.BlockSpec((1, 1, tn), lambda l, j: (l, 0, j)),
        ],
        out_specs=pl.BlockSpec((1, SUBLANES, tn), lambda l, j: (l, 0, j)),
        compiler_params=_params(2),
        name="ada",
    )(cond, w_ada, b_ada.reshape(depth, 1, 6 * D_MODEL))


def _rope_apply(x, cos, sin_signed):
    lane = lax.broadcasted_iota(I32, (1, LANES), 1)
    first = (lane % 16) < 8
    outs = []
    for t in range(x.shape[1] // LANES):
        xt = x[:, t * LANES:(t + 1) * LANES]
        partner = jnp.where(first, pltpu.roll(xt, LANES - 8, axis=1), pltpu.roll(xt, 8, axis=1))
        outs.append(xt * cos[:, t * LANES:(t + 1) * LANES] + partner * sin_signed[:, t * LANES:(t + 1) * LANES])
    return jnp.concatenate(outs, axis=1)


def _proj_in_kernel(*refs, rope, n_alias):
    x_ref, mod_ref, g_ref, w_ref = refs[:4]
    pos = 4
    if rope:
        cos_ref, sin_ref = refs[pos:pos + 2]
        pos += 2
    pos += n_alias
    qa_ref, ka_ref, va_ref, qb_ref, kb_ref, vb_ref, u_ref, wbf_ref = refs[pos:pos + 8]

    @pl.when((pl.program_id(0) == 0) & (pl.program_id(1) == 0))
    def _():
        _cast_rows(wbf_ref, w_ref, 128)

    i1, i2, i3 = NA_WIDTH, 2 * NA_WIDTH, 3 * NA_WIDTH
    i4, i5, i6 = i3 + DIFF_WIDTH, i3 + 2 * DIFF_WIDTH, i3 + 3 * DIFF_WIDTH
    for bi in range(x_ref.shape[0]):
        for r0 in range(0, x_ref.shape[1], ROW_TILE):
            rows = slice(r0, r0 + ROW_TILE)

            def put(ref, val):
                for s in range(ref.shape[1]):
                    ref[bi, s, rows, :] = val.astype(ref.dtype)

            h = _modulated_norm(x_ref[bi, rows, :], g_ref[...], mod_ref[0, 0:1, :], mod_ref[0, 1:2, :])
            z = _dot(h.astype(BF16), wbf_ref[...])
            qa_ref[bi, rows, :] = (z[:, :i1] * (HEAD_DIM ** -0.5 * LOG2E)).astype(qa_ref.dtype)
            put(ka_ref, z[:, i1:i2])
            put(va_ref, z[:, i2:i3])
            qb = z[:, i3:i4]
            kb = z[:, i4:i5]
            if rope:
                qb = _rope_apply(qb, cos_ref[rows, :], sin_ref[rows, :])
                kb = _rope_apply(kb, cos_ref[rows, :], sin_ref[rows, :])
            qb_ref[bi, rows, :] = (qb * (DIFF_QK ** -0.5 * LOG2E)).astype(qb_ref.dtype)
            put(kb_ref, kb)
            put(vb_ref, z[:, i5:i6])
            for hf in range(SSM_WIDTH // LANES):
                u_ref[bi, hf, rows, :] = z[:, i6 + hf * LANES:i6 + (hf + 1) * LANES]


def _proj_in(x, mod, g_norm, w_in, layer, rope_tabs, cache_slots, caches):
    b, n, d = x.shape
    bb, tm = _step_tile(b, n, mod.shape[0], PROJ_IN_CHUNKS)
    rope = rope_tabs is not None
    mod_map = (lambda i, j: (i, 0, 0)) if mod.shape[0] > 1 else (lambda i, j: (0, 0, 0))
    in_specs = [
        pl.BlockSpec((bb, tm, d), lambda i, j: (i, j, 0)),
        pl.BlockSpec((1, 6, d), mod_map),
        pl.BlockSpec((None, 1, d), lambda i, j: (layer, 0, 0)),
        pl.BlockSpec((None, d, IN_WIDTH), lambda i, j: (layer, 0, 0), pipeline_mode=pl.Buffered(1)),
    ]
    args = [x, mod, g_norm, w_in]
    if rope:
        in_specs += [pl.BlockSpec((tm, DIFF_WIDTH), lambda i, j: (j, 0))] * 2
        args += list(rope_tabs)
    aliases = {}
    kv_out = (1, 2, 4, 5)
    if caches is not None:
        for c, o in zip(caches, kv_out):
            aliases[len(args)] = o
            in_specs.append(pl.BlockSpec(memory_space=pl.ANY))
            args.append(c)
    widths = (NA_WIDTH, NA_WIDTH, NA_WIDTH, DIFF_WIDTH, DIFF_WIDTH, DIFF_WIDTH)
    nslab = SSM_WIDTH // LANES
    out_shape, out_specs = [], []
    for o, w in enumerate(widths):
        if o not in kv_out:
            out_shape.append(jax.ShapeDtypeStruct((b, n, w), BF16))
            out_specs.append(pl.BlockSpec((bb, tm, w), lambda i, j: (i, j, 0)))
        elif cache_slots == 0:
            out_shape.append(jax.ShapeDtypeStruct((b, 1, n, w), BF16))
            out_specs.append(pl.BlockSpec((bb, 1, tm, w), lambda i, j: (i, 0, j, 0)))
        elif caches is None:
            out_shape.append(jax.ShapeDtypeStruct((b, cache_slots, n, w), F32))
            out_specs.append(pl.BlockSpec((bb, cache_slots, tm, w), lambda i, j: (i, 0, j, 0)))
        else:
            out_shape.append(jax.ShapeDtypeStruct((b, cache_slots, n, w), F32))
            out_specs.append(pl.BlockSpec((bb, 1, tm, w), lambda i, j: (i, layer, j, 0)))
    out_shape.append(jax.ShapeDtypeStruct((b, nslab, n, LANES), F32))
    out_specs.append(pl.BlockSpec((bb, nslab, tm, LANES), lambda i, j: (i, 0, j, 0)))
    return pl.pallas_call(
        functools.partial(_proj_in_kernel, rope=rope, n_alias=len(aliases)),
        out_shape=out_shape,
        grid=(b // bb, n // tm),
        in_specs=in_specs,
        out_specs=out_specs,
        scratch_shapes=[pltpu.VMEM((d, IN_WIDTH), BF16)],
        input_output_aliases=aliases,
        compiler_params=_params(2),
        name="proj_in",
    )(*args)


def _masked_query(qp, mask):
    return jnp.where(mask, qp, jnp.zeros_like(qp))


def _with_ones(v):
    return jnp.concatenate([v, jnp.ones_like(v)], axis=1)


def _softmax_av_wide(qm, ks, vs, biases):
    ss = []
    for k, bias in zip(ks, biases):
        s = _dot_nt(qm, k)
        if bias is not None:
            s = bias(s)
        ss.append(s)
    m = ss[0].max(axis=-1, keepdims=True)
    for s in ss[1:]:
        m = jnp.maximum(m, s.max(axis=-1, keepdims=True))
    acc = None
    for s, v in zip(ss, vs):
        pv = _dot(jnp.exp2(s - m).astype(BF16), v)
        acc = pv if acc is None else acc + pv
    return acc[:, :LANES] / acc[:, LANES:]


def _softmax_av(qm, ks, vs, biases):
    ss = []
    for k, bias in zip(ks, biases):
        s = _dot_nt(qm, k)
        if bias is not None:
            s = bias(s)
        ss.append(s)
    m = ss[0].max(axis=-1, keepdims=True)
    for s in ss[1:]:
        m = jnp.maximum(m, s.max(axis=-1, keepdims=True))
    acc = None
    den = None
    for s, v in zip(ss, vs):
        p = jnp.exp2(s - m)
        psum = p.sum(axis=-1, keepdims=True)
        den = psum if den is None else den + psum
        pv = _dot(p.astype(BF16), v)
        acc = pv if acc is None else acc + pv
    return acc / den


def _diff_lambda(lam_ref, lam_init):
    lp = lam_ref[...]
    a = jnp.sum(lp[0:1] * lp[1:2], axis=-1, keepdims=True)
    b = jnp.sum(lp[2:3] * lp[3:4], axis=-1, keepdims=True)
    return jnp.exp(a) - jnp.exp(b) + lam_init


def _attn_kernel(*refs, has_ctx, diff, lam_init):
    refs = list(refs)
    q_ref, k_ref, v_ref = refs[:3]
    pos = 3
    if has_ctx:
        kc_ref, vc_ref = refs[pos:pos + 2]
        pos += 2
    if diff:
        lam_ref, subln_ref = refs[pos:pos + 2]
        pos += 2
    o_ref = refs[pos]
    width = q_ref.shape[2]
    lane = lax.broadcasted_iota(I32, (1, LANES), 1)
    if diff:
        lam = _diff_lambda(lam_ref, lam_init)
    for bi, p in [(bi, p) for bi in range(q_ref.shape[0]) for p in range(width // LANES)]:
        sl = slice(p * LANES, (p + 1) * LANES)
        qp = q_ref[bi, :, sl]
        ks = [k_ref[bi, :, sl].astype(BF16)]
        vs = [v_ref[bi, :, sl].astype(BF16)]
        if has_ctx:
            ks.append(kc_ref[:, sl].astype(BF16))
            vs.append(vc_ref[:, sl].astype(BF16))
        nob = [None] * len(ks)
        tq = qp.shape[0]
        wide = sum(k.shape[0] for k in ks) >= DEN_IN_MATMUL_MIN_KEYS
        if wide:
            vs = [_with_ones(v) for v in vs]
        pair = None
        for sub in range(LANES // HEAD_DIM):
            head_mask = (lane // HEAD_DIM) == sub
            if diff:
                qmasks = [(lane // DIFF_QK) == (2 * sub + half) for half in range(2)]
                av = _softmax_av_wide if wide else _softmax_av
                os_ = [av(_masked_query(qp, mk), ks, vs, nob) for mk in qmasks]
                o = jnp.where(head_mask, os_[0] - lam * os_[1], 0.0)
                ms = jnp.sum(o * o, axis=-1, keepdims=True) * (1.0 / DIFF_V)
                o = (o * lax.rsqrt(ms + EPS) * subln_ref[...]) * (1.0 - lam_init)
            else:
                qm = _masked_query(qp, head_mask)
                o = _softmax_av_wide(qm, ks, vs, nob) if wide else _softmax_av(qm, ks, vs, nob)
                o = jnp.where(head_mask, o, 0.0)
            pair = o if pair is None else pair + o
        o_ref[bi, :, sl] = pair.astype(o_ref.dtype)


def _attention(q, k, v, kv_slot, ctx_kv, layer, diff_params, lam_init):
    b, nq, w = q.shape
    nk = k.shape[2]
    tq = min(nq, ATTN_Q_TILE)
    has_ctx = ctx_kv is not None
    diff = diff_params is not None
    bb = ATTN_REQUESTS_PER_STEP if (nq == tq and not has_ctx and b % ATTN_REQUESTS_PER_STEP == 0) else 1
    in_specs = [
        pl.BlockSpec((bb, tq, w), lambda i, j: (i, j, 0)),
        pl.BlockSpec((bb, None, nk, w), lambda i, j: (i, kv_slot, 0, 0)),
        pl.BlockSpec((bb, None, nk, w), lambda i, j: (i, kv_slot, 0, 0)),
    ]
    args = [q, k, v]
    nc = 0
    if has_ctx:
        nc = ctx_kv[0].shape[2]
        in_specs += [pl.BlockSpec((None, None, nc, w), lambda i, j: (i, layer, 0, 0))] * 2
        args += list(ctx_kv)
    if diff:
        lam_p, subln = diff_params
        in_specs += [
            pl.BlockSpec((None, 4, DIFF_QK), lambda i, j: (layer, 0, 0)),
            pl.BlockSpec((None, 1, LANES), lambda i, j: (layer, 0, 0)),
        ]
        args += [lam_p, subln]
    return pl.pallas_call(
        functools.partial(_attn_kernel, has_ctx=has_ctx, diff=diff, lam_init=lam_init),
        out_shape=jax.ShapeDtypeStruct((b, nq, w), BF16),
        grid=(b // bb, nq // tq),
        in_specs=in_specs,
        out_specs=pl.BlockSpec((bb, tq, w), lambda i, j: (i, j, 0)),
        compiler_params=_params(2),
        name="attn_diff" if diff else "attn",
    )(*args)


def _na_kernel(q_ref, k_ref, v_ref, kc_ref, vc_ref, bias_ref, o_ref, *, rows):
    i = pl.program_id(1)
    tq = NA_QROWS * GRID_W
    nkw = NA_KROWS * GRID_W
    ustart = jnp.clip(NA_QROWS * i - WIN_R // 2, 0, rows - NA_KROWS)
    koff = pl.multiple_of(ustart * GRID_W, GRID_W)
    delta = ustart - NA_QROWS * i
    qrow = NA_QROWS * i + lax.broadcasted_iota(I32, (tq, nkw), 0) // GRID_W
    krow = ustart + lax.broadcasted_iota(I32, (tq, nkw), 1) // GRID_W
    wstart = jnp.clip(qrow - WIN_R // 2, 0, rows - WIN_R)
    row_ok = (krow >= wstart) & (krow < wstart + WIN_R)
    nsub = LANES // HEAD_DIM
    lane = lax.broadcasted_iota(I32, (1, LANES), 1)
    for p in range(NA_WIDTH // LANES):
        sl = slice(p * LANES, (p + 1) * LANES)
        qp = q_ref[0, :, sl]
        ks = [k_ref[0, pl.ds(koff, nkw), sl].astype(BF16), kc_ref[:, sl].astype(BF16)]
        vs = [_with_ones(v_ref[0, pl.ds(koff, nkw), sl].astype(BF16)), _with_ones(vc_ref[:, sl].astype(BF16))]
        outs = []
        for sub in range(nsub):
            def win_bias(s, h=p * nsub + sub):
                rows_ = []
                for ri in range(NA_QROWS):
                    tiles = []
                    for pj in range(NA_KROWS // 2):
                        dr = delta + 2 * pj - ri
                        idx = jnp.clip(dr, -WIN_R, WIN_R - 1) + WIN_R
                        tiles.append(bias_ref[h, idx])
                    rows_.append(jnp.concatenate(tiles, axis=1))
                bias = jnp.concatenate(rows_, axis=0)
                return jnp.where(row_ok, s + bias, NEG_INF)

            outs.append(_softmax_av_wide(_masked_query(qp, (lane // HEAD_DIM) == sub), ks, vs, [win_bias, None]))
        pair = jnp.where((lane // HEAD_DIM) == 0, outs[0], outs[1])
        o_ref[0, :, sl] = pair.astype(o_ref.dtype)


def _na_table_kernel(v_ref, o_ref):
    nd = v_ref.shape[0]
    q = lax.broadcasted_iota(I32, (GRID_W, 2 * GRID_W), 0)
    k = lax.broadcasted_iota(I32, (GRID_W, 2 * GRID_W), 1)
    kk = k % GRID_W
    cstart = jnp.clip(q - WIN_C // 2, 0, GRID_W - WIN_C)
    col_ok = (kk >= cstart) & (kk < cstart + WIN_C)
    left = k < GRID_W
    tiles = []
    for dr in range(nd):
        x = jnp.broadcast_to(v_ref[dr:dr + 1, :], (GRID_W, 2 * GRID_W))
        tiles.append((pltpu.roll(x, GRID_W + 1, axis=1, stride=1, stride_axis=0),
                      pltpu.roll(x, 1, axis=1, stride=1, stride_axis=0)))
    zero = jnp.zeros((GRID_W, 2 * GRID_W), F32)
    for j in range(nd + 1):
        lt = tiles[j - 1][0] if j >= 1 else zero
        rt = tiles[j][1] if j < nd else zero
        o_ref[j] = jnp.where(col_ok, jnp.where(left, lt, rt) * LOG2E, NEG_INF)


def _na_bias_table(rpb):
    depth, h, nd, nc = rpb.shape
    assert 2 * GRID_W == LANES and nd == 2 * WIN_R - 1 and nc == 2 * WIN_C - 1
    rpb = rpb.astype(F32)
    lo = GRID_W - WIN_C
    v = jnp.concatenate([jnp.broadcast_to(rpb[..., :1], rpb.shape[:3] + (lo,)), rpb,
                         jnp.broadcast_to(rpb[..., -1:], rpb.shape[:3] + (2 * GRID_W - lo - nc,))], axis=-1)
    return pl.pallas_call(
        _na_table_kernel,
        out_shape=jax.ShapeDtypeStruct((depth, h, nd + 1, GRID_W, 2 * GRID_W), F32),
        grid=(depth, h),
        in_specs=[pl.BlockSpec((None, None, nd, 2 * GRID_W), lambda l, i: (l, i, 0, 0))],
        out_specs=pl.BlockSpec((None, None, nd + 1, GRID_W, 2 * GRID_W), lambda l, i: (l, i, 0, 0, 0)),
        compiler_params=_params(2),
        name="na_table",
    )(v)


def _neighbourhood_attention(q, k, v, kc, vc, bias_tab, layer):
    b, n, w = q.shape
    rows = n // GRID_W
    tq = NA_QROWS * GRID_W
    nc = kc.shape[2]
    nkw = NA_KROWS * GRID_W
    return pl.pallas_call(
        functools.partial(_na_kernel, rows=rows),
        out_shape=jax.ShapeDtypeStruct((b, n, w), BF16),
        grid=(b, n // tq),
        in_specs=[
            pl.BlockSpec((1, tq, w), lambda i, j: (i, j, 0)),
            pl.BlockSpec((1, None, n, w), lambda i, j: (i, 0, 0, 0)),
            pl.BlockSpec((1, None, n, w), lambda i, j: (i, 0, 0, 0)),
            pl.BlockSpec((None, None, nc, w), lambda i, j: (i, layer, 0, 0)),
            pl.BlockSpec((None, None, nc, w), lambda i, j: (i, layer, 0, 0)),
            pl.BlockSpec((None,) + bias_tab.shape[1:], lambda i, j: (layer, 0, 0, 0, 0)),
        ],
        out_specs=pl.BlockSpec((1, tq, w), lambda i, j: (i, j, 0)),
        compiler_params=_params(2),
        name="na_attn",
    )(q, k, v, kc, vc, bias_tab)


def _sublane_scan(er, ei, tr, ti, pw_ref, base, lanes, reverse):
    row = lax.broadcasted_iota(I32, (SUBLANES, LANES), 0)
    if reverse:
        first = row == SUBLANES - 1
        xr = jnp.where(first, tr, pltpu.roll(er, SUBLANES - 1, axis=0))
        xi = jnp.where(first, ti, pltpu.roll(ei, SUBLANES - 1, axis=0))
    else:
        first = row == 0
        xr = jnp.where(first, tr, pltpu.roll(er, 1, axis=0))
        xi = jnp.where(first, ti, pltpu.roll(ei, 1, axis=0))
    for k, s in enumerate((1, 2, 4)):
        ar = pw_ref[base + 2 * k:base + 2 * k + 1, lanes]
        ai = pw_ref[base + 2 * k + 1:base + 2 * k + 2, lanes]
        if reverse:
            keep = row < SUBLANES - s
            sr = jnp.where(keep, pltpu.roll(xr, SUBLANES - s, axis=0), 0.0)
            si = jnp.where(keep, pltpu.roll(xi, SUBLANES - s, axis=0), 0.0)
        else:
            keep = row >= s
            sr = jnp.where(keep, pltpu.roll(xr, s, axis=0), 0.0)
            si = jnp.where(keep, pltpu.roll(xi, s, axis=0), 0.0)
        xr, xi = xr + (ar * sr - ai * si), xi + (ar * si + ai * sr)
    return xr, xi


def _s5_kernel(u_ref, h0re_ref, h0im_ref, bt_ref, ct_ref, a1_ref, ach_ref, pfix_ref, d_ref, wglu_ref, bglu_ref,
               o_ref, fre_ref, fim_ref, up_ref, yp_ref, buf_ref, car_ref, bmat_ref, cmat_ref):
    seq = u_ref.shape[2]
    ch = pfix_ref.shape[1]
    tt = ch * SUBLANES
    ntile = seq // tt
    nlt = SSM_N // LANES
    nhalf = SSM_WIDTH // LANES

    @pl.when(pl.program_id(0) == 0)
    def _():
        g, p, hc = SSM_GROUPS, SSM_STATE, SSM_GROUP_CH
        btb = bt_ref[...].astype(BF16)
        ctb = ct_ref[...].astype(BF16)
        for dc in range(4):
            r = lax.broadcasted_iota(I32, (4 * p, g * p), 0)
            c = lax.broadcasted_iota(I32, (4 * p, g * p), 1)
            sel = jnp.where(r == dc * p + c % p, 1.0, 0.0).astype(BF16)
            r = lax.broadcasted_iota(I32, (g * hc, g * p), 0)
            c = lax.broadcasted_iota(I32, (g * hc, g * p), 1)
            blk = jnp.where(r // hc == c // p, _dot(btb, sel), 0.0)
            bmat_ref[:, dc * g * p:(dc + 1) * g * p] = blk.astype(BF16)
            r = lax.broadcasted_iota(I32, (g * p, 4 * p), 0)
            c = lax.broadcasted_iota(I32, (g * p, 4 * p), 1)
            sel = jnp.where(c == dc * p + r % p, 1.0, 0.0).astype(BF16)
            r = lax.broadcasted_iota(I32, (g * p, g * hc), 0)
            c = lax.broadcasted_iota(I32, (g * p, g * hc), 1)
            blk = jnp.where(r // p == c // hc, _dot(sel, ctb), 0.0)
            cmat_ref[dc * g * p:(dc + 1) * g * p, :] = blk.astype(BF16)

    def permute_in(t, _):
        t0 = pl.multiple_of(t * tt, tt)
        for j in range(0, ch, 2):
            for hf in range(nhalf):
                lanes = slice(hf * LANES, (hf + 1) * LANES)
                rows = jnp.concatenate([u_ref[0, hf, pl.ds(t0 + j + jj, SUBLANES, stride=ch), :] for jj in range(2)],
                                       axis=0)
                r0 = pl.multiple_of(t0 + j * SUBLANES, 2 * SUBLANES)
                up_ref[pl.ds(r0, 2 * SUBLANES), lanes] = rows.astype(BF16)
                yp_ref[pl.ds(r0, 2 * SUBLANES), lanes] = rows * d_ref[:, lanes]
        return 0

    lax.fori_loop(0, ntile, permute_in, 0)

    for d in range(2):
        for lt in range(nlt):
            lanes = slice(lt * LANES, (lt + 1) * LANES)
            car_ref[d, 0, :, lanes] = jnp.broadcast_to(h0re_ref[0, d:d + 1, lanes], (SUBLANES, LANES))
            car_ref[d, 1, :, lanes] = jnp.broadcast_to(h0im_ref[0, d:d + 1, lanes], (SUBLANES, LANES))

    def tile(t, _):
        starts = []
        for d in range(2):
            tix = (ntile - 1 - t) if d == 1 else t
            t0 = pl.multiple_of(tix * tt, tt)
            starts.append(t0)
            buf_ref[d] = _dot(up_ref[pl.ds(t0, tt), :], bmat_ref[:, d * 2 * SSM_N:(d + 1) * 2 * SSM_N])
        for d in range(2):
            reverse = d == 1
            order = range(ch - 1, -1, -1) if reverse else range(ch)
            for lt in range(nlt):
                lre = slice(lt * LANES, (lt + 1) * LANES)
                lim = slice(SSM_N + lt * LANES, SSM_N + (lt + 1) * LANES)
                ar = a1_ref[2 * d:2 * d + 1, lre]
                ai = a1_ref[2 * d + 1:2 * d + 2, lre]
                cr = jnp.zeros((SUBLANES, LANES), F32)
                ci = jnp.zeros((SUBLANES, LANES), F32)
                for j in order:
                    rows = slice(j * SUBLANES, (j + 1) * SUBLANES)
                    cr, ci = (ar * cr - ai * ci) + buf_ref[d, rows, lre], (ar * ci + ai * cr) + buf_ref[d, rows, lim]
                    buf_ref[d, rows, lre] = cr
                    buf_ref[d, rows, lim] = ci
                tr, ti = car_ref[d, 0, :, lre], car_ref[d, 1, :, lre]
                gr, gi = _sublane_scan(cr, ci, tr, ti, ach_ref, 6 * d, lre, reverse)
                a_r, a_i = ach_ref[6 * d:6 * d + 1, lre], ach_ref[6 * d + 1:6 * d + 2, lre]
                nr, ni = (a_r * gr - a_i * gi) + cr, (a_r * gi + a_i * gr) + ci
                edge = 0 if reverse else SUBLANES - 1
                car_ref[d, 0, :, lre] = jnp.broadcast_to(nr[edge:edge + 1, :], (SUBLANES, LANES))
                car_ref[d, 1, :, lre] = jnp.broadcast_to(ni[edge:edge + 1, :], (SUBLANES, LANES))
                for j in range(ch):
                    rows = slice(j * SUBLANES, (j + 1) * SUBLANES)
                    pr = pfix_ref[2 * d, j:j + 1, lre]
                    pi = pfix_ref[2 * d + 1, j:j + 1, lre]
                    buf_ref[d, rows, lre] += pr * gr - pi * gi
                    buf_ref[d, rows, lim] += pr * gi + pi * gr
        for d in range(2):
            yp_ref[pl.ds(starts[d], tt), :] += _dot(buf_ref[d].astype(BF16),
                                                    cmat_ref[d * 2 * SSM_N:(d + 1) * 2 * SSM_N, :])
        return 0

    lax.fori_loop(0, ntile, tile, 0)

    for d in range(2):
        fre_ref[0, d:d + 1, :] = car_ref[d, 0, 0:1, :]
        fim_ref[0, d:d + 1, :] = car_ref[d, 1, 0:1, :]

    wglu = wglu_ref[...].astype(BF16)

    def glu(t, _):
        t0 = pl.multiple_of(t * tt, tt)
        g = jax.nn.gelu(yp_ref[pl.ds(t0, tt), :])
        out = g * jax.nn.sigmoid(_dot(g.astype(BF16), wglu) + bglu_ref[...])
        for j in range(ch):
            for hf in range(nhalf):
                o_ref[0, hf, pl.ds(t0 + j, SUBLANES, stride=ch), :] = out[j * SUBLANES:(j + 1) * SUBLANES,
                                                                          hf * LANES:(hf + 1) * LANES]
        return 0

    lax.fori_loop(0, ntile, glu, 0)


def _cmul(ar, ai, br, bi):
    return ar * br - ai * bi, ar * bi + ai * br


def _s5_tables(a_re, a_im, log_dt, b_re, b_im, c_re, c_im, seq_lens):
    depth = a_re.shape[0]
    g, p, hc = SSM_GROUPS, SSM_STATE, SSM_GROUP_CH
    lr, li = a_re.astype(F32), a_im.astype(F32)
    dt = jnp.exp(log_dt.astype(F32))[..., None]
    mag = jnp.exp(lr * dt)
    ar, ai = mag * jnp.cos(li * dt), mag * jnp.sin(li * dt)
    den = lr * lr + li * li
    qr = ((ar - 1.0) * lr + ai * li) / den
    qi = (ai * lr - (ar - 1.0) * li) / den
    br, bi = _cmul(qr[..., None], qi[..., None], b_re.astype(F32), b_im.astype(F32))
    bparts = jnp.stack([br, bi], axis=2)
    bmat = bparts.transpose(0, 3, 5, 1, 2, 4).reshape(depth, g * hc, 4 * p)
    cparts = jnp.stack([c_re.astype(F32), -c_im.astype(F32)], axis=2)
    cmat = cparts.transpose(0, 1, 2, 5, 3, 4).reshape(depth, 4 * p, g * hc)

    def power(k):
        kk = k[None, None, :, None, None]
        m = jnp.exp(kk * (lr * dt)[:, :, None])
        th = kk * (li * dt)[:, :, None]
        return (m * jnp.cos(th)).reshape(depth, 2, -1, g * p), (m * jnp.sin(th)).reshape(depth, 2, -1, g * p)

    a1 = jnp.stack([ar.reshape(depth, 2, g * p), ai.reshape(depth, 2, g * p)], axis=2).reshape(depth, 4, g * p)
    tabs = {}
    for seq in set(seq_lens):
        ch = min(seq, SCAN_TILE) // SUBLANES
        cr_, ci_ = power(jnp.asarray([ch, 2 * ch, 4 * ch], F32))
        ach = jnp.stack([cr_, ci_], axis=3).reshape(depth, 12, g * p)
        fr, fi = power(jnp.arange(1, ch + 1, dtype=F32))
        fr = jnp.stack([fr[:, 0], fr[:, 1, ::-1]], axis=1)
        fi = jnp.stack([fi[:, 0], fi[:, 1, ::-1]], axis=1)
        pfix = jnp.stack([fr, fi], axis=2).reshape(depth, 4, ch, g * p)
        tabs[seq] = (bmat, cmat, a1, ach, pfix)
    return tabs


def _s5(u, h0re, h0im, tabs, d_skip, w_glu, b_glu, layer):
    b, nh, seq, _ = u.shape
    w = nh * LANES
    tabs = tabs[seq]
    tile = tabs[4].shape[2] * SUBLANES
    lay = lambda a: pl.BlockSpec((None,) + a.shape[1:], lambda i: (layer,) + (0,) * (a.ndim - 1))
    return pl.pallas_call(
        _s5_kernel,
        out_shape=[jax.ShapeDtypeStruct((b, nh, seq, LANES), F32),
                   jax.ShapeDtypeStruct((b, 2, SSM_N), F32),
                   jax.ShapeDtypeStruct((b, 2, SSM_N), F32)],
        grid=(b,),
        in_specs=[
            pl.BlockSpec((1, nh, seq, LANES), lambda i: (i, 0, 0, 0)),
            pl.BlockSpec((1, 2, SSM_N), lambda i: (i, 0, 0)),
            pl.BlockSpec((1, 2, SSM_N), lambda i: (i, 0, 0)),
            lay(tabs[0]), lay(tabs[1]), lay(tabs[2]), lay(tabs[3]), lay(tabs[4]),
            pl.BlockSpec((None, 1, w), lambda i: (layer, 0, 0)),
            pl.BlockSpec((None, w, w), lambda i: (layer, 0, 0)),
            pl.BlockSpec((None, 1, w), lambda i: (layer, 0, 0)),
        ],
        out_specs=[pl.BlockSpec((1, nh, seq, LANES), lambda i: (i, 0, 0, 0)),
                   pl.BlockSpec((1, 2, SSM_N), lambda i: (i, 0, 0)),
                   pl.BlockSpec((1, 2, SSM_N), lambda i: (i, 0, 0))],
        scratch_shapes=[pltpu.VMEM((seq, w), BF16),
                        pltpu.VMEM((seq, w), F32),
                        pltpu.VMEM((2, tile, 2 * SSM_N), F32),
                        pltpu.VMEM((2, 2, SUBLANES, SSM_N), F32),
                        pltpu.VMEM((w, 4 * SSM_N), BF16),
                        pltpu.VMEM((4 * SSM_N, w), BF16)],
        compiler_params=_params(1),
        name="s5",
    )(u, h0re, h0im, *tabs, d_skip, w_glu, b_glu)


def _proj_out_kernel(oa_ref, ob_ref, oc_ref, w_ref, x_ref, mod_ref, g_ref, wr_ref, o_ref, h_ref, lg_ref, wbf_ref):
    @pl.when((pl.program_id(0) == 0) & (pl.program_id(1) == 0))
    def _():
        _cast_rows(wbf_ref, w_ref, 128)

    i1, i2 = NA_WIDTH, NA_WIDTH + DIFF_WIDTH
    for bi in range(x_ref.shape[0]):
        for r0 in range(0, x_ref.shape[1], ROW_TILE):
            rows = slice(r0, r0 + ROW_TILE)
            y = _dot(oa_ref[bi, rows, :].astype(BF16), wbf_ref[:i1, :])
            y += _dot(ob_ref[bi, rows, :].astype(BF16), wbf_ref[i1:i2, :])
            for hf in range(SSM_WIDTH // LANES):
                y += _dot(oc_ref[bi, hf, rows, :].astype(BF16), wbf_ref[i2 + hf * LANES:i2 + (hf + 1) * LANES, :])
            xn = x_ref[bi, rows, :] + mod_ref[0, 2:3, :] * y
            o_ref[bi, rows, :] = xn
            h = _modulated_norm(xn, g_ref[...], mod_ref[0, 3:4, :], mod_ref[0, 4:5, :])
            h_hi = h.astype(BF16)
            h_ref[bi, rows, :] = h_hi
            h_lo = (h - h_hi.astype(F32)).astype(BF16)
            t1 = _dot_nt(wr_ref[...], h_hi)
            t2 = _dot_nt(wr_ref[:N_EXPERTS, :], h_lo)
            lg_ref[bi, :, rows] = t1[:N_EXPERTS, :] + t1[N_EXPERTS:, :] + t2


def _proj_out(oa, ob, oc, w_out, x, mod, g_ffn, wr_t, layer):
    b, n, d = x.shape
    bb, tm = _step_tile(b, n, mod.shape[0], PROJ_OUT_CHUNKS)
    mod_map = (lambda i, j: (i, 0, 0)) if mod.shape[0] > 1 else (lambda i, j: (0, 0, 0))
    kw = w_out.shape[1]
    return pl.pallas_call(
        _proj_out_kernel,
        out_shape=[jax.ShapeDtypeStruct((b, n, d), F32),
                   jax.ShapeDtypeStruct((b, n, d), BF16),
                   jax.ShapeDtypeStruct((b, N_EXPERTS, n), F32)],
        grid=(b // bb, n // tm),
        in_specs=[
            pl.BlockSpec((bb, tm, NA_WIDTH), lambda i, j: (i, j, 0)),
            pl.BlockSpec((bb, tm, DIFF_WIDTH), lambda i, j: (i, j, 0)),
            pl.BlockSpec((bb, SSM_WIDTH // LANES, tm, LANES), lambda i, j: (i, 0, j, 0)),
            pl.BlockSpec((None, kw, d), lambda i, j: (layer, 0, 0), pipeline_mode=pl.Buffered(1)),
            pl.BlockSpec((bb, tm, d), lambda i, j: (i, j, 0)),
            pl.BlockSpec((1, 6, d), mod_map),
            pl.BlockSpec((None, 1, d), lambda i, j: (layer, 0, 0)),
            pl.BlockSpec((None, 2 * N_EXPERTS, d), lambda i, j: (layer, 0, 0)),
        ],
        out_specs=[pl.BlockSpec((bb, tm, d), lambda i, j: (i, j, 0)),
                   pl.BlockSpec((bb, tm, d), lambda i, j: (i, j, 0)),
                   pl.BlockSpec((bb, N_EXPERTS, tm), lambda i, j: (i, 0, j))],
        scratch_shapes=[pltpu.VMEM((kw, d), BF16)],
        compiler_params=_params(2),
        name="proj_out",
    )(oa, ob, oc, w_out, x, mod, g_ffn, wr_t)


def _excl_cumsum_lanes(m):
    e, n = m.shape
    blk = MXU_DIM
    nb = n // blk
    r = lax.broadcasted_iota(I32, (blk, blk), 0)
    c = lax.broadcasted_iota(I32, (blk, blk), 1)
    tri = jnp.where(r < c, 1.0, 0.0).astype(BF16)
    stacked = jnp.concatenate([m[:, k * blk:(k + 1) * blk] for k in range(nb)], axis=0).astype(BF16)
    within = _dot(stacked, tri)
    outs = []
    off = jnp.zeros((e, 1), F32)
    for k in range(nb):
        outs.append(within[k * e:(k + 1) * e, :] + off)
        off = off + jnp.sum(m[:, k * blk:(k + 1) * blk], axis=1, keepdims=True)
    return jnp.concatenate(outs, axis=1)


def _route_kernel(lg_ref, pos_ref, gate_ref, *, cap):
    b, e, n = lg_ref.shape
    lg = lg_ref[...]
    mx = lg.max(axis=1, keepdims=True)
    ex = jnp.exp(lg - mx)
    aff = (ex / ex.sum(axis=1, keepdims=True)).reshape(b * e, n)

    def search(it, cur):
        cand = cur | (jnp.int32(1) << (jnp.int32(30) - it))
        cnt = jnp.sum(jnp.where(aff >= pltpu.bitcast(cand, F32), 1.0, 0.0), axis=1, keepdims=True)
        return jnp.where(cnt >= cap, cand, cur)

    thr = pltpu.bitcast(lax.fori_loop(0, 31, search, jnp.zeros((b * e, 1), I32)), F32)
    gt = aff > thr
    eq = aff == thr
    need = cap - jnp.sum(jnp.where(gt, 1.0, 0.0), axis=1, keepdims=True)
    eq_rank = _excl_cumsum_lanes(jnp.where(eq, 1.0, 0.0))
    sel = gt | (eq & (eq_rank < need))
    slot = _excl_cumsum_lanes(jnp.where(sel, 1.0, 0.0))
    pos = jnp.where(sel, slot, -1.0).astype(I32)
    gate = jnp.where(sel, aff, 0.0)
    for i in range(b):
        pos_ref[:, i, 0, :] = pos[i * e:(i + 1) * e, :]
        gate_ref[:, i, 0, :] = gate[i * e:(i + 1) * e, :]


def _route(logits):
    b, e, n = logits.shape
    cap = EC_CAPACITY_FACTOR * n // N_EXPERTS
    return pl.pallas_call(
        functools.partial(_route_kernel, cap=cap),
        out_shape=[jax.ShapeDtypeStruct((e, b, 1, n), I32),
                   jax.ShapeDtypeStruct((e, b, 1, n), F32)],
        compiler_params=_params(0),
        name="route",
    )(logits)


def _one_hot_rows(pos_row, cap):
    n = pos_row.shape[1]
    slot = lax.broadcasted_iota(I32, (cap, n), 0)
    return slot == pos_row


def _gather_kernel(h_ref, pos_ref, gate_ref, xs_ref, gs_ref, *, cap):
    h = h_ref[0]
    group = max(1, GATHER_ROWS // cap)
    for e0 in range(0, N_EXPERTS, group):
        ohs = [_one_hot_rows(pos_ref[e, 0], cap) for e in range(e0, e0 + group)]
        xs = _dot(jnp.concatenate([jnp.where(oh, 1.0, 0.0).astype(BF16) for oh in ohs], axis=0), h).astype(BF16)
        for i, oh in enumerate(ohs):
            e = e0 + i
            xs_ref[e] = xs[i * cap:(i + 1) * cap, :]
            g = jnp.sum(jnp.where(oh, gate_ref[e, 0], 0.0), axis=1, keepdims=True)
            gs_ref[e] = jnp.broadcast_to(g, (cap, LANES))


def _gather(h, pos, gate):
    b, n, d = h.shape
    cap = EC_CAPACITY_FACTOR * n // N_EXPERTS
    return pl.pallas_call(
        functools.partial(_gather_kernel, cap=cap),
        out_shape=[jax.ShapeDtypeStruct((N_EXPERTS, b * cap, d), BF16),
                   jax.ShapeDtypeStruct((N_EXPERTS, b * cap, LANES), F32)],
        grid=(b,),
        in_specs=[
            pl.BlockSpec((1, n, d), lambda i: (i, 0, 0)),
            pl.BlockSpec((N_EXPERTS, 1, 1, n), lambda i: (0, i, 0, 0)),
            pl.BlockSpec((N_EXPERTS, 1, 1, n), lambda i: (0, i, 0, 0)),
        ],
        out_specs=[pl.BlockSpec((N_EXPERTS, cap, d), lambda i: (0, i, 0)),
                   pl.BlockSpec((N_EXPERTS, cap, LANES), lambda i: (0, i, 0))],
        compiler_params=_params(1),
        name="moe_gather",
    )(h, pos, gate)


def _ffn_kernel(xp_ref, gp_ref, xs_ref, gs_ref, wg_ref, wu_ref, wd_ref, yp_ref, ys_ref, acc_ref):
    j = pl.program_id(1)
    sp = xp_ref.shape[1]

    @pl.when((pl.program_id(0) == 0) & (j == 0))
    def _():
        acc_ref[...] = jnp.zeros_like(acc_ref)

    wg = wg_ref[...].astype(BF16)
    wu = wu_ref[...].astype(BF16)
    wd = wd_ref[...].astype(BF16)
    for x_ref, g_ref, y_ref, rows in ((xp_ref, gp_ref, yp_ref, slice(0, sp)),
                                      (xs_ref, gs_ref, ys_ref, slice(sp, acc_ref.shape[0]))):
        x = x_ref[0]
        a = _dot(x, wg)
        u = _dot(x, wu)
        hid = (a * jax.nn.sigmoid(a)) * u
        acc = jnp.where(j > 0, acc_ref[rows, :], 0.0) + _dot(hid.astype(BF16), wd)
        acc_ref[rows, :] = acc
        y_ref[0] = (acc * g_ref[0][:, 0:1]).astype(BF16)


def _ffn(xp, gp, xs, gs, w_gate, w_up, w_down, layer):
    e, sp, d = xp.shape
    ss = xs.shape[1]
    s = sp + ss
    ff = w_gate.shape[-1]
    tf = FF_TILE
    return pl.pallas_call(
        _ffn_kernel,
        out_shape=[jax.ShapeDtypeStruct((e, sp, d), BF16), jax.ShapeDtypeStruct((e, ss, d), BF16)],
        grid=(e, ff // tf),
        in_specs=[
            pl.BlockSpec((1, sp, d), lambda i, j: (i, 0, 0)),
            pl.BlockSpec((1, sp, LANES), lambda i, j: (i, 0, 0)),
            pl.BlockSpec((1, ss, d), lambda i, j: (i, 0, 0)),
            pl.BlockSpec((1, ss, LANES), lambda i, j: (i, 0, 0)),
            pl.BlockSpec((None, None, d, tf), lambda i, j: (layer, i, 0, j)),
            pl.BlockSpec((None, None, d, tf), lambda i, j: (layer, i, 0, j)),
            pl.BlockSpec((None, None, tf, d), lambda i, j: (layer, i, j, 0)),
        ],
        out_specs=[pl.BlockSpec((1, sp, d), lambda i, j: (i, 0, 0)),
                   pl.BlockSpec((1, ss, d), lambda i, j: (i, 0, 0))],
        scratch_shapes=[pltpu.VMEM((s, d), F32)],
        compiler_params=_params(2),
        name="moe_ffn",
    )(xp, gp, xs, gs, w_gate, w_up, w_down)


def _combine_kernel(ye_ref, pos_ref, x_ref, mod_ref, *rest, cap, final):
    if final:
        g_ref, o_ref = rest
    else:
        (o_ref,) = rest
    group = max(1, MXU_DIM // cap)
    y = None
    for e0 in range(0, N_EXPERTS, group):
        oh = jnp.concatenate([jnp.where(_one_hot_rows(pos_ref[e, 0], cap), 1.0, 0.0).astype(BF16)
                              for e in range(e0, e0 + group)], axis=0)
        ye = jnp.concatenate([ye_ref[e] for e in range(e0, e0 + group)], axis=0)
        part = _dot_tn(oh, ye)
        y = part if y is None else y + part
    xn = x_ref[0] + mod_ref[0, 5:6, :] * y
    if final:
        ms = jnp.mean(xn * xn, axis=-1, keepdims=True)
        xn = xn * lax.rsqrt(ms + EPS) * g_ref[...]
    o_ref[0] = xn


def _combine(ye, pos, x, mod, final_norm):
    b, n, d = x.shape
    cap = EC_CAPACITY_FACTOR * n // N_EXPERTS
    tn = min(n, COMBINE_TILE)
    bm = mod.shape[0]
    mod_map = (lambda i, j: (i, 0, 0)) if bm > 1 else (lambda i, j: (0, 0, 0))
    final = final_norm is not None
    in_specs = [
        pl.BlockSpec((N_EXPERTS, cap, d), lambda i, j: (0, i, 0)),
        pl.BlockSpec((N_EXPERTS, 1, 1, tn), lambda i, j: (0, i, 0, j)),
        pl.BlockSpec((1, tn, d), lambda i, j: (i, j, 0)),
        pl.BlockSpec((1, 6, d), mod_map),
    ]
    args = [ye, pos, x, mod]
    if final:
        in_specs.append(pl.BlockSpec((1, d), lambda i, j: (0, 0)))
        args.append(final_norm)
    return pl.pallas_call(
        functools.partial(_combine_kernel, cap=cap, final=final),
        out_shape=jax.ShapeDtypeStruct((b, n, d), F32),
        grid=(b, n // tn),
        in_specs=in_specs,
        out_specs=pl.BlockSpec((1, tn, d), lambda i, j: (i, j, 0)),
        compiler_params=_params(2),
        name="moe_combine",
    )(*args)


def _rope_tables(n):
    t = np.arange(n)
    row = (t // GRID_W).astype(np.float32)
    col = (t % GRID_W).astype(np.float32)
    nf = DIFF_QK // 4
    inv = np.float32(ROPE_BASE) ** (-np.arange(nf, dtype=np.float32) / np.float32(nf))
    lane = np.arange(DIFF_WIDTH)
    pos = np.where(((lane % DIFF_QK) < DIFF_QK // 2)[None, :], row[:, None], col[:, None])
    ang = (pos * inv[lane % nf][None, :]).astype(np.float32)
    first = (lane % (2 * nf)) < nf
    cos, sin = np.cos(ang).astype(np.float32), np.sin(ang).astype(np.float32)
    return jnp.asarray(cos), jnp.asarray(np.where(first[None, :], -sin, sin))


def kernel(x_prompt, x_sample, cache_na_k, cache_na_v, cache_diff_k, cache_diff_v, state_ssm_re, state_ssm_im,
           c, c_ctx, w_ada, b_ada, norm_mix, norm_ffn, w_in, w_out, na_rpb, diff_lambda, diff_subln,
           ssm_a_re, ssm_a_im, ssm_log_dt, ssm_b_re, ssm_b_im, ssm_c_re, ssm_c_im, ssm_d, ssm_w_glu, ssm_b_glu,
           w_router, w_gate, w_up, w_down, final_norm):
    depth = w_in.shape[0]
    bp, sp, d = x_prompt.shape
    bs, ss, _ = x_sample.shape
    assert d == D_MODEL and bs + 1 <= SUBLANES
    past = cache_na_k.shape[2]

    cond = jnp.zeros((SUBLANES, d), F32).at[0].set(c_ctx).at[1:1 + bs].set(c)
    mods = _ada(cond, w_ada, b_ada).reshape(depth, SUBLANES, 6, d)

    rope_tabs = _rope_tables(ss)
    kc_a = cache_na_k.reshape(bs, depth, past, NA_WIDTH)
    vc_a = cache_na_v.reshape(bs, depth, past, NA_WIDTH)
    kc_b = cache_diff_k.reshape(bs, depth, past, DIFF_WIDTH)
    vc_b = cache_diff_v.reshape(bs, depth, past, DIFF_WIDTH)
    subln = jnp.tile(diff_subln, (1, LANES // DIFF_V)).reshape(depth, 1, LANES)
    norm_mix = norm_mix.reshape(depth, 1, d)
    norm_ffn = norm_ffn.reshape(depth, 1, d)
    ssm_d = ssm_d.reshape(depth, 1, SSM_WIDTH)
    ssm_b_glu = ssm_b_glu.reshape(depth, 1, SSM_WIDTH)
    wr_t = jnp.swapaxes(w_router, 1, 2).astype(F32)
    wr_hi = wr_t.astype(BF16)
    wr_t = jnp.concatenate([wr_hi, (wr_t - wr_hi.astype(F32)).astype(BF16)], axis=1)
    fnorm = final_norm.reshape(1, d)
    zero_state = jnp.zeros((bp, 2, SSM_N), F32)
    tabs = _s5_tables(ssm_a_re, ssm_a_im, ssm_log_dt, ssm_b_re, ssm_b_im, ssm_c_re, ssm_c_im, (sp, ss))
    bias_tab = _na_bias_table(na_rpb)

    xp, xs = x_prompt, x_sample
    new_sre, new_sim = [], []
    caches = None
    for l in range(depth):
        lam_init = 0.8 - 0.6 * math.exp(-0.3 * l)
        mod_p = mods[l, 0:1]
        mod_s = mods[l, 1:1 + bs]
        diff_params = (diff_lambda, subln)

        qa, ka, va, qb, kb, vb, u = _proj_in(xp, mod_p, norm_mix, w_in, l, None, depth, caches)
        caches = (ka, va, kb, vb)
        o_a = _attention(qa, ka, va, l, None, l, None, lam_init)
        o_b = _attention(qb, kb, vb, l, None, l, diff_params, lam_init)
        o_c, fre, fim = _s5(u, zero_state, zero_state, tabs, ssm_d, ssm_w_glu, ssm_b_glu, l)
        xp, hp, lg_p = _proj_out(o_a, o_b, o_c, w_out, xp, mod_p, norm_ffn, wr_t, l)
        new_sre.append(fre.reshape(bp, 2, SSM_GROUPS, SSM_STATE))
        new_sim.append(fim.reshape(bp, 2, SSM_GROUPS, SSM_STATE))

        qa, ka, va, qb, kb, vb, u = _proj_in(xs, mod_s, norm_mix, w_in, l, rope_tabs, 0, None)
        o_a = _neighbourhood_attention(qa, ka, va, kc_a, vc_a, bias_tab, l)
        o_b = _attention(qb, kb, vb, 0, (kc_b, vc_b), l, diff_params, lam_init)
        h0re = state_ssm_re[:, l].reshape(bs, 2, SSM_N)
        h0im = state_ssm_im[:, l].reshape(bs, 2, SSM_N)
        o_c, _, _ = _s5(u, h0re, h0im, tabs, ssm_d, ssm_w_glu, ssm_b_glu, l)
        xs, hs, lg_s = _proj_out(o_a, o_b, o_c, w_out, xs, mod_s, norm_ffn, wr_t, l)

        pos_p, gate_p = _route(lg_p)
        pos_s, gate_s = _route(lg_s)
        xg_p, gs_p = _gather(hp, pos_p, gate_p)
        xg_s, gs_s = _gather(hs, pos_s, gate_s)
        ye_p, ye_s = _ffn(xg_p, gs_p, xg_s, gs_s, w_gate, w_up, w_down, l)
        last = l == depth - 1
        xp = _combine(ye_p, pos_p, xp, mod_p, fnorm if last else None)
        xs = _combine(ye_s, pos_s, xs, mod_s, fnorm if last else None)

    ka, va, kb, vb = caches
    return (xp, xs, ka.reshape(bp, depth, sp, NA_HEADS, HEAD_DIM), va.reshape(bp, depth, sp, NA_HEADS, HEAD_DIM),
            kb.reshape(bp, depth, sp, DIFF_HEADS, DIFF_V), vb.reshape(bp, depth, sp, DIFF_HEADS, DIFF_V),
            jnp.stack(new_sre, axis=1), jnp.stack(new_sim, axis=1))
```

```python
import functools
import math

import jax
import jax.numpy as jnp
import numpy as np
from jax import lax
from jax.experimental import pallas as pl
from jax.experimental.pallas import tpu as pltpu

F32 = jnp.float32
BF16 = jnp.bfloat16
I32 = jnp.int32

D_MODEL = 1024
GRID_W = 64
HEAD_DIM = 64
NA_HEADS = 8
NA_WIDTH = NA_HEADS * HEAD_DIM
WIN_R = 8
WIN_C = 16
DIFF_HEADS = 4
DIFF_QK = 32
DIFF_V = 64
DIFF_WIDTH = DIFF_HEADS * DIFF_V
SSM_GROUPS = 16
SSM_GROUP_CH = 16
SSM_WIDTH = SSM_GROUPS * SSM_GROUP_CH
SSM_STATE = 64
SSM_N = SSM_GROUPS * SSM_STATE
IN_WIDTH = 3 * NA_WIDTH + 3 * DIFF_WIDTH + SSM_WIDTH
N_EXPERTS = 16
EXPERT_FF = 2048
EC_CAPACITY_FACTOR = 2
ROPE_BASE = 10000.0
EPS = 1e-6
NEG_INF = -1e30
LOG2E = math.log2(math.e)

LANES = 128
SUBLANES = 8
MXU_DIM = 256
VMEM_LIMIT_CAP = 60000 * 1024

ROW_TILE = 256
PROJ_IN_CHUNKS = 4
PROJ_OUT_CHUNKS = 2
ATTN_Q_TILE = 512
ATTN_REQUESTS_PER_STEP = 4
NA_QROWS = 4
NA_KROWS = 12
SCAN_TILE = 512
FF_TILE = 1024
COMBINE_TILE = 512
GATHER_ROWS = 512
MOE_STEP_TOKENS = 1024
DEN_IN_MATMUL_MIN_KEYS = 1024


def _params(ndims):
    return pltpu.CompilerParams(
        dimension_semantics=("arbitrary",) * ndims,
        vmem_limit_bytes=VMEM_LIMIT_CAP,
    )


def _step_tile(b, n, n_mod, chunks):
    tm = min(n, ROW_TILE * chunks)
    bb = max(1, ROW_TILE * chunks // tm)
    if n_mod > 1 or b % bb:
        bb = 1
    return bb, tm


def _dot(a, b):
    return jnp.dot(a, b, preferred_element_type=F32)


def _dot_nt(a, b):
    return lax.dot_general(a, b, (((1,), (1,)), ((), ())), preferred_element_type=F32)


def _dot_tn(a, b):
    return lax.dot_general(a, b, (((0,), (0,)), ((), ())), preferred_element_type=F32)


def _cast_rows(dst_ref, src_ref, rows):
    n = src_ref.shape[0]
    for r in range(0, n, rows):
        dst_ref[r:r + rows, :] = src_ref[r:r + rows, :].astype(BF16)


def _modulated_norm(x, g, shift, scale):
    ms = jnp.mean(x * x, axis=-1, keepdims=True)
    return (x * lax.rsqrt(ms + EPS) * g) * (1.0 + scale) + shift


def _ada_kernel(c_ref, w_ref, b_ref, o_ref):
    c = c_ref[...]
    s = c * jax.nn.sigmoid(c)
    o_ref[0] = _dot(s.astype(BF16), w_ref[0].astype(BF16)) + b_ref[0]


def _ada(cond, w_ada, b_ada):
    depth = w_ada.shape[0]
    tn = 1536
    nt = 6 * D_MODEL // tn
    return pl.pallas_call(
        _ada_kernel,
        out_shape=jax.ShapeDtypeStruct((depth, SUBLANES, 6 * D_MODEL), F32),
        grid=(depth, nt),
        in_specs=[
            pl.BlockSpec((SUBLANES, D_MODEL), lambda l, j: (0, 0)),
            pl.BlockSpec((1, D_MODEL, tn), lambda l, j: (l, 0, j)),
            pl.BlockSpec((1, 1, tn), lambda l, j: (l, 0, j)),
        ],
        out_specs=pl.BlockSpec((1, SUBLANES, tn), lambda l, j: (l, 0, j)),
        compiler_params=_params(2),
        name="ada",
    )(cond, w_ada, b_ada.reshape(depth, 1, 6 * D_MODEL))


def _rope_apply(x, cos, sin_signed):
    lane = lax.broadcasted_iota(I32, (1, LANES), 1)
    first = (lane % 16) < 8
    outs = []
    for t in range(x.shape[1] // LANES):
        xt = x[:, t * LANES:(t + 1) * LANES]
        partner = jnp.where(first, pltpu.roll(xt, LANES - 8, axis=1), pltpu.roll(xt, 8, axis=1))
        outs.append(xt * cos[:, t * LANES:(t + 1) * LANES] + partner * sin_signed[:, t * LANES:(t + 1) * LANES])
    return jnp.concatenate(outs, axis=1)


def _proj_in_kernel(*refs, rope, n_alias):
    x_ref, mod_ref, g_ref, w_ref = refs[:4]
    pos = 4
    if rope:
        cos_ref, sin_ref = refs[pos:pos + 2]
        pos += 2
    pos += n_alias
    qa_ref, ka_ref, va_ref, qb_ref, kb_ref, vb_ref, u_ref, wbf_ref = refs[pos:pos + 8]

    @pl.when((pl.program_id(0) == 0) & (pl.program_id(1) == 0))
    def _():
        _cast_rows(wbf_ref, w_ref, 128)

    i1, i2, i3 = NA_WIDTH, 2 * NA_WIDTH, 3 * NA_WIDTH
    i4, i5, i6 = i3 + DIFF_WIDTH, i3 + 2 * DIFF_WIDTH, i3 + 3 * DIFF_WIDTH
    for bi in range(x_ref.shape[0]):
        for r0 in range(0, x_ref.shape[1], ROW_TILE):
            rows = slice(r0, r0 + ROW_TILE)

            def put(ref, val):
                for s in range(ref.shape[1]):
                    ref[bi, s, rows, :] = val.astype(ref.dtype)

            h = _modulated_norm(x_ref[bi, rows, :], g_ref[...], mod_ref[0, 0:1, :], mod_ref[0, 1:2, :])
            z = _dot(h.astype(BF16), wbf_ref[...])
            qa_ref[bi, rows, :] = (z[:, :i1] * (HEAD_DIM ** -0.5 * LOG2E)).astype(qa_ref.dtype)
            put(ka_ref, z[:, i1:i2])
            put(va_ref, z[:, i2:i3])
            qb = z[:, i3:i4]
            kb = z[:, i4:i5]
            if rope:
                qb = _rope_apply(qb, cos_ref[rows, :], sin_ref[rows, :])
                kb = _rope_apply(kb, cos_ref[rows, :], sin_ref[rows, :])
            qb_ref[bi, rows, :] = (qb * (DIFF_QK ** -0.5 * LOG2E)).astype(qb_ref.dtype)
            put(kb_ref, kb)
            put(vb_ref, z[:, i5:i6])
            for hf in range(SSM_WIDTH // LANES):
                u_ref[bi, hf, rows, :] = z[:, i6 + hf * LANES:i6 + (hf + 1) * LANES]


def _proj_in(x, mod, g_norm, w_in, layer, rope_tabs, cache_slots, caches):
    b, n, d = x.shape
    bb, tm = _step_tile(b, n, mod.shape[0], PROJ_IN_CHUNKS)
    rope = rope_tabs is not None
    mod_map = (lambda i, j: (i, 0, 0)) if mod.shape[0] > 1 else (lambda i, j: (0, 0, 0))
    in_specs = [
        pl.BlockSpec((bb, tm, d), lambda i, j: (i, j, 0)),
        pl.BlockSpec((1, 6, d), mod_map),
        pl.BlockSpec((None, 1, d), lambda i, j: (layer, 0, 0)),
        pl.BlockSpec((None, d, IN_WIDTH), lambda i, j: (layer, 0, 0), pipeline_mode=pl.Buffered(1)),
    ]
    args = [x, mod, g_norm, w_in]
    if rope:
        in_specs += [pl.BlockSpec((tm, DIFF_WIDTH), lambda i, j: (j, 0))] * 2
        args += list(rope_tabs)
    aliases = {}
    kv_out = (1, 2, 4, 5)
    if caches is not None:
        for c, o in zip(caches, kv_out):
            aliases[len(args)] = o
            in_specs.append(pl.BlockSpec(memory_space=pl.ANY))
            args.append(c)
    widths = (NA_WIDTH, NA_WIDTH, NA_WIDTH, DIFF_WIDTH, DIFF_WIDTH, DIFF_WIDTH)
    nslab = SSM_WIDTH // LANES
    out_shape, out_specs = [], []
    for o, w in enumerate(widths):
        if o not in kv_out:
            out_shape.append(jax.ShapeDtypeStruct((b, n, w), BF16))
            out_specs.append(pl.BlockSpec((bb, tm, w), lambda i, j: (i, j, 0)))
        elif cache_slots == 0:
            out_shape.append(jax.ShapeDtypeStruct((b, 1, n, w), BF16))
            out_specs.append(pl.BlockSpec((bb, 1, tm, w), lambda i, j: (i, 0, j, 0)))
        elif caches is None:
            out_shape.append(jax.ShapeDtypeStruct((b, cache_slots, n, w), F32))
            out_specs.append(pl.BlockSpec((bb, cache_slots, tm, w), lambda i, j: (i, 0, j, 0)))
        else:
            out_shape.append(jax.ShapeDtypeStruct((b, cache_slots, n, w), F32))
            out_specs.append(pl.BlockSpec((bb, 1, tm, w), lambda i, j: (i, layer, j, 0)))
    out_shape.append(jax.ShapeDtypeStruct((b, nslab, n, LANES), F32))
    out_specs.append(pl.BlockSpec((bb, nslab, tm, LANES), lambda i, j: (i, 0, j, 0)))
    return pl.pallas_call(
        functools.partial(_proj_in_kernel, rope=rope, n_alias=len(aliases)),
        out_shape=out_shape,
        grid=(b // bb, n // tm),
        in_specs=in_specs,
        out_specs=out_specs,
        scratch_shapes=[pltpu.VMEM((d, IN_WIDTH), BF16)],
        input_output_aliases=aliases,
        compiler_params=_params(2),
        name="proj_in",
    )(*args)


def _masked_query(qp, mask):
    return jnp.where(mask, qp, jnp.zeros_like(qp))


def _with_ones(v):
    return jnp.concatenate([v, jnp.ones_like(v)], axis=1)


def _softmax_av_wide(qm, ks, vs, biases):
    ss = []
    for k, bias in zip(ks, biases):
        s = _dot_nt(qm, k)
        if bias is not None:
            s = bias(s)
        ss.append(s)
    m = ss[0].max(axis=-1, keepdims=True)
    for s in ss[1:]:
        m = jnp.maximum(m, s.max(axis=-1, keepdims=True))
    acc = None
    for s, v in zip(ss, vs):
        pv = _dot(jnp.exp2(s - m).astype(BF16), v)
        acc = pv if acc is None else acc + pv
    return acc[:, :LANES] / acc[:, LANES:]


def _softmax_av(qm, ks, vs, biases):
    ss = []
    for k, bias in zip(ks, biases):
        s = _dot_nt(qm, k)
        if bias is not None:
            s = bias(s)
        ss.append(s)
    m = ss[0].max(axis=-1, keepdims=True)
    for s in ss[1:]:
        m = jnp.maximum(m, s.max(axis=-1, keepdims=True))
    acc = None
    den = None
    for s, v in zip(ss, vs):
        p = jnp.exp2(s - m)
        psum = p.sum(axis=-1, keepdims=True)
        den = psum if den is None else den + psum
        pv = _dot(p.astype(BF16), v)
        acc = pv if acc is None else acc + pv
    return acc / den


def _diff_lambda(lam_ref, lam_init):
    lp = lam_ref[...]
    a = jnp.sum(lp[0:1] * lp[1:2], axis=-1, keepdims=True)
    b = jnp.sum(lp[2:3] * lp[3:4], axis=-1, keepdims=True)
    return jnp.exp(a) - jnp.exp(b) + lam_init


def _attn_kernel(*refs, has_ctx, diff, lam_init):
    refs = list(refs)
    q_ref, k_ref, v_ref = refs[:3]
    pos = 3
    if has_ctx:
        kc_ref, vc_ref = refs[pos:pos + 2]
        pos += 2
    if diff:
        lam_ref, subln_ref = refs[pos:pos + 2]
        pos += 2
    o_ref = refs[pos]
    width = q_ref.shape[2]
    lane = lax.broadcasted_iota(I32, (1, LANES), 1)
    if diff:
        lam = _diff_lambda(lam_ref, lam_init)
    for bi, p in [(bi, p) for bi in range(q_ref.shape[0]) for p in range(width // LANES)]:
        sl = slice(p * LANES, (p + 1) * LANES)
        qp = q_ref[bi, :, sl]
        ks = [k_ref[bi, :, sl].astype(BF16)]
        vs = [v_ref[bi, :, sl].astype(BF16)]
        if has_ctx:
            ks.append(kc_ref[:, sl].astype(BF16))
            vs.append(vc_ref[:, sl].astype(BF16))
        nob = [None] * len(ks)
        tq = qp.shape[0]
        wide = sum(k.shape[0] for k in ks) >= DEN_IN_MATMUL_MIN_KEYS
        if wide:
            vs = [_with_ones(v) for v in vs]
        pair = None
        for sub in range(LANES // HEAD_DIM):
            head_mask = (lane // HEAD_DIM) == sub
            if diff:
                qmasks = [(lane // DIFF_QK) == (2 * sub + half) for half in range(2)]
                av = _softmax_av_wide if wide else _softmax_av
                os_ = [av(_masked_query(qp, mk), ks, vs, nob) for mk in qmasks]
                o = jnp.where(head_mask, os_[0] - lam * os_[1], 0.0)
                ms = jnp.sum(o * o, axis=-1, keepdims=True) * (1.0 / DIFF_V)
                o = (o * lax.rsqrt(ms + EPS) * subln_ref[...]) * (1.0 - lam_init)
            else:
                qm = _masked_query(qp, head_mask)
                o = _softmax_av_wide(qm, ks, vs, nob) if wide else _softmax_av(qm, ks, vs, nob)
                o = jnp.where(head_mask, o, 0.0)
            pair = o if pair is None else pair + o
        o_ref[bi, :, sl] = pair.astype(o_ref.dtype)


def _attention(q, k, v, kv_slot, ctx_kv, layer, diff_params, lam_init):
    b, nq, w = q.shape
    nk = k.shape[2]
    tq = min(nq, ATTN_Q_TILE)
    has_ctx = ctx_kv is not None
    diff = diff_params is not None
    bb = ATTN_REQUESTS_PER_STEP if (nq == tq and not has_ctx and b % ATTN_REQUESTS_PER_STEP == 0) else 1
    in_specs = [
        pl.BlockSpec((bb, tq, w), lambda i, j: (i, j, 0)),
        pl.BlockSpec((bb, None, nk, w), lambda i, j: (i, kv_slot, 0, 0)),
        pl.BlockSpec((bb, None, nk, w), lambda i, j: (i, kv_slot, 0, 0)),
    ]
    args = [q, k, v]
    nc = 0
    if has_ctx:
        nc = ctx_kv[0].shape[2]
        in_specs += [pl.BlockSpec((None, None, nc, w), lambda i, j: (i, layer, 0, 0))] * 2
        args += list(ctx_kv)
    if diff:
        lam_p, subln = diff_params
        in_specs += [
            pl.BlockSpec((None, 4, DIFF_QK), lambda i, j: (layer, 0, 0)),
            pl.BlockSpec((None, 1, LANES), lambda i, j: (layer, 0, 0)),
        ]
        args += [lam_p, subln]
    return pl.pallas_call(
        functools.partial(_attn_kernel, has_ctx=has_ctx, diff=diff, lam_init=lam_init),
        out_shape=jax.ShapeDtypeStruct((b, nq, w), BF16),
        grid=(b // bb, nq // tq),
        in_specs=in_specs,
        out_specs=pl.BlockSpec((bb, tq, w), lambda i, j: (i, j, 0)),
        compiler_params=_params(2),
        name="attn_diff" if diff else "attn",
    )(*args)


def _na_kernel(q_ref, k_ref, v_ref, kc_ref, vc_ref, bias_ref, o_ref, *, rows):
    i = pl.program_id(1)
    tq = NA_QROWS * GRID_W
    nkw = NA_KROWS * GRID_W
    ustart = jnp.clip(NA_QROWS * i - WIN_R // 2, 0, rows - NA_KROWS)
    koff = pl.multiple_of(ustart * GRID_W, GRID_W)
    delta = ustart - NA_QROWS * i
    qrow = NA_QROWS * i + lax.broadcasted_iota(I32, (tq, nkw), 0) // GRID_W
    krow = ustart + lax.broadcasted_iota(I32, (tq, nkw), 1) // GRID_W
    wstart = jnp.clip(qrow - WIN_R // 2, 0, rows - WIN_R)
    row_ok = (krow >= wstart) & (krow < wstart + WIN_R)
    nsub = LANES // HEAD_DIM
    lane = lax.broadcasted_iota(I32, (1, LANES), 1)
    for p in range(NA_WIDTH // LANES):
        sl = slice(p * LANES, (p + 1) * LANES)
        qp = q_ref[0, :, sl]
        ks = [k_ref[0, pl.ds(koff, nkw), sl].astype(BF16), kc_ref[:, sl].astype(BF16)]
        vs = [_with_ones(v_ref[0, pl.ds(koff, nkw), sl].astype(BF16)), _with_ones(vc_ref[:, sl].astype(BF16))]
        outs = []
        for sub in range(nsub):
            def win_bias(s, h=p * nsub + sub):
                rows_ = []
                for ri in range(NA_QROWS):
                    tiles = []
                    for pj in range(NA_KROWS // 2):
                        dr = delta + 2 * pj - ri
                        idx = jnp.clip(dr, -WIN_R, WIN_R - 1) + WIN_R
                        tiles.append(bias_ref[h, idx])
                    rows_.append(jnp.concatenate(tiles, axis=1))
                bias = jnp.concatenate(rows_, axis=0)
                return jnp.where(row_ok, s + bias, NEG_INF)

            outs.append(_softmax_av_wide(_masked_query(qp, (lane // HEAD_DIM) == sub), ks, vs, [win_bias, None]))
        pair = jnp.where((lane // HEAD_DIM) == 0, outs[0], outs[1])
        o_ref[0, :, sl] = pair.astype(o_ref.dtype)


def _na_table_kernel(v_ref, o_ref):
    nd = v_ref.shape[0]
    q = lax.broadcasted_iota(I32, (GRID_W, 2 * GRID_W), 0)
    k = lax.broadcasted_iota(I32, (GRID_W, 2 * GRID_W), 1)
    kk = k % GRID_W
    cstart = jnp.clip(q - WIN_C // 2, 0, GRID_W - WIN_C)
    col_ok = (kk >= cstart) & (kk < cstart + WIN_C)
    left = k < GRID_W
    tiles = []
    for dr in range(nd):
        x = jnp.broadcast_to(v_ref[dr:dr + 1, :], (GRID_W, 2 * GRID_W))
        tiles.append((pltpu.roll(x, GRID_W + 1, axis=1, stride=1, stride_axis=0),
                      pltpu.roll(x, 1, axis=1, stride=1, stride_axis=0)))
    zero = jnp.zeros((GRID_W, 2 * GRID_W), F32)
    for j in range(nd + 1):
        lt = tiles[j - 1][0] if j >= 1 else zero
        rt = tiles[j][1] if j < nd else zero
        o_ref[j] = jnp.where(col_ok, jnp.where(left, lt, rt) * LOG2E, NEG_INF)


def _na_bias_table(rpb):
    depth, h, nd, nc = rpb.shape
    assert 2 * GRID_W == LANES and nd == 2 * WIN_R - 1 and nc == 2 * WIN_C - 1
    rpb = rpb.astype(F32)
    lo = GRID_W - WIN_C
    v = jnp.concatenate([jnp.broadcast_to(rpb[..., :1], rpb.shape[:3] + (lo,)), rpb,
                         jnp.broadcast_to(rpb[..., -1:], rpb.shape[:3] + (2 * GRID_W - lo - nc,))], axis=-1)
    return pl.pallas_call(
        _na_table_kernel,
        out_shape=jax.ShapeDtypeStruct((depth, h, nd + 1, GRID_W, 2 * GRID_W), F32),
        grid=(depth, h),
        in_specs=[pl.BlockSpec((None, None, nd, 2 * GRID_W), lambda l, i: (l, i, 0, 0))],
        out_specs=pl.BlockSpec((None, None, nd + 1, GRID_W, 2 * GRID_W), lambda l, i: (l, i, 0, 0, 0)),
        compiler_params=_params(2),
        name="na_table",
    )(v)


def _neighbourhood_attention(q, k, v, kc, vc, bias_tab, layer):
    b, n, w = q.shape
    rows = n // GRID_W
    tq = NA_QROWS * GRID_W
    nc = kc.shape[2]
    nkw = NA_KROWS * GRID_W
    return pl.pallas_call(
        functools.partial(_na_kernel, rows=rows),
        out_shape=jax.ShapeDtypeStruct((b, n, w), BF16),
        grid=(b, n // tq),
        in_specs=[
            pl.BlockSpec((1, tq, w), lambda i, j: (i, j, 0)),
            pl.BlockSpec((1, None, n, w), lambda i, j: (i, 0, 0, 0)),
            pl.BlockSpec((1, None, n, w), lambda i, j: (i, 0, 0, 0)),
            pl.BlockSpec((None, None, nc, w), lambda i, j: (i, layer, 0, 0)),
            pl.BlockSpec((None, None, nc, w), lambda i, j: (i, layer, 0, 0)),
            pl.BlockSpec((None,) + bias_tab.shape[1:], lambda i, j: (layer, 0, 0, 0, 0)),
        ],
        out_specs=pl.BlockSpec((1, tq, w), lambda i, j: (i, j, 0)),
        compiler_params=_params(2),
        name="na_attn",
    )(q, k, v, kc, vc, bias_tab)


def _sublane_scan(er, ei, tr, ti, pw_ref, base, lanes, reverse):
    row = lax.broadcasted_iota(I32, (SUBLANES, LANES), 0)
    if reverse:
        first = row == SUBLANES - 1
        xr = jnp.where(first, tr, pltpu.roll(er, SUBLANES - 1, axis=0))
        xi = jnp.where(first, ti, pltpu.roll(ei, SUBLANES - 1, axis=0))
    else:
        first = row == 0
        xr = jnp.where(first, tr, pltpu.roll(er, 1, axis=0))
        xi = jnp.where(first, ti, pltpu.roll(ei, 1, axis=0))
    for k, s in enumerate((1, 2, 4)):
        ar = pw_ref[base + 2 * k:base + 2 * k + 1, lanes]
        ai = pw_ref[base + 2 * k + 1:base + 2 * k + 2, lanes]
        if reverse:
            keep = row < SUBLANES - s
            sr = jnp.where(keep, pltpu.roll(xr, SUBLANES - s, axis=0), 0.0)
            si = jnp.where(keep, pltpu.roll(xi, SUBLANES - s, axis=0), 0.0)
        else:
            keep = row >= s
            sr = jnp.where(keep, pltpu.roll(xr, s, axis=0), 0.0)
            si = jnp.where(keep, pltpu.roll(xi, s, axis=0), 0.0)
        xr, xi = xr + (ar * sr - ai * si), xi + (ar * si + ai * sr)
    return xr, xi


def _s5_kernel(u_ref, h0re_ref, h0im_ref, bt_ref, ct_ref, a1_ref, ach_ref, pfix_ref, d_ref, wglu_ref, bglu_ref,
               o_ref, fre_ref, fim_ref, up_ref, yp_ref, buf_ref, car_ref, bmat_ref, cmat_ref):
    seq = u_ref.shape[2]
    ch = pfix_ref.shape[1]
    tt = ch * SUBLANES
    ntile = seq // tt
    nlt = SSM_N // LANES
    nhalf = SSM_WIDTH // LANES

    @pl.when(pl.program_id(0) == 0)
    def _():
        g, p, hc = SSM_GROUPS, SSM_STATE, SSM_GROUP_CH
        btb = bt_ref[...].astype(BF16)
        ctb = ct_ref[...].astype(BF16)
        for dc in range(4):
            r = lax.broadcasted_iota(I32, (4 * p, g * p), 0)
            c = lax.broadcasted_iota(I32, (4 * p, g * p), 1)
            sel = jnp.where(r == dc * p + c % p, 1.0, 0.0).astype(BF16)
            r = lax.broadcasted_iota(I32, (g * hc, g * p), 0)
            c = lax.broadcasted_iota(I32, (g * hc, g * p), 1)
            blk = jnp.where(r // hc == c // p, _dot(btb, sel), 0.0)
            bmat_ref[:, dc * g * p:(dc + 1) * g * p] = blk.astype(BF16)
            r = lax.broadcasted_iota(I32, (g * p, 4 * p), 0)
            c = lax.broadcasted_iota(I32, (g * p, 4 * p), 1)
            sel = jnp.where(c == dc * p + r % p, 1.0, 0.0).astype(BF16)
            r = lax.broadcasted_iota(I32, (g * p, g * hc), 0)
            c = lax.broadcasted_iota(I32, (g * p, g * hc), 1)
            blk = jnp.where(r // p == c // hc, _dot(sel, ctb), 0.0)
            cmat_ref[dc * g * p:(dc + 1) * g * p, :] = blk.astype(BF16)

    def permute_in(t, _):
        t0 = pl.multiple_of(t * tt, tt)
        for j in range(0, ch, 2):
            for hf in range(nhalf):
                lanes = slice(hf * LANES, (hf + 1) * LANES)
                rows = jnp.concatenate([u_ref[0, hf, pl.ds(t0 + j + jj, SUBLANES, stride=ch), :] for jj in range(2)],
                                       axis=0)
                r0 = pl.multiple_of(t0 + j * SUBLANES, 2 * SUBLANES)
                up_ref[pl.ds(r0, 2 * SUBLANES), lanes] = rows.astype(BF16)
                yp_ref[pl.ds(r0, 2 * SUBLANES), lanes] = rows * d_ref[:, lanes]
        return 0

    lax.fori_loop(0, ntile, permute_in, 0)

    for d in range(2):
        for lt in range(nlt):
            lanes = slice(lt * LANES, (lt + 1) * LANES)
            car_ref[d, 0, :, lanes] = jnp.broadcast_to(h0re_ref[0, d:d + 1, lanes], (SUBLANES, LANES))
            car_ref[d, 1, :, lanes] = jnp.broadcast_to(h0im_ref[0, d:d + 1, lanes], (SUBLANES, LANES))

    def tile(t, _):
        starts = []
        for d in range(2):
            tix = (ntile - 1 - t) if d == 1 else t
            t0 = pl.multiple_of(tix * tt, tt)
            starts.append(t0)
            buf_ref[d] = _dot(up_ref[pl.ds(t0, tt), :], bmat_ref[:, d * 2 * SSM_N:(d + 1) * 2 * SSM_N])
        for d in range(2):
            reverse = d == 1
            order = range(ch - 1, -1, -1) if reverse else range(ch)
            for lt in range(nlt):
                lre = slice(lt * LANES, (lt + 1) * LANES)
                lim = slice(SSM_N + lt * LANES, SSM_N + (lt + 1) * LANES)
                ar = a1_ref[2 * d:2 * d + 1, lre]
                ai = a1_ref[2 * d + 1:2 * d + 2, lre]
                cr = jnp.zeros((SUBLANES, LANES), F32)
                ci = jnp.zeros((SUBLANES, LANES), F32)
                for j in order:
                    rows = slice(j * SUBLANES, (j + 1) * SUBLANES)
                    cr, ci = (ar * cr - ai * ci) + buf_ref[d, rows, lre], (ar * ci + ai * cr) + buf_ref[d, rows, lim]
                    buf_ref[d, rows, lre] = cr
                    buf_ref[d, rows, lim] = ci
                tr, ti = car_ref[d, 0, :, lre], car_ref[d, 1, :, lre]
                gr, gi = _sublane_scan(cr, ci, tr, ti, ach_ref, 6 * d, lre, reverse)
                a_r, a_i = ach_ref[6 * d:6 * d + 1, lre], ach_ref[6 * d + 1:6 * d + 2, lre]
                nr, ni = (a_r * gr - a_i * gi) + cr, (a_r * gi + a_i * gr) + ci
                edge = 0 if reverse else SUBLANES - 1
                car_ref[d, 0, :, lre] = jnp.broadcast_to(nr[edge:edge + 1, :], (SUBLANES, LANES))
                car_ref[d, 1, :, lre] = jnp.broadcast_to(ni[edge:edge + 1, :], (SUBLANES, LANES))
                for j in range(ch):
                    rows = slice(j * SUBLANES, (j + 1) * SUBLANES)
                    pr = pfix_ref[2 * d, j:j + 1, lre]
                    pi = pfix_ref[2 * d + 1, j:j + 1, lre]
                    buf_ref[d, rows, lre] += pr * gr - pi * gi
                    buf_ref[d, rows, lim] += pr * gi + pi * gr
        for d in range(2):
            yp_ref[pl.ds(starts[d], tt), :] += _dot(buf_ref[d].astype(BF16),
                                                    cmat_ref[d * 2 * SSM_N:(d + 1) * 2 * SSM_N, :])
        return 0

    lax.fori_loop(0, ntile, tile, 0)

    for d in range(2):
        fre_ref[0, d:d + 1, :] = car_ref[d, 0, 0:1, :]
        fim_ref[0, d:d + 1, :] = car_ref[d, 1, 0:1, :]

    wglu = wglu_ref[...].astype(BF16)

    def glu(t, _):
        t0 = pl.multiple_of(t * tt, tt)
        g = jax.nn.gelu(yp_ref[pl.ds(t0, tt), :])
        out = g * jax.nn.sigmoid(_dot(g.astype(BF16), wglu) + bglu_ref[...])
        for j in range(ch):
            for hf in range(nhalf):
                o_ref[0, hf, pl.ds(t0 + j, SUBLANES, stride=ch), :] = out[j * SUBLANES:(j + 1) * SUBLANES,
                                                                          hf * LANES:(hf + 1) * LANES]
        return 0

    lax.fori_loop(0, ntile, glu, 0)


def _cmul(ar, ai, br, bi):
    return ar * br - ai * bi, ar * bi + ai * br


def _s5_tables(a_re, a_im, log_dt, b_re, b_im, c_re, c_im, seq_lens):
    depth = a_re.shape[0]
    g, p, hc = SSM_GROUPS, SSM_STATE, SSM_GROUP_CH
    lr, li = a_re.astype(F32), a_im.astype(F32)
    dt = jnp.exp(log_dt.astype(F32))[..., None]
    mag = jnp.exp(lr * dt)
    ar, ai = mag * jnp.cos(li * dt), mag * jnp.sin(li * dt)
    den = lr * lr + li * li
    qr = ((ar - 1.0) * lr + ai * li) / den
    qi = (ai * lr - (ar - 1.0) * li) / den
    br, bi = _cmul(qr[..., None], qi[..., None], b_re.astype(F32), b_im.astype(F32))
    bparts = jnp.stack([br, bi], axis=2)
    bmat = bparts.transpose(0, 3, 5, 1, 2, 4).reshape(depth, g * hc, 4 * p)
    cparts = jnp.stack([c_re.astype(F32), -c_im.astype(F32)], axis=2)
    cmat = cparts.transpose(0, 1, 2, 5, 3, 4).reshape(depth, 4 * p, g * hc)

    def power(k):
        kk = k[None, None, :, None, None]
        m = jnp.exp(kk * (lr * dt)[:, :, None])
        th = kk * (li * dt)[:, :, None]
        return (m * jnp.cos(th)).reshape(depth, 2, -1, g * p), (m * jnp.sin(th)).reshape(depth, 2, -1, g * p)

    a1 = jnp.stack([ar.reshape(depth, 2, g * p), ai.reshape(depth, 2, g * p)], axis=2).reshape(depth, 4, g * p)
    tabs = {}
    for seq in set(seq_lens):
        ch = min(seq, SCAN_TILE) // SUBLANES
        cr_, ci_ = power(jnp.asarray([ch, 2 * ch, 4 * ch], F32))
        ach = jnp.stack([cr_, ci_], axis=3).reshape(depth, 12, g * p)
        fr, fi = power(jnp.arange(1, ch + 1, dtype=F32))
        fr = jnp.stack([fr[:, 0], fr[:, 1, ::-1]], axis=1)
        fi = jnp.stack([fi[:, 0], fi[:, 1, ::-1]], axis=1)
        pfix = jnp.stack([fr, fi], axis=2).reshape(depth, 4, ch, g * p)
        tabs[seq] = (bmat, cmat, a1, ach, pfix)
    return tabs


def _s5(u, h0re, h0im, tabs, d_skip, w_glu, b_glu, layer):
    b, nh, seq, _ = u.shape
    w = nh * LANES
    tabs = tabs[seq]
    tile = tabs[4].shape[2] * SUBLANES
    lay = lambda a: pl.BlockSpec((None,) + a.shape[1:], lambda i: (layer,) + (0,) * (a.ndim - 1))
    return pl.pallas_call(
        _s5_kernel,
        out_shape=[jax.ShapeDtypeStruct((b, nh, seq, LANES), F32),
                   jax.ShapeDtypeStruct((b, 2, SSM_N), F32),
                   jax.ShapeDtypeStruct((b, 2, SSM_N), F32)],
        grid=(b,),
        in_specs=[
            pl.BlockSpec((1, nh, seq, LANES), lambda i: (i, 0, 0, 0)),
            pl.BlockSpec((1, 2, SSM_N), lambda i: (i, 0, 0)),
            pl.BlockSpec((1, 2, SSM_N), lambda i: (i, 0, 0)),
            lay(tabs[0]), lay(tabs[1]), lay(tabs[2]), lay(tabs[3]), lay(tabs[4]),
            pl.BlockSpec((None, 1, w), lambda i: (layer, 0, 0)),
            pl.BlockSpec((None, w, w), lambda i: (layer, 0, 0)),
            pl.BlockSpec((None, 1, w), lambda i: (layer, 0, 0)),
        ],
        out_specs=[pl.BlockSpec((1, nh, seq, LANES), lambda i: (i, 0, 0, 0)),
                   pl.BlockSpec((1, 2, SSM_N), lambda i: (i, 0, 0)),
                   pl.BlockSpec((1, 2, SSM_N), lambda i: (i, 0, 0))],
        scratch_shapes=[pltpu.VMEM((seq, w), BF16),
                        pltpu.VMEM((seq, w), F32),
                        pltpu.VMEM((2, tile, 2 * SSM_N), F32),
                        pltpu.VMEM((2, 2, SUBLANES, SSM_N), F32),
                        pltpu.VMEM((w, 4 * SSM_N), BF16),
                        pltpu.VMEM((4 * SSM_N, w), BF16)],
        compiler_params=_params(1),
        name="s5",
    )(u, h0re, h0im, *tabs, d_skip, w_glu, b_glu)


def _proj_out_kernel(oa_ref, ob_ref, oc_ref, w_ref, x_ref, mod_ref, g_ref, wr_ref, o_ref, h_ref, lg_ref, wbf_ref):
    @pl.when((pl.program_id(0) == 0) & (pl.program_id(1) == 0))
    def _():
        _cast_rows(wbf_ref, w_ref, 128)

    i1, i2 = NA_WIDTH, NA_WIDTH + DIFF_WIDTH
    for bi in range(x_ref.shape[0]):
        for r0 in range(0, x_ref.shape[1], ROW_TILE):
            rows = slice(r0, r0 + ROW_TILE)
            y = _dot(oa_ref[bi, rows, :].astype(BF16), wbf_ref[:i1, :])
            y += _dot(ob_ref[bi, rows, :].astype(BF16), wbf_ref[i1:i2, :])
            for hf in range(SSM_WIDTH // LANES):
                y += _dot(oc_ref[bi, hf, rows, :].astype(BF16), wbf_ref[i2 + hf * LANES:i2 + (hf + 1) * LANES, :])
            xn = x_ref[bi, rows, :] + mod_ref[0, 2:3, :] * y
            o_ref[bi, rows, :] = xn
            h = _modulated_norm(xn, g_ref[...], mod_ref[0, 3:4, :], mod_ref[0, 4:5, :])
            h_hi = h.astype(BF16)
            h_ref[bi, rows, :] = h_hi
            h_lo = (h - h_hi.astype(F32)).astype(BF16)
            t1 = _dot_nt(wr_ref[...], h_hi)
            t2 = _dot_nt(wr_ref[:N_EXPERTS, :], h_lo)
            lg_ref[bi, :, rows] = t1[:N_EXPERTS, :] + t1[N_EXPERTS:, :] + t2


def _proj_out(oa, ob, oc, w_out, x, mod, g_ffn, wr_t, layer):
    b, n, d = x.shape
    bb, tm = _step_tile(b, n, mod.shape[0], PROJ_OUT_CHUNKS)
    mod_map = (lambda i, j: (i, 0, 0)) if mod.shape[0] > 1 else (lambda i, j: (0, 0, 0))
    kw = w_out.shape[1]
    return pl.pallas_call(
        _proj_out_kernel,
        out_shape=[jax.ShapeDtypeStruct((b, n, d), F32),
                   jax.ShapeDtypeStruct((b, n, d), BF16),
                   jax.ShapeDtypeStruct((b, N_EXPERTS, n), F32)],
        grid=(b // bb, n // tm),
        in_specs=[
            pl.BlockSpec((bb, tm, NA_WIDTH), lambda i, j: (i, j, 0)),
            pl.BlockSpec((bb, tm, DIFF_WIDTH), lambda i, j: (i, j, 0)),
            pl.BlockSpec((bb, SSM_WIDTH // LANES, tm, LANES), lambda i, j: (i, 0, j, 0)),
            pl.BlockSpec((None, kw, d), lambda i, j: (layer, 0, 0), pipeline_mode=pl.Buffered(1)),
            pl.BlockSpec((bb, tm, d), lambda i, j: (i, j, 0)),
            pl.BlockSpec((1, 6, d), mod_map),
            pl.BlockSpec((None, 1, d), lambda i, j: (layer, 0, 0)),
            pl.BlockSpec((None, 2 * N_EXPERTS, d), lambda i, j: (layer, 0, 0)),
        ],
        out_specs=[pl.BlockSpec((bb, tm, d), lambda i, j: (i, j, 0)),
                   pl.BlockSpec((bb, tm, d), lambda i, j: (i, j, 0)),
                   pl.BlockSpec((bb, N_EXPERTS, tm), lambda i, j: (i, 0, j))],
        scratch_shapes=[pltpu.VMEM((kw, d), BF16)],
        compiler_params=_params(2),
        name="proj_out",
    )(oa, ob, oc, w_out, x, mod, g_ffn, wr_t)


def _excl_cumsum_lanes(m):
    e, n = m.shape
    blk = MXU_DIM
    nb = n // blk
    r = lax.broadcasted_iota(I32, (blk, blk), 0)
    c = lax.broadcasted_iota(I32, (blk, blk), 1)
    tri = jnp.where(r < c, 1.0, 0.0).astype(BF16)
    stacked = jnp.concatenate([m[:, k * blk:(k + 1) * blk] for k in range(nb)], axis=0).astype(BF16)
    within = _dot(stacked, tri)
    outs = []
    off = jnp.zeros((e, 1), F32)
    for k in range(nb):
        outs.append(within[k * e:(k + 1) * e, :] + off)
        off = off + jnp.sum(m[:, k * blk:(k + 1) * blk], axis=1, keepdims=True)
    return jnp.concatenate(outs, axis=1)


def _route_kernel(lg_ref, pos_ref, gate_ref, *, cap):
    b, e, n = lg_ref.shape
    lg = lg_ref[...]
    mx = lg.max(axis=1, keepdims=True)
    ex = jnp.exp(lg - mx)
    aff = (ex / ex.sum(axis=1, keepdims=True)).reshape(b * e, n)

    def search(it, cur):
        cand = cur | (jnp.int32(1) << (jnp.int32(30) - it))
        cnt = jnp.sum(jnp.where(aff >= pltpu.bitcast(cand, F32), 1.0, 0.0), axis=1, keepdims=True)
        return jnp.where(cnt >= cap, cand, cur)

    thr = pltpu.bitcast(lax.fori_loop(0, 31, search, jnp.zeros((b * e, 1), I32)), F32)
    gt = aff > thr
    eq = aff == thr
    need = cap - jnp.sum(jnp.where(gt, 1.0, 0.0), axis=1, keepdims=True)
    eq_rank = _excl_cumsum_lanes(jnp.where(eq, 1.0, 0.0))
    sel = gt | (eq & (eq_rank < need))
    slot = _excl_cumsum_lanes(jnp.where(sel, 1.0, 0.0))
    pos = jnp.where(sel, slot, -1.0).astype(I32)
    gate = jnp.where(sel, aff, 0.0)
    for i in range(b):
        pos_ref[:, i, 0, :] = pos[i * e:(i + 1) * e, :]
        gate_ref[:, i, 0, :] = gate[i * e:(i + 1) * e, :]


def _route(logits):
    b, e, n = logits.shape
    cap = EC_CAPACITY_FACTOR * n // N_EXPERTS
    return pl.pallas_call(
        functools.partial(_route_kernel, cap=cap),
        out_shape=[jax.ShapeDtypeStruct((e, b, 1, n), I32),
                   jax.ShapeDtypeStruct((e, b, 1, n), F32)],
        compiler_params=_params(0),
        name="route",
    )(logits)


def _one_hot_rows(pos_row, cap):
    n = pos_row.shape[1]
    slot = lax.broadcasted_iota(I32, (cap, n), 0)
    return slot == pos_row


def _gather_kernel(h_ref, pos_ref, gate_ref, xs_ref, gs_ref, *, cap):
    group = max(1, GATHER_ROWS // cap)
    for bi in range(h_ref.shape[0]):
        h = h_ref[bi]
        for e0 in range(0, N_EXPERTS, group):
            ohs = [_one_hot_rows(pos_ref[e, bi], cap) for e in range(e0, e0 + group)]
            xs = _dot(jnp.concatenate([jnp.where(oh, 1.0, 0.0).astype(BF16) for oh in ohs], axis=0), h).astype(BF16)
            for i, oh in enumerate(ohs):
                e = e0 + i
                rows = slice(bi * cap, (bi + 1) * cap)
                xs_ref[e, rows, :] = xs[i * cap:(i + 1) * cap, :]
                g = jnp.sum(jnp.where(oh, gate_ref[e, bi], 0.0), axis=1, keepdims=True)
                gs_ref[e, rows, :] = jnp.broadcast_to(g, (cap, LANES))


def _requests_per_step(b, n):
    bb = max(1, MOE_STEP_TOKENS // n)
    return bb if b % bb == 0 else 1


def _gather(h, pos, gate):
    b, n, d = h.shape
    cap = EC_CAPACITY_FACTOR * n // N_EXPERTS
    bb = _requests_per_step(b, n)
    return pl.pallas_call(
        functools.partial(_gather_kernel, cap=cap),
        out_shape=[jax.ShapeDtypeStruct((N_EXPERTS, b * cap, d), BF16),
                   jax.ShapeDtypeStruct((N_EXPERTS, b * cap, LANES), F32)],
        grid=(b // bb,),
        in_specs=[
            pl.BlockSpec((bb, n, d), lambda i: (i, 0, 0)),
            pl.BlockSpec((N_EXPERTS, bb, 1, n), lambda i: (0, i, 0, 0)),
            pl.BlockSpec((N_EXPERTS, bb, 1, n), lambda i: (0, i, 0, 0)),
        ],
        out_specs=[pl.BlockSpec((N_EXPERTS, bb * cap, d), lambda i: (0, i, 0)),
                   pl.BlockSpec((N_EXPERTS, bb * cap, LANES), lambda i: (0, i, 0))],
        compiler_params=_params(1),
        name="moe_gather",
    )(h, pos, gate)


def _ffn_kernel(xp_ref, gp_ref, xs_ref, gs_ref, wg_ref, wu_ref, wd_ref, yp_ref, ys_ref, acc_ref):
    j = pl.program_id(1)
    sp = xp_ref.shape[1]

    @pl.when((pl.program_id(0) == 0) & (j == 0))
    def _():
        acc_ref[...] = jnp.zeros_like(acc_ref)

    wg = wg_ref[...].astype(BF16)
    wu = wu_ref[...].astype(BF16)
    wd = wd_ref[...].astype(BF16)
    for x_ref, g_ref, y_ref, rows in ((xp_ref, gp_ref, yp_ref, slice(0, sp)),
                                      (xs_ref, gs_ref, ys_ref, slice(sp, acc_ref.shape[0]))):
        x = x_ref[0]
        a = _dot(x, wg)
        u = _dot(x, wu)
        hid = (a * jax.nn.sigmoid(a)) * u
        acc = jnp.where(j > 0, acc_ref[rows, :], 0.0) + _dot(hid.astype(BF16), wd)
        acc_ref[rows, :] = acc
        y_ref[0] = (acc * g_ref[0][:, 0:1]).astype(BF16)


def _ffn(xp, gp, xs, gs, w_gate, w_up, w_down, layer):
    e, sp, d = xp.shape
    ss = xs.shape[1]
    s = sp + ss
    ff = w_gate.shape[-1]
    tf = FF_TILE
    return pl.pallas_call(
        _ffn_kernel,
        out_shape=[jax.ShapeDtypeStruct((e, sp, d), BF16), jax.ShapeDtypeStruct((e, ss, d), BF16)],
        grid=(e, ff // tf),
        in_specs=[
            pl.BlockSpec((1, sp, d), lambda i, j: (i, 0, 0)),
            pl.BlockSpec((1, sp, LANES), lambda i, j: (i, 0, 0)),
            pl.BlockSpec((1, ss, d), lambda i, j: (i, 0, 0)),
            pl.BlockSpec((1, ss, LANES), lambda i, j: (i, 0, 0)),
            pl.BlockSpec((None, None, d, tf), lambda i, j: (layer, i, 0, j)),
            pl.BlockSpec((None, None, d, tf), lambda i, j: (layer, i, 0, j)),
            pl.BlockSpec((None, None, tf, d), lambda i, j: (layer, i, j, 0)),
        ],
        out_specs=[pl.BlockSpec((1, sp, d), lambda i, j: (i, 0, 0)),
                   pl.BlockSpec((1, ss, d), lambda i, j: (i, 0, 0))],
        scratch_shapes=[pltpu.VMEM((s, d), F32)],
        compiler_params=_params(2),
        name="moe_ffn",
    )(xp, gp, xs, gs, w_gate, w_up, w_down)


def _combine_kernel(ye_ref, pos_ref, x_ref, mod_ref, *rest, cap, final):
    if final:
        g_ref, o_ref = rest
    else:
        (o_ref,) = rest
    group = max(1, MXU_DIM // cap)
    for bi in range(x_ref.shape[0]):
        y = None
        for e0 in range(0, N_EXPERTS, group):
            oh = jnp.concatenate([jnp.where(_one_hot_rows(pos_ref[e, bi], cap), 1.0, 0.0).astype(BF16)
                                  for e in range(e0, e0 + group)], axis=0)
            ye = jnp.concatenate([ye_ref[e, bi * cap:(bi + 1) * cap, :] for e in range(e0, e0 + group)], axis=0)
            part = _dot_tn(oh, ye)
            y = part if y is None else y + part
        xn = x_ref[bi] + mod_ref[0, 5:6, :] * y
        if final:
            ms = jnp.mean(xn * xn, axis=-1, keepdims=True)
            xn = xn * lax.rsqrt(ms + EPS) * g_ref[...]
        o_ref[bi] = xn


def _combine(ye, pos, x, mod, final_norm):
    b, n, d = x.shape
    cap = EC_CAPACITY_FACTOR * n // N_EXPERTS
    tn = min(n, COMBINE_TILE)
    bm = mod.shape[0]
    bb = _requests_per_step(b, n) if bm == 1 else 1
    mod_map = (lambda i, j: (i, 0, 0)) if bm > 1 else (lambda i, j: (0, 0, 0))
    final = final_norm is not None
    in_specs = [
        pl.BlockSpec((N_EXPERTS, bb * cap, d), lambda i, j: (0, i, 0)),
        pl.BlockSpec((N_EXPERTS, bb, 1, tn), lambda i, j: (0, i, 0, j)),
        pl.BlockSpec((bb, tn, d), lambda i, j: (i, j, 0)),
        pl.BlockSpec((1, 6, d), mod_map),
    ]
    args = [ye, pos, x, mod]
    if final:
        in_specs.append(pl.BlockSpec((1, d), lambda i, j: (0, 0)))
        args.append(final_norm)
    return pl.pallas_call(
        functools.partial(_combine_kernel, cap=cap, final=final),
        out_shape=jax.ShapeDtypeStruct((b, n, d), F32),
        grid=(b // bb, n // tn),
        in_specs=in_specs,
        out_specs=pl.BlockSpec((bb, tn, d), lambda i, j: (i, j, 0)),
        compiler_params=_params(2),
        name="moe_combine",
    )(*args)


def _rope_tables(n):
    t = np.arange(n)
    row = (t // GRID_W).astype(np.float32)
    col = (t % GRID_W).astype(np.float32)
    nf = DIFF_QK // 4
    inv = np.float32(ROPE_BASE) ** (-np.arange(nf, dtype=np.float32) / np.float32(nf))
    lane = np.arange(DIFF_WIDTH)
    pos = np.where(((lane % DIFF_QK) < DIFF_QK // 2)[None, :], row[:, None], col[:, None])
    ang = (pos * inv[lane % nf][None, :]).astype(np.float32)
    first = (lane % (2 * nf)) < nf
    cos, sin = np.cos(ang).astype(np.float32), np.sin(ang).astype(np.float32)
    return jnp.asarray(cos), jnp.asarray(np.where(first[None, :], -sin, sin))


def kernel(x_prompt, x_sample, cache_na_k, cache_na_v, cache_diff_k, cache_diff_v, state_ssm_re, state_ssm_im,
           c, c_ctx, w_ada, b_ada, norm_mix, norm_ffn, w_in, w_out, na_rpb, diff_lambda, diff_subln,
           ssm_a_re, ssm_a_im, ssm_log_dt, ssm_b_re, ssm_b_im, ssm_c_re, ssm_c_im, ssm_d, ssm_w_glu, ssm_b_glu,
           w_router, w_gate, w_up, w_down, final_norm):
    depth = w_in.shape[0]
    bp, sp, d = x_prompt.shape
    bs, ss, _ = x_sample.shape
    assert d == D_MODEL and bs + 1 <= SUBLANES
    past = cache_na_k.shape[2]

    cond = jnp.zeros((SUBLANES, d), F32).at[0].set(c_ctx).at[1:1 + bs].set(c)
    mods = _ada(cond, w_ada, b_ada).reshape(depth, SUBLANES, 6, d)

    rope_tabs = _rope_tables(ss)
    kc_a = cache_na_k.reshape(bs, depth, past, NA_WIDTH)
    vc_a = cache_na_v.reshape(bs, depth, past, NA_WIDTH)
    kc_b = cache_diff_k.reshape(bs, depth, past, DIFF_WIDTH)
    vc_b = cache_diff_v.reshape(bs, depth, past, DIFF_WIDTH)
    subln = jnp.tile(diff_subln, (1, LANES // DIFF_V)).reshape(depth, 1, LANES)
    norm_mix = norm_mix.reshape(depth, 1, d)
    norm_ffn = norm_ffn.reshape(depth, 1, d)
    ssm_d = ssm_d.reshape(depth, 1, SSM_WIDTH)
    ssm_b_glu = ssm_b_glu.reshape(depth, 1, SSM_WIDTH)
    wr_t = jnp.swapaxes(w_router, 1, 2).astype(F32)
    wr_hi = wr_t.astype(BF16)
    wr_t = jnp.concatenate([wr_hi, (wr_t - wr_hi.astype(F32)).astype(BF16)], axis=1)
    fnorm = final_norm.reshape(1, d)
    zero_state = jnp.zeros((bp, 2, SSM_N), F32)
    tabs = _s5_tables(ssm_a_re, ssm_a_im, ssm_log_dt, ssm_b_re, ssm_b_im, ssm_c_re, ssm_c_im, (sp, ss))
    bias_tab = _na_bias_table(na_rpb)

    xp, xs = x_prompt, x_sample
    new_sre, new_sim = [], []
    caches = None
    for l in range(depth):
        lam_init = 0.8 - 0.6 * math.exp(-0.3 * l)
        mod_p = mods[l, 0:1]
        mod_s = mods[l, 1:1 + bs]
        diff_params = (diff_lambda, subln)

        qa, ka, va, qb, kb, vb, u = _proj_in(xp, mod_p, norm_mix, w_in, l, None, depth, caches)
        caches = (ka, va, kb, vb)
        o_a = _attention(qa, ka, va, l, None, l, None, lam_init)
        o_b = _attention(qb, kb, vb, l, None, l, diff_params, lam_init)
        o_c, fre, fim = _s5(u, zero_state, zero_state, tabs, ssm_d, ssm_w_glu, ssm_b_glu, l)
        xp, hp, lg_p = _proj_out(o_a, o_b, o_c, w_out, xp, mod_p, norm_ffn, wr_t, l)
        new_sre.append(fre.reshape(bp, 2, SSM_GROUPS, SSM_STATE))
        new_sim.append(fim.reshape(bp, 2, SSM_GROUPS, SSM_STATE))

        qa, ka, va, qb, kb, vb, u = _proj_in(xs, mod_s, norm_mix, w_in, l, rope_tabs, 0, None)
        o_a = _neighbourhood_attention(qa, ka, va, kc_a, vc_a, bias_tab, l)
        o_b = _attention(qb, kb, vb, 0, (kc_b, vc_b), l, diff_params, lam_init)
        h0re = state_ssm_re[:, l].reshape(bs, 2, SSM_N)
        h0im = state_ssm_im[:, l].reshape(bs, 2, SSM_N)
        o_c, _, _ = _s5(u, h0re, h0im, tabs, ssm_d, ssm_w_glu, ssm_b_glu, l)
        xs, hs, lg_s = _proj_out(o_a, o_b, o_c, w_out, xs, mod_s, norm_ffn, wr_t, l)

        pos_p, gate_p = _route(lg_p)
        pos_s, gate_s = _route(lg_s)
        xg_p, gs_p = _gather(hp, pos_p, gate_p)
        xg_s, gs_s = _gather(hs, pos_s, gate_s)
        ye_p, ye_s = _ffn(xg_p, gs_p, xg_s, gs_s, w_gate, w_up, w_down, l)
        last = l == depth - 1
        xp = _combine(ye_p, pos_p, xp, mod_p, fnorm if last else None)
        xs = _combine(ye_s, pos_s, xs, mod_s, fnorm if last else None)

    ka, va, kb, vb = caches
    return (xp, xs, ka.reshape(bp, depth, sp, NA_HEADS, HEAD_DIM), va.reshape(bp, depth, sp, NA_HEADS, HEAD_DIM),
            kb.reshape(bp, depth, sp, DIFF_HEADS, DIFF_V), vb.reshape(bp, depth, sp, DIFF_HEADS, DIFF_V),
            jnp.stack(new_sre, axis=1), jnp.stack(new_sim, axis=1))
```

```python
import functools
import math

import jax
import jax.numpy as jnp
import numpy as np
from jax import lax
from jax.experimental import pallas as pl
from jax.experimental.pallas import tpu as pltpu

F32 = jnp.float32
BF16 = jnp.bfloat16
I32 = jnp.int32

D_MODEL = 1024
GRID_W = 64
HEAD_DIM = 64
NA_HEADS = 8
NA_WIDTH = NA_HEADS * HEAD_DIM
WIN_R = 8
WIN_C = 16
DIFF_HEADS = 4
DIFF_QK = 32
DIFF_V = 64
DIFF_WIDTH = DIFF_HEADS * DIFF_V
SSM_GROUPS = 16
SSM_GROUP_CH = 16
SSM_WIDTH = SSM_GROUPS * SSM_GROUP_CH
SSM_STATE = 64
SSM_N = SSM_GROUPS * SSM_STATE
IN_WIDTH = 3 * NA_WIDTH + 3 * DIFF_WIDTH + SSM_WIDTH
N_EXPERTS = 16
EC_CAPACITY_FACTOR = 2
ROPE_BASE = 10000.0
EPS = 1e-6
NEG_INF = -1e30
LOG2E = math.log2(math.e)

LANES = 128
SUBLANES = 8
MXU_DIM = 256
VMEM_LIMIT_CAP = 60000 * 1024

ROW_TILE = 256
PROJ_IN_CHUNKS = 4
PROJ_OUT_CHUNKS = 2
ATTN_Q_TILE = 512
ATTN_REQUESTS_PER_STEP = 4
NA_QROWS = 4
NA_KROWS = 12
SCAN_TILE = 512
FF_TILE = 1024
COMBINE_TILE = 1024
GATHER_ROWS = 512
MOE_STEP_TOKENS = 1024
DEN_IN_MATMUL_MIN_KEYS = 1024


def _params(ndims):
    return pltpu.CompilerParams(
        dimension_semantics=("arbitrary",) * ndims,
        vmem_limit_bytes=VMEM_LIMIT_CAP,
    )


def _step_tile(b, n, n_mod, chunks):
    tm = min(n, ROW_TILE * chunks)
    bb = max(1, ROW_TILE * chunks // tm)
    if n_mod > 1 or b % bb:
        bb = 1
    return bb, tm


def _dot(a, b):
    return jnp.dot(a, b, preferred_element_type=F32)


def _dot_nt(a, b):
    return lax.dot_general(a, b, (((1,), (1,)), ((), ())), preferred_element_type=F32)


def _dot_tn(a, b):
    return lax.dot_general(a, b, (((0,), (0,)), ((), ())), preferred_element_type=F32)


def _cast_rows(dst_ref, src_ref, rows):
    n = src_ref.shape[0]
    for r in range(0, n, rows):
        dst_ref[r:r + rows, :] = src_ref[r:r + rows, :].astype(BF16)


def _modulated_norm(x, g, shift, scale):
    ms = jnp.mean(x * x, axis=-1, keepdims=True)
    return (x * lax.rsqrt(ms + EPS) * g) * (1.0 + scale) + shift


def _ada_kernel(c_ref, w_ref, b_ref, o_ref):
    c = c_ref[...]
    s = c * jax.nn.sigmoid(c)
    o_ref[0] = _dot(s.astype(BF16), w_ref[0].astype(BF16)) + b_ref[0]


def _ada(cond, w_ada, b_ada):
    depth = w_ada.shape[0]
    tn = 1536
    nt = 6 * D_MODEL // tn
    return pl.pallas_call(
        _ada_kernel,
        out_shape=jax.ShapeDtypeStruct((depth, SUBLANES, 6 * D_MODEL), F32),
        grid=(depth, nt),
        in_specs=[
            pl.BlockSpec((SUBLANES, D_MODEL), lambda l, j: (0, 0)),
            pl.BlockSpec((1, D_MODEL, tn), lambda l, j: (l, 0, j)),
            pl.BlockSpec((1, 1, tn), lambda l, j: (l, 0, j)),
        ],
        out_specs=pl.BlockSpec((1, SUBLANES, tn), lambda l, j: (l, 0, j)),
        compiler_params=_params(2),
        name="ada",
    )(cond, w_ada, b_ada.reshape(depth, 1, 6 * D_MODEL))


def _rope_apply(x, cos, sin_signed):
    lane = lax.broadcasted_iota(I32, (1, LANES), 1)
    first = (lane % 16) < 8
    outs = []
    for t in range(x.shape[1] // LANES):
        xt = x[:, t * LANES:(t + 1) * LANES]
        partner = jnp.where(first, pltpu.roll(xt, LANES - 8, axis=1), pltpu.roll(xt, 8, axis=1))
        outs.append(xt * cos[:, t * LANES:(t + 1) * LANES] + partner * sin_signed[:, t * LANES:(t + 1) * LANES])
    return jnp.concatenate(outs, axis=1)


def _proj_in_kernel(*refs, rope, n_alias):
    x_ref, mod_ref, g_ref, w_ref = refs[:4]
    pos = 4
    if rope:
        cos_ref, sin_ref = refs[pos:pos + 2]
        pos += 2
    pos += n_alias
    qa_ref, ka_ref, va_ref, qb_ref, kb_ref, vb_ref, u_ref, wbf_ref = refs[pos:pos + 8]

    @pl.when((pl.program_id(0) == 0) & (pl.program_id(1) == 0))
    def _():
        _cast_rows(wbf_ref, w_ref, 128)

    i1, i2, i3 = NA_WIDTH, 2 * NA_WIDTH, 3 * NA_WIDTH
    i4, i5, i6 = i3 + DIFF_WIDTH, i3 + 2 * DIFF_WIDTH, i3 + 3 * DIFF_WIDTH
    for bi in range(x_ref.shape[0]):
        for r0 in range(0, x_ref.shape[1], ROW_TILE):
            rows = slice(r0, r0 + ROW_TILE)

            def put(ref, val):
                for s in range(ref.shape[1]):
                    ref[bi, s, rows, :] = val.astype(ref.dtype)

            h = _modulated_norm(x_ref[bi, rows, :], g_ref[...], mod_ref[0, 0:1, :], mod_ref[0, 1:2, :])
            z = _dot(h.astype(BF16), wbf_ref[...])
            qa_ref[bi, rows, :] = (z[:, :i1] * (HEAD_DIM ** -0.5 * LOG2E)).astype(qa_ref.dtype)
            put(ka_ref, z[:, i1:i2])
            put(va_ref, z[:, i2:i3])
            qb = z[:, i3:i4]
            kb = z[:, i4:i5]
            if rope:
                qb = _rope_apply(qb, cos_ref[rows, :], sin_ref[rows, :])
                kb = _rope_apply(kb, cos_ref[rows, :], sin_ref[rows, :])
            qb_ref[bi, rows, :] = (qb * (DIFF_QK ** -0.5 * LOG2E)).astype(qb_ref.dtype)
            put(kb_ref, kb)
            put(vb_ref, z[:, i5:i6])
            for hf in range(SSM_WIDTH // LANES):
                u_ref[bi, hf, rows, :] = z[:, i6 + hf * LANES:i6 + (hf + 1) * LANES]


def _proj_in(x, mod, g_norm, w_in, layer, rope_tabs, cache_slots, caches):
    b, n, d = x.shape
    bb, tm = _step_tile(b, n, mod.shape[0], PROJ_IN_CHUNKS)
    rope = rope_tabs is not None
    mod_map = (lambda i, j: (i, 0, 0)) if mod.shape[0] > 1 else (lambda i, j: (0, 0, 0))
    in_specs = [
        pl.BlockSpec((bb, tm, d), lambda i, j: (i, j, 0)),
        pl.BlockSpec((1, 6, d), mod_map),
        pl.BlockSpec((None, 1, d), lambda i, j: (layer, 0, 0)),
        pl.BlockSpec((None, d, IN_WIDTH), lambda i, j: (layer, 0, 0), pipeline_mode=pl.Buffered(1)),
    ]
    args = [x, mod, g_norm, w_in]
    if rope:
        in_specs += [pl.BlockSpec((tm, DIFF_WIDTH), lambda i, j: (j, 0))] * 2
        args += list(rope_tabs)
    aliases = {}
    kv_out = (1, 2, 4, 5)
    if caches is not None:
        for c, o in zip(caches, kv_out):
            aliases[len(args)] = o
            in_specs.append(pl.BlockSpec(memory_space=pl.ANY))
            args.append(c)
    widths = (NA_WIDTH, NA_WIDTH, NA_WIDTH, DIFF_WIDTH, DIFF_WIDTH, DIFF_WIDTH)
    nslab = SSM_WIDTH // LANES
    out_shape, out_specs = [], []
    for o, w in enumerate(widths):
        if o not in kv_out:
            out_shape.append(jax.ShapeDtypeStruct((b, n, w), BF16))
            out_specs.append(pl.BlockSpec((bb, tm, w), lambda i, j: (i, j, 0)))
        elif cache_slots == 0:
            out_shape.append(jax.ShapeDtypeStruct((b, 1, n, w), BF16))
            out_specs.append(pl.BlockSpec((bb, 1, tm, w), lambda i, j: (i, 0, j, 0)))
        elif caches is None:
            out_shape.append(jax.ShapeDtypeStruct((b, cache_slots, n, w), F32))
            out_specs.append(pl.BlockSpec((bb, cache_slots, tm, w), lambda i, j: (i, 0, j, 0)))
        else:
            out_shape.append(jax.ShapeDtypeStruct((b, cache_slots, n, w), F32))
            out_specs.append(pl.BlockSpec((bb, 1, tm, w), lambda i, j: (i, layer, j, 0)))
    out_shape.append(jax.ShapeDtypeStruct((b, nslab, n, LANES), F32))
    out_specs.append(pl.BlockSpec((bb, nslab, tm, LANES), lambda i, j: (i, 0, j, 0)))
    return pl.pallas_call(
        functools.partial(_proj_in_kernel, rope=rope, n_alias=len(aliases)),
        out_shape=out_shape,
        grid=(b // bb, n // tm),
        in_specs=in_specs,
        out_specs=out_specs,
        scratch_shapes=[pltpu.VMEM((d, IN_WIDTH), BF16)],
        input_output_aliases=aliases,
        compiler_params=_params(2),
        name="proj_in",
    )(*args)


def _masked_query(qp, mask):
    return jnp.where(mask, qp, jnp.zeros_like(qp))


def _with_ones(v):
    return jnp.concatenate([v, jnp.ones_like(v)], axis=1)


def _softmax_av_wide(qm, ks, vs, biases):
    ss = []
    for k, bias in zip(ks, biases):
        s = _dot_nt(qm, k)
        if bias is not None:
            s = bias(s)
        ss.append(s)
    m = ss[0].max(axis=-1, keepdims=True)
    for s in ss[1:]:
        m = jnp.maximum(m, s.max(axis=-1, keepdims=True))
    acc = None
    for s, v in zip(ss, vs):
        pv = _dot(jnp.exp2(s - m).astype(BF16), v)
        acc = pv if acc is None else acc + pv
    return acc[:, :LANES] / acc[:, LANES:]


def _softmax_av(qm, ks, vs, biases):
    ss = []
    for k, bias in zip(ks, biases):
        s = _dot_nt(qm, k)
        if bias is not None:
            s = bias(s)
        ss.append(s)
    m = ss[0].max(axis=-1, keepdims=True)
    for s in ss[1:]:
        m = jnp.maximum(m, s.max(axis=-1, keepdims=True))
    acc = None
    den = None
    for s, v in zip(ss, vs):
        p = jnp.exp2(s - m)
        psum = p.sum(axis=-1, keepdims=True)
        den = psum if den is None else den + psum
        pv = _dot(p.astype(BF16), v)
        acc = pv if acc is None else acc + pv
    return acc / den


def _diff_lambda(lam_ref, lam_init):
    lp = lam_ref[...]
    a = jnp.sum(lp[0:1] * lp[1:2], axis=-1, keepdims=True)
    b = jnp.sum(lp[2:3] * lp[3:4], axis=-1, keepdims=True)
    return jnp.exp(a) - jnp.exp(b) + lam_init


def _attn_kernel(*refs, has_ctx, diff, lam_init):
    refs = list(refs)
    q_ref, k_ref, v_ref = refs[:3]
    pos = 3
    if has_ctx:
        kc_ref, vc_ref = refs[pos:pos + 2]
        pos += 2
    if diff:
        lam_ref, subln_ref = refs[pos:pos + 2]
        pos += 2
    o_ref = refs[pos]
    width = q_ref.shape[2]
    lane = lax.broadcasted_iota(I32, (1, LANES), 1)
    if diff:
        lam = _diff_lambda(lam_ref, lam_init)
    for bi, p in [(bi, p) for bi in range(q_ref.shape[0]) for p in range(width // LANES)]:
        sl = slice(p * LANES, (p + 1) * LANES)
        qp = q_ref[bi, :, sl]
        ks = [k_ref[bi, :, sl].astype(BF16)]
        vs = [v_ref[bi, :, sl].astype(BF16)]
        if has_ctx:
            ks.append(kc_ref[:, sl].astype(BF16))
            vs.append(vc_ref[:, sl].astype(BF16))
        nob = [None] * len(ks)
        tq = qp.shape[0]
        wide = sum(k.shape[0] for k in ks) >= DEN_IN_MATMUL_MIN_KEYS
        if wide:
            vs = [_with_ones(v) for v in vs]
        pair = None
        for sub in range(LANES // HEAD_DIM):
            head_mask = (lane // HEAD_DIM) == sub
            if diff:
                qmasks = [(lane // DIFF_QK) == (2 * sub + half) for half in range(2)]
                av = _softmax_av_wide if wide else _softmax_av
                os_ = [av(_masked_query(qp, mk), ks, vs, nob) for mk in qmasks]
                o = jnp.where(head_mask, os_[0] - lam * os_[1], 0.0)
                ms = jnp.sum(o * o, axis=-1, keepdims=True) * (1.0 / DIFF_V)
                o = (o * lax.rsqrt(ms + EPS) * subln_ref[...]) * (1.0 - lam_init)
            else:
                qm = _masked_query(qp, head_mask)
                o = _softmax_av_wide(qm, ks, vs, nob) if wide else _softmax_av(qm, ks, vs, nob)
                o = jnp.where(head_mask, o, 0.0)
            pair = o if pair is None else pair + o
        o_ref[bi, :, sl] = pair.astype(o_ref.dtype)


def _attention(q, k, v, kv_slot, ctx_kv, layer, diff_params, lam_init):
    b, nq, w = q.shape
    nk = k.shape[2]
    tq = min(nq, ATTN_Q_TILE)
    has_ctx = ctx_kv is not None
    diff = diff_params is not None
    bb = ATTN_REQUESTS_PER_STEP if (nq == tq and not has_ctx and b % ATTN_REQUESTS_PER_STEP == 0) else 1
    in_specs = [
        pl.BlockSpec((bb, tq, w), lambda i, j: (i, j, 0)),
        pl.BlockSpec((bb, None, nk, w), lambda i, j: (i, kv_slot, 0, 0)),
        pl.BlockSpec((bb, None, nk, w), lambda i, j: (i, kv_slot, 0, 0)),
    ]
    args = [q, k, v]
    nc = 0
    if has_ctx:
        nc = ctx_kv[0].shape[2]
        in_specs += [pl.BlockSpec((None, None, nc, w), lambda i, j: (i, layer, 0, 0))] * 2
        args += list(ctx_kv)
    if diff:
        lam_p, subln = diff_params
        in_specs += [
            pl.BlockSpec((None, 4, DIFF_QK), lambda i, j: (layer, 0, 0)),
            pl.BlockSpec((None, 1, LANES), lambda i, j: (layer, 0, 0)),
        ]
        args += [lam_p, subln]
    return pl.pallas_call(
        functools.partial(_attn_kernel, has_ctx=has_ctx, diff=diff, lam_init=lam_init),
        out_shape=jax.ShapeDtypeStruct((b, nq, w), BF16),
        grid=(b // bb, nq // tq),
        in_specs=in_specs,
        out_specs=pl.BlockSpec((bb, tq, w), lambda i, j: (i, j, 0)),
        compiler_params=_params(2),
        name="attn_diff" if diff else "attn",
    )(*args)


def _na_kernel(q_ref, k_ref, v_ref, kc_ref, vc_ref, bias_ref, o_ref, *, rows):
    i = pl.program_id(1)
    tq = NA_QROWS * GRID_W
    nkw = NA_KROWS * GRID_W
    ustart = jnp.clip(NA_QROWS * i - WIN_R // 2, 0, rows - NA_KROWS)
    koff = pl.multiple_of(ustart * GRID_W, GRID_W)
    delta = ustart - NA_QROWS * i
    qrow = NA_QROWS * i + lax.broadcasted_iota(I32, (tq, nkw), 0) // GRID_W
    krow = ustart + lax.broadcasted_iota(I32, (tq, nkw), 1) // GRID_W
    wstart = jnp.clip(qrow - WIN_R // 2, 0, rows - WIN_R)
    row_ok = (krow >= wstart) & (krow < wstart + WIN_R)
    nsub = LANES // HEAD_DIM
    lane = lax.broadcasted_iota(I32, (1, LANES), 1)
    for p in range(NA_WIDTH // LANES):
        sl = slice(p * LANES, (p + 1) * LANES)
        qp = q_ref[0, :, sl]
        ks = [k_ref[0, pl.ds(koff, nkw), sl].astype(BF16), kc_ref[:, sl].astype(BF16)]
        vs = [_with_ones(v_ref[0, pl.ds(koff, nkw), sl].astype(BF16)), _with_ones(vc_ref[:, sl].astype(BF16))]
        outs = []
        for sub in range(nsub):
            def win_bias(s, h=p * nsub + sub):
                rows_ = []
                for ri in range(NA_QROWS):
                    tiles = []
                    for pj in range(NA_KROWS // 2):
                        dr = delta + 2 * pj - ri
                        idx = jnp.clip(dr, -WIN_R, WIN_R - 1) + WIN_R
                        tiles.append(bias_ref[h, idx])
                    rows_.append(jnp.concatenate(tiles, axis=1))
                bias = jnp.concatenate(rows_, axis=0)
                return jnp.where(row_ok, s + bias, NEG_INF)

            outs.append(_softmax_av_wide(_masked_query(qp, (lane // HEAD_DIM) == sub), ks, vs, [win_bias, None]))
        pair = jnp.where((lane // HEAD_DIM) == 0, outs[0], outs[1])
        o_ref[0, :, sl] = pair.astype(o_ref.dtype)


def _na_table_kernel(v_ref, o_ref):
    nd = v_ref.shape[0]
    q = lax.broadcasted_iota(I32, (GRID_W, 2 * GRID_W), 0)
    k = lax.broadcasted_iota(I32, (GRID_W, 2 * GRID_W), 1)
    kk = k % GRID_W
    cstart = jnp.clip(q - WIN_C // 2, 0, GRID_W - WIN_C)
    col_ok = (kk >= cstart) & (kk < cstart + WIN_C)
    left = k < GRID_W
    tiles = []
    for dr in range(nd):
        x = jnp.broadcast_to(v_ref[dr:dr + 1, :], (GRID_W, 2 * GRID_W))
        tiles.append((pltpu.roll(x, GRID_W + 1, axis=1, stride=1, stride_axis=0),
                      pltpu.roll(x, 1, axis=1, stride=1, stride_axis=0)))
    zero = jnp.zeros((GRID_W, 2 * GRID_W), F32)
    for j in range(nd + 1):
        lt = tiles[j - 1][0] if j >= 1 else zero
        rt = tiles[j][1] if j < nd else zero
        o_ref[j] = jnp.where(col_ok, jnp.where(left, lt, rt) * LOG2E, NEG_INF)


def _na_bias_table(rpb):
    depth, h, nd, nc = rpb.shape
    assert 2 * GRID_W == LANES and nd == 2 * WIN_R - 1 and nc == 2 * WIN_C - 1
    rpb = rpb.astype(F32)
    lo = GRID_W - WIN_C
    v = jnp.concatenate([jnp.broadcast_to(rpb[..., :1], rpb.shape[:3] + (lo,)), rpb,
                         jnp.broadcast_to(rpb[..., -1:], rpb.shape[:3] + (2 * GRID_W - lo - nc,))], axis=-1)
    return pl.pallas_call(
        _na_table_kernel,
        out_shape=jax.ShapeDtypeStruct((depth, h, nd + 1, GRID_W, 2 * GRID_W), F32),
        grid=(depth, h),
        in_specs=[pl.BlockSpec((None, None, nd, 2 * GRID_W), lambda l, i: (l, i, 0, 0))],
        out_specs=pl.BlockSpec((None, None, nd + 1, GRID_W, 2 * GRID_W), lambda l, i: (l, i, 0, 0, 0)),
        compiler_params=_params(2),
        name="na_table",
    )(v)


def _neighbourhood_attention(q, k, v, kc, vc, bias_tab, layer):
    b, n, w = q.shape
    rows = n // GRID_W
    tq = NA_QROWS * GRID_W
    nc = kc.shape[2]
    nkw = NA_KROWS * GRID_W
    return pl.pallas_call(
        functools.partial(_na_kernel, rows=rows),
        out_shape=jax.ShapeDtypeStruct((b, n, w), BF16),
        grid=(b, n // tq),
        in_specs=[
            pl.BlockSpec((1, tq, w), lambda i, j: (i, j, 0)),
            pl.BlockSpec((1, None, n, w), lambda i, j: (i, 0, 0, 0)),
            pl.BlockSpec((1, None, n, w), lambda i, j: (i, 0, 0, 0)),
            pl.BlockSpec((None, None, nc, w), lambda i, j: (i, layer, 0, 0)),
            pl.BlockSpec((None, None, nc, w), lambda i, j: (i, layer, 0, 0)),
            pl.BlockSpec((None,) + bias_tab.shape[1:], lambda i, j: (layer, 0, 0, 0, 0)),
        ],
        out_specs=pl.BlockSpec((1, tq, w), lambda i, j: (i, j, 0)),
        compiler_params=_params(2),
        name="na_attn",
    )(q, k, v, kc, vc, bias_tab)


def _sublane_scan(er, ei, tr, ti, pw_ref, base, lanes, reverse):
    row = lax.broadcasted_iota(I32, (SUBLANES, LANES), 0)
    if reverse:
        first = row == SUBLANES - 1
        xr = jnp.where(first, tr, pltpu.roll(er, SUBLANES - 1, axis=0))
        xi = jnp.where(first, ti, pltpu.roll(ei, SUBLANES - 1, axis=0))
    else:
        first = row == 0
        xr = jnp.where(first, tr, pltpu.roll(er, 1, axis=0))
        xi = jnp.where(first, ti, pltpu.roll(ei, 1, axis=0))
    for k, s in enumerate((1, 2, 4)):
        ar = pw_ref[base + 2 * k:base + 2 * k + 1, lanes]
        ai = pw_ref[base + 2 * k + 1:base + 2 * k + 2, lanes]
        if reverse:
            keep = row < SUBLANES - s
            sr = jnp.where(keep, pltpu.roll(xr, SUBLANES - s, axis=0), 0.0)
            si = jnp.where(keep, pltpu.roll(xi, SUBLANES - s, axis=0), 0.0)
        else:
            keep = row >= s
            sr = jnp.where(keep, pltpu.roll(xr, s, axis=0), 0.0)
            si = jnp.where(keep, pltpu.roll(xi, s, axis=0), 0.0)
        xr, xi = xr + (ar * sr - ai * si), xi + (ar * si + ai * sr)
    return xr, xi


def _s5_kernel(u_ref, h0re_ref, h0im_ref, bt_ref, ct_ref, a1_ref, ach_ref, pfix_ref, d_ref, wglu_ref, bglu_ref,
               o_ref, fre_ref, fim_ref, up_ref, yp_ref, buf_ref, car_ref, bmat_ref, cmat_ref):
    seq = u_ref.shape[2]
    ch = pfix_ref.shape[1]
    tt = ch * SUBLANES
    ntile = seq // tt
    nlt = SSM_N // LANES
    nhalf = SSM_WIDTH // LANES

    @pl.when(pl.program_id(0) == 0)
    def _():
        g, p, hc = SSM_GROUPS, SSM_STATE, SSM_GROUP_CH
        btb = bt_ref[...].astype(BF16)
        ctb = ct_ref[...].astype(BF16)
        for dc in range(4):
            r = lax.broadcasted_iota(I32, (4 * p, g * p), 0)
            c = lax.broadcasted_iota(I32, (4 * p, g * p), 1)
            sel = jnp.where(r == dc * p + c % p, 1.0, 0.0).astype(BF16)
            r = lax.broadcasted_iota(I32, (g * hc, g * p), 0)
            c = lax.broadcasted_iota(I32, (g * hc, g * p), 1)
            blk = jnp.where(r // hc == c // p, _dot(btb, sel), 0.0)
            bmat_ref[:, dc * g * p:(dc + 1) * g * p] = blk.astype(BF16)
            r = lax.broadcasted_iota(I32, (g * p, 4 * p), 0)
            c = lax.broadcasted_iota(I32, (g * p, 4 * p), 1)
            sel = jnp.where(c == dc * p + r % p, 1.0, 0.0).astype(BF16)
            r = lax.broadcasted_iota(I32, (g * p, g * hc), 0)
            c = lax.broadcasted_iota(I32, (g * p, g * hc), 1)
            blk = jnp.where(r // p == c // hc, _dot(sel, ctb), 0.0)
            cmat_ref[dc * g * p:(dc + 1) * g * p, :] = blk.astype(BF16)

    def permute_in(t, _):
        t0 = pl.multiple_of(t * tt, tt)
        for j in range(0, ch, 2):
            for hf in range(nhalf):
                lanes = slice(hf * LANES, (hf + 1) * LANES)
                rows = jnp.concatenate([u_ref[0, hf, pl.ds(t0 + j + jj, SUBLANES, stride=ch), :] for jj in range(2)],
                                       axis=0)
                r0 = pl.multiple_of(t0 + j * SUBLANES, 2 * SUBLANES)
                up_ref[pl.ds(r0, 2 * SUBLANES), lanes] = rows.astype(BF16)
                yp_ref[pl.ds(r0, 2 * SUBLANES), lanes] = rows * d_ref[:, lanes]
        return 0

    lax.fori_loop(0, ntile, permute_in, 0)

    for d in range(2):
        for lt in range(nlt):
            lanes = slice(lt * LANES, (lt + 1) * LANES)
            car_ref[d, 0, :, lanes] = jnp.broadcast_to(h0re_ref[0, d:d + 1, lanes], (SUBLANES, LANES))
            car_ref[d, 1, :, lanes] = jnp.broadcast_to(h0im_ref[0, d:d + 1, lanes], (SUBLANES, LANES))

    def tile(t, _):
        starts = []
        for d in range(2):
            tix = (ntile - 1 - t) if d == 1 else t
            t0 = pl.multiple_of(tix * tt, tt)
            starts.append(t0)
            buf_ref[d] = _dot(up_ref[pl.ds(t0, tt), :], bmat_ref[:, d * 2 * SSM_N:(d + 1) * 2 * SSM_N])
        for d in range(2):
            reverse = d == 1
            order = range(ch - 1, -1, -1) if reverse else range(ch)
            for lt in range(nlt):
                lre = slice(lt * LANES, (lt + 1) * LANES)
                lim = slice(SSM_N + lt * LANES, SSM_N + (lt + 1) * LANES)
                ar = a1_ref[2 * d:2 * d + 1, lre]
                ai = a1_ref[2 * d + 1:2 * d + 2, lre]
                cr = jnp.zeros((SUBLANES, LANES), F32)
                ci = jnp.zeros((SUBLANES, LANES), F32)
                for j in order:
                    rows = slice(j * SUBLANES, (j + 1) * SUBLANES)
                    cr, ci = (ar * cr - ai * ci) + buf_ref[d, rows, lre], (ar * ci + ai * cr) + buf_ref[d, rows, lim]
                    buf_ref[d, rows, lre] = cr
                    buf_ref[d, rows, lim] = ci
                tr, ti = car_ref[d, 0, :, lre], car_ref[d, 1, :, lre]
                gr, gi = _sublane_scan(cr, ci, tr, ti, ach_ref, 6 * d, lre, reverse)
                a_r, a_i = ach_ref[6 * d:6 * d + 1, lre], ach_ref[6 * d + 1:6 * d + 2, lre]
                nr, ni = (a_r * gr - a_i * gi) + cr, (a_r * gi + a_i * gr) + ci
                edge = 0 if reverse else SUBLANES - 1
                car_ref[d, 0, :, lre] = jnp.broadcast_to(nr[edge:edge + 1, :], (SUBLANES, LANES))
                car_ref[d, 1, :, lre] = jnp.broadcast_to(ni[edge:edge + 1, :], (SUBLANES, LANES))
                for j in range(ch):
                    rows = slice(j * SUBLANES, (j + 1) * SUBLANES)
                    pr = pfix_ref[2 * d, j:j + 1, lre]
                    pi = pfix_ref[2 * d + 1, j:j + 1, lre]
                    buf_ref[d, rows, lre] += pr * gr - pi * gi
                    buf_ref[d, rows, lim] += pr * gi + pi * gr
        for d in range(2):
            yp_ref[pl.ds(starts[d], tt), :] += _dot(buf_ref[d].astype(BF16),
                                                    cmat_ref[d * 2 * SSM_N:(d + 1) * 2 * SSM_N, :])
        return 0

    lax.fori_loop(0, ntile, tile, 0)

    for d in range(2):
        fre_ref[0, d:d + 1, :] = car_ref[d, 0, 0:1, :]
        fim_ref[0, d:d + 1, :] = car_ref[d, 1, 0:1, :]

    wglu = wglu_ref[...].astype(BF16)

    def glu(t, _):
        t0 = pl.multiple_of(t * tt, tt)
        g = jax.nn.gelu(yp_ref[pl.ds(t0, tt), :])
        out = g * jax.nn.sigmoid(_dot(g.astype(BF16), wglu) + bglu_ref[...])
        for j in range(ch):
            for hf in range(nhalf):
                o_ref[0, hf, pl.ds(t0 + j, SUBLANES, stride=ch), :] = out[j * SUBLANES:(j + 1) * SUBLANES,
                                                                          hf * LANES:(hf + 1) * LANES]
        return 0

    lax.fori_loop(0, ntile, glu, 0)


def _cmul(ar, ai, br, bi):
    return ar * br - ai * bi, ar * bi + ai * br


def _s5_tables(a_re, a_im, log_dt, b_re, b_im, c_re, c_im, seq_lens):
    depth = a_re.shape[0]
    g, p, hc = SSM_GROUPS, SSM_STATE, SSM_GROUP_CH
    lr, li = a_re.astype(F32), a_im.astype(F32)
    dt = jnp.exp(log_dt.astype(F32))[..., None]
    mag = jnp.exp(lr * dt)
    ar, ai = mag * jnp.cos(li * dt), mag * jnp.sin(li * dt)
    den = lr * lr + li * li
    qr = ((ar - 1.0) * lr + ai * li) / den
    qi = (ai * lr - (ar - 1.0) * li) / den
    br, bi = _cmul(qr[..., None], qi[..., None], b_re.astype(F32), b_im.astype(F32))
    bparts = jnp.stack([br, bi], axis=2)
    bmat = bparts.transpose(0, 3, 5, 1, 2, 4).reshape(depth, g * hc, 4 * p)
    cparts = jnp.stack([c_re.astype(F32), -c_im.astype(F32)], axis=2)
    cmat = cparts.transpose(0, 1, 2, 5, 3, 4).reshape(depth, 4 * p, g * hc)

    def power(k):
        kk = k[None, None, :, None, None]
        m = jnp.exp(kk * (lr * dt)[:, :, None])
        th = kk * (li * dt)[:, :, None]
        return (m * jnp.cos(th)).reshape(depth, 2, -1, g * p), (m * jnp.sin(th)).reshape(depth, 2, -1, g * p)

    a1 = jnp.stack([ar.reshape(depth, 2, g * p), ai.reshape(depth, 2, g * p)], axis=2).reshape(depth, 4, g * p)
    tabs = {}
    for seq in set(seq_lens):
        ch = min(seq, SCAN_TILE) // SUBLANES
        cr_, ci_ = power(jnp.asarray([ch, 2 * ch, 4 * ch], F32))
        ach = jnp.stack([cr_, ci_], axis=3).reshape(depth, 12, g * p)
        fr, fi = power(jnp.arange(1, ch + 1, dtype=F32))
        fr = jnp.stack([fr[:, 0], fr[:, 1, ::-1]], axis=1)
        fi = jnp.stack([fi[:, 0], fi[:, 1, ::-1]], axis=1)
        pfix = jnp.stack([fr, fi], axis=2).reshape(depth, 4, ch, g * p)
        tabs[seq] = (bmat, cmat, a1, ach, pfix)
    return tabs


def _s5(u, h0re, h0im, tabs, d_skip, w_glu, b_glu, layer):
    b, nh, seq, _ = u.shape
    w = nh * LANES
    tabs = tabs[seq]
    tile = tabs[4].shape[2] * SUBLANES
    lay = lambda a: pl.BlockSpec((None,) + a.shape[1:], lambda i: (layer,) + (0,) * (a.ndim - 1))
    return pl.pallas_call(
        _s5_kernel,
        out_shape=[jax.ShapeDtypeStruct((b, nh, seq, LANES), F32),
                   jax.ShapeDtypeStruct((b, 2, SSM_N), F32),
                   jax.ShapeDtypeStruct((b, 2, SSM_N), F32)],
        grid=(b,),
        in_specs=[
            pl.BlockSpec((1, nh, seq, LANES), lambda i: (i, 0, 0, 0)),
            pl.BlockSpec((1, 2, SSM_N), lambda i: (i, 0, 0)),
            pl.BlockSpec((1, 2, SSM_N), lambda i: (i, 0, 0)),
            lay(tabs[0]), lay(tabs[1]), lay(tabs[2]), lay(tabs[3]), lay(tabs[4]),
            pl.BlockSpec((None, 1, w), lambda i: (layer, 0, 0)),
            pl.BlockSpec((None, w, w), lambda i: (layer, 0, 0)),
            pl.BlockSpec((None, 1, w), lambda i: (layer, 0, 0)),
        ],
        out_specs=[pl.BlockSpec((1, nh, seq, LANES), lambda i: (i, 0, 0, 0)),
                   pl.BlockSpec((1, 2, SSM_N), lambda i: (i, 0, 0)),
                   pl.BlockSpec((1, 2, SSM_N), lambda i: (i, 0, 0))],
        scratch_shapes=[pltpu.VMEM((seq, w), BF16),
                        pltpu.VMEM((seq, w), F32),
                        pltpu.VMEM((2, tile, 2 * SSM_N), F32),
                        pltpu.VMEM((2, 2, SUBLANES, SSM_N), F32),
                        pltpu.VMEM((w, 4 * SSM_N), BF16),
                        pltpu.VMEM((4 * SSM_N, w), BF16)],
        compiler_params=_params(1),
        name="s5",
    )(u, h0re, h0im, *tabs, d_skip, w_glu, b_glu)


def _proj_out_kernel(oa_ref, ob_ref, oc_ref, w_ref, x_ref, mod_ref, g_ref, wr_ref, o_ref, h_ref, lg_ref, wbf_ref):
    @pl.when((pl.program_id(0) == 0) & (pl.program_id(1) == 0))
    def _():
        _cast_rows(wbf_ref, w_ref, 128)

    i1, i2 = NA_WIDTH, NA_WIDTH + DIFF_WIDTH
    for bi in range(x_ref.shape[0]):
        for r0 in range(0, x_ref.shape[1], ROW_TILE):
            rows = slice(r0, r0 + ROW_TILE)
            y = _dot(oa_ref[bi, rows, :].astype(BF16), wbf_ref[:i1, :])
            y += _dot(ob_ref[bi, rows, :].astype(BF16), wbf_ref[i1:i2, :])
            for hf in range(SSM_WIDTH // LANES):
                y += _dot(oc_ref[bi, hf, rows, :].astype(BF16), wbf_ref[i2 + hf * LANES:i2 + (hf + 1) * LANES, :])
            xn = x_ref[bi, rows, :] + mod_ref[0, 2:3, :] * y
            o_ref[bi, rows, :] = xn
            h = _modulated_norm(xn, g_ref[...], mod_ref[0, 3:4, :], mod_ref[0, 4:5, :])
            h_hi = h.astype(BF16)
            h_ref[bi, rows, :] = h_hi
            h_lo = (h - h_hi.astype(F32)).astype(BF16)
            t1 = _dot_nt(wr_ref[...], h_hi)
            t2 = _dot_nt(wr_ref[:N_EXPERTS, :], h_lo)
            lg_ref[bi, :, rows] = t1[:N_EXPERTS, :] + t1[N_EXPERTS:, :] + t2


def _proj_out(oa, ob, oc, w_out, x, mod, g_ffn, wr_t, layer):
    b, n, d = x.shape
    bb, tm = _step_tile(b, n, mod.shape[0], PROJ_OUT_CHUNKS)
    mod_map = (lambda i, j: (i, 0, 0)) if mod.shape[0] > 1 else (lambda i, j: (0, 0, 0))
    kw = w_out.shape[1]
    return pl.pallas_call(
        _proj_out_kernel,
        out_shape=[jax.ShapeDtypeStruct((b, n, d), F32),
                   jax.ShapeDtypeStruct((b, n, d), BF16),
                   jax.ShapeDtypeStruct((b, N_EXPERTS, n), F32)],
        grid=(b // bb, n // tm),
        in_specs=[
            pl.BlockSpec((bb, tm, NA_WIDTH), lambda i, j: (i, j, 0)),
            pl.BlockSpec((bb, tm, DIFF_WIDTH), lambda i, j: (i, j, 0)),
            pl.BlockSpec((bb, SSM_WIDTH // LANES, tm, LANES), lambda i, j: (i, 0, j, 0)),
            pl.BlockSpec((None, kw, d), lambda i, j: (layer, 0, 0), pipeline_mode=pl.Buffered(1)),
            pl.BlockSpec((bb, tm, d), lambda i, j: (i, j, 0)),
            pl.BlockSpec((1, 6, d), mod_map),
            pl.BlockSpec((None, 1, d), lambda i, j: (layer, 0, 0)),
            pl.BlockSpec((None, 2 * N_EXPERTS, d), lambda i, j: (layer, 0, 0)),
        ],
        out_specs=[pl.BlockSpec((bb, tm, d), lambda i, j: (i, j, 0)),
                   pl.BlockSpec((bb, tm, d), lambda i, j: (i, j, 0)),
                   pl.BlockSpec((bb, N_EXPERTS, tm), lambda i, j: (i, 0, j))],
        scratch_shapes=[pltpu.VMEM((kw, d), BF16)],
        compiler_params=_params(2),
        name="proj_out",
    )(oa, ob, oc, w_out, x, mod, g_ffn, wr_t)


def _excl_cumsum_lanes(m):
    e, n = m.shape
    blk = MXU_DIM
    nb = n // blk
    r = lax.broadcasted_iota(I32, (blk, blk), 0)
    c = lax.broadcasted_iota(I32, (blk, blk), 1)
    tri = jnp.where(r < c, 1.0, 0.0).astype(BF16)
    stacked = jnp.concatenate([m[:, k * blk:(k + 1) * blk] for k in range(nb)], axis=0).astype(BF16)
    within = _dot(stacked, tri)
    outs = []
    off = jnp.zeros((e, 1), F32)
    for k in range(nb):
        outs.append(within[k * e:(k + 1) * e, :] + off)
        off = off + jnp.sum(m[:, k * blk:(k + 1) * blk], axis=1, keepdims=True)
    return jnp.concatenate(outs, axis=1)


def _route_kernel(lg_ref, pos_ref, gate_ref, *, cap):
    b, e, n = lg_ref.shape
    lg = lg_ref[...]
    mx = lg.max(axis=1, keepdims=True)
    ex = jnp.exp(lg - mx)
    aff = (ex / ex.sum(axis=1, keepdims=True)).reshape(b * e, n)

    def search(it, cur):
        cand = cur | (jnp.int32(1) << (jnp.int32(30) - it))
        cnt = jnp.sum(jnp.where(aff >= pltpu.bitcast(cand, F32), 1.0, 0.0), axis=1, keepdims=True)
        return jnp.where(cnt >= cap, cand, cur)

    thr = pltpu.bitcast(lax.fori_loop(0, 31, search, jnp.zeros((b * e, 1), I32)), F32)
    gt = aff > thr
    eq = aff == thr
    need = cap - jnp.sum(jnp.where(gt, 1.0, 0.0), axis=1, keepdims=True)
    eq_rank = _excl_cumsum_lanes(jnp.where(eq, 1.0, 0.0))
    sel = gt | (eq & (eq_rank < need))
    slot = _excl_cumsum_lanes(jnp.where(sel, 1.0, 0.0))
    pos = jnp.where(sel, slot, -1.0).astype(I32)
    gate = jnp.where(sel, aff, 0.0)
    for i in range(b):
        pos_ref[:, i, 0, :] = pos[i * e:(i + 1) * e, :]
        gate_ref[:, i, 0, :] = gate[i * e:(i + 1) * e, :]


def _route(logits):
    b, e, n = logits.shape
    cap = EC_CAPACITY_FACTOR * n // N_EXPERTS
    return pl.pallas_call(
        functools.partial(_route_kernel, cap=cap),
        out_shape=[jax.ShapeDtypeStruct((e, b, 1, n), I32),
                   jax.ShapeDtypeStruct((e, b, 1, n), F32)],
        compiler_params=_params(0),
        name="route",
    )(logits)


def _one_hot_rows(pos_row, cap):
    n = pos_row.shape[1]
    slot = lax.broadcasted_iota(I32, (cap, n), 0)
    return slot == pos_row


def _gather_kernel(h_ref, pos_ref, gate_ref, xs_ref, gs_ref, *, cap):
    group = max(1, GATHER_ROWS // cap)
    for bi in range(h_ref.shape[0]):
        h = h_ref[bi]
        for e0 in range(0, N_EXPERTS, group):
            ohs = [_one_hot_rows(pos_ref[e, bi], cap) for e in range(e0, e0 + group)]
            xs = _dot(jnp.concatenate([jnp.where(oh, 1.0, 0.0).astype(BF16) for oh in ohs], axis=0), h).astype(BF16)
            for i, oh in enumerate(ohs):
                e = e0 + i
                rows = slice(bi * cap, (bi + 1) * cap)
                xs_ref[e, rows, :] = xs[i * cap:(i + 1) * cap, :]
                g = jnp.sum(jnp.where(oh, gate_ref[e, bi], 0.0), axis=1, keepdims=True)
                gs_ref[e, rows, :] = jnp.broadcast_to(g, (cap, LANES))


def _requests_per_step(b, n):
    bb = max(1, MOE_STEP_TOKENS // n)
    return bb if b % bb == 0 else 1


def _gather(h, pos, gate):
    b, n, d = h.shape
    cap = EC_CAPACITY_FACTOR * n // N_EXPERTS
    bb = _requests_per_step(b, n)
    return pl.pallas_call(
        functools.partial(_gather_kernel, cap=cap),
        out_shape=[jax.ShapeDtypeStruct((N_EXPERTS, b * cap, d), BF16),
                   jax.ShapeDtypeStruct((N_EXPERTS, b * cap, LANES), F32)],
        grid=(b // bb,),
        in_specs=[
            pl.BlockSpec((bb, n, d), lambda i: (i, 0, 0)),
            pl.BlockSpec((N_EXPERTS, bb, 1, n), lambda i: (0, i, 0, 0)),
            pl.BlockSpec((N_EXPERTS, bb, 1, n), lambda i: (0, i, 0, 0)),
        ],
        out_specs=[pl.BlockSpec((N_EXPERTS, bb * cap, d), lambda i: (0, i, 0)),
                   pl.BlockSpec((N_EXPERTS, bb * cap, LANES), lambda i: (0, i, 0))],
        compiler_params=_params(1),
        name="moe_gather",
    )(h, pos, gate)


def _ffn_kernel(xp_ref, gp_ref, xs_ref, gs_ref, wg_ref, wu_ref, wd_ref, yp_ref, ys_ref, acc_ref):
    j = pl.program_id(1)
    sp = xp_ref.shape[1]

    @pl.when((pl.program_id(0) == 0) & (j == 0))
    def _():
        acc_ref[...] = jnp.zeros_like(acc_ref)

    wg = wg_ref[...].astype(BF16)
    wu = wu_ref[...].astype(BF16)
    wd = wd_ref[...].astype(BF16)
    for x_ref, g_ref, y_ref, rows in ((xp_ref, gp_ref, yp_ref, slice(0, sp)),
                                      (xs_ref, gs_ref, ys_ref, slice(sp, acc_ref.shape[0]))):
        x = x_ref[0]
        a = _dot(x, wg)
        u = _dot(x, wu)
        hid = (a * jax.nn.sigmoid(a)) * u
        acc = jnp.where(j > 0, acc_ref[rows, :], 0.0) + _dot(hid.astype(BF16), wd)
        acc_ref[rows, :] = acc
        y_ref[0] = (acc * g_ref[0][:, 0:1]).astype(BF16)


def _ffn(xp, gp, xs, gs, w_gate, w_up, w_down, layer):
    e, sp, d = xp.shape
    ss = xs.shape[1]
    s = sp + ss
    ff = w_gate.shape[-1]
    tf = FF_TILE
    return pl.pallas_call(
        _ffn_kernel,
        out_shape=[jax.ShapeDtypeStruct((e, sp, d), BF16), jax.ShapeDtypeStruct((e, ss, d), BF16)],
        grid=(e, ff // tf),
        in_specs=[
            pl.BlockSpec((1, sp, d), lambda i, j: (i, 0, 0)),
            pl.BlockSpec((1, sp, LANES), lambda i, j: (i, 0, 0)),
            pl.BlockSpec((1, ss, d), lambda i, j: (i, 0, 0)),
            pl.BlockSpec((1, ss, LANES), lambda i, j: (i, 0, 0)),
            pl.BlockSpec((None, None, d, tf), lambda i, j: (layer, i, 0, j)),
            pl.BlockSpec((None, None, d, tf), lambda i, j: (layer, i, 0, j)),
            pl.BlockSpec((None, None, tf, d), lambda i, j: (layer, i, j, 0)),
        ],
        out_specs=[pl.BlockSpec((1, sp, d), lambda i, j: (i, 0, 0)),
                   pl.BlockSpec((1, ss, d), lambda i, j: (i, 0, 0))],
        scratch_shapes=[pltpu.VMEM((s, d), F32)],
        compiler_params=_params(2),
        name="moe_ffn",
    )(xp, gp, xs, gs, w_gate, w_up, w_down)


def _combine_kernel(ye_ref, pos_ref, x_ref, mod_ref, *rest, cap, final):
    if final:
        g_ref, o_ref = rest
    else:
        (o_ref,) = rest
    group = max(1, MXU_DIM // cap)
    for bi in range(x_ref.shape[0]):
        y = None
        for e0 in range(0, N_EXPERTS, group):
            oh = jnp.concatenate([jnp.where(_one_hot_rows(pos_ref[e, bi], cap), 1.0, 0.0).astype(BF16)
                                  for e in range(e0, e0 + group)], axis=0)
            ye = jnp.concatenate([ye_ref[e, bi * cap:(bi + 1) * cap, :] for e in range(e0, e0 + group)], axis=0)
            part = _dot_tn(oh, ye)
            y = part if y is None else y + part
        xn = x_ref[bi] + mod_ref[0, 5:6, :] * y
        if final:
            ms = jnp.mean(xn * xn, axis=-1, keepdims=True)
            xn = xn * lax.rsqrt(ms + EPS) * g_ref[...]
        o_ref[bi] = xn


def _combine(ye, pos, x, mod, final_norm):
    b, n, d = x.shape
    cap = EC_CAPACITY_FACTOR * n // N_EXPERTS
    tn = min(n, COMBINE_TILE)
    bm = mod.shape[0]
    bb = _requests_per_step(b, n) if bm == 1 else 1
    mod_map = (lambda i, j: (i, 0, 0)) if bm > 1 else (lambda i, j: (0, 0, 0))
    final = final_norm is not None
    in_specs = [
        pl.BlockSpec((N_EXPERTS, bb * cap, d), lambda i, j: (0, i, 0)),
        pl.BlockSpec((N_EXPERTS, bb, 1, tn), lambda i, j: (0, i, 0, j)),
        pl.BlockSpec((bb, tn, d), lambda i, j: (i, j, 0)),
        pl.BlockSpec((1, 6, d), mod_map),
    ]
    args = [ye, pos, x, mod]
    if final:
        in_specs.append(pl.BlockSpec((1, d), lambda i, j: (0, 0)))
        args.append(final_norm)
    return pl.pallas_call(
        functools.partial(_combine_kernel, cap=cap, final=final),
        out_shape=jax.ShapeDtypeStruct((b, n, d), F32),
        grid=(b // bb, n // tn),
        in_specs=in_specs,
        out_specs=pl.BlockSpec((bb, tn, d), lambda i, j: (i, j, 0)),
        compiler_params=_params(2),
        name="moe_combine",
    )(*args)


def _rope_tables(n):
    t = np.arange(n)
    row = (t // GRID_W).astype(np.float32)
    col = (t % GRID_W).astype(np.float32)
    nf = DIFF_QK // 4
    inv = np.float32(ROPE_BASE) ** (-np.arange(nf, dtype=np.float32) / np.float32(nf))
    lane = np.arange(DIFF_WIDTH)
    pos = np.where(((lane % DIFF_QK) < DIFF_QK // 2)[None, :], row[:, None], col[:, None])
    ang = (pos * inv[lane % nf][None, :]).astype(np.float32)
    first = (lane % (2 * nf)) < nf
    cos, sin = np.cos(ang).astype(np.float32), np.sin(ang).astype(np.float32)
    return jnp.asarray(cos), jnp.asarray(np.where(first[None, :], -sin, sin))


def kernel(x_prompt, x_sample, cache_na_k, cache_na_v, cache_diff_k, cache_diff_v, state_ssm_re, state_ssm_im,
           c, c_ctx, w_ada, b_ada, norm_mix, norm_ffn, w_in, w_out, na_rpb, diff_lambda, diff_subln,
           ssm_a_re, ssm_a_im, ssm_log_dt, ssm_b_re, ssm_b_im, ssm_c_re, ssm_c_im, ssm_d, ssm_w_glu, ssm_b_glu,
           w_router, w_gate, w_up, w_down, final_norm):
    depth = w_in.shape[0]
    bp, sp, d = x_prompt.shape
    bs, ss, _ = x_sample.shape
    assert d == D_MODEL and bs + 1 <= SUBLANES
    past = cache_na_k.shape[2]

    cond = jnp.zeros((SUBLANES, d), F32).at[0].set(c_ctx).at[1:1 + bs].set(c)
    mods = _ada(cond, w_ada, b_ada).reshape(depth, SUBLANES, 6, d)

    rope_tabs = _rope_tables(ss)
    kc_a = cache_na_k.reshape(bs, depth, past, NA_WIDTH)
    vc_a = cache_na_v.reshape(bs, depth, past, NA_WIDTH)
    kc_b = cache_diff_k.reshape(bs, depth, past, DIFF_WIDTH)
    vc_b = cache_diff_v.reshape(bs, depth, past, DIFF_WIDTH)
    subln = jnp.tile(diff_subln, (1, LANES // DIFF_V)).reshape(depth, 1, LANES)
    norm_mix = norm_mix.reshape(depth, 1, d)
    norm_ffn = norm_ffn.reshape(depth, 1, d)
    ssm_d = ssm_d.reshape(depth, 1, SSM_WIDTH)
    ssm_b_glu = ssm_b_glu.reshape(depth, 1, SSM_WIDTH)
    wr_t = jnp.swapaxes(w_router, 1, 2).astype(F32)
    wr_hi = wr_t.astype(BF16)
    wr_t = jnp.concatenate([wr_hi, (wr_t - wr_hi.astype(F32)).astype(BF16)], axis=1)
    fnorm = final_norm.reshape(1, d)
    zero_state = jnp.zeros((bp, 2, SSM_N), F32)
    tabs = _s5_tables(ssm_a_re, ssm_a_im, ssm_log_dt, ssm_b_re, ssm_b_im, ssm_c_re, ssm_c_im, (sp, ss))
    bias_tab = _na_bias_table(na_rpb)

    xp, xs = x_prompt, x_sample
    new_sre, new_sim = [], []
    caches = None
    for l in range(depth):
        lam_init = 0.8 - 0.6 * math.exp(-0.3 * l)
        mod_p = mods[l, 0:1]
        mod_s = mods[l, 1:1 + bs]
        diff_params = (diff_lambda, subln)

        qa, ka, va, qb, kb, vb, u = _proj_in(xp, mod_p, norm_mix, w_in, l, None, depth, caches)
        caches = (ka, va, kb, vb)
        o_a = _attention(qa, ka, va, l, None, l, None, lam_init)
        o_b = _attention(qb, kb, vb, l, None, l, diff_params, lam_init)
        o_c, fre, fim = _s5(u, zero_state, zero_state, tabs, ssm_d, ssm_w_glu, ssm_b_glu, l)
        xp, hp, lg_p = _proj_out(o_a, o_b, o_c, w_out, xp, mod_p, norm_ffn, wr_t, l)
        new_sre.append(fre.reshape(bp, 2, SSM_GROUPS, SSM_STATE))
        new_sim.append(fim.reshape(bp, 2, SSM_GROUPS, SSM_STATE))

        qa, ka, va, qb, kb, vb, u = _proj_in(xs, mod_s, norm_mix, w_in, l, rope_tabs, 0, None)
        o_a = _neighbourhood_attention(qa, ka, va, kc_a, vc_a, bias_tab, l)
        o_b = _attention(qb, kb, vb, 0, (kc_b, vc_b), l, diff_params, lam_init)
        h0re = state_ssm_re[:, l].reshape(bs, 2, SSM_N)
        h0im = state_ssm_im[:, l].reshape(bs, 2, SSM_N)
        o_c, _, _ = _s5(u, h0re, h0im, tabs, ssm_d, ssm_w_glu, ssm_b_glu, l)
        xs, hs, lg_s = _proj_out(o_a, o_b, o_c, w_out, xs, mod_s, norm_ffn, wr_t, l)

        pos_p, gate_p = _route(lg_p)
        pos_s, gate_s = _route(lg_s)
        xg_p, gs_p = _gather(hp, pos_p, gate_p)
        xg_s, gs_s = _gather(hs, pos_s, gate_s)
        ye_p, ye_s = _ffn(xg_p, gs_p, xg_s, gs_s, w_gate, w_up, w_down, l)
        last = l == depth - 1
        xp = _combine(ye_p, pos_p, xp, mod_p, fnorm if last else None)
        xs = _combine(ye_s, pos_s, xs, mod_s, fnorm if last else None)

    ka, va, kb, vb = caches
    return (xp, xs, ka.reshape(bp, depth, sp, NA_HEADS, HEAD_DIM), va.reshape(bp, depth, sp, NA_HEADS, HEAD_DIM),
            kb.reshape(bp, depth, sp, DIFF_HEADS, DIFF_V), vb.reshape(bp, depth, sp, DIFF_HEADS, DIFF_V),
            jnp.stack(new_sre, axis=1), jnp.stack(new_sim, axis=1))
```

```python
import functools
import math

import jax
import jax.numpy as jnp
import numpy as np
from jax import lax
from jax.experimental import pallas as pl
from jax.experimental.pallas import tpu as pltpu

F32 = jnp.float32
BF16 = jnp.bfloat16
I32 = jnp.int32

D_MODEL = 1024
GRID_W = 64
HEAD_DIM = 64
NA_HEADS = 8
NA_WIDTH = NA_HEADS * HEAD_DIM
WIN_R = 8
WIN_C = 16
DIFF_HEADS = 4
DIFF_QK = 32
DIFF_V = 64
DIFF_WIDTH = DIFF_HEADS * DIFF_V
SSM_GROUPS = 16
SSM_GROUP_CH = 16
SSM_WIDTH = SSM_GROUPS * SSM_GROUP_CH
SSM_STATE = 64
SSM_N = SSM_GROUPS * SSM_STATE
IN_WIDTH = 3 * NA_WIDTH + 3 * DIFF_WIDTH + SSM_WIDTH
N_EXPERTS = 16
EC_CAPACITY_FACTOR = 2
ROPE_BASE = 10000.0
EPS = 1e-6
NEG_INF = -1e30
LOG2E = math.log2(math.e)

LANES = 128
SUBLANES = 8
MXU_DIM = 256
VMEM_LIMIT_CAP = 60000 * 1024

ROW_TILE = 256
PROJ_IN_CHUNKS = 4
PROJ_OUT_CHUNKS = 2
ATTN_Q_TILE = 512
ATTN_REQUESTS_PER_STEP = 4
NA_QROWS = 4
NA_KROWS = 12
SCAN_TILE = 512
FF_TILE = 1024
COMBINE_TILE = 1024
GATHER_ROWS = 512
MOE_STEP_TOKENS = 1024
DEN_IN_MATMUL_MIN_KEYS = 256


def _params(ndims):
    return pltpu.CompilerParams(
        dimension_semantics=("arbitrary",) * ndims,
        vmem_limit_bytes=VMEM_LIMIT_CAP,
    )


def _step_tile(b, n, n_mod, chunks):
    tm = min(n, ROW_TILE * chunks)
    bb = max(1, ROW_TILE * chunks // tm)
    if n_mod > 1 or b % bb:
        bb = 1
    return bb, tm


def _dot(a, b):
    return jnp.dot(a, b, preferred_element_type=F32)


def _dot_nt(a, b):
    return lax.dot_general(a, b, (((1,), (1,)), ((), ())), preferred_element_type=F32)


def _dot_tn(a, b):
    return lax.dot_general(a, b, (((0,), (0,)), ((), ())), preferred_element_type=F32)


def _cast_rows(dst_ref, src_ref, rows):
    n = src_ref.shape[0]
    for r in range(0, n, rows):
        dst_ref[r:r + rows, :] = src_ref[r:r + rows, :].astype(BF16)


def _modulated_norm(x, g, shift, scale):
    ms = jnp.mean(x * x, axis=-1, keepdims=True)
    return (x * lax.rsqrt(ms + EPS) * g) * (1.0 + scale) + shift


def _ada_kernel(c_ref, w_ref, b_ref, o_ref):
    c = c_ref[...]
    s = c * jax.nn.sigmoid(c)
    o_ref[0] = _dot(s.astype(BF16), w_ref[0].astype(BF16)) + b_ref[0]


def _ada(cond, w_ada, b_ada):
    depth = w_ada.shape[0]
    tn = 1536
    nt = 6 * D_MODEL // tn
    return pl.pallas_call(
        _ada_kernel,
        out_shape=jax.ShapeDtypeStruct((depth, SUBLANES, 6 * D_MODEL), F32),
        grid=(depth, nt),
        in_specs=[
            pl.BlockSpec((SUBLANES, D_MODEL), lambda l, j: (0, 0)),
            pl.BlockSpec((1, D_MODEL, tn), lambda l, j: (l, 0, j)),
            pl.BlockSpec((1, 1, tn), lambda l, j: (l, 0, j)),
        ],
        out_specs=pl.BlockSpec((1, SUBLANES, tn), lambda l, j: (l, 0, j)),
        compiler_params=_params(2),
        name="ada",
    )(cond, w_ada, b_ada.reshape(depth, 1, 6 * D_MODEL))


def _rope_apply(x, cos, sin_signed):
    lane = lax.broadcasted_iota(I32, (1, LANES), 1)
    first = (lane % 16) < 8
    outs = []
    for t in range(x.shape[1] // LANES):
        xt = x[:, t * LANES:(t + 1) * LANES]
        partner = jnp.where(first, pltpu.roll(xt, LANES - 8, axis=1), pltpu.roll(xt, 8, axis=1))
        outs.append(xt * cos[:, t * LANES:(t + 1) * LANES] + partner * sin_signed[:, t * LANES:(t + 1) * LANES])
    return jnp.concatenate(outs, axis=1)


def _proj_in_kernel(*refs, rope, n_alias):
    x_ref, mod_ref, g_ref, w_ref = refs[:4]
    pos = 4
    if rope:
        cos_ref, sin_ref = refs[pos:pos + 2]
        pos += 2
    pos += n_alias
    qa_ref, ka_ref, va_ref, qb_ref, kb_ref, vb_ref, u_ref, wbf_ref = refs[pos:pos + 8]

    @pl.when((pl.program_id(0) == 0) & (pl.program_id(1) == 0))
    def _():
        _cast_rows(wbf_ref, w_ref, 128)

    i1, i2, i3 = NA_WIDTH, 2 * NA_WIDTH, 3 * NA_WIDTH
    i4, i5, i6 = i3 + DIFF_WIDTH, i3 + 2 * DIFF_WIDTH, i3 + 3 * DIFF_WIDTH
    for bi in range(x_ref.shape[0]):
        for r0 in range(0, x_ref.shape[1], ROW_TILE):
            rows = slice(r0, r0 + ROW_TILE)

            def put(ref, val):
                for s in range(ref.shape[1]):
                    ref[bi, s, rows, :] = val.astype(ref.dtype)

            h = _modulated_norm(x_ref[bi, rows, :], g_ref[...], mod_ref[0, 0:1, :], mod_ref[0, 1:2, :])
            z = _dot(h.astype(BF16), wbf_ref[...])
            qa_ref[bi, rows, :] = (z[:, :i1] * (HEAD_DIM ** -0.5 * LOG2E)).astype(qa_ref.dtype)
            put(ka_ref, z[:, i1:i2])
            put(va_ref, z[:, i2:i3])
            qb = z[:, i3:i4]
            kb = z[:, i4:i5]
            if rope:
                qb = _rope_apply(qb, cos_ref[rows, :], sin_ref[rows, :])
                kb = _rope_apply(kb, cos_ref[rows, :], sin_ref[rows, :])
            qb_ref[bi, rows, :] = (qb * (DIFF_QK ** -0.5 * LOG2E)).astype(qb_ref.dtype)
            put(kb_ref, kb)
            put(vb_ref, z[:, i5:i6])
            for hf in range(SSM_WIDTH // LANES):
                u_ref[bi, hf, rows, :] = z[:, i6 + hf * LANES:i6 + (hf + 1) * LANES]


def _proj_in(x, mod, g_norm, w_in, layer, rope_tabs, cache_slots, caches):
    b, n, d = x.shape
    bb, tm = _step_tile(b, n, mod.shape[0], PROJ_IN_CHUNKS)
    rope = rope_tabs is not None
    mod_map = (lambda i, j: (i, 0, 0)) if mod.shape[0] > 1 else (lambda i, j: (0, 0, 0))
    in_specs = [
        pl.BlockSpec((bb, tm, d), lambda i, j: (i, j, 0)),
        pl.BlockSpec((1, 6, d), mod_map),
        pl.BlockSpec((None, 1, d), lambda i, j: (layer, 0, 0)),
        pl.BlockSpec((None, d, IN_WIDTH), lambda i, j: (layer, 0, 0), pipeline_mode=pl.Buffered(1)),
    ]
    args = [x, mod, g_norm, w_in]
    if rope:
        in_specs += [pl.BlockSpec((tm, DIFF_WIDTH), lambda i, j: (j, 0))] * 2
        args += list(rope_tabs)
    aliases = {}
    kv_out = (1, 2, 4, 5)
    if caches is not None:
        for c, o in zip(caches, kv_out):
            aliases[len(args)] = o
            in_specs.append(pl.BlockSpec(memory_space=pl.ANY))
            args.append(c)
    widths = (NA_WIDTH, NA_WIDTH, NA_WIDTH, DIFF_WIDTH, DIFF_WIDTH, DIFF_WIDTH)
    nslab = SSM_WIDTH // LANES
    out_shape, out_specs = [], []
    for o, w in enumerate(widths):
        if o not in kv_out:
            out_shape.append(jax.ShapeDtypeStruct((b, n, w), BF16))
            out_specs.append(pl.BlockSpec((bb, tm, w), lambda i, j: (i, j, 0)))
        elif cache_slots == 0:
            out_shape.append(jax.ShapeDtypeStruct((b, 1, n, w), BF16))
            out_specs.append(pl.BlockSpec((bb, 1, tm, w), lambda i, j: (i, 0, j, 0)))
        elif caches is None:
            out_shape.append(jax.ShapeDtypeStruct((b, cache_slots, n, w), F32))
            out_specs.append(pl.BlockSpec((bb, cache_slots, tm, w), lambda i, j: (i, 0, j, 0)))
        else:
            out_shape.append(jax.ShapeDtypeStruct((b, cache_slots, n, w), F32))
            out_specs.append(pl.BlockSpec((bb, 1, tm, w), lambda i, j: (i, layer, j, 0)))
    out_shape.append(jax.ShapeDtypeStruct((b, nslab, n, LANES), F32))
    out_specs.append(pl.BlockSpec((bb, nslab, tm, LANES), lambda i, j: (i, 0, j, 0)))
    return pl.pallas_call(
        functools.partial(_proj_in_kernel, rope=rope, n_alias=len(aliases)),
        out_shape=out_shape,
        grid=(b // bb, n // tm),
        in_specs=in_specs,
        out_specs=out_specs,
        scratch_shapes=[pltpu.VMEM((d, IN_WIDTH), BF16)],
        input_output_aliases=aliases,
        compiler_params=_params(2),
        name="proj_in",
    )(*args)


def _masked_query(qp, mask):
    return jnp.where(mask, qp, jnp.zeros_like(qp))


def _with_ones(v):
    return jnp.concatenate([v, jnp.ones_like(v)], axis=1)


def _softmax_av_wide(qm, ks, vs, biases):
    ss = []
    for k, bias in zip(ks, biases):
        s = _dot_nt(qm, k)
        if bias is not None:
            s = bias(s)
        ss.append(s)
    m = ss[0].max(axis=-1, keepdims=True)
    for s in ss[1:]:
        m = jnp.maximum(m, s.max(axis=-1, keepdims=True))
    acc = None
    for s, v in zip(ss, vs):
        pv = _dot(jnp.exp2(s - m).astype(BF16), v)
        acc = pv if acc is None else acc + pv
    return acc[:, :LANES] / acc[:, LANES:]


def _softmax_av(qm, ks, vs, biases):
    ss = []
    for k, bias in zip(ks, biases):
        s = _dot_nt(qm, k)
        if bias is not None:
            s = bias(s)
        ss.append(s)
    m = ss[0].max(axis=-1, keepdims=True)
    for s in ss[1:]:
        m = jnp.maximum(m, s.max(axis=-1, keepdims=True))
    acc = None
    den = None
    for s, v in zip(ss, vs):
        p = jnp.exp2(s - m)
        psum = p.sum(axis=-1, keepdims=True)
        den = psum if den is None else den + psum
        pv = _dot(p.astype(BF16), v)
        acc = pv if acc is None else acc + pv
    return acc / den


def _diff_lambda(lam_ref, lam_init):
    lp = lam_ref[...]
    a = jnp.sum(lp[0:1] * lp[1:2], axis=-1, keepdims=True)
    b = jnp.sum(lp[2:3] * lp[3:4], axis=-1, keepdims=True)
    return jnp.exp(a) - jnp.exp(b) + lam_init


def _attn_kernel(*refs, has_ctx, diff, lam_init):
    refs = list(refs)
    q_ref, k_ref, v_ref = refs[:3]
    pos = 3
    if has_ctx:
        kc_ref, vc_ref = refs[pos:pos + 2]
        pos += 2
    if diff:
        lam_ref, subln_ref = refs[pos:pos + 2]
        pos += 2
    o_ref = refs[pos]
    width = q_ref.shape[2]
    lane = lax.broadcasted_iota(I32, (1, LANES), 1)
    if diff:
        lam = _diff_lambda(lam_ref, lam_init)
    for bi, p in [(bi, p) for bi in range(q_ref.shape[0]) for p in range(width // LANES)]:
        sl = slice(p * LANES, (p + 1) * LANES)
        qp = q_ref[bi, :, sl]
        ks = [k_ref[bi, :, sl].astype(BF16)]
        vs = [v_ref[bi, :, sl].astype(BF16)]
        if has_ctx:
            ks.append(kc_ref[:, sl].astype(BF16))
            vs.append(vc_ref[:, sl].astype(BF16))
        nob = [None] * len(ks)
        tq = qp.shape[0]
        wide = sum(k.shape[0] for k in ks) >= DEN_IN_MATMUL_MIN_KEYS
        if wide:
            vs = [_with_ones(v) for v in vs]
        pair = None
        for sub in range(LANES // HEAD_DIM):
            head_mask = (lane // HEAD_DIM) == sub
            if diff:
                qmasks = [(lane // DIFF_QK) == (2 * sub + half) for half in range(2)]
                av = _softmax_av_wide if wide else _softmax_av
                os_ = [av(_masked_query(qp, mk), ks, vs, nob) for mk in qmasks]
                o = jnp.where(head_mask, os_[0] - lam * os_[1], 0.0)
                ms = jnp.sum(o * o, axis=-1, keepdims=True) * (1.0 / DIFF_V)
                o = (o * lax.rsqrt(ms + EPS) * subln_ref[...]) * (1.0 - lam_init)
            else:
                qm = _masked_query(qp, head_mask)
                o = _softmax_av_wide(qm, ks, vs, nob) if wide else _softmax_av(qm, ks, vs, nob)
                o = jnp.where(head_mask, o, 0.0)
            pair = o if pair is None else pair + o
        o_ref[bi, :, sl] = pair.astype(o_ref.dtype)


def _attention(q, k, v, kv_slot, ctx_kv, layer, diff_params, lam_init):
    b, nq, w = q.shape
    nk = k.shape[2]
    tq = min(nq, ATTN_Q_TILE)
    has_ctx = ctx_kv is not None
    diff = diff_params is not None
    bb = ATTN_REQUESTS_PER_STEP if (nq == tq and not has_ctx and b % ATTN_REQUESTS_PER_STEP == 0) else 1
    in_specs = [
        pl.BlockSpec((bb, tq, w), lambda i, j: (i, j, 0)),
        pl.BlockSpec((bb, None, nk, w), lambda i, j: (i, kv_slot, 0, 0)),
        pl.BlockSpec((bb, None, nk, w), lambda i, j: (i, kv_slot, 0, 0)),
    ]
    args = [q, k, v]
    nc = 0
    if has_ctx:
        nc = ctx_kv[0].shape[2]
        in_specs += [pl.BlockSpec((None, None, nc, w), lambda i, j: (i, layer, 0, 0))] * 2
        args += list(ctx_kv)
    if diff:
        lam_p, subln = diff_params
        in_specs += [
            pl.BlockSpec((None, 4, DIFF_QK), lambda i, j: (layer, 0, 0)),
            pl.BlockSpec((None, 1, LANES), lambda i, j: (layer, 0, 0)),
        ]
        args += [lam_p, subln]
    return pl.pallas_call(
        functools.partial(_attn_kernel, has_ctx=has_ctx, diff=diff, lam_init=lam_init),
        out_shape=jax.ShapeDtypeStruct((b, nq, w), BF16),
        grid=(b // bb, nq // tq),
        in_specs=in_specs,
        out_specs=pl.BlockSpec((bb, tq, w), lambda i, j: (i, j, 0)),
        compiler_params=_params(2),
        name="attn_diff" if diff else "attn",
    )(*args)


def _na_kernel(q_ref, k_ref, v_ref, kc_ref, vc_ref, bias_ref, o_ref, *, rows):
    i = pl.program_id(1)
    tq = NA_QROWS * GRID_W
    nkw = NA_KROWS * GRID_W
    ustart = jnp.clip(NA_QROWS * i - WIN_R // 2, 0, rows - NA_KROWS)
    koff = pl.multiple_of(ustart * GRID_W, GRID_W)
    delta = ustart - NA_QROWS * i
    qrow = NA_QROWS * i + lax.broadcasted_iota(I32, (tq, nkw), 0) // GRID_W
    krow = ustart + lax.broadcasted_iota(I32, (tq, nkw), 1) // GRID_W
    wstart = jnp.clip(qrow - WIN_R // 2, 0, rows - WIN_R)
    row_ok = (krow >= wstart) & (krow < wstart + WIN_R)
    nsub = LANES // HEAD_DIM
    lane = lax.broadcasted_iota(I32, (1, LANES), 1)
    for p in range(NA_WIDTH // LANES):
        sl = slice(p * LANES, (p + 1) * LANES)
        qp = q_ref[0, :, sl]
        ks = [k_ref[0, pl.ds(koff, nkw), sl].astype(BF16), kc_ref[:, sl].astype(BF16)]
        vs = [_with_ones(v_ref[0, pl.ds(koff, nkw), sl].astype(BF16)), _with_ones(vc_ref[:, sl].astype(BF16))]
        outs = []
        for sub in range(nsub):
            def win_bias(s, h=p * nsub + sub):
                rows_ = []
                for ri in range(NA_QROWS):
                    tiles = []
                    for pj in range(NA_KROWS // 2):
                        dr = delta + 2 * pj - ri
                        idx = jnp.clip(dr, -WIN_R, WIN_R - 1) + WIN_R
                        tiles.append(bias_ref[h, idx])
                    rows_.append(jnp.concatenate(tiles, axis=1))
                bias = jnp.concatenate(rows_, axis=0)
                return jnp.where(row_ok, s + bias, NEG_INF)

            outs.append(_softmax_av_wide(_masked_query(qp, (lane // HEAD_DIM) == sub), ks, vs, [win_bias, None]))
        pair = jnp.where((lane // HEAD_DIM) == 0, outs[0], outs[1])
        o_ref[0, :, sl] = pair.astype(o_ref.dtype)


def _na_table_kernel(v_ref, o_ref):
    nd = v_ref.shape[0]
    q = lax.broadcasted_iota(I32, (GRID_W, 2 * GRID_W), 0)
    k = lax.broadcasted_iota(I32, (GRID_W, 2 * GRID_W), 1)
    kk = k % GRID_W
    cstart = jnp.clip(q - WIN_C // 2, 0, GRID_W - WIN_C)
    col_ok = (kk >= cstart) & (kk < cstart + WIN_C)
    left = k < GRID_W
    tiles = []
    for dr in range(nd):
        x = jnp.broadcast_to(v_ref[dr:dr + 1, :], (GRID_W, 2 * GRID_W))
        tiles.append((pltpu.roll(x, GRID_W + 1, axis=1, stride=1, stride_axis=0),
                      pltpu.roll(x, 1, axis=1, stride=1, stride_axis=0)))
    zero = jnp.zeros((GRID_W, 2 * GRID_W), F32)
    for j in range(nd + 1):
        lt = tiles[j - 1][0] if j >= 1 else zero
        rt = tiles[j][1] if j < nd else zero
        o_ref[j] = jnp.where(col_ok, jnp.where(left, lt, rt) * LOG2E, NEG_INF)


def _na_bias_table(rpb):
    depth, h, nd, nc = rpb.shape
    assert 2 * GRID_W == LANES and nd == 2 * WIN_R - 1 and nc == 2 * WIN_C - 1
    rpb = rpb.astype(F32)
    lo = GRID_W - WIN_C
    v = jnp.concatenate([jnp.broadcast_to(rpb[..., :1], rpb.shape[:3] + (lo,)), rpb,
                         jnp.broadcast_to(rpb[..., -1:], rpb.shape[:3] + (2 * GRID_W - lo - nc,))], axis=-1)
    return pl.pallas_call(
        _na_table_kernel,
        out_shape=jax.ShapeDtypeStruct((depth, h, nd + 1, GRID_W, 2 * GRID_W), F32),
        grid=(depth, h),
        in_specs=[pl.BlockSpec((None, None, nd, 2 * GRID_W), lambda l, i: (l, i, 0, 0))],
        out_specs=pl.BlockSpec((None, None, nd + 1, GRID_W, 2 * GRID_W), lambda l, i: (l, i, 0, 0, 0)),
        compiler_params=_params(2),
        name="na_table",
    )(v)


def _neighbourhood_attention(q, k, v, kc, vc, bias_tab, layer):
    b, n, w = q.shape
    rows = n // GRID_W
    tq = NA_QROWS * GRID_W
    nc = kc.shape[2]
    nkw = NA_KROWS * GRID_W
    return pl.pallas_call(
        functools.partial(_na_kernel, rows=rows),
        out_shape=jax.ShapeDtypeStruct((b, n, w), BF16),
        grid=(b, n // tq),
        in_specs=[
            pl.BlockSpec((1, tq, w), lambda i, j: (i, j, 0)),
            pl.BlockSpec((1, None, n, w), lambda i, j: (i, 0, 0, 0)),
            pl.BlockSpec((1, None, n, w), lambda i, j: (i, 0, 0, 0)),
            pl.BlockSpec((None, None, nc, w), lambda i, j: (i, layer, 0, 0)),
            pl.BlockSpec((None, None, nc, w), lambda i, j: (i, layer, 0, 0)),
            pl.BlockSpec((None,) + bias_tab.shape[1:], lambda i, j: (layer, 0, 0, 0, 0)),
        ],
        out_specs=pl.BlockSpec((1, tq, w), lambda i, j: (i, j, 0)),
        compiler_params=_params(2),
        name="na_attn",
    )(q, k, v, kc, vc, bias_tab)


def _sublane_scan(er, ei, tr, ti, pw_ref, base, lanes, reverse):
    row = lax.broadcasted_iota(I32, (SUBLANES, LANES), 0)
    if reverse:
        first = row == SUBLANES - 1
        xr = jnp.where(first, tr, pltpu.roll(er, SUBLANES - 1, axis=0))
        xi = jnp.where(first, ti, pltpu.roll(ei, SUBLANES - 1, axis=0))
    else:
        first = row == 0
        xr = jnp.where(first, tr, pltpu.roll(er, 1, axis=0))
        xi = jnp.where(first, ti, pltpu.roll(ei, 1, axis=0))
    for k, s in enumerate((1, 2, 4)):
        ar = pw_ref[base + 2 * k:base + 2 * k + 1, lanes]
        ai = pw_ref[base + 2 * k + 1:base + 2 * k + 2, lanes]
        if reverse:
            keep = row < SUBLANES - s
            sr = jnp.where(keep, pltpu.roll(xr, SUBLANES - s, axis=0), 0.0)
            si = jnp.where(keep, pltpu.roll(xi, SUBLANES - s, axis=0), 0.0)
        else:
            keep = row >= s
            sr = jnp.where(keep, pltpu.roll(xr, s, axis=0), 0.0)
            si = jnp.where(keep, pltpu.roll(xi, s, axis=0), 0.0)
        xr, xi = xr + (ar * sr - ai * si), xi + (ar * si + ai * sr)
    return xr, xi


def _s5_kernel(u_ref, h0re_ref, h0im_ref, bt_ref, ct_ref, a1_ref, ach_ref, pfix_ref, d_ref, wglu_ref, bglu_ref,
               o_ref, fre_ref, fim_ref, up_ref, yp_ref, buf_ref, car_ref, bmat_ref, cmat_ref):
    seq = u_ref.shape[2]
    ch = pfix_ref.shape[1]
    tt = ch * SUBLANES
    ntile = seq // tt
    nlt = SSM_N // LANES
    nhalf = SSM_WIDTH // LANES

    @pl.when(pl.program_id(0) == 0)
    def _():
        g, p, hc = SSM_GROUPS, SSM_STATE, SSM_GROUP_CH
        btb = bt_ref[...].astype(BF16)
        ctb = ct_ref[...].astype(BF16)
        for dc in range(4):
            r = lax.broadcasted_iota(I32, (4 * p, g * p), 0)
            c = lax.broadcasted_iota(I32, (4 * p, g * p), 1)
            sel = jnp.where(r == dc * p + c % p, 1.0, 0.0).astype(BF16)
            r = lax.broadcasted_iota(I32, (g * hc, g * p), 0)
            c = lax.broadcasted_iota(I32, (g * hc, g * p), 1)
            blk = jnp.where(r // hc == c // p, _dot(btb, sel), 0.0)
            bmat_ref[:, dc * g * p:(dc + 1) * g * p] = blk.astype(BF16)
            r = lax.broadcasted_iota(I32, (g * p, 4 * p), 0)
            c = lax.broadcasted_iota(I32, (g * p, 4 * p), 1)
            sel = jnp.where(c == dc * p + r % p, 1.0, 0.0).astype(BF16)
            r = lax.broadcasted_iota(I32, (g * p, g * hc), 0)
            c = lax.broadcasted_iota(I32, (g * p, g * hc), 1)
            blk = jnp.where(r // p == c // hc, _dot(sel, ctb), 0.0)
            cmat_ref[dc * g * p:(dc + 1) * g * p, :] = blk.astype(BF16)

    def permute_in(t, _):
        t0 = pl.multiple_of(t * tt, tt)
        for j in range(0, ch, 2):
            for hf in range(nhalf):
                lanes = slice(hf * LANES, (hf + 1) * LANES)
                rows = jnp.concatenate([u_ref[0, hf, pl.ds(t0 + j + jj, SUBLANES, stride=ch), :] for jj in range(2)],
                                       axis=0)
                r0 = pl.multiple_of(t0 + j * SUBLANES, 2 * SUBLANES)
                up_ref[pl.ds(r0, 2 * SUBLANES), lanes] = rows.astype(BF16)
                yp_ref[pl.ds(r0, 2 * SUBLANES), lanes] = rows * d_ref[:, lanes]
        return 0

    lax.fori_loop(0, ntile, permute_in, 0)

    for d in range(2):
        for lt in range(nlt):
            lanes = slice(lt * LANES, (lt + 1) * LANES)
            car_ref[d, 0, :, lanes] = jnp.broadcast_to(h0re_ref[0, d:d + 1, lanes], (SUBLANES, LANES))
            car_ref[d, 1, :, lanes] = jnp.broadcast_to(h0im_ref[0, d:d + 1, lanes], (SUBLANES, LANES))

    def tile(t, _):
        starts = []
        for d in range(2):
            tix = (ntile - 1 - t) if d == 1 else t
            t0 = pl.multiple_of(tix * tt, tt)
            starts.append(t0)
            buf_ref[d] = _dot(up_ref[pl.ds(t0, tt), :], bmat_ref[:, d * 2 * SSM_N:(d + 1) * 2 * SSM_N])
        for d in range(2):
            reverse = d == 1
            order = range(ch - 1, -1, -1) if reverse else range(ch)
            for lt in range(nlt):
                lre = slice(lt * LANES, (lt + 1) * LANES)
                lim = slice(SSM_N + lt * LANES, SSM_N + (lt + 1) * LANES)
                ar = a1_ref[2 * d:2 * d + 1, lre]
                ai = a1_ref[2 * d + 1:2 * d + 2, lre]
                cr = jnp.zeros((SUBLANES, LANES), F32)
                ci = jnp.zeros((SUBLANES, LANES), F32)
                for j in order:
                    rows = slice(j * SUBLANES, (j + 1) * SUBLANES)
                    cr, ci = (ar * cr - ai * ci) + buf_ref[d, rows, lre], (ar * ci + ai * cr) + buf_ref[d, rows, lim]
                    buf_ref[d, rows, lre] = cr
                    buf_ref[d, rows, lim] = ci
                tr, ti = car_ref[d, 0, :, lre], car_ref[d, 1, :, lre]
                gr, gi = _sublane_scan(cr, ci, tr, ti, ach_ref, 6 * d, lre, reverse)
                a_r, a_i = ach_ref[6 * d:6 * d + 1, lre], ach_ref[6 * d + 1:6 * d + 2, lre]
                nr, ni = (a_r * gr - a_i * gi) + cr, (a_r * gi + a_i * gr) + ci
                edge = 0 if reverse else SUBLANES - 1
                car_ref[d, 0, :, lre] = jnp.broadcast_to(nr[edge:edge + 1, :], (SUBLANES, LANES))
                car_ref[d, 1, :, lre] = jnp.broadcast_to(ni[edge:edge + 1, :], (SUBLANES, LANES))
                for j in range(ch):
                    rows = slice(j * SUBLANES, (j + 1) * SUBLANES)
                    pr = pfix_ref[2 * d, j:j + 1, lre]
                    pi = pfix_ref[2 * d + 1, j:j + 1, lre]
                    buf_ref[d, rows, lre] += pr * gr - pi * gi
                    buf_ref[d, rows, lim] += pr * gi + pi * gr
        for d in range(2):
            yp_ref[pl.ds(starts[d], tt), :] += _dot(buf_ref[d].astype(BF16),
                                                    cmat_ref[d * 2 * SSM_N:(d + 1) * 2 * SSM_N, :])
        return 0

    lax.fori_loop(0, ntile, tile, 0)

    for d in range(2):
        fre_ref[0, d:d + 1, :] = car_ref[d, 0, 0:1, :]
        fim_ref[0, d:d + 1, :] = car_ref[d, 1, 0:1, :]

    wglu = wglu_ref[...].astype(BF16)

    def glu(t, _):
        t0 = pl.multiple_of(t * tt, tt)
        g = jax.nn.gelu(yp_ref[pl.ds(t0, tt), :])
        out = g * jax.nn.sigmoid(_dot(g.astype(BF16), wglu) + bglu_ref[...])
        for j in range(ch):
            for hf in range(nhalf):
                o_ref[0, hf, pl.ds(t0 + j, SUBLANES, stride=ch), :] = out[j * SUBLANES:(j + 1) * SUBLANES,
                                                                          hf * LANES:(hf + 1) * LANES]
        return 0

    lax.fori_loop(0, ntile, glu, 0)


def _cmul(ar, ai, br, bi):
    return ar * br - ai * bi, ar * bi + ai * br


def _s5_tables(a_re, a_im, log_dt, b_re, b_im, c_re, c_im, seq_lens):
    depth = a_re.shape[0]
    g, p, hc = SSM_GROUPS, SSM_STATE, SSM_GROUP_CH
    lr, li = a_re.astype(F32), a_im.astype(F32)
    dt = jnp.exp(log_dt.astype(F32))[..., None]
    mag = jnp.exp(lr * dt)
    ar, ai = mag * jnp.cos(li * dt), mag * jnp.sin(li * dt)
    den = lr * lr + li * li
    qr = ((ar - 1.0) * lr + ai * li) / den
    qi = (ai * lr - (ar - 1.0) * li) / den
    br, bi = _cmul(qr[..., None], qi[..., None], b_re.astype(F32), b_im.astype(F32))
    bparts = jnp.stack([br, bi], axis=2)
    bmat = bparts.transpose(0, 3, 5, 1, 2, 4).reshape(depth, g * hc, 4 * p)
    cparts = jnp.stack([c_re.astype(F32), -c_im.astype(F32)], axis=2)
    cmat = cparts.transpose(0, 1, 2, 5, 3, 4).reshape(depth, 4 * p, g * hc)

    def power(k):
        kk = k[None, None, :, None, None]
        m = jnp.exp(kk * (lr * dt)[:, :, None])
        th = kk * (li * dt)[:, :, None]
        return (m * jnp.cos(th)).reshape(depth, 2, -1, g * p), (m * jnp.sin(th)).reshape(depth, 2, -1, g * p)

    a1 = jnp.stack([ar.reshape(depth, 2, g * p), ai.reshape(depth, 2, g * p)], axis=2).reshape(depth, 4, g * p)
    tabs = {}
    for seq in set(seq_lens):
        ch = min(seq, SCAN_TILE) // SUBLANES
        cr_, ci_ = power(jnp.asarray([ch, 2 * ch, 4 * ch], F32))
        ach = jnp.stack([cr_, ci_], axis=3).reshape(depth, 12, g * p)
        fr, fi = power(jnp.arange(1, ch + 1, dtype=F32))
        fr = jnp.stack([fr[:, 0], fr[:, 1, ::-1]], axis=1)
        fi = jnp.stack([fi[:, 0], fi[:, 1, ::-1]], axis=1)
        pfix = jnp.stack([fr, fi], axis=2).reshape(depth, 4, ch, g * p)
        tabs[seq] = (bmat, cmat, a1, ach, pfix)
    return tabs


def _s5(u, h0re, h0im, tabs, d_skip, w_glu, b_glu, layer):
    b, nh, seq, _ = u.shape
    w = nh * LANES
    tabs = tabs[seq]
    tile = tabs[4].shape[2] * SUBLANES
    lay = lambda a: pl.BlockSpec((None,) + a.shape[1:], lambda i: (layer,) + (0,) * (a.ndim - 1))
    return pl.pallas_call(
        _s5_kernel,
        out_shape=[jax.ShapeDtypeStruct((b, nh, seq, LANES), F32),
                   jax.ShapeDtypeStruct((b, 2, SSM_N), F32),
                   jax.ShapeDtypeStruct((b, 2, SSM_N), F32)],
        grid=(b,),
        in_specs=[
            pl.BlockSpec((1, nh, seq, LANES), lambda i: (i, 0, 0, 0)),
            pl.BlockSpec((1, 2, SSM_N), lambda i: (i, 0, 0)),
            pl.BlockSpec((1, 2, SSM_N), lambda i: (i, 0, 0)),
            lay(tabs[0]), lay(tabs[1]), lay(tabs[2]), lay(tabs[3]), lay(tabs[4]),
            pl.BlockSpec((None, 1, w), lambda i: (layer, 0, 0)),
            pl.BlockSpec((None, w, w), lambda i: (layer, 0, 0)),
            pl.BlockSpec((None, 1, w), lambda i: (layer, 0, 0)),
        ],
        out_specs=[pl.BlockSpec((1, nh, seq, LANES), lambda i: (i, 0, 0, 0)),
                   pl.BlockSpec((1, 2, SSM_N), lambda i: (i, 0, 0)),
                   pl.BlockSpec((1, 2, SSM_N), lambda i: (i, 0, 0))],
        scratch_shapes=[pltpu.VMEM((seq, w), BF16),
                        pltpu.VMEM((seq, w), F32),
                        pltpu.VMEM((2, tile, 2 * SSM_N), F32),
                        pltpu.VMEM((2, 2, SUBLANES, SSM_N), F32),
                        pltpu.VMEM((w, 4 * SSM_N), BF16),
                        pltpu.VMEM((4 * SSM_N, w), BF16)],
        compiler_params=_params(1),
        name="s5",
    )(u, h0re, h0im, *tabs, d_skip, w_glu, b_glu)


def _proj_out_kernel(oa_ref, ob_ref, oc_ref, w_ref, x_ref, mod_ref, g_ref, wr_ref, o_ref, h_ref, lg_ref, wbf_ref):
    @pl.when((pl.program_id(0) == 0) & (pl.program_id(1) == 0))
    def _():
        _cast_rows(wbf_ref, w_ref, 128)

    i1, i2 = NA_WIDTH, NA_WIDTH + DIFF_WIDTH
    for bi in range(x_ref.shape[0]):
        for r0 in range(0, x_ref.shape[1], ROW_TILE):
            rows = slice(r0, r0 + ROW_TILE)
            y = _dot(oa_ref[bi, rows, :].astype(BF16), wbf_ref[:i1, :])
            y += _dot(ob_ref[bi, rows, :].astype(BF16), wbf_ref[i1:i2, :])
            for hf in range(SSM_WIDTH // LANES):
                y += _dot(oc_ref[bi, hf, rows, :].astype(BF16), wbf_ref[i2 + hf * LANES:i2 + (hf + 1) * LANES, :])
            xn = x_ref[bi, rows, :] + mod_ref[0, 2:3, :] * y
            o_ref[bi, rows, :] = xn
            h = _modulated_norm(xn, g_ref[...], mod_ref[0, 3:4, :], mod_ref[0, 4:5, :])
            h_hi = h.astype(BF16)
            h_ref[bi, rows, :] = h_hi
            h_lo = (h - h_hi.astype(F32)).astype(BF16)
            t1 = _dot_nt(wr_ref[...], h_hi)
            t2 = _dot_nt(wr_ref[:N_EXPERTS, :], h_lo)
            lg_ref[bi, :, rows] = t1[:N_EXPERTS, :] + t1[N_EXPERTS:, :] + t2


def _proj_out(oa, ob, oc, w_out, x, mod, g_ffn, wr_t, layer):
    b, n, d = x.shape
    bb, tm = _step_tile(b, n, mod.shape[0], PROJ_OUT_CHUNKS)
    mod_map = (lambda i, j: (i, 0, 0)) if mod.shape[0] > 1 else (lambda i, j: (0, 0, 0))
    kw = w_out.shape[1]
    return pl.pallas_call(
        _proj_out_kernel,
        out_shape=[jax.ShapeDtypeStruct((b, n, d), F32),
                   jax.ShapeDtypeStruct((b, n, d), BF16),
                   jax.ShapeDtypeStruct((b, N_EXPERTS, n), F32)],
        grid=(b // bb, n // tm),
        in_specs=[
            pl.BlockSpec((bb, tm, NA_WIDTH), lambda i, j: (i, j, 0)),
            pl.BlockSpec((bb, tm, DIFF_WIDTH), lambda i, j: (i, j, 0)),
            pl.BlockSpec((bb, SSM_WIDTH // LANES, tm, LANES), lambda i, j: (i, 0, j, 0)),
            pl.BlockSpec((None, kw, d), lambda i, j: (layer, 0, 0), pipeline_mode=pl.Buffered(1)),
            pl.BlockSpec((bb, tm, d), lambda i, j: (i, j, 0)),
            pl.BlockSpec((1, 6, d), mod_map),
            pl.BlockSpec((None, 1, d), lambda i, j: (layer, 0, 0)),
            pl.BlockSpec((None, 2 * N_EXPERTS, d), lambda i, j: (layer, 0, 0)),
        ],
        out_specs=[pl.BlockSpec((bb, tm, d), lambda i, j: (i, j, 0)),
                   pl.BlockSpec((bb, tm, d), lambda i, j: (i, j, 0)),
                   pl.BlockSpec((bb, N_EXPERTS, tm), lambda i, j: (i, 0, j))],
        scratch_shapes=[pltpu.VMEM((kw, d), BF16)],
        compiler_params=_params(2),
        name="proj_out",
    )(oa, ob, oc, w_out, x, mod, g_ffn, wr_t)


def _excl_cumsum_lanes(m):
    e, n = m.shape
    blk = MXU_DIM
    nb = n // blk
    r = lax.broadcasted_iota(I32, (blk, blk), 0)
    c = lax.broadcasted_iota(I32, (blk, blk), 1)
    tri = jnp.where(r < c, 1.0, 0.0).astype(BF16)
    stacked = jnp.concatenate([m[:, k * blk:(k + 1) * blk] for k in range(nb)], axis=0).astype(BF16)
    within = _dot(stacked, tri)
    outs = []
    off = jnp.zeros((e, 1), F32)
    for k in range(nb):
        outs.append(within[k * e:(k + 1) * e, :] + off)
        off = off + jnp.sum(m[:, k * blk:(k + 1) * blk], axis=1, keepdims=True)
    return jnp.concatenate(outs, axis=1)


def _route_kernel(lg_ref, pos_ref, gate_ref, *, cap):
    b, e, n = lg_ref.shape
    lg = lg_ref[...]
    mx = lg.max(axis=1, keepdims=True)
    ex = jnp.exp(lg - mx)
    aff = (ex / ex.sum(axis=1, keepdims=True)).reshape(b * e, n)

    def search(it, cur):
        cand = cur | (jnp.int32(1) << (jnp.int32(30) - it))
        cnt = jnp.sum(jnp.where(aff >= pltpu.bitcast(cand, F32), 1.0, 0.0), axis=1, keepdims=True)
        return jnp.where(cnt >= cap, cand, cur)

    thr = pltpu.bitcast(lax.fori_loop(0, 31, search, jnp.zeros((b * e, 1), I32)), F32)
    gt = aff > thr
    eq = aff == thr
    need = cap - jnp.sum(jnp.where(gt, 1.0, 0.0), axis=1, keepdims=True)
    eq_rank = _excl_cumsum_lanes(jnp.where(eq, 1.0, 0.0))
    sel = gt | (eq & (eq_rank < need))
    slot = _excl_cumsum_lanes(jnp.where(sel, 1.0, 0.0))
    pos = jnp.where(sel, slot, -1.0).astype(I32)
    gate = jnp.where(sel, aff, 0.0)
    for i in range(b):
        pos_ref[:, i, 0, :] = pos[i * e:(i + 1) * e, :]
        gate_ref[:, i, 0, :] = gate[i * e:(i + 1) * e, :]


def _route(logits):
    b, e, n = logits.shape
    cap = EC_CAPACITY_FACTOR * n // N_EXPERTS
    return pl.pallas_call(
        functools.partial(_route_kernel, cap=cap),
        out_shape=[jax.ShapeDtypeStruct((e, b, 1, n), I32),
                   jax.ShapeDtypeStruct((e, b, 1, n), F32)],
        compiler_params=_params(0),
        name="route",
    )(logits)


def _one_hot_rows(pos_row, cap):
    n = pos_row.shape[1]
    slot = lax.broadcasted_iota(I32, (cap, n), 0)
    return slot == pos_row


def _gather_kernel(h_ref, pos_ref, gate_ref, xs_ref, gs_ref, *, cap):
    group = max(1, GATHER_ROWS // cap)
    for bi in range(h_ref.shape[0]):
        h = h_ref[bi]
        for e0 in range(0, N_EXPERTS, group):
            ohs = [_one_hot_rows(pos_ref[e, bi], cap) for e in range(e0, e0 + group)]
            xs = _dot(jnp.concatenate([jnp.where(oh, 1.0, 0.0).astype(BF16) for oh in ohs], axis=0), h).astype(BF16)
            for i, oh in enumerate(ohs):
                e = e0 + i
                rows = slice(bi * cap, (bi + 1) * cap)
                xs_ref[e, rows, :] = xs[i * cap:(i + 1) * cap, :]
                g = jnp.sum(jnp.where(oh, gate_ref[e, bi], 0.0), axis=1, keepdims=True)
                gs_ref[e, rows, :] = jnp.broadcast_to(g, (cap, LANES))


def _requests_per_step(b, n):
    bb = max(1, MOE_STEP_TOKENS // n)
    return bb if b % bb == 0 else 1


def _gather(h, pos, gate):
    b, n, d = h.shape
    cap = EC_CAPACITY_FACTOR * n // N_EXPERTS
    bb = _requests_per_step(b, n)
    return pl.pallas_call(
        functools.partial(_gather_kernel, cap=cap),
        out_shape=[jax.ShapeDtypeStruct((N_EXPERTS, b * cap, d), BF16),
                   jax.ShapeDtypeStruct((N_EXPERTS, b * cap, LANES), F32)],
        grid=(b // bb,),
        in_specs=[
            pl.BlockSpec((bb, n, d), lambda i: (i, 0, 0)),
            pl.BlockSpec((N_EXPERTS, bb, 1, n), lambda i: (0, i, 0, 0)),
            pl.BlockSpec((N_EXPERTS, bb, 1, n), lambda i: (0, i, 0, 0)),
        ],
        out_specs=[pl.BlockSpec((N_EXPERTS, bb * cap, d), lambda i: (0, i, 0)),
                   pl.BlockSpec((N_EXPERTS, bb * cap, LANES), lambda i: (0, i, 0))],
        compiler_params=_params(1),
        name="moe_gather",
    )(h, pos, gate)


def _ffn_kernel(xp_ref, gp_ref, xs_ref, gs_ref, wg_ref, wu_ref, wd_ref, yp_ref, ys_ref, acc_ref):
    j = pl.program_id(1)
    sp = xp_ref.shape[1]

    @pl.when((pl.program_id(0) == 0) & (j == 0))
    def _():
        acc_ref[...] = jnp.zeros_like(acc_ref)

    wg = wg_ref[...].astype(BF16)
    wu = wu_ref[...].astype(BF16)
    wd = wd_ref[...].astype(BF16)
    for x_ref, g_ref, y_ref, rows in ((xp_ref, gp_ref, yp_ref, slice(0, sp)),
                                      (xs_ref, gs_ref, ys_ref, slice(sp, acc_ref.shape[0]))):
        x = x_ref[0]
        a = _dot(x, wg)
        u = _dot(x, wu)
        hid = (a * jax.nn.sigmoid(a)) * u
        acc = jnp.where(j > 0, acc_ref[rows, :], 0.0) + _dot(hid.astype(BF16), wd)
        acc_ref[rows, :] = acc
        y_ref[0] = (acc * g_ref[0][:, 0:1]).astype(BF16)


def _ffn(xp, gp, xs, gs, w_gate, w_up, w_down, layer):
    e, sp, d = xp.shape
    ss = xs.shape[1]
    s = sp + ss
    ff = w_gate.shape[-1]
    tf = FF_TILE
    return pl.pallas_call(
        _ffn_kernel,
        out_shape=[jax.ShapeDtypeStruct((e, sp, d), BF16), jax.ShapeDtypeStruct((e, ss, d), BF16)],
        grid=(e, ff // tf),
        in_specs=[
            pl.BlockSpec((1, sp, d), lambda i, j: (i, 0, 0)),
            pl.BlockSpec((1, sp, LANES), lambda i, j: (i, 0, 0)),
            pl.BlockSpec((1, ss, d), lambda i, j: (i, 0, 0)),
            pl.BlockSpec((1, ss, LANES), lambda i, j: (i, 0, 0)),
            pl.BlockSpec((None, None, d, tf), lambda i, j: (layer, i, 0, j)),
            pl.BlockSpec((None, None, d, tf), lambda i, j: (layer, i, 0, j)),
            pl.BlockSpec((None, None, tf, d), lambda i, j: (layer, i, j, 0)),
        ],
        out_specs=[pl.BlockSpec((1, sp, d), lambda i, j: (i, 0, 0)),
                   pl.BlockSpec((1, ss, d), lambda i, j: (i, 0, 0))],
        scratch_shapes=[pltpu.VMEM((s, d), F32)],
        compiler_params=_params(2),
        name="moe_ffn",
    )(xp, gp, xs, gs, w_gate, w_up, w_down)


def _combine_kernel(ye_ref, pos_ref, x_ref, mod_ref, *rest, cap, final):
    if final:
        g_ref, o_ref = rest
    else:
        (o_ref,) = rest
    group = max(1, MXU_DIM // cap)
    for bi in range(x_ref.shape[0]):
        y = None
        for e0 in range(0, N_EXPERTS, group):
            oh = jnp.concatenate([jnp.where(_one_hot_rows(pos_ref[e, bi], cap), 1.0, 0.0).astype(BF16)
                                  for e in range(e0, e0 + group)], axis=0)
            ye = jnp.concatenate([ye_ref[e, bi * cap:(bi + 1) * cap, :] for e in range(e0, e0 + group)], axis=0)
            part = _dot_tn(oh, ye)
            y = part if y is None else y + part
        xn = x_ref[bi] + mod_ref[0, 5:6, :] * y
        if final:
            ms = jnp.mean(xn * xn, axis=-1, keepdims=True)
            xn = xn * lax.rsqrt(ms + EPS) * g_ref[...]
        o_ref[bi] = xn


def _combine(ye, pos, x, mod, final_norm):
    b, n, d = x.shape
    cap = EC_CAPACITY_FACTOR * n // N_EXPERTS
    tn = min(n, COMBINE_TILE)
    bm = mod.shape[0]
    bb = _requests_per_step(b, n) if bm == 1 else 1
    mod_map = (lambda i, j: (i, 0, 0)) if bm > 1 else (lambda i, j: (0, 0, 0))
    final = final_norm is not None
    in_specs = [
        pl.BlockSpec((N_EXPERTS, bb * cap, d), lambda i, j: (0, i, 0)),
        pl.BlockSpec((N_EXPERTS, bb, 1, tn), lambda i, j: (0, i, 0, j)),
        pl.BlockSpec((bb, tn, d), lambda i, j: (i, j, 0)),
        pl.BlockSpec((1, 6, d), mod_map),
    ]
    args = [ye, pos, x, mod]
    if final:
        in_specs.append(pl.BlockSpec((1, d), lambda i, j: (0, 0)))
        args.append(final_norm)
    return pl.pallas_call(
        functools.partial(_combine_kernel, cap=cap, final=final),
        out_shape=jax.ShapeDtypeStruct((b, n, d), F32),
        grid=(b // bb, n // tn),
        in_specs=in_specs,
        out_specs=pl.BlockSpec((bb, tn, d), lambda i, j: (i, j, 0)),
        compiler_params=_params(2),
        name="moe_combine",
    )(*args)


def _rope_tables(n):
    t = np.arange(n)
    row = (t // GRID_W).astype(np.float32)
    col = (t % GRID_W).astype(np.float32)
    nf = DIFF_QK // 4
    inv = np.float32(ROPE_BASE) ** (-np.arange(nf, dtype=np.float32) / np.float32(nf))
    lane = np.arange(DIFF_WIDTH)
    pos = np.where(((lane % DIFF_QK) < DIFF_QK // 2)[None, :], row[:, None], col[:, None])
    ang = (pos * inv[lane % nf][None, :]).astype(np.float32)
    first = (lane % (2 * nf)) < nf
    cos, sin = np.cos(ang).astype(np.float32), np.sin(ang).astype(np.float32)
    return jnp.asarray(cos), jnp.asarray(np.where(first[None, :], -sin, sin))


def kernel(x_prompt, x_sample, cache_na_k, cache_na_v, cache_diff_k, cache_diff_v, state_ssm_re, state_ssm_im,
           c, c_ctx, w_ada, b_ada, norm_mix, norm_ffn, w_in, w_out, na_rpb, diff_lambda, diff_subln,
           ssm_a_re, ssm_a_im, ssm_log_dt, ssm_b_re, ssm_b_im, ssm_c_re, ssm_c_im, ssm_d, ssm_w_glu, ssm_b_glu,
           w_router, w_gate, w_up, w_down, final_norm):
    depth = w_in.shape[0]
    bp, sp, d = x_prompt.shape
    bs, ss, _ = x_sample.shape
    assert d == D_MODEL and bs + 1 <= SUBLANES
    past = cache_na_k.shape[2]

    cond = jnp.zeros((SUBLANES, d), F32).at[0].set(c_ctx).at[1:1 + bs].set(c)
    mods = _ada(cond, w_ada, b_ada).reshape(depth, SUBLANES, 6, d)

    rope_tabs = _rope_tables(ss)
    kc_a = cache_na_k.reshape(bs, depth, past, NA_WIDTH)
    vc_a = cache_na_v.reshape(bs, depth, past, NA_WIDTH)
    kc_b = cache_diff_k.reshape(bs, depth, past, DIFF_WIDTH)
    vc_b = cache_diff_v.reshape(bs, depth, past, DIFF_WIDTH)
    subln = jnp.tile(diff_subln, (1, LANES // DIFF_V)).reshape(depth, 1, LANES)
    norm_mix = norm_mix.reshape(depth, 1, d)
    norm_ffn = norm_ffn.reshape(depth, 1, d)
    ssm_d = ssm_d.reshape(depth, 1, SSM_WIDTH)
    ssm_b_glu = ssm_b_glu.reshape(depth, 1, SSM_WIDTH)
    wr_t = jnp.swapaxes(w_router, 1, 2).astype(F32)
    wr_hi = wr_t.astype(BF16)
    wr_t = jnp.concatenate([wr_hi, (wr_t - wr_hi.astype(F32)).astype(BF16)], axis=1)
    fnorm = final_norm.reshape(1, d)
    zero_state = jnp.zeros((bp, 2, SSM_N), F32)
    tabs = _s5_tables(ssm_a_re, ssm_a_im, ssm_log_dt, ssm_b_re, ssm_b_im, ssm_c_re, ssm_c_im, (sp, ss))
    bias_tab = _na_bias_table(na_rpb)

    xp, xs = x_prompt, x_sample
    new_sre, new_sim = [], []
    caches = None
    for l in range(depth):
        lam_init = 0.8 - 0.6 * math.exp(-0.3 * l)
        mod_p = mods[l, 0:1]
        mod_s = mods[l, 1:1 + bs]
        diff_params = (diff_lambda, subln)

        qa, ka, va, qb, kb, vb, u = _proj_in(xp, mod_p, norm_mix, w_in, l, None, depth, caches)
        caches = (ka, va, kb, vb)
        o_a = _attention(qa, ka, va, l, None, l, None, lam_init)
        o_b = _attention(qb, kb, vb, l, None, l, diff_params, lam_init)
        o_c, fre, fim = _s5(u, zero_state, zero_state, tabs, ssm_d, ssm_w_glu, ssm_b_glu, l)
        xp, hp, lg_p = _proj_out(o_a, o_b, o_c, w_out, xp, mod_p, norm_ffn, wr_t, l)
        new_sre.append(fre.reshape(bp, 2, SSM_GROUPS, SSM_STATE))
        new_sim.append(fim.reshape(bp, 2, SSM_GROUPS, SSM_STATE))

        qa, ka, va, qb, kb, vb, u = _proj_in(xs, mod_s, norm_mix, w_in, l, rope_tabs, 0, None)
        o_a = _neighbourhood_attention(qa, ka, va, kc_a, vc_a, bias_tab, l)
        o_b = _attention(qb, kb, vb, 0, (kc_b, vc_b), l, diff_params, lam_init)
        h0re = state_ssm_re[:, l].reshape(bs, 2, SSM_N)
        h0im = state_ssm_im[:, l].reshape(bs, 2, SSM_N)
        o_c, _, _ = _s5(u, h0re, h0im, tabs, ssm_d, ssm_w_glu, ssm_b_glu, l)
        xs, hs, lg_s = _proj_out(o_a, o_b, o_c, w_out, xs, mod_s, norm_ffn, wr_t, l)

        pos_p, gate_p = _route(lg_p)
        pos_s, gate_s = _route(lg_s)
        xg_p, gs_p = _gather(hp, pos_p, gate_p)
        xg_s, gs_s = _gather(hs, pos_s, gate_s)
        ye_p, ye_s = _ffn(xg_p, gs_p, xg_s, gs_s, w_gate, w_up, w_down, l)
        last = l == depth - 1
        xp = _combine(ye_p, pos_p, xp, mod_p, fnorm if last else None)
        xs = _combine(ye_s, pos_s, xs, mod_s, fnorm if last else None)

    ka, va, kb, vb = caches
    return (xp, xs, ka.reshape(bp, depth, sp, NA_HEADS, HEAD_DIM), va.reshape(bp, depth, sp, NA_HEADS, HEAD_DIM),
            kb.reshape(bp, depth, sp, DIFF_HEADS, DIFF_V), vb.reshape(bp, depth, sp, DIFF_HEADS, DIFF_V),
            jnp.stack(new_sre, axis=1), jnp.stack(new_sim, axis=1))
```
